```python
import math
import jax, jax.numpy as jnp
from jax import lax
import numpy as np

D_MODEL = 1024
BATCH = 32
SEQ = 256
DEPTH = 2
DEC_BATCH = 8
DEC_SEQ = 4096
PAST_LEN = 256

GRID_W = 64
BRANCH_W = 512
N_BRANCH = 4
SGU_CHUNK = 128
SGU_GROUPS = 4
SGU_WIDTH = BRANCH_W
HG_HEADS = 4
HG_DK = 128
HG_DV = 128
GD_HEADS = 4
GD_DK = 128
GD_DV = 128
GD_CONV = 3
NA_HEADS = 8
NA_DH = 64
NA_KH = 8
NA_KW = 16
NA_Q_BLOCK = 128
SCAN_CHUNK = 64
N_EXP = 32
TOP_K = 4
D_FF = 1024
SWIGLU_LIMIT = 7.0
SWIGLU_ALPHA = 1.702
MOE_BLOCK = 128
EPS = 1e-6

HG_W = HG_HEADS * HG_DK
GD_QKV = 2 * GD_HEADS * GD_DK + GD_HEADS * GD_DV
IN_WIDTHS = (SGU_WIDTH, SGU_WIDTH, HG_W, HG_W, HG_W, HG_HEADS * HG_DV, HG_HEADS * HG_DV,
             GD_QKV, 2 * GD_HEADS, 2 * GD_HEADS, GD_HEADS * GD_DV, 3 * NA_HEADS * NA_DH, N_BRANCH * D_MODEL)
N_IN = sum(IN_WIDTHS)
F32 = jnp.float32

kernel_name = 'hybrid_flow_prefix_trunk_step'


def _rmsnorm(x, g):
    xf = x.astype(F32)
    y = xf * lax.rsqrt(jnp.mean(xf * xf, axis=-1, keepdims=True) + EPS)
    return (y * g.astype(F32)).astype(x.dtype)


def _l2norm(x):
    return x * lax.rsqrt(jnp.sum(x * x, axis=-1, keepdims=True) + EPS)


def _split_cols(z):
    parts, off = [], 0
    for w in IN_WIDTHS:
        parts.append(z[..., off:off + w])
        off += w
    return parts


def _modulation(cvec, w, b):
    m = (jax.nn.silu(cvec) @ w + b)[..., None, :]
    return jnp.split(m, 6, axis=-1)


def _hgrn_lower_bounds(hg_lb):
    cs = jnp.cumsum(jax.nn.softmax(hg_lb.astype(F32), axis=1), axis=1)
    return cs - cs[:, :1]


def _to_chunks(a):
    B, L = a.shape[:2]
    a = a.reshape((B, L // SCAN_CHUNK, SCAN_CHUNK) + a.shape[2:])
    return jnp.swapaxes(jnp.moveaxis(a, 1, 0), 2, 3)


def _from_chunks(o):
    n, B, H, C, d = o.shape
    return jnp.moveaxis(jnp.swapaxes(o, 2, 3), 0, 1).reshape(B, n * C, H, d)


def _chunk_sgu(u, v, g_norm, w_s, b_s):
    B, L, _ = v.shape
    u = jax.nn.gelu(u)
    v = _rmsnorm(jax.nn.gelu(v), g_norm)
    vc = v.reshape(B, L // SGU_CHUNK, SGU_CHUNK, SGU_GROUPS, SGU_WIDTH // SGU_GROUPS)
    s = jnp.einsum('gpq,bnqgc->bnpgc', w_s, vc) + b_s.T[:, :, None]
    return u * s.reshape(B, L, SGU_WIDTH)


def _gla_chunk_scan(q, k, v, logf, s0):
    incl = jnp.tril(jnp.ones((SCAN_CHUNK, SCAN_CHUNK), bool))[:, :, None]

    def step(S, xs):
        qc, kc, vc, lf = xs
        b = jnp.cumsum(lf, axis=2)
        decay = jnp.exp(jnp.where(incl, b[:, :, :, None, :] - b[:, :, None, :, :], -jnp.inf))
        att = jnp.einsum('bhtd,bhsd,bhtsd->bhts', qc, kc, decay)
        o = jnp.einsum('bhts,bhse->bhte', att, vc) + jnp.einsum('bhtd,bhde->bhte', qc * jnp.exp(b), S)
        b_last = b[:, :, -1:, :]
        S = S * jnp.exp(b_last)[:, :, 0, :, None] + jnp.einsum('bhsd,bhse->bhde', kc * jnp.exp(b_last - b), vc)
        return S, o

    s_fin, o = lax.scan(step, s0, tuple(_to_chunks(t) for t in (q, k, v, logf)))
    return _from_chunks(o), s_fin


def _delta_chunk_scan(q, k, v, g, beta, s0):
    incl = jnp.tril(jnp.ones((SCAN_CHUNK, SCAN_CHUNK), bool))
    strict = jnp.tril(jnp.ones((SCAN_CHUNK, SCAN_CHUNK), bool), -1)
    eye = jnp.eye(SCAN_CHUNK, dtype=F32)

    def step(S, xs):
        qc, kc, vc, gc, bc = xs
        gcum = jnp.cumsum(gc, axis=-1)
        decay = jnp.exp(jnp.where(incl, gcum[..., :, None] - gcum[..., None, :], -jnp.inf))
        kb = kc * bc[..., None]
        a = jnp.where(strict, jnp.einsum('bhtd,bhsd->bhts', kb, kc) * decay, 0.0)
        rhs = jnp.concatenate([vc * bc[..., None], kb * jnp.exp(gcum)[..., None]], axis=-1)
        sol = lax.linalg.triangular_solve(eye + a, rhs, left_side=True, lower=True, unit_diagonal=True)
        u = sol[..., :GD_DV] - jnp.einsum('bhtd,bhde->bhte', sol[..., GD_DV:], S)
        att = jnp.einsum('bhtd,bhsd->bhts', qc, kc) * decay
        o = jnp.einsum('bhtd,bhde->bhte', qc * jnp.exp(gcum)[..., None], S) + jnp.einsum('bhts,bhse->bhte', att, u)
        g_last = gcum[..., -1:]
        S = S * jnp.exp(g_last)[..., None] + jnp.einsum('bhsd,bhse->bhde', kc * jnp.exp(g_last - gcum)[..., None], u)
        return S, o

    s_fin, o = lax.scan(step, s0, tuple(_to_chunks(t) for t in (q, k, v, g, beta)))
    return _from_chunks(o), s_fin


def _hgrn2(q_raw, f_fwd, f_bwd, i_raw, g_raw, lb, onorm, s0):
    B, L, _ = q_raw.shape
    shp_k = (B, L, HG_HEADS, HG_DK)
    q = (jax.nn.silu(q_raw.astype(F32)) * HG_DK ** -0.5).reshape(shp_k)
    v = i_raw.astype(F32).reshape(B, L, HG_HEADS, HG_DV)
    o_sum, finals = 0.0, []
    for d, zf in enumerate((f_fwd, f_bwd)):
        zf = zf.astype(F32)
        lb_d = lb[d]
        logf = jnp.logaddexp(jnp.log(lb_d), jnp.log1p(-lb_d) + jax.nn.log_sigmoid(zf)).reshape(shp_k)
        k = ((1.0 - lb_d) * jax.nn.sigmoid(-zf)).reshape(shp_k)
        seq = (q, k, v, logf)
        if d == 1:
            seq = tuple(jnp.flip(t, axis=1) for t in seq)
        o, s_fin = _gla_chunk_scan(*seq, s0[:, d].astype(F32))
        if d == 1:
            o = jnp.flip(o, axis=1)
        o_sum = o_sum + o
        finals.append(s_fin)
    o = _rmsnorm(o_sum, onorm).reshape(B, L, -1) * jax.nn.silu(g_raw.astype(F32))
    return o, jnp.stack(finals, axis=1)


def _short_conv(x, w):
    return lax.conv_general_dilated(x, w[:, None, :].astype(x.dtype), window_strides=(1,),
                                    padding=((GD_CONV // 2, GD_CONV // 2),),
                                    dimension_numbers=('NWC', 'WIO', 'NWC'),
                                    feature_group_count=x.shape[-1])


def _gated_deltanet(qkv_raw, a_raw, b_raw, g_raw, conv_w, A_log, dt_bias, onorm, s0):
    B, L, _ = qkv_raw.shape
    qkv = jax.nn.silu(_short_conv(qkv_raw, conv_w).astype(F32))
    nq = GD_HEADS * GD_DK
    q = _l2norm(qkv[..., :nq].reshape(B, L, GD_HEADS, GD_DK)) * GD_DK ** -0.5
    k = _l2norm(qkv[..., nq:2 * nq].reshape(B, L, GD_HEADS, GD_DK))
    v = qkv[..., 2 * nq:].reshape(B, L, GD_HEADS, GD_DV)
    a = a_raw.astype(F32).reshape(B, L, 2, GD_HEADS)
    bt = b_raw.astype(F32).reshape(B, L, 2, GD_HEADS)
    o_sum, finals = 0.0, []
    for d in range(2):
        g = -jnp.exp(A_log[d].astype(F32)) * jax.nn.softplus(a[:, :, d] + dt_bias[d].astype(F32))
        beta = jax.nn.sigmoid(bt[:, :, d])
        seq = (q, k, v, g, beta)
        if d == 1:
            seq = tuple(jnp.flip(t, axis=1) for t in seq)
        o, s_fin = _delta_chunk_scan(*seq, s0[:, d].astype(F32))
        if d == 1:
            o = jnp.flip(o, axis=1)
        o_sum = o_sum + o
        finals.append(s_fin)
    o = _rmsnorm(o_sum, onorm).reshape(B, L, -1) * jax.nn.silu(g_raw.astype(F32))
    return o, jnp.stack(finals, axis=1)


def _context_attention(q, k, v):
    B, L, H, Dh = q.shape
    qb = jnp.moveaxis(q.reshape(B, L // NA_Q_BLOCK, NA_Q_BLOCK, H, Dh), 1, 0)

    def block(q_blk):
        p = jax.nn.softmax(jnp.einsum('bqhd,bkhd->bhqk', q_blk, k).astype(F32), axis=-1)
        return jnp.einsum('bhqk,bkhd->bqhd', p.astype(v.dtype), v)

    o = lax.map(block, qb)
    return jnp.moveaxis(o, 0, 1).reshape(B, L, H * Dh)


def _neighbourhood_attention(q, k, v, k_ctx, v_ctx, rpb):
    B, S, H, Dh = q.shape
    rows = S // GRID_W
    kh = min(NA_KH, rows)
    qg = q.reshape(B, rows, GRID_W, H, Dh)
    kg = k.reshape(B, rows, GRID_W, H, Dh)
    vg = v.reshape(B, rows, GRID_W, H, Dh)
    col = jnp.arange(GRID_W)
    col_start = jnp.clip(col - NA_KW // 2, 0, GRID_W - NA_KW)
    col_ok = (col[None, :] >= col_start[:, None]) & (col[None, :] < col_start[:, None] + NA_KW)
    dc_idx = jnp.clip(col[None, :] - col[:, None], -(NA_KW - 1), NA_KW - 1) + NA_KW - 1

    def row_block(r):
        r0 = jnp.clip(r - kh // 2, 0, rows - kh)
        q_r = lax.dynamic_index_in_dim(qg, r, axis=1, keepdims=False)
        k_r = lax.dynamic_slice_in_dim(kg, r0, kh, axis=1).reshape(B, kh * GRID_W, H, Dh)
        v_r = lax.dynamic_slice_in_dim(vg, r0, kh, axis=1).reshape(B, kh * GRID_W, H, Dh)
        dr_idx = r0 + jnp.arange(kh) - r + NA_KH - 1
        bias = rpb[:, dr_idx][:, :, dc_idx]
        bias = jnp.where(col_ok[None, None], bias.astype(F32), -jnp.inf)
        bias = bias.transpose(0, 2, 1, 3).reshape(H, GRID_W, kh * GRID_W)
        s_nb = jnp.einsum('bqhd,bkhd->bhqk', q_r, k_r).astype(F32) + bias[None]
        s_cx = jnp.einsum('bqhd,bkhd->bhqk', q_r, k_ctx).astype(F32)
        m = jnp.maximum(jnp.max(s_nb, axis=-1, keepdims=True), jnp.max(s_cx, axis=-1, keepdims=True))
        e_nb = jnp.exp(s_nb - m)
        e_cx = jnp.exp(s_cx - m)
        den = jnp.sum(e_nb, axis=-1, keepdims=True) + jnp.sum(e_cx, axis=-1, keepdims=True)
        return (jnp.einsum('bhqk,bkhd->bqhd', (e_nb / den).astype(v.dtype), v_r)
                + jnp.einsum('bhqk,bkhd->bqhd', (e_cx / den).astype(v.dtype), v_ctx))

    o = lax.map(row_block, jnp.arange(rows))
    return jnp.moveaxis(o, 0, 1).reshape(B, S, H * Dh)


def _token_mixers(h, lp, ctx):
    B, L, _ = h.shape
    z = h @ lp['w_in']
    (su, sv, hq, hf_fwd, hf_bwd, hi, hgate, gqkv, ga, gb, ggate, nqkv, mgate) = _split_cols(z)
    o_a = _chunk_sgu(su, sv, lp['sgu_norm'], lp['sgu_w'], lp['sgu_b'])
    if ctx is None:
        s_hg0 = jnp.zeros((B, 2, HG_HEADS, HG_DK, HG_DV), F32)
        s_gd0 = jnp.zeros((B, 2, GD_HEADS, GD_DK, GD_DV), F32)
    else:
        s_hg0, s_gd0 = ctx[2], ctx[3]
    o_b, s_hg = _hgrn2(hq, hf_fwd, hf_bwd, hi, hgate, lp['hg_lb'], lp['hg_onorm'], s_hg0)
    o_c, s_gd = _gated_deltanet(gqkv, ga, gb, ggate, lp['gd_conv'], lp['gd_A_log'], lp['gd_dt_bias'],
                                lp['gd_onorm'], s_gd0)
    q, k, v = [t.reshape(B, L, NA_HEADS, NA_DH) for t in jnp.split(nqkv, 3, axis=-1)]
    q = q * NA_DH ** -0.5
    if ctx is None:
        o_d = _context_attention(q, k, v)
    else:
        o_d = _neighbourhood_attention(q, k, v, ctx[0], ctx[1], lp['na_rpb'])
    gates = jax.nn.sigmoid(mgate.reshape(B, L, N_BRANCH, D_MODEL))
    merged = gates[:, :, 0] * (o_a.astype(h.dtype) @ lp['w_branch'][0])
    for n_, o_n in enumerate((o_b, o_c, o_d), start=1):
        merged = merged + gates[:, :, n_] * (o_n.astype(h.dtype) @ lp['w_branch'][n_])
    return merged @ lp['w_out'], (k, v, s_hg, s_gd)


def _moe(h, lp):
    T, D = h.shape
    logits = (h @ lp['w_router'] + lp['b_router']).astype(F32)
    top_v, top_e = lax.top_k(logits, TOP_K)
    wts = jax.nn.softmax(top_v, axis=-1).astype(h.dtype)
    n_assign = T * TOP_K
    n_blocks = -(-n_assign // MOE_BLOCK) + N_EXP
    n_pad = n_blocks * MOE_BLOCK
    e_flat = top_e.reshape(n_assign)
    order = jnp.argsort(e_flat)
    e_sorted = e_flat[order]
    tok_sorted = (order // TOP_K).astype(jnp.int32)
    w_sorted = wts.reshape(n_assign)[order]
    counts = jnp.zeros((N_EXP,), jnp.int32).at[e_flat].add(1)
    start = jnp.cumsum(counts) - counts
    padded = (counts + MOE_BLOCK - 1) // MOE_BLOCK * MOE_BLOCK
    pad_end = jnp.cumsum(padded)
    pad_start = pad_end - padded
    pos = pad_start[e_sorted] + jnp.arange(n_assign, dtype=jnp.int32) - start[e_sorted]
    tok = jnp.zeros((n_pad,), jnp.int32).at[pos].set(tok_sorted)
    gw = jnp.zeros((n_pad,), h.dtype).at[pos].set(w_sorted)
    blk_e = jnp.minimum(jnp.searchsorted(pad_end, jnp.arange(n_blocks, dtype=jnp.int32) * MOE_BLOCK,
                                         side='right'), N_EXP - 1)
    xb = h[tok].reshape(n_blocks, MOE_BLOCK, D)

    def expert_block(args):
        xe, e = args
        gu = xe @ lp['w_gu'][e] + lp['b_gu'][e]
        a = jnp.minimum(gu[:, :D_FF], SWIGLU_LIMIT)
        lin = jnp.clip(gu[:, D_FF:], -SWIGLU_LIMIT, SWIGLU_LIMIT)
        y = a * jax.nn.sigmoid(SWIGLU_ALPHA * a) * (lin + 1.0)
        return y @ lp['w_dn'][e] + lp['b_dn'][e]

    yb = lax.map(expert_block, (xb, blk_e)).reshape(n_pad, D)
    return jnp.zeros((T, D), h.dtype).at[tok].add(yb * gw[:, None])


def _layer(x, cvec, lp, ctx):
    sh1, sc1, g1, sh2, sc2, g2 = _modulation(cvec, lp['w_ada'], lp['b_ada'])
    h = _rmsnorm(x, lp['norm1']) * (1.0 + sc1) + sh1
    mix, ctx_side = _token_mixers(h, lp, ctx)
    x = x + g1 * mix
    h = _rmsnorm(x, lp['norm2']) * (1.0 + sc2) + sh2
    B, L, D = x.shape
    x = x + g2 * _moe(h.reshape(B * L, D), lp).reshape(B, L, D)
    return x, ctx_side


def setup_inputs(seed: int = 0) -> dict:
    key = jax.random.key(seed)
    ks = jax.random.split(key, 32)

    def nrm(i, shape, scale):
        return jax.random.normal(ks[i], shape, F32) * scale

    D = D_MODEL
    dt = jnp.exp(jax.random.uniform(ks[21], (DEPTH, 2, GD_HEADS), F32) * (math.log(0.1) - math.log(0.001))
                 + math.log(0.001))
    return {
        'x_prompt': nrm(0, (BATCH, SEQ, D), 1.0),
        'x_sample': nrm(1, (DEC_BATCH, DEC_SEQ, D), 1.0),
        'c': nrm(2, (DEC_BATCH, D), 1.0),
        'cache_na_k': nrm(3, (DEC_BATCH, DEPTH, PAST_LEN, NA_HEADS, NA_DH), 1.0),
        'cache_na_v': nrm(4, (DEC_BATCH, DEPTH, PAST_LEN, NA_HEADS, NA_DH), 1.0),
        'state_hgrn': nrm(5, (DEC_BATCH, DEPTH, 2, HG_HEADS, HG_DK, HG_DV), 0.5),
        'state_gdn': nrm(6, (DEC_BATCH, DEPTH, 2, GD_HEADS, GD_DK, GD_DV), 0.1),
        'c_ctx': nrm(7, (D,), 1.0),
        'w_ada': nrm(8, (DEPTH, D, 6 * D), 0.5 * D ** -0.5),
        'b_ada': nrm(9, (DEPTH, 6 * D), 0.02),
        'norm1': 1.0 + nrm(10, (DEPTH, D), 0.05),
        'norm2': 1.0 + nrm(11, (DEPTH, D), 0.05),
        'norm_f': 1.0 + nrm(12, (D,), 0.05),
        'w_in': nrm(13, (DEPTH, D, N_IN), D ** -0.5),
        'sgu_norm': 1.0 + nrm(14, (DEPTH, SGU_WIDTH), 0.05),
        'sgu_w': nrm(15, (DEPTH, SGU_GROUPS, SGU_CHUNK, SGU_CHUNK), SGU_CHUNK ** -0.5),
        'sgu_b': 1.0 + nrm(16, (DEPTH, SGU_GROUPS, SGU_CHUNK), 0.05),
        'hg_lb': nrm(17, (2, DEPTH, HG_W), 0.5),
        'hg_onorm': 1.0 + nrm(18, (DEPTH, HG_DV), 0.05),
        'gd_conv': nrm(19, (DEPTH, GD_CONV, GD_QKV), GD_CONV ** -0.5),
        'gd_A_log': jnp.log(jax.random.uniform(ks[20], (DEPTH, 2, GD_HEADS), F32, 1.0, 16.0)),
        'gd_dt_bias': dt + jnp.log(-jnp.expm1(-dt)),
        'gd_onorm': 1.0 + nrm(22, (DEPTH, GD_DV), 0.05),
        'na_rpb': nrm(23, (DEPTH, NA_HEADS, 2 * NA_KH - 1, 2 * NA_KW - 1), 0.1),
        'w_branch': nrm(24, (DEPTH, N_BRANCH, BRANCH_W, D), BRANCH_W ** -0.5),
        'w_out': nrm(25, (DEPTH, D, D), D ** -0.5),
        'w_router': nrm(26, (DEPTH, D, N_EXP), D ** -0.5),
        'b_router': nrm(27, (DEPTH, N_EXP), 0.01),
        'w_gu': nrm(28, (DEPTH, N_EXP, D, 2 * D_FF), D ** -0.5),
        'b_gu': nrm(29, (DEPTH, N_EXP, 2 * D_FF), 0.02),
        'w_dn': nrm(30, (DEPTH, N_EXP, D_FF, D), D_FF ** -0.5),
        'b_dn': nrm(31, (DEPTH, N_EXP, D), 0.02),
    }


def reference(x_prompt, x_sample, c, cache_na_k, cache_na_v, state_hgrn, state_gdn, c_ctx, w_ada, b_ada,
              norm1, norm2, norm_f, w_in, sgu_norm, sgu_w, sgu_b, hg_lb, hg_onorm, gd_conv, gd_A_log,
              gd_dt_bias, gd_onorm, na_rpb, w_branch, w_out, w_router, b_router, w_gu, b_gu, w_dn, b_dn):
    lb = _hgrn_lower_bounds(hg_lb)

    def layer_params(l):
        return {'w_ada': w_ada[l], 'b_ada': b_ada[l], 'norm1': norm1[l], 'norm2': norm2[l],
                'w_in': w_in[l], 'sgu_norm': sgu_norm[l], 'sgu_w': sgu_w[l], 'sgu_b': sgu_b[l],
                'hg_lb': lb[:, l], 'hg_onorm': hg_onorm[l], 'gd_conv': gd_conv[l],
                'gd_A_log': gd_A_log[l], 'gd_dt_bias': gd_dt_bias[l], 'gd_onorm': gd_onorm[l],
                'na_rpb': na_rpb[l], 'w_branch': w_branch[l], 'w_out': w_out[l],
                'w_router': w_router[l], 'b_router': b_router[l], 'w_gu': w_gu[l],
                'b_gu': b_gu[l], 'w_dn': w_dn[l], 'b_dn': b_dn[l]}

    xp = x_prompt
    ks_, vs_, hs_, gs_ = [], [], [], []
    for l in range(DEPTH):
        xp, (k_l, v_l, hg_l, gd_l) = _layer(xp, c_ctx, layer_params(l), None)
        ks_.append(k_l)
        vs_.append(v_l)
        hs_.append(hg_l)
        gs_.append(gd_l)
    y_prompt = _rmsnorm(xp, norm_f)
    new_cache_na_k = jnp.stack(ks_, axis=1)
    new_cache_na_v = jnp.stack(vs_, axis=1)
    new_state_hgrn = jnp.stack(hs_, axis=1).astype(x_prompt.dtype)
    new_state_gdn = jnp.stack(gs_, axis=1).astype(x_prompt.dtype)

    xs = x_sample
    for l in range(DEPTH):
        ctx = (cache_na_k[:, l], cache_na_v[:, l], state_hgrn[:, l], state_gdn[:, l])
        xs, _ = _layer(xs, c, layer_params(l), ctx)
    y_sample = _rmsnorm(xs, norm_f)
    return (y_prompt, y_sample, new_cache_na_k, new_cache_na_v, new_state_hgrn, new_state_gdn)
```

```python
import functools
import math

import numpy as np
import jax
import jax.numpy as jnp
from jax import lax
from jax.experimental import pallas as pl
from jax.experimental.pallas import tpu as pltpu

D_MODEL = 1024
DEPTH = 2
GRID_W = 64
BRANCH_W = 512
N_BRANCH = 4
SGU_CHUNK = 128
SGU_GROUPS = 4
HG_HEADS = 4
HG_DK = 128
HG_DV = 128
GD_HEADS = 4
GD_DK = 128
GD_DV = 128
NA_HEADS = 8
NA_DH = 64
NA_KH = 8
NA_KW = 16
N_EXP = 32
TOP_K = 4
D_FF = 1024
SWIGLU_LIMIT = 7.0
SWIGLU_ALPHA = 1.702
EPS = 1e-6

F32 = jnp.float32
BF16 = jnp.bfloat16
HI = lax.Precision.HIGHEST

LANES = 128
MOD_ROWS = 16
SCAN_C = 64
MOE_BLOCK = 128
NA_QROWS = 4
NA_KROWS = NA_QROWS + NA_KH
VMEM_LIMIT = 48 * 1024 * 1024

NT = (((1,), (1,)), ((), ()))
TN = (((0,), (0,)), ((), ()))


def _cparams(*sem):
    return pltpu.CompilerParams(dimension_semantics=sem, vmem_limit_bytes=VMEM_LIMIT)


def _bdot(a, b):
    return jnp.dot(a.astype(BF16), b.astype(BF16), preferred_element_type=F32)


def _bdot_g(a, b, dims):
    return lax.dot_general(a.astype(BF16), b.astype(BF16), dims, preferred_element_type=F32)


def _hdot(a, b):
    return jnp.dot(a, b, precision=HI, preferred_element_type=F32)


def _silu(x):
    return x * jax.nn.sigmoid(x)


def _log_sigmoid(x):
    return jnp.minimum(x, 0.0) - jnp.log1p(jnp.exp(-jnp.abs(x)))


def _logaddexp(a, b):
    return jnp.maximum(a, b) + jnp.log1p(jnp.exp(-jnp.abs(a - b)))


def _softplus(x):
    return jnp.maximum(x, 0.0) + jnp.log1p(jnp.exp(-jnp.abs(x)))


def _ada_kernel(c_ref, w_ref, b_ref, o_ref):
    o_ref[0] = _hdot(_silu(c_ref[...]), w_ref[0]) + b_ref[0]


def _modulation(cvecs, w_ada, b_ada):
    tn = 1536
    out = pl.pallas_call(
        _ada_kernel,
        grid=(DEPTH, 6 * D_MODEL // tn),
        in_specs=[pl.BlockSpec((MOD_ROWS, D_MODEL), lambda l, j: (0, 0)),
                  pl.BlockSpec((1, D_MODEL, tn), lambda l, j: (l, 0, j)),
                  pl.BlockSpec((1, 1, tn), lambda l, j: (l, 0, j))],
        out_specs=pl.BlockSpec((1, MOD_ROWS, tn), lambda l, j: (l, 0, j)),
        out_shape=jax.ShapeDtypeStruct((DEPTH, MOD_ROWS, 6 * D_MODEL), F32),
        compiler_params=_cparams("arbitrary", "arbitrary"),
        name="ada_modulation",
    )(cvecs, w_ada, b_ada.reshape(DEPTH, 1, 6 * D_MODEL))
    return out.reshape(DEPTH, MOD_ROWS * 6, 1, D_MODEL)


def _rms(x):
    return x * lax.rsqrt(jnp.mean(x * x, axis=-1, keepdims=True) + EPS)


def _normmod_kernel(x_ref, g_ref, sc_ref, sh_ref, o_ref):
    h = (_rms(x_ref[...]) * g_ref[...]) * (1.0 + sc_ref[0]) + sh_ref[0]
    o_ref[...] = h.astype(o_ref.dtype)


def _normmod(x, g, mods, mod_row, tm, part_shift, part_scale):
    T = x.shape[0]
    return pl.pallas_call(
        _normmod_kernel,
        grid=(T // tm,),
        in_specs=[pl.BlockSpec((tm, D_MODEL), lambda i: (i, 0)),
                  pl.BlockSpec((1, D_MODEL), lambda i: (0, 0)),
                  pl.BlockSpec((1, 1, D_MODEL), lambda i: (mod_row(i * tm) * 6 + part_scale, 0, 0)),
                  pl.BlockSpec((1, 1, D_MODEL), lambda i: (mod_row(i * tm) * 6 + part_shift, 0, 0))],
        out_specs=pl.BlockSpec((tm, D_MODEL), lambda i: (i, 0)),
        out_shape=jax.ShapeDtypeStruct((T, D_MODEL), BF16),
        compiler_params=_cparams("parallel"),
        name="normmod",
    )(x, g.reshape(1, D_MODEL), mods, mods)


def _mm_kernel(a_ref, w_ref, o_ref):
    o_ref[...] = jnp.dot(a_ref[...], w_ref[...], preferred_element_type=F32).astype(o_ref.dtype)


def _matmul(a, w, tm, tn, out_dtype=F32):
    T, K = a.shape
    N = w.shape[1]
    return pl.pallas_call(
        _mm_kernel,
        grid=(N // tn, T // tm),
        in_specs=[pl.BlockSpec((tm, K), lambda j, i: (i, 0)),
                  pl.BlockSpec((K, tn), lambda j, i: (0, j))],
        out_specs=pl.BlockSpec((tm, tn), lambda j, i: (i, j)),
        out_shape=jax.ShapeDtypeStruct((T, N), out_dtype),
        compiler_params=_cparams("parallel", "parallel"),
        name="in_proj",
    )(a, w)


def _sgu_kernel(u_ref, v_ref, gn_ref, ws_ref, bs_ref, o_ref):
    rows = u_ref.shape[0]
    gw = BRANCH_W // SGU_GROUPS
    u = jax.nn.gelu(u_ref[...])
    v = (_rms(jax.nn.gelu(v_ref[...])) * gn_ref[...]).astype(BF16)
    for n in range(rows // SGU_CHUNK):
        r = slice(n * SGU_CHUNK, (n + 1) * SGU_CHUNK)
        for g in range(SGU_GROUPS):
            cs = slice(g * gw, (g + 1) * gw)
            s = jnp.dot(ws_ref[g], v[r, cs], preferred_element_type=F32) + bs_ref[:, cs]
            o_ref[r, cs] = (u[r, cs] * s).astype(o_ref.dtype)


def _sgu(z_sgu, g_norm, w_s, b_s, rows):
    T = z_sgu.shape[0]
    gw = BRANCH_W // SGU_GROUPS
    b_exp = jnp.repeat(b_s.T, gw, axis=1)
    return pl.pallas_call(
        _sgu_kernel,
        grid=(T // rows,),
        in_specs=[pl.BlockSpec((rows, BRANCH_W), lambda i: (i, 0)),
                  pl.BlockSpec((rows, BRANCH_W), lambda i: (i, 1)),
                  pl.BlockSpec((1, BRANCH_W), lambda i: (0, 0)),
                  pl.BlockSpec((SGU_GROUPS, SGU_CHUNK, SGU_CHUNK), lambda i: (0, 0, 0)),
                  pl.BlockSpec((SGU_CHUNK, BRANCH_W), lambda i: (0, 0))],
        out_specs=pl.BlockSpec((rows, BRANCH_W), lambda i: (i, 0)),
        out_shape=jax.ShapeDtypeStruct((T, BRANCH_W), BF16),
        compiler_params=_cparams("parallel"),
        name="sgu",
    )(z_sgu, z_sgu, g_norm.reshape(1, BRANCH_W), w_s.astype(BF16), b_exp)


def _order(reverse):
    p = np.arange(SCAN_C)
    return SCAN_C - 1 - p if reverse else p


def _gla_consts():
    C = SCAN_C
    nlev = int(math.log2(C))
    mats, masks = [], []
    for reverse in (False, True):
        p = _order(reverse)
        pt, pr = p[:, None], p[None, :]
        m_d, k_d = [], []
        for lev in range(nlev):
            w = C >> (lev + 1)
            parent = p // (2 * w)
            later = (p % (2 * w)) >= w
            anchor = (parent * 2 * w + w - 1)[:, None]
            m = np.where(later[:, None], (pr > anchor) & (pr <= pt), (pr > pt) & (pr <= anchor))
            m_d.append(m)
            k_d.append((parent[:, None] == parent[None, :]) & later[:, None] & ~later[None, :])
        m_d.append(pr <= pt)
        m_d.append(pr > pt)
        k_d.append(np.eye(C, dtype=bool))
        mats.append(np.concatenate(m_d, axis=0))
        masks.append(np.stack(k_d))
    return (np.stack(mats).astype(np.float32), np.stack(masks).astype(np.float32))


def _delta_consts():
    C = SCAN_C
    tri, sl, incl, strict = [], [], [], []
    for reverse in (False, True):
        p = _order(reverse)
        pt, pr = p[:, None], p[None, :]
        tri.append(np.concatenate([pr <= pt, pr > pt], axis=0))
        sl.append(pt > pr)
        incl.append(pr <= pt)
        strict.append(pr < pt)
    f = lambda a: np.stack(a).astype(np.float32)
    return f(tri), f(sl), f(incl), f(strict)


def _hgrn_kernel(qf_ref, ff_ref, vf_ref, qb_ref, fb_ref, vb_ref, lb_ref, mat_ref, msk_ref, *rest, has_state):
    if has_state:
        s0_ref, of_ref, ob_ref, sfin_ref, st_ref = rest
    else:
        of_ref, ob_ref, sfin_ref, st_ref = rest
    C = SCAN_C
    nlev = msk_ref.shape[1] - 1
    c = pl.program_id(1)
    last_c = pl.num_programs(1) - 1

    @pl.when(c == 0)
    def _():
        for d in range(2):
            for h in range(HG_HEADS):
                if has_state:
                    st_ref[d, h] = s0_ref[0, d, h].T
                else:
                    st_ref[d, h] = jnp.zeros((HG_DV, HG_DK), F32)

    dirs = ((qf_ref, ff_ref, vf_ref, of_ref), (qb_ref, fb_ref, vb_ref, ob_ref))
    for d, (q_ref, f_ref, v_ref, o_ref) in enumerate(dirs):
        for h in range(HG_HEADS):
            cs = slice(h * HG_DK, (h + 1) * HG_DK)
            zf = f_ref[:, cs]
            loglb = lb_ref[d:d + 1, cs]
            log1m = lb_ref[2 + d:3 + d, cs]
            onem = lb_ref[4 + d:5 + d, cs]
            logf = _logaddexp(loglb, log1m + _log_sigmoid(zf))
            k = onem * jax.nn.sigmoid(-zf)
            q = _silu(q_ref[:, cs]) * (HG_DK ** -0.5)
            v = v_ref[:, cs].astype(BF16)
            fac = jnp.exp(_hdot(mat_ref[d], logf))
            att = msk_ref[d, nlev] * _bdot_g(q, k, NT)
            for i in range(nlev):
                fi = fac[i * C:(i + 1) * C]
                att = att + msk_ref[d, i] * _bdot_g(q * fi, k * fi, NT)
            eb = fac[nlev * C:(nlev + 1) * C]
            er = fac[(nlev + 1) * C:]
            st = st_ref[d, h]
            o = _bdot(att, v) + _bdot_g(q * eb, st, NT)
            o_ref[:, cs] = o
            e_last = eb[C - 1:C] if d == 0 else eb[0:1]
            st_new = st * e_last + _bdot_g(v, k * er, TN)
            st_ref[d, h] = st_new

            @pl.when(c == last_c)
            def _():
                sfin_ref[0, d, h] = st_new.T


def _hgrn(z_hg, lbp, s0, B, L):
    n = L // SCAN_C
    mats, masks = _gla_consts()
    blk = (SCAN_C, HG_HEADS * HG_DK)
    fwd = lambda col: pl.BlockSpec(blk, lambda b, c: (b * n + c, col))
    bwd = lambda col: pl.BlockSpec(blk, lambda b, c: (b * n + n - 1 - c, col))
    st_blk = (1, 2, HG_HEADS, HG_DK, HG_DV)
    in_specs = [fwd(0), fwd(1), fwd(3), bwd(0), bwd(2), bwd(3),
                pl.BlockSpec(lbp.shape, lambda b, c: (0, 0)),
                pl.BlockSpec(mats.shape, lambda b, c: (0, 0, 0)),
                pl.BlockSpec(masks.shape, lambda b, c: (0, 0, 0, 0))]
    args = [z_hg] * 6 + [lbp, jnp.asarray(mats), jnp.asarray(masks)]
    if s0 is not None:
        in_specs.append(pl.BlockSpec(st_blk, lambda b, c: (b, 0, 0, 0, 0)))
        args.append(s0)
    return pl.pallas_call(
        functools.partial(_hgrn_kernel, has_state=s0 is not None),
        grid=(B, n),
        in_specs=in_specs,
        out_specs=[pl.BlockSpec(blk, lambda b, c: (b * n + c, 0)),
                   pl.BlockSpec(blk, lambda b, c: (b * n + n - 1 - c, 0)),
                   pl.BlockSpec(st_blk, lambda b, c: (b, 0, 0, 0, 0))],
        out_shape=[jax.ShapeDtypeStruct((B * L, HG_HEADS * HG_DV), F32),
                   jax.ShapeDtypeStruct((B * L, HG_HEADS * HG_DV), F32),
                   jax.ShapeDtypeStruct((B,) + st_blk[1:], F32)],
        scratch_shapes=[pltpu.VMEM((2, HG_HEADS, HG_DV, HG_DK), F32)],
        compiler_params=_cparams("parallel", "arbitrary"),
        name="hgrn_scan",
    )(*args)


GD_NQ = GD_HEADS * GD_DK
GD_QKV = 2 * GD_NQ + GD_HEADS * GD_DV
HALO = 8


def _gdprep_kernel(x_ref, prev_ref, next_ref, w_ref, o_ref, *, tiles_per_seq):
    R = x_ref.shape[0]
    t = pl.program_id(0) % tiles_per_seq
    x = x_ref[...]
    prev_row = jnp.where(t == 0, 0.0, prev_ref[HALO - 1:HALO, :])
    next_row = jnp.where(t == tiles_per_seq - 1, 0.0, next_ref[0:1, :])
    row = lax.broadcasted_iota(jnp.int32, x.shape, 0)
    xm1 = jnp.where(row == 0, prev_row, pltpu.roll(x, 1, 0))
    xp1 = jnp.where(row == R - 1, next_row, pltpu.roll(x, R - 1, 0))
    y = _silu(w_ref[0:1, :] * xm1 + w_ref[1:2, :] * x + w_ref[2:3, :] * xp1)
    for j in range(2 * GD_HEADS):
        cs = slice(j * GD_DK, (j + 1) * GD_DK)
        seg = y[:, cs]
        seg = seg * lax.rsqrt(jnp.sum(seg * seg, axis=-1, keepdims=True) + EPS)
        if j < GD_HEADS:
            seg = seg * (GD_DK ** -0.5)
        o_ref[:, cs] = seg
    o_ref[:, 2 * GD_NQ:] = y[:, 2 * GD_NQ:]


def _gdprep(z_gd, conv_w, L, rows):
    T = z_gd.shape[0]
    tps = L // rows
    hb = rows // HALO
    nhalo = T // HALO
    return pl.pallas_call(
        functools.partial(_gdprep_kernel, tiles_per_seq=tps),
        grid=(T // rows,),
        in_specs=[pl.BlockSpec((rows, GD_QKV), lambda i: (i, 0)),
                  pl.BlockSpec((HALO, GD_QKV), lambda i: (jnp.maximum(i * hb - 1, 0), 0)),
                  pl.BlockSpec((HALO, GD_QKV), lambda i: (jnp.minimum((i + 1) * hb, nhalo - 1), 0)),
                  pl.BlockSpec((3, GD_QKV), lambda i: (0, 0))],
        out_specs=pl.BlockSpec((rows, GD_QKV), lambda i: (i, 0)),
        out_shape=jax.ShapeDtypeStruct((T, GD_QKV), F32),
        compiler_params=_cparams("parallel"),
        name="gdn_prep",
    )(z_gd, z_gd, z_gd, conv_w)


def _gdn_kernel(xf_ref, abf_ref, xb_ref, abb_ref, par_ref, tri_ref, sl_ref, incl_ref, strict_ref, *rest, has_state):
    if has_state:
        s0_ref, of_ref, ob_ref, sfin_ref, st_ref = rest
    else:
        of_ref, ob_ref, sfin_ref, st_ref = rest
    C = SCAN_C
    c = pl.program_id(1)
    last_c = pl.num_programs(1) - 1
    nsq = int(math.log2(C))
    eye = (lax.broadcasted_iota(jnp.int32, (C, C), 0) == lax.broadcasted_iota(jnp.int32, (C, C), 1)).astype(F32)

    @pl.when(c == 0)
    def _():
        for d in range(2):
            for h in range(GD_HEADS):
                if has_state:
                    st_ref[d, h] = s0_ref[0, d, h].T
                else:
                    st_ref[d, h] = jnp.zeros((GD_DV, GD_DK), F32)

    for d, (x_ref, ab_ref, o_ref) in enumerate(((xf_ref, abf_ref, of_ref), (xb_ref, abb_ref, ob_ref))):
        ab = ab_ref[...]
        g_all = -jnp.exp(par_ref[0:1, :]) * _softplus(ab + par_ref[1:2, :])
        beta_all = jax.nn.sigmoid(ab)
        for h in range(GD_HEADS):
            j = d * GD_HEADS + h
            q = x_ref[:, h * GD_DK:(h + 1) * GD_DK]
            k = x_ref[:, GD_NQ + h * GD_DK:GD_NQ + (h + 1) * GD_DK]
            v = x_ref[:, 2 * GD_NQ + h * GD_DV:2 * GD_NQ + (h + 1) * GD_DV]
            g_b = jnp.broadcast_to(g_all[:, j:j + 1], (C, LANES))
            beta = jnp.broadcast_to(beta_all[:, 2 * GD_HEADS + j:2 * GD_HEADS + j + 1], (C, LANES))
            sums = _hdot(tri_ref[d], g_b)
            e_cum = jnp.exp(sums[:C])
            e_rest = jnp.exp(sums[C:])
            decay = jnp.exp(_hdot(incl_ref[d], g_b[:, :C] * sl_ref[d]))
            kb = k * beta
            a = strict_ref[d] * decay * lax.dot_general(kb, k, NT, precision=HI, preferred_element_type=F32)
            inv = eye - a
            pw = _hdot(a, a)
            for i in range(1, nsq):
                inv = inv + _hdot(inv, pw)
                if i < nsq - 1:
                    pw = _hdot(pw, pw)
            sol_v = _hdot(inv, v * beta)
            sol_k = _hdot(inv, kb * e_cum)
            st = st_ref[d, h]
            u = sol_v - _bdot_g(sol_k, st, NT)
            att = incl_ref[d] * decay * _bdot_g(q, k, NT)
            o_ref[:, h * GD_DV:(h + 1) * GD_DV] = _bdot_g(q * e_cum, st, NT) + _bdot(att, u)
            e_last = e_cum[C - 1:C] if d == 0 else e_cum[0:1]
            st_new = st * e_last + _bdot_g(u, k * e_rest, TN)
            st_ref[d, h] = st_new

            @pl.when(c == last_c)
            def _():
                sfin_ref[0, d, h] = st_new.T


def _gdn(qkv, z_gd, par, s0, B, L):
    n = L // SCAN_C
    tri, sl, incl, strict = (jnp.asarray(a) for a in _delta_consts())
    ab_col = (GD_QKV + GD_HEADS * GD_DV) // LANES
    xblk = (SCAN_C, GD_QKV)
    ablk = (SCAN_C, LANES)
    oblk = (SCAN_C, GD_HEADS * GD_DV)
    st_blk = (1, 2, GD_HEADS, GD_DK, GD_DV)
    fwd = lambda b, c: b * n + c
    bwd = lambda b, c: b * n + n - 1 - c
    const3 = lambda a: pl.BlockSpec(a.shape, lambda b, c: (0, 0, 0))
    in_specs = [pl.BlockSpec(xblk, lambda b, c: (fwd(b, c), 0)),
                pl.BlockSpec(ablk, lambda b, c: (fwd(b, c), ab_col)),
                pl.BlockSpec(xblk, lambda b, c: (bwd(b, c), 0)),
                pl.BlockSpec(ablk, lambda b, c: (bwd(b, c), ab_col)),
                pl.BlockSpec(par.shape, lambda b, c: (0, 0)),
                const3(tri), const3(sl), const3(incl), const3(strict)]
    args = [qkv, z_gd, qkv, z_gd, par, tri, sl, incl, strict]
    if s0 is not None:
        in_specs.append(pl.BlockSpec(st_blk, lambda b, c: (b, 0, 0, 0, 0)))
        args.append(s0)
    return pl.pallas_call(
        functools.partial(_gdn_kernel, has_state=s0 is not None),
        grid=(B, n),
        in_specs=in_specs,
        out_specs=[pl.BlockSpec(oblk, lambda b, c: (fwd(b, c), 0)),
                   pl.BlockSpec(oblk, lambda b, c: (bwd(b, c), 0)),
                   pl.BlockSpec(st_blk, lambda b, c: (b, 0, 0, 0, 0))],
        out_shape=[jax.ShapeDtypeStruct((B * L, GD_HEADS * GD_DV), F32),
                   jax.ShapeDtypeStruct((B * L, GD_HEADS * GD_DV), F32),
                   jax.ShapeDtypeStruct((B,) + st_blk[1:], F32)],
        scratch_shapes=[pltpu.VMEM((2, GD_HEADS, GD_DV, GD_DK), F32)],
        compiler_params=_cparams("parallel", "arbitrary"),
        name="gdn_scan",
    )(*args)


NA_W = NA_HEADS * NA_DH


def _softmax_pv(s, v):
    m = jnp.max(s, axis=-1, keepdims=True)
    e = jnp.exp(s - m)
    den = jnp.sum(e, axis=-1, keepdims=True)
    return jnp.dot(e.astype(BF16), v, preferred_element_type=F32) / den


def _ctx_attn_kernel(q_ref, k_ref, v_ref, o_ref):
    for h in range(NA_HEADS):
        cs = slice(h * NA_DH, (h + 1) * NA_DH)
        q = (q_ref[:, cs] * (NA_DH ** -0.5)).astype(BF16)
        s = lax.dot_general(q, k_ref[:, cs].astype(BF16), NT, preferred_element_type=F32)
        o_ref[:, cs] = _softmax_pv(s, v_ref[:, cs].astype(BF16)).astype(o_ref.dtype)


def _ctx_attn(z_na, B, L):
    blk = (L, NA_W)
    return pl.pallas_call(
        _ctx_attn_kernel,
        grid=(B,),
        in_specs=[pl.BlockSpec(blk, lambda b: (b, 0)),
                  pl.BlockSpec(blk, lambda b: (b, 1)),
                  pl.BlockSpec(blk, lambda b: (b, 2))],
        out_specs=pl.BlockSpec(blk, lambda b: (b, 0)),
        out_shape=jax.ShapeDtypeStruct((B * L, NA_W), BF16),
        compiler_params=_cparams("parallel"),
        name="ctx_attn",
    )(z_na, z_na, z_na)


def _na_kernel(q_ref, *rest, n_kblk, nkeys_nb):
    k_refs = rest[:n_kblk]
    v_refs = rest[n_kblk:2 * n_kblk]
    kc_ref, vc_ref, bias_ref, o_ref, kbuf, vbuf = rest[2 * n_kblk:]
    qb = q_ref.shape[0]
    for i in range(n_kblk):
        kbuf[i * qb:(i + 1) * qb, :] = k_refs[i][...].astype(BF16)
        vbuf[i * qb:(i + 1) * qb, :] = v_refs[i][...].astype(BF16)
    kbuf[nkeys_nb:, :] = kc_ref[0].astype(BF16)
    vbuf[nkeys_nb:, :] = vc_ref[0].astype(BF16)
    for h in range(NA_HEADS):
        cs = slice(h * NA_DH, (h + 1) * NA_DH)
        q = (q_ref[:, cs] * (NA_DH ** -0.5)).astype(BF16)
        s = lax.dot_general(q, kbuf[:, cs], NT, preferred_element_type=F32)
        s_nb = s[:, :nkeys_nb] + bias_ref[0, h]
        s_cx = s[:, nkeys_nb:]
        m = jnp.maximum(jnp.max(s_nb, axis=-1, keepdims=True), jnp.max(s_cx, axis=-1, keepdims=True))
        e_nb = jnp.exp(s_nb - m)
        e_cx = jnp.exp(s_cx - m)
        den = jnp.sum(e_nb, axis=-1, keepdims=True) + jnp.sum(e_cx, axis=-1, keepdims=True)
        pv = (jnp.dot(e_nb.astype(BF16), vbuf[:nkeys_nb, cs], preferred_element_type=F32)
              + jnp.dot(e_cx.astype(BF16), vbuf[nkeys_nb:, cs], preferred_element_type=F32))
        o_ref[:, cs] = (pv / den).astype(o_ref.dtype)


def _na_bias(rpb, rows):
    qr, qc = np.divmod(np.arange(NA_QROWS * GRID_W), GRID_W)
    kr, kc = np.divmod(np.arange(NA_KROWS * GRID_W), GRID_W)
    nblk = rows // NA_QROWS
    dr_l, dc_l, ok_l = [], [], []
    for m in (0, 1, nblk - 1):
        r = (NA_QROWS * m + qr)[:, None]
        start = np.clip(NA_QROWS * m - NA_KH // 2, 0, rows - NA_KROWS)
        kra = (start + kr)[None, :]
        r0 = np.clip(r - NA_KH // 2, 0, rows - NA_KH)
        row_ok = (kra >= r0) & (kra < r0 + NA_KH)
        col_start = np.clip(qc - NA_KW // 2, 0, GRID_W - NA_KW)[:, None]
        col_ok = (kc[None, :] >= col_start) & (kc[None, :] < col_start + NA_KW)
        dr_l.append(np.clip(kra - r + NA_KH - 1, 0, 2 * NA_KH - 2))
        dc_l.append(np.broadcast_to(np.clip(kc[None, :] - qc[:, None], -(NA_KW - 1), NA_KW - 1) + NA_KW - 1,
                                    row_ok.shape))
        ok_l.append(row_ok & col_ok)
    dr, dc, ok = np.stack(dr_l), np.stack(dc_l), np.stack(ok_l)
    bias = jnp.transpose(rpb[:, dr, dc], (1, 0, 2, 3))
    return jnp.where(ok[:, None], bias.astype(F32), -jnp.inf)


def _na_attn(z_na, k_ctx, v_ctx, bias, B, S):
    rows = S // GRID_W
    qb = NA_QROWS * GRID_W
    nblk = rows // NA_QROWS
    n_kblk = NA_KROWS // NA_QROWS
    lc = k_ctx.shape[1]
    nkeys_nb = NA_KROWS * GRID_W
    kstart = lambda m: jnp.clip(m - 1, 0, nblk - n_kblk)
    variant = lambda m: jnp.where(m == 0, 0, jnp.where(m == nblk - 1, 2, 1))
    kv_specs = lambda col: [pl.BlockSpec((qb, NA_W), functools.partial(
        lambda b, m, i, col: (b * nblk + kstart(m) + i, col), i=i, col=col)) for i in range(n_kblk)]
    return pl.pallas_call(
        functools.partial(_na_kernel, n_kblk=n_kblk, nkeys_nb=nkeys_nb),
        grid=(B, nblk),
        in_specs=([pl.BlockSpec((qb, NA_W), lambda b, m: (b * nblk + m, 0))] + kv_specs(1) + kv_specs(2)
                  + [pl.BlockSpec((1, lc, NA_W), lambda b, m: (b, 0, 0)),
                     pl.BlockSpec((1, lc, NA_W), lambda b, m: (b, 0, 0)),
                     pl.BlockSpec((1,) + bias.shape[1:], lambda b, m: (variant(m), 0, 0, 0))]),
        out_specs=pl.BlockSpec((qb, NA_W), lambda b, m: (b * nblk + m, 0)),
        out_shape=jax.ShapeDtypeStruct((B * S, NA_W), BF16),
        scratch_shapes=[pltpu.VMEM((nkeys_nb + lc, NA_W), BF16), pltpu.VMEM((nkeys_nb + lc, NA_W), BF16)],
        compiler_params=_cparams("parallel", "arbitrary"),
        name="na_attn",
    )(z_na, *([z_na] * (2 * n_kblk)), k_ctx, v_ctx, bias)


def _head_rms(o, g_row):
    parts = []
    for h in range(o.shape[1] // LANES):
        parts.append(_rms(o[:, h * LANES:(h + 1) * LANES]) * g_row)
    return jnp.concatenate(parts, axis=1)


def _merge_kernel(x_ref, oa_ref, hf_ref, hb_ref, hg_ref, gf_ref, gb_ref, gg_ref, od_ref, mg_ref,
                  hn_ref, gn_ref, wb_ref, wo_ref, g1_ref, o_ref):
    o_b = _head_rms(hf_ref[...] + hb_ref[...], hn_ref[...]) * _silu(hg_ref[...])
    o_c = _head_rms(gf_ref[...] + gb_ref[...], gn_ref[...]) * _silu(gg_ref[...])
    branches = (oa_ref[...], o_b.astype(BF16), o_c.astype(BF16), od_ref[...])
    merged = None
    for n_, o_n in enumerate(branches):
        gate = jax.nn.sigmoid(mg_ref[:, n_ * D_MODEL:(n_ + 1) * D_MODEL])
        term = gate * jnp.dot(o_n, wb_ref[n_], preferred_element_type=F32)
        merged = term if merged is None else merged + term
    mix = jnp.dot(merged.astype(BF16), wo_ref[...], preferred_element_type=F32)
    o_ref[...] = x_ref[...] + g1_ref[0] * mix


def _merge(x, o_a, o_hf, o_hb, z_hg, o_gf, o_gb, z_gd, o_d, z_mg, hg_onorm, gd_onorm, wb, wo, mods, mod_row, tm):
    T = x.shape[0]
    row = lambda w, col=0: pl.BlockSpec((tm, w), lambda i: (i, col))
    full = lambda a: pl.BlockSpec(a.shape, lambda i: (0,) * a.ndim)
    hn = hg_onorm.reshape(1, HG_DV)
    gn = gd_onorm.reshape(1, GD_DV)
    return pl.pallas_call(
        _merge_kernel,
        grid=(T // tm,),
        in_specs=[row(D_MODEL), row(BRANCH_W), row(BRANCH_W), row(BRANCH_W), row(BRANCH_W, 4),
                  row(BRANCH_W), row(BRANCH_W), row(BRANCH_W, GD_QKV // BRANCH_W), row(BRANCH_W),
                  row(N_BRANCH * D_MODEL), full(hn), full(gn), full(wb), full(wo),
                  pl.BlockSpec((1, 1, D_MODEL), lambda i: (mod_row(i * tm) * 6 + 2, 0, 0))],
        out_specs=row(D_MODEL),
        out_shape=jax.ShapeDtypeStruct((T, D_MODEL), F32),
        compiler_params=_cparams("parallel"),
        name="merge",
    )(x, o_a, o_hf, o_hb, z_hg, o_gf, o_gb, z_gd, o_d, z_mg, hn, gn, wb, wo, mods)


def _router_kernel(x_ref, g_ref, sc_ref, sh_ref, wr_ref, br_ref, h_ref, e_ref, w_ref):
    h = (_rms(x_ref[...]) * g_ref[...]) * (1.0 + sc_ref[0]) + sh_ref[0]
    h_ref[...] = h.astype(h_ref.dtype)
    logits = _hdot(h, wr_ref[...]) + br_ref[...]
    lane = lax.broadcasted_iota(jnp.int32, logits.shape, 1)
    e_out = jnp.zeros(logits.shape, jnp.int32)
    v_out = jnp.zeros(logits.shape, F32)
    top0 = None
    for k in range(TOP_K):
        m = jnp.max(logits, axis=-1, keepdims=True)
        idx = jnp.min(jnp.where(logits == m, lane, LANES), axis=-1, keepdims=True)
        if k == 0:
            top0 = m
        e_out = jnp.where(lane == k, idx, e_out)
        v_out = jnp.where(lane == k, jnp.exp(m - top0), v_out)
        logits = jnp.where(lane == idx, -jnp.inf, logits)
    e_ref[...] = e_out
    w_ref[...] = v_out / jnp.sum(v_out, axis=-1, keepdims=True)


def _router(x, g, mods, mod_row, w_router, b_router, tm):
    T = x.shape[0]
    wr = jnp.zeros((D_MODEL, LANES), F32).at[:, :N_EXP].set(w_router)
    br = jnp.full((1, LANES), -jnp.inf, F32).at[0, :N_EXP].set(b_router)
    row = lambda w: pl.BlockSpec((tm, w), lambda i: (i, 0))
    return pl.pallas_call(
        _router_kernel,
        grid=(T // tm,),
        in_specs=[row(D_MODEL),
                  pl.BlockSpec((1, D_MODEL), lambda i: (0, 0)),
                  pl.BlockSpec((1, 1, D_MODEL), lambda i: (mod_row(i * tm) * 6 + 4, 0, 0)),
                  pl.BlockSpec((1, 1, D_MODEL), lambda i: (mod_row(i * tm) * 6 + 3, 0, 0)),
                  pl.BlockSpec((D_MODEL, LANES), lambda i: (0, 0)),
                  pl.BlockSpec((1, LANES), lambda i: (0, 0))],
        out_specs=[row(D_MODEL), row(LANES), row(LANES)],
        out_shape=[jax.ShapeDtypeStruct((T, D_MODEL), BF16),
                   jax.ShapeDtypeStruct((T, LANES), jnp.int32),
                   jax.ShapeDtypeStruct((T, LANES), F32)],
        compiler_params=_cparams("parallel"),
        name="router",
    )(x, g.reshape(1, D_MODEL), mods, mods, wr, br)


def _expert_kernel(blk_e_ref, x_ref, wgu_ref, bgu_ref, wdn_ref, bdn_ref, o_ref):
    gu = jnp.dot(x_ref[...], wgu_ref[0], preferred_element_type=F32) + bgu_ref[0]
    a = jnp.minimum(gu[:, :D_FF], SWIGLU_LIMIT)
    lin = jnp.clip(gu[:, D_FF:], -SWIGLU_LIMIT, SWIGLU_LIMIT)
    y = a * jax.nn.sigmoid(SWIGLU_ALPHA * a) * (lin + 1.0)
    o_ref[...] = jnp.dot(y.astype(BF16), wdn_ref[0], preferred_element_type=F32) + bdn_ref[0]


def _experts(xb, blk_e, w_gu, b_gu, w_dn, b_dn):
    n_pad = xb.shape[0]
    n_blocks = n_pad // MOE_BLOCK
    grid_spec = pltpu.PrefetchScalarGridSpec(
        num_scalar_prefetch=1,
        grid=(n_blocks,),
        in_specs=[pl.BlockSpec((MOE_BLOCK, D_MODEL), lambda i, e: (i, 0)),
                  pl.BlockSpec((1, D_MODEL, 2 * D_FF), lambda i, e: (e[i], 0, 0)),
                  pl.BlockSpec((1, 1, 2 * D_FF), lambda i, e: (e[i], 0, 0)),
                  pl.BlockSpec((1, D_FF, D_MODEL), lambda i, e: (e[i], 0, 0)),
                  pl.BlockSpec((1, 1, D_MODEL), lambda i, e: (e[i], 0, 0))],
        out_specs=pl.BlockSpec((MOE_BLOCK, D_MODEL), lambda i, e: (i, 0)),
    )
    return pl.pallas_call(
        _expert_kernel,
        grid_spec=grid_spec,
        out_shape=jax.ShapeDtypeStruct((n_pad, D_MODEL), F32),
        compiler_params=_cparams("arbitrary"),
        name="experts",
    )(blk_e, xb, w_gu, b_gu.reshape(N_EXP, 1, 2 * D_FF), w_dn, b_dn.reshape(N_EXP, 1, D_MODEL))


def _combine_kernel(x_ref, y_ref, w_ref, g2_ref, nf_ref, o_ref, *, final_norm):
    acc = None
    for k in range(TOP_K):
        term = y_ref[:, k * D_MODEL:(k + 1) * D_MODEL] * w_ref[:, k:k + 1]
        acc = term if acc is None else acc + term
    x = x_ref[...] + g2_ref[0] * acc
    if final_norm:
        x = _rms(x) * nf_ref[...]
    o_ref[...] = x


def _combine(x, yg, wts, mods, mod_row, norm_f, final_norm, tm):
    T = x.shape[0]
    row = lambda w: pl.BlockSpec((tm, w), lambda i: (i, 0))
    return pl.pallas_call(
        functools.partial(_combine_kernel, final_norm=final_norm),
        grid=(T // tm,),
        in_specs=[row(D_MODEL), row(TOP_K * D_MODEL), row(LANES),
                  pl.BlockSpec((1, 1, D_MODEL), lambda i: (mod_row(i * tm) * 6 + 5, 0, 0)),
                  pl.BlockSpec((1, D_MODEL), lambda i: (0, 0))],
        out_specs=row(D_MODEL),
        out_shape=jax.ShapeDtypeStruct((T, D_MODEL), F32),
        compiler_params=_cparams("parallel"),
        name="combine",
    )(x, yg, wts, mods, norm_f.reshape(1, D_MODEL))


def _route(top_e, T):
    n_assign = T * TOP_K
    n_blocks = n_assign // MOE_BLOCK + N_EXP
    e_flat = top_e.reshape(n_assign)
    order = jnp.argsort(e_flat)
    e_sorted = e_flat[order]
    counts = jnp.zeros((N_EXP,), jnp.int32).at[e_flat].add(1)
    start = jnp.cumsum(counts) - counts
    padded = (counts + MOE_BLOCK - 1) // MOE_BLOCK * MOE_BLOCK
    pad_end = jnp.cumsum(padded)
    pad_start = pad_end - padded
    pos_sorted = pad_start[e_sorted] + jnp.arange(n_assign, dtype=jnp.int32) - start[e_sorted]
    tok = jnp.zeros((n_blocks * MOE_BLOCK,), jnp.int32).at[pos_sorted].set((order // TOP_K).astype(jnp.int32))
    pos = jnp.zeros((n_assign,), jnp.int32).at[order].set(pos_sorted)
    blk_e = jnp.minimum(jnp.searchsorted(pad_end, jnp.arange(n_blocks, dtype=jnp.int32) * MOE_BLOCK, side='right'),
                        N_EXP - 1).astype(jnp.int32)
    return tok, pos, blk_e


def _prep_layer(l, w_in, sgu_w, w_branch, w_out, w_gu, w_dn, gd_A_log, gd_dt_bias, lb):
    offs = np.cumsum([0, BRANCH_W, BRANCH_W, 512, 512, 512, 512, 512, GD_QKV, 8, 8, 512, 3 * NA_W, N_BRANCH * D_MODEL])
    w = w_in[l]
    seg = lambda i, j: w[:, offs[i]:offs[j]]
    w_gd = jnp.concatenate([seg(7, 8), seg(10, 11), seg(8, 10),
                            jnp.zeros((D_MODEL, LANES - 4 * GD_HEADS), F32)], axis=1)
    par = jnp.zeros((2, LANES), F32)
    par = par.at[0, :2 * GD_HEADS].set(gd_A_log[l].reshape(-1)).at[1, :2 * GD_HEADS].set(gd_dt_bias[l].reshape(-1))
    lb_l = lb[:, l]
    return {
        'w_sgu': seg(0, 2).astype(BF16), 'w_hg': seg(2, 7).astype(BF16), 'w_gd': w_gd.astype(BF16),
        'w_na': seg(11, 12).astype(BF16), 'w_mg': seg(12, 13).astype(BF16),
        'sgu_w': sgu_w[l], 'wb': w_branch[l].astype(BF16), 'wo': w_out[l].astype(BF16),
        'w_gu': w_gu[l].astype(BF16), 'w_dn': w_dn[l].astype(BF16), 'gd_par': par,
        'lbp': jnp.concatenate([jnp.log(lb_l), jnp.log1p(-lb_l), 1.0 - lb_l], axis=0),
    }


def _layer(x, B, L, mods, mod_row, tm, lw, p, ctx, norm_f, final_norm):
    T = B * L
    h = _normmod(x, p['norm1'], mods, mod_row, tm, part_shift=0, part_scale=1)
    z_sgu = _matmul(h, lw['w_sgu'], 1024, 1024)
    z_hg = _matmul(h, lw['w_hg'], 1024, 1280)
    z_gd = _matmul(h, lw['w_gd'], 512, 2176)
    z_na = _matmul(h, lw['w_na'], 1024, 768)
    z_mg = _matmul(h, lw['w_mg'], 1024, 1024)

    o_a = _sgu(z_sgu, p['sgu_norm'], lw['sgu_w'], p['sgu_b'], 256)
    s_hg0 = None if ctx is None else ctx[2]
    s_gd0 = None if ctx is None else ctx[3]
    o_hf, o_hb, s_hg = _hgrn(z_hg, lw['lbp'], s_hg0, B, L)
    qkv = _gdprep(z_gd, p['gd_conv'], L, 256)
    o_gf, o_gb, s_gd = _gdn(qkv, z_gd, lw['gd_par'], s_gd0, B, L)
    if ctx is None:
        o_d = _ctx_attn(z_na, B, L)
    else:
        o_d = _na_attn(z_na, ctx[0], ctx[1], _na_bias(p['na_rpb'], L // GRID_W), B, L)
    x = _merge(x, o_a, o_hf, o_hb, z_hg, o_gf, o_gb, z_gd, o_d, z_mg, p['hg_onorm'], p['gd_onorm'],
               lw['wb'], lw['wo'], mods, mod_row, 256)

    h2, top_e, wts = _router(x, p['norm2'], mods, mod_row, p['w_router'], p['b_router'], 512)
    tok, pos, blk_e = _route(top_e[:, :TOP_K], T)
    xb = jnp.take(h2, tok, axis=0)
    yb = _experts(xb, blk_e, lw['w_gu'], p['b_gu'], lw['w_dn'], p['b_dn'])
    yg = jnp.take(yb, pos, axis=0).reshape(T, TOP_K * D_MODEL)
    x = _combine(x, yg, wts, mods, mod_row, norm_f, final_norm, 256)
    side = (z_na[:, NA_W:2 * NA_W].reshape(B, L, NA_HEADS, NA_DH),
            z_na[:, 2 * NA_W:].reshape(B, L, NA_HEADS, NA_DH), s_hg, s_gd)
    return x, side


def kernel(x_prompt, x_sample, c, cache_na_k, cache_na_v, state_hgrn, state_gdn, c_ctx, w_ada, b_ada, norm1, norm2, norm_f, w_in, sgu_norm, sgu_w, sgu_b, hg_lb, hg_onorm, gd_conv, gd_A_log, gd_dt_bias, gd_onorm, na_rpb, w_branch, w_out, w_router, b_router, w_gu, b_gu, w_dn, b_dn):
    Bp, Lp, D = x_prompt.shape
    Bs, Ls, _ = x_sample.shape
    ctx_row = Bs
    cvecs = jnp.zeros((MOD_ROWS, D), F32).at[:Bs].set(c).at[ctx_row].set(c_ctx)
    mods = _modulation(cvecs, w_ada, b_ada)

    cs = jnp.cumsum(jax.nn.softmax(hg_lb.astype(F32), axis=1), axis=1)
    lb = cs - cs[:, :1]

    tm = 1024
    xp = x_prompt.reshape(Bp * Lp, D)
    xs = x_sample.reshape(Bs * Ls, D)
    sides = []
    for l in range(DEPTH):
        lw = _prep_layer(l, w_in, sgu_w, w_branch, w_out, w_gu, w_dn, gd_A_log, gd_dt_bias, lb)
        p = {'norm1': norm1[l], 'norm2': norm2[l], 'sgu_norm': sgu_norm[l], 'sgu_b': sgu_b[l],
             'gd_conv': gd_conv[l], 'hg_onorm': hg_onorm[l], 'gd_onorm': gd_onorm[l], 'na_rpb': na_rpb[l],
             'w_router': w_router[l], 'b_router': b_router[l], 'b_gu': b_gu[l], 'b_dn': b_dn[l]}
        final = l == DEPTH - 1
        xp, side = _layer(xp, Bp, Lp, mods[l], lambda r: ctx_row, tm, lw, p, None, norm_f, final)
        sides.append(side)
        ctx = (cache_na_k[:, l].reshape(Bs, -1, NA_W), cache_na_v[:, l].reshape(Bs, -1, NA_W),
               state_hgrn[:, l], state_gdn[:, l])
        xs, _ = _layer(xs, Bs, Ls, mods[l], lambda r: r // Ls, tm, lw, p, ctx, norm_f, final)

    return (xp.reshape(Bp, Lp, D), xs.reshape(Bs, Ls, D),
            jnp.stack([s[0] for s in sides], axis=1), jnp.stack([s[1] for s in sides], axis=1),
            jnp.stack([s[2] for s in sides], axis=1), jnp.stack([s[3] for s in sides], axis=1))
```

```python
import functools
import math

import numpy as np
import jax
import jax.numpy as jnp
from jax import lax
from jax.experimental import pallas as pl
from jax.experimental.pallas import tpu as pltpu

D_MODEL = 1024
DEPTH = 2
GRID_W = 64
BRANCH_W = 512
N_BRANCH = 4
SGU_CHUNK = 128
SGU_GROUPS = 4
HG_HEADS = 4
HG_DK = 128
HG_DV = 128
GD_HEADS = 4
GD_DK = 128
GD_DV = 128
NA_HEADS = 8
NA_DH = 64
NA_KH = 8
NA_KW = 16
N_EXP = 32
TOP_K = 4
D_FF = 1024
SWIGLU_LIMIT = 7.0
SWIGLU_ALPHA = 1.702
EPS = 1e-6

F32 = jnp.float32
BF16 = jnp.bfloat16
HI = lax.Precision.HIGHEST

LANES = 128
MOD_ROWS = 16
SCAN_C = 64
MOE_BLOCK = 128
NA_QROWS = 4
NA_KROWS = NA_QROWS + NA_KH
VMEM_LIMIT = 48 * 1024 * 1024

NT = (((1,), (1,)), ((), ()))
TN = (((0,), (0,)), ((), ()))


def _cparams(*sem):
    return pltpu.CompilerParams(dimension_semantics=sem, vmem_limit_bytes=VMEM_LIMIT)


def _bdot(a, b):
    return jnp.dot(a.astype(BF16), b.astype(BF16), preferred_element_type=F32)


def _bdot_g(a, b, dims):
    return lax.dot_general(a.astype(BF16), b.astype(BF16), dims, preferred_element_type=F32)


def _hdot(a, b):
    return jnp.dot(a, b, precision=HI, preferred_element_type=F32)


def _dot01(m, x):
    hi = x.astype(BF16)
    r1 = x - hi.astype(F32)
    mid = r1.astype(BF16)
    lo = (r1 - mid.astype(F32)).astype(BF16)
    d = lambda t: jnp.dot(m, t, preferred_element_type=F32)
    return (d(lo) + d(mid)) + d(hi)


def _silu(x):
    return x * jax.nn.sigmoid(x)


def _log_sigmoid(x):
    return jnp.minimum(x, 0.0) - jnp.log1p(jnp.exp(-jnp.abs(x)))


def _logaddexp(a, b):
    return jnp.maximum(a, b) + jnp.log1p(jnp.exp(-jnp.abs(a - b)))


def _softplus(x):
    return jnp.maximum(x, 0.0) + jnp.log1p(jnp.exp(-jnp.abs(x)))


def _ada_kernel(c_ref, w_ref, b_ref, o_ref):
    o_ref[0] = _hdot(_silu(c_ref[...]), w_ref[0]) + b_ref[0]


def _modulation(cvecs, w_ada, b_ada):
    tn = 1536
    out = pl.pallas_call(
        _ada_kernel,
        grid=(DEPTH, 6 * D_MODEL // tn),
        in_specs=[pl.BlockSpec((MOD_ROWS, D_MODEL), lambda l, j: (0, 0)),
                  pl.BlockSpec((1, D_MODEL, tn), lambda l, j: (l, 0, j)),
                  pl.BlockSpec((1, 1, tn), lambda l, j: (l, 0, j))],
        out_specs=pl.BlockSpec((1, MOD_ROWS, tn), lambda l, j: (l, 0, j)),
        out_shape=jax.ShapeDtypeStruct((DEPTH, MOD_ROWS, 6 * D_MODEL), F32),
        compiler_params=_cparams("arbitrary", "arbitrary"),
        name="ada_modulation",
    )(cvecs, w_ada, b_ada.reshape(DEPTH, 1, 6 * D_MODEL))
    return out.reshape(DEPTH, MOD_ROWS * 6, 1, D_MODEL)


def _rms(x):
    return x * lax.rsqrt(jnp.mean(x * x, axis=-1, keepdims=True) + EPS)


def _normmod_kernel(x_ref, g_ref, sc_ref, sh_ref, o_ref):
    h = (_rms(x_ref[...]) * g_ref[...]) * (1.0 + sc_ref[0]) + sh_ref[0]
    o_ref[...] = h.astype(o_ref.dtype)


def _normmod(x, g, mods, mod_row, tm, part_shift, part_scale):
    T = x.shape[0]
    return pl.pallas_call(
        _normmod_kernel,
        grid=(T // tm,),
        in_specs=[pl.BlockSpec((tm, D_MODEL), lambda i: (i, 0)),
                  pl.BlockSpec((1, D_MODEL), lambda i: (0, 0)),
                  pl.BlockSpec((1, 1, D_MODEL), lambda i: (mod_row(i * tm) * 6 + part_scale, 0, 0)),
                  pl.BlockSpec((1, 1, D_MODEL), lambda i: (mod_row(i * tm) * 6 + part_shift, 0, 0))],
        out_specs=pl.BlockSpec((tm, D_MODEL), lambda i: (i, 0)),
        out_shape=jax.ShapeDtypeStruct((T, D_MODEL), BF16),
        compiler_params=_cparams("parallel"),
        name="normmod",
    )(x, g.reshape(1, D_MODEL), mods, mods)


def _mm_kernel(a_ref, w_ref, o_ref):
    o_ref[...] = jnp.dot(a_ref[...], w_ref[...], preferred_element_type=F32).astype(o_ref.dtype)


def _matmul(a, w, tm, tn, out_dtype=F32):
    T, K = a.shape
    N = w.shape[1]
    return pl.pallas_call(
        _mm_kernel,
        grid=(N // tn, T // tm),
        in_specs=[pl.BlockSpec((tm, K), lambda j, i: (i, 0)),
                  pl.BlockSpec((K, tn), lambda j, i: (0, j))],
        out_specs=pl.BlockSpec((tm, tn), lambda j, i: (i, j)),
        out_shape=jax.ShapeDtypeStruct((T, N), out_dtype),
        compiler_params=_cparams("parallel", "parallel"),
        name="in_proj",
    )(a, w)


def _sgu_kernel(u_ref, v_ref, gn_ref, ws_ref, bs_ref, o_ref):
    rows = u_ref.shape[0]
    gw = BRANCH_W // SGU_GROUPS
    u = jax.nn.gelu(u_ref[...])
    v = (_rms(jax.nn.gelu(v_ref[...])) * gn_ref[...]).astype(BF16)
    for n in range(rows // SGU_CHUNK):
        r = slice(n * SGU_CHUNK, (n + 1) * SGU_CHUNK)
        for g in range(SGU_GROUPS):
            cs = slice(g * gw, (g + 1) * gw)
            s = jnp.dot(ws_ref[g], v[r, cs], preferred_element_type=F32) + bs_ref[:, cs]
            o_ref[r, cs] = (u[r, cs] * s).astype(o_ref.dtype)


def _sgu(z_sgu, g_norm, w_s, b_s, rows):
    T = z_sgu.shape[0]
    gw = BRANCH_W // SGU_GROUPS
    b_exp = jnp.repeat(b_s.T, gw, axis=1)
    return pl.pallas_call(
        _sgu_kernel,
        grid=(T // rows,),
        in_specs=[pl.BlockSpec((rows, BRANCH_W), lambda i: (i, 0)),
                  pl.BlockSpec((rows, BRANCH_W), lambda i: (i, 1)),
                  pl.BlockSpec((1, BRANCH_W), lambda i: (0, 0)),
                  pl.BlockSpec((SGU_GROUPS, SGU_CHUNK, SGU_CHUNK), lambda i: (0, 0, 0)),
                  pl.BlockSpec((SGU_CHUNK, BRANCH_W), lambda i: (0, 0))],
        out_specs=pl.BlockSpec((rows, BRANCH_W), lambda i: (i, 0)),
        out_shape=jax.ShapeDtypeStruct((T, BRANCH_W), BF16),
        compiler_params=_cparams("parallel"),
        name="sgu",
    )(z_sgu, z_sgu, g_norm.reshape(1, BRANCH_W), w_s.astype(BF16), b_exp)


def _order(reverse):
    p = np.arange(SCAN_C)
    return SCAN_C - 1 - p if reverse else p


def _gla_consts():
    C = SCAN_C
    nlev = int(math.log2(C))
    mats, masks = [], []
    for reverse in (False, True):
        p = _order(reverse)
        pt, pr = p[:, None], p[None, :]
        m_d, k_d = [], []
        for lev in range(nlev):
            w = C >> (lev + 1)
            parent = p // (2 * w)
            later = (p % (2 * w)) >= w
            anchor = (parent * 2 * w + w - 1)[:, None]
            m = np.where(later[:, None], (pr > anchor) & (pr <= pt), (pr > pt) & (pr <= anchor))
            m_d.append(m)
            k_d.append((parent[:, None] == parent[None, :]) & later[:, None] & ~later[None, :])
        m_d.append(pr <= pt)
        m_d.append(pr > pt)
        k_d.append(np.eye(C, dtype=bool))
        mats.append(np.concatenate(m_d, axis=0))
        masks.append(np.stack(k_d))
    return (np.stack(mats).astype(np.float32), np.stack(masks).astype(np.float32))


def _delta_consts():
    C = SCAN_C
    tri, sl, incl, strict = [], [], [], []
    for reverse in (False, True):
        p = _order(reverse)
        pt, pr = p[:, None], p[None, :]
        tri.append(np.concatenate([pr <= pt, pr > pt], axis=0))
        sl.append(np.concatenate([pt > pr, np.zeros((C, LANES - C), bool), np.ones((C, LANES), bool)], axis=1))
        incl.append(pr <= pt)
        strict.append(pr < pt)
    f = lambda a: np.stack(a).astype(np.float32)
    return f(tri), f(sl), f(incl), f(strict)


def _hgrn_kernel(qf_ref, ff_ref, vf_ref, qb_ref, fb_ref, vb_ref, lb_ref, mat_ref, msk_ref, *rest, has_state):
    if has_state:
        s0_ref, of_ref, ob_ref, sfin_ref, st_ref = rest
    else:
        of_ref, ob_ref, sfin_ref, st_ref = rest
    C = SCAN_C
    nlev = msk_ref.shape[1] - 1
    c = pl.program_id(1)
    last_c = pl.num_programs(1) - 1

    @pl.when(c == 0)
    def _():
        for d in range(2):
            for h in range(HG_HEADS):
                if has_state:
                    st_ref[d, h] = s0_ref[0, d, h].T
                else:
                    st_ref[d, h] = jnp.zeros((HG_DV, HG_DK), F32)

    q_refs, f_refs, v_refs, o_refs = (qf_ref, qb_ref), (ff_ref, fb_ref), (vf_ref, vb_ref), (of_ref, ob_ref)
    chains = [(d, h) for d in range(2) for h in range(HG_HEADS)]
    col = lambda h: slice(h * HG_DK, (h + 1) * HG_DK)
    q, k, fac, att = {}, {}, {}, {}
    for d, h in chains:
        zf = f_refs[d][:, col(h)]
        loglb = lb_ref[d:d + 1, col(h)]
        log1m = lb_ref[2 + d:3 + d, col(h)]
        onem = lb_ref[4 + d:5 + d, col(h)]
        logf = _logaddexp(loglb, log1m + _log_sigmoid(zf))
        k[d, h] = onem * jax.nn.sigmoid(-zf)
        q[d, h] = _silu(q_refs[d][:, col(h)]) * (HG_DK ** -0.5)
        fac[d, h] = jnp.exp(_dot01(mat_ref[d], logf))
    for d, h in chains:
        acc = msk_ref[d, nlev] * _bdot_g(q[d, h], k[d, h], NT)
        for i in range(nlev):
            fi = fac[d, h][i * C:(i + 1) * C]
            acc = acc + msk_ref[d, i] * _bdot_g(q[d, h] * fi, k[d, h] * fi, NT)
        att[d, h] = acc
    for d, h in chains:
        eb = fac[d, h][nlev * C:(nlev + 1) * C]
        o_refs[d][:, col(h)] = (_bdot(att[d, h], v_refs[d][:, col(h)])
                                + _bdot_g(q[d, h] * eb, st_ref[d, h], NT))
    for d, h in chains:
        eb = fac[d, h][nlev * C:(nlev + 1) * C]
        er = fac[d, h][(nlev + 1) * C:]
        e_last = eb[C - 1:C] if d == 0 else eb[0:1]
        st_ref[d, h] = st_ref[d, h] * e_last + _bdot_g(v_refs[d][:, col(h)], k[d, h] * er, TN)

    @pl.when(c == last_c)
    def _():
        for d in range(2):
            for h in range(HG_HEADS):
                sfin_ref[0, d, h] = st_ref[d, h].T


def _hgrn(z_hg, lbp, s0, B, L):
    n = L // SCAN_C
    mats, masks = _gla_consts()
    blk = (SCAN_C, HG_HEADS * HG_DK)
    fwd = lambda col: pl.BlockSpec(blk, lambda b, c: (b * n + c, col))
    bwd = lambda col: pl.BlockSpec(blk, lambda b, c: (b * n + n - 1 - c, col))
    st_blk = (1, 2, HG_HEADS, HG_DK, HG_DV)
    in_specs = [fwd(0), fwd(1), fwd(3), bwd(0), bwd(2), bwd(3),
                pl.BlockSpec(lbp.shape, lambda b, c: (0, 0)),
                pl.BlockSpec(mats.shape, lambda b, c: (0, 0, 0)),
                pl.BlockSpec(masks.shape, lambda b, c: (0, 0, 0, 0))]
    args = [z_hg] * 6 + [lbp, jnp.asarray(mats, BF16), jnp.asarray(masks)]
    if s0 is not None:
        in_specs.append(pl.BlockSpec(st_blk, lambda b, c: (b, 0, 0, 0, 0)))
        args.append(s0)
    return pl.pallas_call(
        functools.partial(_hgrn_kernel, has_state=s0 is not None),
        grid=(B, n),
        in_specs=in_specs,
        out_specs=[pl.BlockSpec(blk, lambda b, c: (b * n + c, 0)),
                   pl.BlockSpec(blk, lambda b, c: (b * n + n - 1 - c, 0)),
                   pl.BlockSpec(st_blk, lambda b, c: (b, 0, 0, 0, 0))],
        out_shape=[jax.ShapeDtypeStruct((B * L, HG_HEADS * HG_DV), F32),
                   jax.ShapeDtypeStruct((B * L, HG_HEADS * HG_DV), F32),
                   jax.ShapeDtypeStruct((B,) + st_blk[1:], F32)],
        scratch_shapes=[pltpu.VMEM((2, HG_HEADS, HG_DV, HG_DK), F32)],
        compiler_params=_cparams("parallel", "arbitrary"),
        name="hgrn_scan",
    )(*args)


GD_NQ = GD_HEADS * GD_DK
GD_QKV = 2 * GD_NQ + GD_HEADS * GD_DV
HALO = 8


def _gdprep_kernel(x_ref, prev_ref, next_ref, w_ref, o_ref, *, tiles_per_seq):
    R = x_ref.shape[0]
    t = pl.program_id(0) % tiles_per_seq
    x = x_ref[...]
    prev_row = jnp.where(t == 0, 0.0, prev_ref[HALO - 1:HALO, :])
    next_row = jnp.where(t == tiles_per_seq - 1, 0.0, next_ref[0:1, :])
    row = lax.broadcasted_iota(jnp.int32, x.shape, 0)
    xm1 = jnp.where(row == 0, prev_row, pltpu.roll(x, 1, 0))
    xp1 = jnp.where(row == R - 1, next_row, pltpu.roll(x, R - 1, 0))
    y = _silu(w_ref[0:1, :] * xm1 + w_ref[1:2, :] * x + w_ref[2:3, :] * xp1)
    for j in range(2 * GD_HEADS):
        cs = slice(j * GD_DK, (j + 1) * GD_DK)
        seg = y[:, cs]
        seg = seg * lax.rsqrt(jnp.sum(seg * seg, axis=-1, keepdims=True) + EPS)
        if j < GD_HEADS:
            seg = seg * (GD_DK ** -0.5)
        o_ref[:, cs] = seg
    o_ref[:, 2 * GD_NQ:] = y[:, 2 * GD_NQ:]


def _gdprep(z_gd, conv_w, L, rows):
    T = z_gd.shape[0]
    tps = L // rows
    hb = rows // HALO
    nhalo = T // HALO
    return pl.pallas_call(
        functools.partial(_gdprep_kernel, tiles_per_seq=tps),
        grid=(T // rows,),
        in_specs=[pl.BlockSpec((rows, GD_QKV), lambda i: (i, 0)),
                  pl.BlockSpec((HALO, GD_QKV), lambda i: (jnp.maximum(i * hb - 1, 0), 0)),
                  pl.BlockSpec((HALO, GD_QKV), lambda i: (jnp.minimum((i + 1) * hb, nhalo - 1), 0)),
                  pl.BlockSpec((3, GD_QKV), lambda i: (0, 0))],
        out_specs=pl.BlockSpec((rows, GD_QKV), lambda i: (i, 0)),
        out_shape=jax.ShapeDtypeStruct((T, GD_QKV), F32),
        compiler_params=_cparams("parallel"),
        name="gdn_prep",
    )(z_gd, z_gd, z_gd, conv_w)


def _gdn_kernel(xf_ref, abf_ref, xb_ref, abb_ref, par_ref, tri_ref, sl_ref, incl_ref, strict_ref, *rest, has_state):
    if has_state:
        s0_ref, of_ref, ob_ref, sfin_ref, st_ref = rest
    else:
        of_ref, ob_ref, sfin_ref, st_ref = rest
    C = SCAN_C
    c = pl.program_id(1)
    last_c = pl.num_programs(1) - 1
    nsq = int(math.log2(C))

    @pl.when(c == 0)
    def _():
        for d in range(2):
            for h in range(GD_HEADS):
                if has_state:
                    st_ref[d, h] = s0_ref[0, d, h].T
                else:
                    st_ref[d, h] = jnp.zeros((GD_DV, GD_DK), F32)

    x_refs, ab_refs, o_refs = (xf_ref, xb_ref), (abf_ref, abb_ref), (of_ref, ob_ref)
    chains = [(d, h) for d in range(2) for h in range(GD_HEADS)]
    q_of = lambda d, h: x_refs[d][:, h * GD_DK:(h + 1) * GD_DK]
    k_of = lambda d, h: x_refs[d][:, GD_NQ + h * GD_DK:GD_NQ + (h + 1) * GD_DK]
    v_of = lambda d, h: x_refs[d][:, 2 * GD_NQ + h * GD_DV:2 * GD_NQ + (h + 1) * GD_DV]
    g_all, beta_all = [], []
    for d in range(2):
        ab = ab_refs[d][...]
        g_all.append(-jnp.exp(par_ref[0:1, :]) * _softplus(ab + par_ref[1:2, :]))
        beta_all.append(jax.nn.sigmoid(ab))
    decay, e_cum, e_rest, kb, a = {}, {}, {}, {}, {}
    for d, h in chains:
        j = d * GD_HEADS + h
        g_b = jnp.broadcast_to(g_all[d][:, j:j + 1], (C, 2 * LANES))
        sums = _dot01(tri_ref[d], g_b * sl_ref[d])
        decay[d, h] = jnp.exp(sums[:C, :C])
        e_cum[d, h] = jnp.exp(sums[:C, LANES:])
        e_rest[d, h] = jnp.exp(sums[C:, LANES:])
    for d, h in chains:
        j = d * GD_HEADS + h
        beta = jnp.broadcast_to(beta_all[d][:, 2 * GD_HEADS + j:2 * GD_HEADS + j + 1], (C, LANES))
        k = k_of(d, h)
        kb[d, h] = k * beta
        a[d, h] = strict_ref[d] * decay[d, h] * _bdot_g(kb[d, h], k, NT)
    inv_m = {ch: -a[ch] for ch in chains}
    pw = {ch: _bdot(a[ch], a[ch]) for ch in chains}
    for i in range(1, nsq):
        inv_m = {ch: inv_m[ch] + pw[ch] + _bdot(inv_m[ch], pw[ch]) for ch in chains}
        if i < nsq - 1:
            pw = {ch: _bdot(pw[ch], pw[ch]) for ch in chains}
    sol = {}
    for d, h in chains:
        j = d * GD_HEADS + h
        beta = jnp.broadcast_to(beta_all[d][:, 2 * GD_HEADS + j:2 * GD_HEADS + j + 1], (C, LANES))
        rhs = jnp.concatenate([v_of(d, h) * beta, kb[d, h] * e_cum[d, h]], axis=1)
        sol[d, h] = rhs + _bdot(inv_m[d, h], rhs)
    u = {ch: sol[ch][:, :GD_DV] - _bdot_g(sol[ch][:, GD_DV:], st_ref[ch[0], ch[1]], NT) for ch in chains}
    att = {(d, h): incl_ref[d] * decay[d, h] * _bdot_g(q_of(d, h), k_of(d, h), NT) for d, h in chains}
    for d, h in chains:
        o_refs[d][:, h * GD_DV:(h + 1) * GD_DV] = (_bdot_g(q_of(d, h) * e_cum[d, h], st_ref[d, h], NT)
                                                   + _bdot(att[d, h], u[d, h]))
    for d, h in chains:
        e_last = e_cum[d, h][C - 1:C] if d == 0 else e_cum[d, h][0:1]
        st_ref[d, h] = st_ref[d, h] * e_last + _bdot_g(u[d, h], k_of(d, h) * e_rest[d, h], TN)

    @pl.when(c == last_c)
    def _():
        for d in range(2):
            for h in range(GD_HEADS):
                sfin_ref[0, d, h] = st_ref[d, h].T


def _gdn(qkv, z_gd, par, s0, B, L):
    n = L // SCAN_C
    tri, sl, incl, strict = (jnp.asarray(a) for a in _delta_consts())
    tri = tri.astype(BF16)
    ab_col = (GD_QKV + GD_HEADS * GD_DV) // LANES
    xblk = (SCAN_C, GD_QKV)
    ablk = (SCAN_C, LANES)
    oblk = (SCAN_C, GD_HEADS * GD_DV)
    st_blk = (1, 2, GD_HEADS, GD_DK, GD_DV)
    fwd = lambda b, c: b * n + c
    bwd = lambda b, c: b * n + n - 1 - c
    const3 = lambda a: pl.BlockSpec(a.shape, lambda b, c: (0, 0, 0))
    in_specs = [pl.BlockSpec(xblk, lambda b, c: (fwd(b, c), 0)),
                pl.BlockSpec(ablk, lambda b, c: (fwd(b, c), ab_col)),
                pl.BlockSpec(xblk, lambda b, c: (bwd(b, c), 0)),
                pl.BlockSpec(ablk, lambda b, c: (bwd(b, c), ab_col)),
                pl.BlockSpec(par.shape, lambda b, c: (0, 0)),
                const3(tri), const3(sl), const3(incl), const3(strict)]
    args = [qkv, z_gd, qkv, z_gd, par, tri, sl, incl, strict]
    if s0 is not None:
        in_specs.append(pl.BlockSpec(st_blk, lambda b, c: (b, 0, 0, 0, 0)))
        args.append(s0)
    return pl.pallas_call(
        functools.partial(_gdn_kernel, has_state=s0 is not None),
        grid=(B, n),
        in_specs=in_specs,
        out_specs=[pl.BlockSpec(oblk, lambda b, c: (fwd(b, c), 0)),
                   pl.BlockSpec(oblk, lambda b, c: (bwd(b, c), 0)),
                   pl.BlockSpec(st_blk, lambda b, c: (b, 0, 0, 0, 0))],
        out_shape=[jax.ShapeDtypeStruct((B * L, GD_HEADS * GD_DV), F32),
                   jax.ShapeDtypeStruct((B * L, GD_HEADS * GD_DV), F32),
                   jax.ShapeDtypeStruct((B,) + st_blk[1:], F32)],
        scratch_shapes=[pltpu.VMEM((2, GD_HEADS, GD_DV, GD_DK), F32)],
        compiler_params=_cparams("parallel", "arbitrary"),
        name="gdn_scan",
    )(*args)


NA_W = NA_HEADS * NA_DH


def _softmax_pv(s, v):
    m = jnp.max(s, axis=-1, keepdims=True)
    e = jnp.exp(s - m)
    den = jnp.sum(e, axis=-1, keepdims=True)
    return jnp.dot(e.astype(BF16), v, preferred_element_type=F32) / den


def _ctx_attn_kernel(q_ref, k_ref, v_ref, o_ref):
    for h in range(NA_HEADS):
        cs = slice(h * NA_DH, (h + 1) * NA_DH)
        q = (q_ref[:, cs] * (NA_DH ** -0.5)).astype(BF16)
        s = lax.dot_general(q, k_ref[:, cs].astype(BF16), NT, preferred_element_type=F32)
        o_ref[:, cs] = _softmax_pv(s, v_ref[:, cs].astype(BF16)).astype(o_ref.dtype)


def _ctx_attn(z_na, B, L):
    blk = (L, NA_W)
    return pl.pallas_call(
        _ctx_attn_kernel,
        grid=(B,),
        in_specs=[pl.BlockSpec(blk, lambda b: (b, 0)),
                  pl.BlockSpec(blk, lambda b: (b, 1)),
                  pl.BlockSpec(blk, lambda b: (b, 2))],
        out_specs=pl.BlockSpec(blk, lambda b: (b, 0)),
        out_shape=jax.ShapeDtypeStruct((B * L, NA_W), BF16),
        compiler_params=_cparams("parallel"),
        name="ctx_attn",
    )(z_na, z_na, z_na)


def _na_kernel(q_ref, *rest, n_kblk, nkeys_nb):
    k_refs = rest[:n_kblk]
    v_refs = rest[n_kblk:2 * n_kblk]
    kc_ref, vc_ref, bias_ref, o_ref, kbuf, vbuf = rest[2 * n_kblk:]
    qb = q_ref.shape[0]
    for i in range(n_kblk):
        kbuf[i * qb:(i + 1) * qb, :] = k_refs[i][...].astype(BF16)
        vbuf[i * qb:(i + 1) * qb, :] = v_refs[i][...].astype(BF16)
    kbuf[nkeys_nb:, :] = kc_ref[0].astype(BF16)
    vbuf[nkeys_nb:, :] = vc_ref[0].astype(BF16)
    for h in range(NA_HEADS):
        cs = slice(h * NA_DH, (h + 1) * NA_DH)
        q = (q_ref[:, cs] * (NA_DH ** -0.5)).astype(BF16)
        s = lax.dot_general(q, kbuf[:, cs], NT, preferred_element_type=F32)
        s_nb = s[:, :nkeys_nb] + bias_ref[0, h]
        s_cx = s[:, nkeys_nb:]
        m = jnp.maximum(jnp.max(s_nb, axis=-1, keepdims=True), jnp.max(s_cx, axis=-1, keepdims=True))
        e_nb = jnp.exp(s_nb - m)
        e_cx = jnp.exp(s_cx - m)
        den = jnp.sum(e_nb, axis=-1, keepdims=True) + jnp.sum(e_cx, axis=-1, keepdims=True)
        pv = (jnp.dot(e_nb.astype(BF16), vbuf[:nkeys_nb, cs], preferred_element_type=F32)
              + jnp.dot(e_cx.astype(BF16), vbuf[nkeys_nb:, cs], preferred_element_type=F32))
        o_ref[:, cs] = (pv / den).astype(o_ref.dtype)


def _na_bias(rpb, rows):
    qr, qc = np.divmod(np.arange(NA_QROWS * GRID_W), GRID_W)
    kr, kc = np.divmod(np.arange(NA_KROWS * GRID_W), GRID_W)
    nblk = rows // NA_QROWS
    dr_l, dc_l, ok_l = [], [], []
    for m in (0, 1, nblk - 1):
        r = (NA_QROWS * m + qr)[:, None]
        start = np.clip(NA_QROWS * m - NA_KH // 2, 0, rows - NA_KROWS)
        kra = (start + kr)[None, :]
        r0 = np.clip(r - NA_KH // 2, 0, rows - NA_KH)
        row_ok = (kra >= r0) & (kra < r0 + NA_KH)
        col_start = np.clip(qc - NA_KW // 2, 0, GRID_W - NA_KW)[:, None]
        col_ok = (kc[None, :] >= col_start) & (kc[None, :] < col_start + NA_KW)
        dr_l.append(np.clip(kra - r + NA_KH - 1, 0, 2 * NA_KH - 2))
        dc_l.append(np.broadcast_to(np.clip(kc[None, :] - qc[:, None], -(NA_KW - 1), NA_KW - 1) + NA_KW - 1,
                                    row_ok.shape))
        ok_l.append(row_ok & col_ok)
    dr, dc, ok = np.stack(dr_l), np.stack(dc_l), np.stack(ok_l)
    bias = jnp.transpose(rpb[:, dr, dc], (1, 0, 2, 3))
    return jnp.where(ok[:, None], bias.astype(F32), -jnp.inf)


def _na_attn(z_na, k_ctx, v_ctx, bias, B, S):
    rows = S // GRID_W
    qb = NA_QROWS * GRID_W
    nblk = rows // NA_QROWS
    n_kblk = NA_KROWS // NA_QROWS
    lc = k_ctx.shape[1]
    nkeys_nb = NA_KROWS * GRID_W
    kstart = lambda m: jnp.clip(m - 1, 0, nblk - n_kblk)
    variant = lambda m: jnp.where(m == 0, 0, jnp.where(m == nblk - 1, 2, 1))
    kv_specs = lambda col: [pl.BlockSpec((qb, NA_W), functools.partial(
        lambda b, m, i, col: (b * nblk + kstart(m) + i, col), i=i, col=col)) for i in range(n_kblk)]
    return pl.pallas_call(
        functools.partial(_na_kernel, n_kblk=n_kblk, nkeys_nb=nkeys_nb),
        grid=(B, nblk),
        in_specs=([pl.BlockSpec((qb, NA_W), lambda b, m: (b * nblk + m, 0))] + kv_specs(1) + kv_specs(2)
                  + [pl.BlockSpec((1, lc, NA_W), lambda b, m: (b, 0, 0)),
                     pl.BlockSpec((1, lc, NA_W), lambda b, m: (b, 0, 0)),
                     pl.BlockSpec((1,) + bias.shape[1:], lambda b, m: (variant(m), 0, 0, 0))]),
        out_specs=pl.BlockSpec((qb, NA_W), lambda b, m: (b * nblk + m, 0)),
        out_shape=jax.ShapeDtypeStruct((B * S, NA_W), BF16),
        scratch_shapes=[pltpu.VMEM((nkeys_nb + lc, NA_W), BF16), pltpu.VMEM((nkeys_nb + lc, NA_W), BF16)],
        compiler_params=_cparams("parallel", "arbitrary"),
        name="na_attn",
    )(z_na, *([z_na] * (2 * n_kblk)), k_ctx, v_ctx, bias)


def _head_rms(o, g_row):
    parts = []
    for h in range(o.shape[1] // LANES):
        parts.append(_rms(o[:, h * LANES:(h + 1) * LANES]) * g_row)
    return jnp.concatenate(parts, axis=1)


def _merge_kernel(x_ref, oa_ref, hf_ref, hb_ref, hg_ref, gf_ref, gb_ref, gg_ref, od_ref, mg_ref,
                  hn_ref, gn_ref, wb_ref, wo_ref, g1_ref, o_ref):
    o_b = _head_rms(hf_ref[...] + hb_ref[...], hn_ref[...]) * _silu(hg_ref[...])
    o_c = _head_rms(gf_ref[...] + gb_ref[...], gn_ref[...]) * _silu(gg_ref[...])
    branches = (oa_ref[...], o_b.astype(BF16), o_c.astype(BF16), od_ref[...])
    merged = None
    for n_, o_n in enumerate(branches):
        gate = jax.nn.sigmoid(mg_ref[:, n_ * D_MODEL:(n_ + 1) * D_MODEL])
        term = gate * jnp.dot(o_n, wb_ref[n_], preferred_element_type=F32)
        merged = term if merged is None else merged + term
    mix = jnp.dot(merged.astype(BF16), wo_ref[...], preferred_element_type=F32)
    o_ref[...] = x_ref[...] + g1_ref[0] * mix


def _merge(x, o_a, o_hf, o_hb, z_hg, o_gf, o_gb, z_gd, o_d, z_mg, hg_onorm, gd_onorm, wb, wo, mods, mod_row, tm):
    T = x.shape[0]
    row = lambda w, col=0: pl.BlockSpec((tm, w), lambda i: (i, col))
    full = lambda a: pl.BlockSpec(a.shape, lambda i: (0,) * a.ndim)
    hn = hg_onorm.reshape(1, HG_DV)
    gn = gd_onorm.reshape(1, GD_DV)
    return pl.pallas_call(
        _merge_kernel,
        grid=(T // tm,),
        in_specs=[row(D_MODEL), row(BRANCH_W), row(BRANCH_W), row(BRANCH_W), row(BRANCH_W, 4),
                  row(BRANCH_W), row(BRANCH_W), row(BRANCH_W, GD_QKV // BRANCH_W), row(BRANCH_W),
                  row(N_BRANCH * D_MODEL), full(hn), full(gn), full(wb), full(wo),
                  pl.BlockSpec((1, 1, D_MODEL), lambda i: (mod_row(i * tm) * 6 + 2, 0, 0))],
        out_specs=row(D_MODEL),
        out_shape=jax.ShapeDtypeStruct((T, D_MODEL), F32),
        compiler_params=_cparams("parallel"),
        name="merge",
    )(x, o_a, o_hf, o_hb, z_hg, o_gf, o_gb, z_gd, o_d, z_mg, hn, gn, wb, wo, mods)


def _router_kernel(x_ref, g_ref, sc_ref, sh_ref, wr_ref, br_ref, h_ref, e_ref, w_ref):
    h = (_rms(x_ref[...]) * g_ref[...]) * (1.0 + sc_ref[0]) + sh_ref[0]
    h_ref[...] = h.astype(h_ref.dtype)
    logits = _hdot(h, wr_ref[...]) + br_ref[...]
    lane = lax.broadcasted_iota(jnp.int32, logits.shape, 1)
    e_out = jnp.zeros(logits.shape, jnp.int32)
    v_out = jnp.zeros(logits.shape, F32)
    top0 = None
    for k in range(TOP_K):
        m = jnp.max(logits, axis=-1, keepdims=True)
        idx = jnp.min(jnp.where(logits == m, lane, LANES), axis=-1, keepdims=True)
        if k == 0:
            top0 = m
        e_out = jnp.where(lane == k, idx, e_out)
        v_out = jnp.where(lane == k, jnp.exp(m - top0), v_out)
        logits = jnp.where(lane == idx, -jnp.inf, logits)
    e_ref[...] = e_out
    w_ref[...] = v_out / jnp.sum(v_out, axis=-1, keepdims=True)


def _router(x, g, mods, mod_row, w_router, b_router, tm):
    T = x.shape[0]
    wr = jnp.zeros((D_MODEL, LANES), F32).at[:, :N_EXP].set(w_router)
    br = jnp.full((1, LANES), -jnp.inf, F32).at[0, :N_EXP].set(b_router)
    row = lambda w: pl.BlockSpec((tm, w), lambda i: (i, 0))
    return pl.pallas_call(
        _router_kernel,
        grid=(T // tm,),
        in_specs=[row(D_MODEL),
                  pl.BlockSpec((1, D_MODEL), lambda i: (0, 0)),
                  pl.BlockSpec((1, 1, D_MODEL), lambda i: (mod_row(i * tm) * 6 + 4, 0, 0)),
                  pl.BlockSpec((1, 1, D_MODEL), lambda i: (mod_row(i * tm) * 6 + 3, 0, 0)),
                  pl.BlockSpec((D_MODEL, LANES), lambda i: (0, 0)),
                  pl.BlockSpec((1, LANES), lambda i: (0, 0))],
        out_specs=[row(D_MODEL), row(LANES), row(LANES)],
        out_shape=[jax.ShapeDtypeStruct((T, D_MODEL), BF16),
                   jax.ShapeDtypeStruct((T, LANES), jnp.int32),
                   jax.ShapeDtypeStruct((T, LANES), F32)],
        compiler_params=_cparams("parallel"),
        name="router",
    )(x, g.reshape(1, D_MODEL), mods, mods, wr, br)


def _expert_kernel(blk_e_ref, x_ref, wgu_ref, bgu_ref, wdn_ref, bdn_ref, o_ref):
    gu = jnp.dot(x_ref[...], wgu_ref[0], preferred_element_type=F32) + bgu_ref[0]
    a = jnp.minimum(gu[:, :D_FF], SWIGLU_LIMIT)
    lin = jnp.clip(gu[:, D_FF:], -SWIGLU_LIMIT, SWIGLU_LIMIT)
    y = a * jax.nn.sigmoid(SWIGLU_ALPHA * a) * (lin + 1.0)
    o_ref[...] = jnp.dot(y.astype(BF16), wdn_ref[0], preferred_element_type=F32) + bdn_ref[0]


def _experts(xb, blk_e, w_gu, b_gu, w_dn, b_dn):
    n_pad = xb.shape[0]
    n_blocks = n_pad // MOE_BLOCK
    grid_spec = pltpu.PrefetchScalarGridSpec(
        num_scalar_prefetch=1,
        grid=(n_blocks,),
        in_specs=[pl.BlockSpec((MOE_BLOCK, D_MODEL), lambda i, e: (i, 0)),
                  pl.BlockSpec((1, D_MODEL, 2 * D_FF), lambda i, e: (e[i], 0, 0)),
                  pl.BlockSpec((1, 1, 2 * D_FF), lambda i, e: (e[i], 0, 0)),
                  pl.BlockSpec((1, D_FF, D_MODEL), lambda i, e: (e[i], 0, 0)),
                  pl.BlockSpec((1, 1, D_MODEL), lambda i, e: (e[i], 0, 0))],
        out_specs=pl.BlockSpec((MOE_BLOCK, D_MODEL), lambda i, e: (i, 0)),
    )
    return pl.pallas_call(
        _expert_kernel,
        grid_spec=grid_spec,
        out_shape=jax.ShapeDtypeStruct((n_pad, D_MODEL), F32),
        compiler_params=_cparams("arbitrary"),
        name="experts",
    )(blk_e, xb, w_gu, b_gu.reshape(N_EXP, 1, 2 * D_FF), w_dn, b_dn.reshape(N_EXP, 1, D_MODEL))


def _combine_kernel(x_ref, y_ref, w_ref, g2_ref, nf_ref, o_ref, *, final_norm):
    acc = None
    for k in range(TOP_K):
        term = y_ref[:, k * D_MODEL:(k + 1) * D_MODEL] * w_ref[:, k:k + 1]
        acc = term if acc is None else acc + term
    x = x_ref[...] + g2_ref[0] * acc
    if final_norm:
        x = _rms(x) * nf_ref[...]
    o_ref[...] = x


def _combine(x, yg, wts, mods, mod_row, norm_f, final_norm, tm):
    T = x.shape[0]
    row = lambda w: pl.BlockSpec((tm, w), lambda i: (i, 0))
    return pl.pallas_call(
        functools.partial(_combine_kernel, final_norm=final_norm),
        grid=(T // tm,),
        in_specs=[row(D_MODEL), row(TOP_K * D_MODEL), row(LANES),
                  pl.BlockSpec((1, 1, D_MODEL), lambda i: (mod_row(i * tm) * 6 + 5, 0, 0)),
                  pl.BlockSpec((1, D_MODEL), lambda i: (0, 0))],
        out_specs=row(D_MODEL),
        out_shape=jax.ShapeDtypeStruct((T, D_MODEL), F32),
        compiler_params=_cparams("parallel"),
        name="combine",
    )(x, yg, wts, mods, norm_f.reshape(1, D_MODEL))


def _route(top_e, T):
    n_assign = T * TOP_K
    n_blocks = n_assign // MOE_BLOCK + N_EXP
    e_flat = top_e.reshape(n_assign)
    order = jnp.argsort(e_flat)
    e_sorted = e_flat[order]
    counts = jnp.zeros((N_EXP,), jnp.int32).at[e_flat].add(1)
    start = jnp.cumsum(counts) - counts
    padded = (counts + MOE_BLOCK - 1) // MOE_BLOCK * MOE_BLOCK
    pad_end = jnp.cumsum(padded)
    pad_start = pad_end - padded
    pos_sorted = pad_start[e_sorted] + jnp.arange(n_assign, dtype=jnp.int32) - start[e_sorted]
    tok = jnp.zeros((n_blocks * MOE_BLOCK,), jnp.int32).at[pos_sorted].set((order // TOP_K).astype(jnp.int32))
    pos = jnp.zeros((n_assign,), jnp.int32).at[order].set(pos_sorted)
    blk_e = jnp.minimum(jnp.searchsorted(pad_end, jnp.arange(n_blocks, dtype=jnp.int32) * MOE_BLOCK, side='right'),
                        N_EXP - 1).astype(jnp.int32)
    return tok, pos, blk_e


def _prep_layer(l, w_in, sgu_w, w_branch, w_out, w_gu, w_dn, gd_A_log, gd_dt_bias, lb):
    offs = np.cumsum([0, BRANCH_W, BRANCH_W, 512, 512, 512, 512, 512, GD_QKV, 8, 8, 512, 3 * NA_W, N_BRANCH * D_MODEL])
    w = w_in[l]
    seg = lambda i, j: w[:, offs[i]:offs[j]]
    w_gd = jnp.concatenate([seg(7, 8), seg(10, 11), seg(8, 10),
                            jnp.zeros((D_MODEL, LANES - 4 * GD_HEADS), F32)], axis=1)
    par = jnp.zeros((2, LANES), F32)
    par = par.at[0, :2 * GD_HEADS].set(gd_A_log[l].reshape(-1)).at[1, :2 * GD_HEADS].set(gd_dt_bias[l].reshape(-1))
    lb_l = lb[:, l]
    return {
        'w_sgu': seg(0, 2).astype(BF16), 'w_hg': seg(2, 7).astype(BF16), 'w_gd': w_gd.astype(BF16),
        'w_na': seg(11, 12).astype(BF16), 'w_mg': seg(12, 13).astype(BF16),
        'sgu_w': sgu_w[l], 'wb': w_branch[l].astype(BF16), 'wo': w_out[l].astype(BF16),
        'w_gu': w_gu[l].astype(BF16), 'w_dn': w_dn[l].astype(BF16), 'gd_par': par,
        'lbp': jnp.concatenate([jnp.log(lb_l), jnp.log1p(-lb_l), 1.0 - lb_l], axis=0),
    }


def _layer(x, B, L, mods, mod_row, tm, lw, p, ctx, norm_f, final_norm):
    T = B * L
    h = _normmod(x, p['norm1'], mods, mod_row, tm, part_shift=0, part_scale=1)
    z_sgu = _matmul(h, lw['w_sgu'], 1024, 1024)
    z_hg = _matmul(h, lw['w_hg'], 1024, 1280)
    z_gd = _matmul(h, lw['w_gd'], 512, 2176)
    z_na = _matmul(h, lw['w_na'], 1024, 768)
    z_mg = _matmul(h, lw['w_mg'], 1024, 1024)

    o_a = _sgu(z_sgu, p['sgu_norm'], lw['sgu_w'], p['sgu_b'], 256)
    s_hg0 = None if ctx is None else ctx[2]
    s_gd0 = None if ctx is None else ctx[3]
    o_hf, o_hb, s_hg = _hgrn(z_hg, lw['lbp'], s_hg0, B, L)
    qkv = _gdprep(z_gd, p['gd_conv'], L, 256)
    o_gf, o_gb, s_gd = _gdn(qkv, z_gd, lw['gd_par'], s_gd0, B, L)
    if ctx is None:
        o_d = _ctx_attn(z_na, B, L)
    else:
        o_d = _na_attn(z_na, ctx[0], ctx[1], _na_bias(p['na_rpb'], L // GRID_W), B, L)
    x = _merge(x, o_a, o_hf, o_hb, z_hg, o_gf, o_gb, z_gd, o_d, z_mg, p['hg_onorm'], p['gd_onorm'],
               lw['wb'], lw['wo'], mods, mod_row, 256)

    h2, top_e, wts = _router(x, p['norm2'], mods, mod_row, p['w_router'], p['b_router'], 512)
    tok, pos, blk_e = _route(top_e[:, :TOP_K], T)
    xb = jnp.take(h2, tok, axis=0)
    yb = _experts(xb, blk_e, lw['w_gu'], p['b_gu'], lw['w_dn'], p['b_dn'])
    yg = jnp.take(yb, pos, axis=0).reshape(T, TOP_K * D_MODEL)
    x = _combine(x, yg, wts, mods, mod_row, norm_f, final_norm, 256)
    side = (z_na[:, NA_W:2 * NA_W].reshape(B, L, NA_HEADS, NA_DH),
            z_na[:, 2 * NA_W:].reshape(B, L, NA_HEADS, NA_DH), s_hg, s_gd)
    return x, side


def kernel(x_prompt, x_sample, c, cache_na_k, cache_na_v, state_hgrn, state_gdn, c_ctx, w_ada, b_ada, norm1, norm2, norm_f, w_in, sgu_norm, sgu_w, sgu_b, hg_lb, hg_onorm, gd_conv, gd_A_log, gd_dt_bias, gd_onorm, na_rpb, w_branch, w_out, w_router, b_router, w_gu, b_gu, w_dn, b_dn):
    Bp, Lp, D = x_prompt.shape
    Bs, Ls, _ = x_sample.shape
    ctx_row = Bs
    cvecs = jnp.zeros((MOD_ROWS, D), F32).at[:Bs].set(c).at[ctx_row].set(c_ctx)
    mods = _modulation(cvecs, w_ada, b_ada)

    cs = jnp.cumsum(jax.nn.softmax(hg_lb.astype(F32), axis=1), axis=1)
    lb = cs - cs[:, :1]

    tm = 1024
    xp = x_prompt.reshape(Bp * Lp, D)
    xs = x_sample.reshape(Bs * Ls, D)
    sides = []
    for l in range(DEPTH):
        lw = _prep_layer(l, w_in, sgu_w, w_branch, w_out, w_gu, w_dn, gd_A_log, gd_dt_bias, lb)
        p = {'norm1': norm1[l], 'norm2': norm2[l], 'sgu_norm': sgu_norm[l], 'sgu_b': sgu_b[l],
             'gd_conv': gd_conv[l], 'hg_onorm': hg_onorm[l], 'gd_onorm': gd_onorm[l], 'na_rpb': na_rpb[l],
             'w_router': w_router[l], 'b_router': b_router[l], 'b_gu': b_gu[l], 'b_dn': b_dn[l]}
        final = l == DEPTH - 1
        xp, side = _layer(xp, Bp, Lp, mods[l], lambda r: ctx_row, tm, lw, p, None, norm_f, final)
        sides.append(side)
        ctx = (cache_na_k[:, l].reshape(Bs, -1, NA_W), cache_na_v[:, l].reshape(Bs, -1, NA_W),
               state_hgrn[:, l], state_gdn[:, l])
        xs, _ = _layer(xs, Bs, Ls, mods[l], lambda r: r // Ls, tm, lw, p, ctx, norm_f, final)

    return (xp.reshape(Bp, Lp, D), xs.reshape(Bs, Ls, D),
            jnp.stack([s[0] for s in sides], axis=1), jnp.stack([s[1] for s in sides], axis=1),
            jnp.stack([s[2] for s in sides], axis=1), jnp.stack([s[3] for s in sides], axis=1))
```

```python
import functools
import math

import numpy as np
import jax
import jax.numpy as jnp
from jax import lax
from jax.experimental import pallas as pl
from jax.experimental.pallas import tpu as pltpu
from jax.experimental.pallas import tpu_sc as plsc

D_MODEL = 1024
DEPTH = 2
GRID_W = 64
BRANCH_W = 512
N_BRANCH = 4
SGU_CHUNK = 128
SGU_GROUPS = 4
HG_HEADS = 4
HG_DK = 128
HG_DV = 128
GD_HEADS = 4
GD_DK = 128
GD_DV = 128
NA_HEADS = 8
NA_DH = 64
NA_KH = 8
NA_KW = 16
N_EXP = 32
TOP_K = 4
D_FF = 1024
SWIGLU_LIMIT = 7.0
SWIGLU_ALPHA = 1.702
EPS = 1e-6

F32 = jnp.float32
BF16 = jnp.bfloat16
HI = lax.Precision.HIGHEST

LANES = 128
MOD_ROWS = 16
SCAN_C = 64
MOE_BLOCK = 128
NA_QROWS = 4
NA_KROWS = NA_QROWS + NA_KH
VMEM_LIMIT = 48 * 1024 * 1024

NT = (((1,), (1,)), ((), ()))
TN = (((0,), (0,)), ((), ()))


def _cparams(*sem):
    return pltpu.CompilerParams(dimension_semantics=sem, vmem_limit_bytes=VMEM_LIMIT)


def _bdot(a, b):
    return jnp.dot(a.astype(BF16), b.astype(BF16), preferred_element_type=F32)


def _bdot_g(a, b, dims):
    return lax.dot_general(a.astype(BF16), b.astype(BF16), dims, preferred_element_type=F32)


def _hdot(a, b):
    return jnp.dot(a, b, precision=HI, preferred_element_type=F32)


def _dot01(m, x):
    hi = x.astype(BF16)
    r1 = x - hi.astype(F32)
    mid = r1.astype(BF16)
    lo = (r1 - mid.astype(F32)).astype(BF16)
    d = lambda t: jnp.dot(m, t, preferred_element_type=F32)
    return (d(lo) + d(mid)) + d(hi)


def _dot3(a, b):
    ah = a.astype(BF16)
    al = (a - ah.astype(F32)).astype(BF16)
    bh = b.astype(BF16)
    bl = (b - bh.astype(F32)).astype(BF16)
    d = lambda x, y: jnp.dot(x, y, preferred_element_type=F32)
    return (d(al, bh) + d(ah, bl)) + d(ah, bh)


def _silu(x):
    return x * jax.nn.sigmoid(x)


def _log_sigmoid(x):
    return jnp.minimum(x, 0.0) - jnp.log1p(jnp.exp(-jnp.abs(x)))


def _logaddexp(a, b):
    return jnp.maximum(a, b) + jnp.log1p(jnp.exp(-jnp.abs(a - b)))


def _softplus(x):
    return jnp.maximum(x, 0.0) + jnp.log1p(jnp.exp(-jnp.abs(x)))


def _ada_kernel(c_ref, w_ref, b_ref, o_ref):
    o_ref[0] = _hdot(_silu(c_ref[...]), w_ref[0]) + b_ref[0]


def _modulation(cvecs, w_ada, b_ada):
    tn = 1536
    out = pl.pallas_call(
        _ada_kernel,
        grid=(DEPTH, 6 * D_MODEL // tn),
        in_specs=[pl.BlockSpec((MOD_ROWS, D_MODEL), lambda l, j: (0, 0)),
                  pl.BlockSpec((1, D_MODEL, tn), lambda l, j: (l, 0, j)),
                  pl.BlockSpec((1, 1, tn), lambda l, j: (l, 0, j))],
        out_specs=pl.BlockSpec((1, MOD_ROWS, tn), lambda l, j: (l, 0, j)),
        out_shape=jax.ShapeDtypeStruct((DEPTH, MOD_ROWS, 6 * D_MODEL), F32),
        compiler_params=_cparams("arbitrary", "arbitrary"),
        name="ada_modulation",
    )(cvecs, w_ada, b_ada.reshape(DEPTH, 1, 6 * D_MODEL))
    return out.reshape(DEPTH, MOD_ROWS * 6, 1, D_MODEL)


def _rms(x):
    return x * lax.rsqrt(jnp.mean(x * x, axis=-1, keepdims=True) + EPS)


def _normmod_kernel(x_ref, g_ref, sc_ref, sh_ref, o_ref):
    h = (_rms(x_ref[...]) * g_ref[...]) * (1.0 + sc_ref[0]) + sh_ref[0]
    o_ref[...] = h.astype(o_ref.dtype)


def _normmod(x, g, mods, mod_row, tm, part_shift, part_scale):
    T = x.shape[0]
    return pl.pallas_call(
        _normmod_kernel,
        grid=(T // tm,),
        in_specs=[pl.BlockSpec((tm, D_MODEL), lambda i: (i, 0)),
                  pl.BlockSpec((1, D_MODEL), lambda i: (0, 0)),
                  pl.BlockSpec((1, 1, D_MODEL), lambda i: (mod_row(i * tm) * 6 + part_scale, 0, 0)),
                  pl.BlockSpec((1, 1, D_MODEL), lambda i: (mod_row(i * tm) * 6 + part_shift, 0, 0))],
        out_specs=pl.BlockSpec((tm, D_MODEL), lambda i: (i, 0)),
        out_shape=jax.ShapeDtypeStruct((T, D_MODEL), BF16),
        compiler_params=_cparams("parallel"),
        name="normmod",
    )(x, g.reshape(1, D_MODEL), mods, mods)


def _mm_kernel(a_ref, w_ref, o_ref):
    o_ref[...] = jnp.dot(a_ref[...], w_ref[...], preferred_element_type=F32).astype(o_ref.dtype)


def _matmul(a, w, tm, tn, out_dtype=F32):
    T, K = a.shape
    N = w.shape[1]
    return pl.pallas_call(
        _mm_kernel,
        grid=(N // tn, T // tm),
        in_specs=[pl.BlockSpec((tm, K), lambda j, i: (i, 0)),
                  pl.BlockSpec((K, tn), lambda j, i: (0, j))],
        out_specs=pl.BlockSpec((tm, tn), lambda j, i: (i, j)),
        out_shape=jax.ShapeDtypeStruct((T, N), out_dtype),
        compiler_params=_cparams("parallel", "parallel"),
        name="in_proj",
    )(a, w)


def _sgu_kernel(u_ref, v_ref, gn_ref, ws_ref, bs_ref, o_ref):
    rows = u_ref.shape[0]
    gw = BRANCH_W // SGU_GROUPS
    u = jax.nn.gelu(u_ref[...])
    v = (_rms(jax.nn.gelu(v_ref[...])) * gn_ref[...]).astype(BF16)
    for n in range(rows // SGU_CHUNK):
        r = slice(n * SGU_CHUNK, (n + 1) * SGU_CHUNK)
        for g in range(SGU_GROUPS):
            cs = slice(g * gw, (g + 1) * gw)
            s = jnp.dot(ws_ref[g], v[r, cs], preferred_element_type=F32) + bs_ref[:, cs]
            o_ref[r, cs] = (u[r, cs] * s).astype(o_ref.dtype)


def _sgu(z_sgu, g_norm, w_s, b_s, rows):
    T = z_sgu.shape[0]
    gw = BRANCH_W // SGU_GROUPS
    b_exp = jnp.repeat(b_s.T, gw, axis=1)
    return pl.pallas_call(
        _sgu_kernel,
        grid=(T // rows,),
        in_specs=[pl.BlockSpec((rows, BRANCH_W), lambda i: (i, 0)),
                  pl.BlockSpec((rows, BRANCH_W), lambda i: (i, 1)),
                  pl.BlockSpec((1, BRANCH_W), lambda i: (0, 0)),
                  pl.BlockSpec((SGU_GROUPS, SGU_CHUNK, SGU_CHUNK), lambda i: (0, 0, 0)),
                  pl.BlockSpec((SGU_CHUNK, BRANCH_W), lambda i: (0, 0))],
        out_specs=pl.BlockSpec((rows, BRANCH_W), lambda i: (i, 0)),
        out_shape=jax.ShapeDtypeStruct((T, BRANCH_W), BF16),
        compiler_params=_cparams("parallel"),
        name="sgu",
    )(z_sgu, z_sgu, g_norm.reshape(1, BRANCH_W), w_s.astype(BF16), b_exp)


def _order(reverse):
    p = np.arange(SCAN_C)
    return SCAN_C - 1 - p if reverse else p


def _gla_consts():
    C = SCAN_C
    nlev = int(math.log2(C))
    mats, masks = [], []
    for reverse in (False, True):
        p = _order(reverse)
        pt, pr = p[:, None], p[None, :]
        m_d, k_d = [], []
        for lev in range(nlev):
            w = C >> (lev + 1)
            parent = p // (2 * w)
            later = (p % (2 * w)) >= w
            anchor = (parent * 2 * w + w - 1)[:, None]
            m = np.where(later[:, None], (pr > anchor) & (pr <= pt), (pr > pt) & (pr <= anchor))
            m_d.append(m)
            k_d.append((parent[:, None] == parent[None, :]) & later[:, None] & ~later[None, :])
        m_d.append(pr <= pt)
        m_d.append(pr > pt)
        k_d.append(np.eye(C, dtype=bool))
        mats.append(np.concatenate(m_d, axis=0))
        masks.append(np.stack(k_d))
    return (np.stack(mats).astype(np.float32), np.stack(masks).astype(np.float32))


def _delta_consts():
    C = SCAN_C
    tri, sl, incl, strict = [], [], [], []
    for reverse in (False, True):
        p = _order(reverse)
        pt, pr = p[:, None], p[None, :]
        tri.append(np.concatenate([pr <= pt, pr > pt], axis=0))
        sl.append(np.concatenate([pt > pr, np.zeros((C, LANES - C), bool), np.ones((C, LANES), bool)], axis=1))
        incl.append(pr <= pt)
        strict.append(pr < pt)
    f = lambda a: np.stack(a).astype(np.float32)
    return f(tri), f(sl), f(incl), f(strict)


def _hgrn_kernel(qf_ref, ff_ref, vf_ref, qb_ref, fb_ref, vb_ref, lb_ref, mat_ref, msk_ref, *rest, has_state):
    if has_state:
        s0_ref, of_ref, ob_ref, sfin_ref, st_ref = rest
    else:
        of_ref, ob_ref, sfin_ref, st_ref = rest
    C = SCAN_C
    nlev = msk_ref.shape[1] - 1
    c = pl.program_id(1)
    last_c = pl.num_programs(1) - 1

    @pl.when(c == 0)
    def _():
        for d in range(2):
            for h in range(HG_HEADS):
                if has_state:
                    st_ref[d, h] = s0_ref[0, d, h].T
                else:
                    st_ref[d, h] = jnp.zeros((HG_DV, HG_DK), F32)

    q_refs, f_refs, v_refs, o_refs = (qf_ref, qb_ref), (ff_ref, fb_ref), (vf_ref, vb_ref), (of_ref, ob_ref)
    chains = [(d, h) for d in range(2) for h in range(HG_HEADS)]
    col = lambda h: slice(h * HG_DK, (h + 1) * HG_DK)
    q, k, fac, att = {}, {}, {}, {}
    for d, h in chains:
        zf = f_refs[d][:, col(h)]
        loglb = lb_ref[d:d + 1, col(h)]
        log1m = lb_ref[2 + d:3 + d, col(h)]
        onem = lb_ref[4 + d:5 + d, col(h)]
        logf = _logaddexp(loglb, log1m + _log_sigmoid(zf))
        k[d, h] = onem * jax.nn.sigmoid(-zf)
        q[d, h] = _silu(q_refs[d][:, col(h)]) * (HG_DK ** -0.5)
        fac[d, h] = jnp.exp(_dot01(mat_ref[d], logf))
    for d, h in chains:
        acc = msk_ref[d, nlev] * _bdot_g(q[d, h], k[d, h], NT)
        for i in range(nlev):
            fi = fac[d, h][i * C:(i + 1) * C]
            acc = acc + msk_ref[d, i] * _bdot_g(q[d, h] * fi, k[d, h] * fi, NT)
        att[d, h] = acc
    for d, h in chains:
        eb = fac[d, h][nlev * C:(nlev + 1) * C]
        o_refs[d][:, col(h)] = (_bdot(att[d, h], v_refs[d][:, col(h)])
                                + _bdot_g(q[d, h] * eb, st_ref[d, h], NT))
    for d, h in chains:
        eb = fac[d, h][nlev * C:(nlev + 1) * C]
        er = fac[d, h][(nlev + 1) * C:]
        e_last = eb[C - 1:C] if d == 0 else eb[0:1]
        st_ref[d, h] = st_ref[d, h] * e_last + _bdot_g(v_refs[d][:, col(h)], k[d, h] * er, TN)

    @pl.when(c == last_c)
    def _():
        for d in range(2):
            for h in range(HG_HEADS):
                sfin_ref[0, d, h] = st_ref[d, h].T


def _hgrn(z_hg, lbp, s0, B, L):
    n = L // SCAN_C
    mats, masks = _gla_consts()
    blk = (SCAN_C, HG_HEADS * HG_DK)
    fwd = lambda col: pl.BlockSpec(blk, lambda b, c: (b * n + c, col))
    bwd = lambda col: pl.BlockSpec(blk, lambda b, c: (b * n + n - 1 - c, col))
    st_blk = (1, 2, HG_HEADS, HG_DK, HG_DV)
    in_specs = [fwd(0), fwd(1), fwd(3), bwd(0), bwd(2), bwd(3),
                pl.BlockSpec(lbp.shape, lambda b, c: (0, 0)),
                pl.BlockSpec(mats.shape, lambda b, c: (0, 0, 0)),
                pl.BlockSpec(masks.shape, lambda b, c: (0, 0, 0, 0))]
    args = [z_hg] * 6 + [lbp, jnp.asarray(mats, BF16), jnp.asarray(masks)]
    if s0 is not None:
        in_specs.append(pl.BlockSpec(st_blk, lambda b, c: (b, 0, 0, 0, 0)))
        args.append(s0)
    return pl.pallas_call(
        functools.partial(_hgrn_kernel, has_state=s0 is not None),
        grid=(B, n),
        in_specs=in_specs,
        out_specs=[pl.BlockSpec(blk, lambda b, c: (b * n + c, 0)),
                   pl.BlockSpec(blk, lambda b, c: (b * n + n - 1 - c, 0)),
                   pl.BlockSpec(st_blk, lambda b, c: (b, 0, 0, 0, 0))],
        out_shape=[jax.ShapeDtypeStruct((B * L, HG_HEADS * HG_DV), F32),
                   jax.ShapeDtypeStruct((B * L, HG_HEADS * HG_DV), F32),
                   jax.ShapeDtypeStruct((B,) + st_blk[1:], F32)],
        scratch_shapes=[pltpu.VMEM((2, HG_HEADS, HG_DV, HG_DK), F32)],
        compiler_params=_cparams("parallel", "arbitrary"),
        name="hgrn_scan",
    )(*args)


GD_NQ = GD_HEADS * GD_DK
GD_QKV = 2 * GD_NQ + GD_HEADS * GD_DV
HALO = 8


def _gdprep_kernel(x_ref, prev_ref, next_ref, w_ref, o_ref, *, tiles_per_seq):
    R = x_ref.shape[0]
    t = pl.program_id(0) % tiles_per_seq
    x = x_ref[...]
    prev_row = jnp.where(t == 0, 0.0, prev_ref[HALO - 1:HALO, :])
    next_row = jnp.where(t == tiles_per_seq - 1, 0.0, next_ref[0:1, :])
    row = lax.broadcasted_iota(jnp.int32, x.shape, 0)
    xm1 = jnp.where(row == 0, prev_row, pltpu.roll(x, 1, 0))
    xp1 = jnp.where(row == R - 1, next_row, pltpu.roll(x, R - 1, 0))
    y = _silu(w_ref[0:1, :] * xm1 + w_ref[1:2, :] * x + w_ref[2:3, :] * xp1)
    for j in range(2 * GD_HEADS):
        cs = slice(j * GD_DK, (j + 1) * GD_DK)
        seg = y[:, cs]
        seg = seg * lax.rsqrt(jnp.sum(seg * seg, axis=-1, keepdims=True) + EPS)
        if j < GD_HEADS:
            seg = seg * (GD_DK ** -0.5)
        o_ref[:, cs] = seg
    o_ref[:, 2 * GD_NQ:] = y[:, 2 * GD_NQ:]


def _gdprep(z_gd, conv_w, L, rows):
    T = z_gd.shape[0]
    tps = L // rows
    hb = rows // HALO
    nhalo = T // HALO
    return pl.pallas_call(
        functools.partial(_gdprep_kernel, tiles_per_seq=tps),
        grid=(T // rows,),
        in_specs=[pl.BlockSpec((rows, GD_QKV), lambda i: (i, 0)),
                  pl.BlockSpec((HALO, GD_QKV), lambda i: (jnp.maximum(i * hb - 1, 0), 0)),
                  pl.BlockSpec((HALO, GD_QKV), lambda i: (jnp.minimum((i + 1) * hb, nhalo - 1), 0)),
                  pl.BlockSpec((3, GD_QKV), lambda i: (0, 0))],
        out_specs=pl.BlockSpec((rows, GD_QKV), lambda i: (i, 0)),
        out_shape=jax.ShapeDtypeStruct((T, GD_QKV), F32),
        compiler_params=_cparams("parallel"),
        name="gdn_prep",
    )(z_gd, z_gd, z_gd, conv_w)


def _gdn_kernel(xf_ref, abf_ref, xb_ref, abb_ref, par_ref, tri_ref, sl_ref, incl_ref, strict_ref, msk_ref, *rest,
                has_state):
    if has_state:
        s0_ref, of_ref, ob_ref, sfin_ref, st_ref = rest
    else:
        of_ref, ob_ref, sfin_ref, st_ref = rest
    C = SCAN_C
    c = pl.program_id(1)
    last_c = pl.num_programs(1) - 1
    nlev = msk_ref.shape[1] - 1

    @pl.when(c == 0)
    def _():
        for d in range(2):
            for h in range(GD_HEADS):
                if has_state:
                    st_ref[d, h] = s0_ref[0, d, h].T
                else:
                    st_ref[d, h] = jnp.zeros((GD_DV, GD_DK), F32)

    x_refs, ab_refs, o_refs = (xf_ref, xb_ref), (abf_ref, abb_ref), (of_ref, ob_ref)
    chains = [(d, h) for d in range(2) for h in range(GD_HEADS)]
    q_of = lambda d, h: x_refs[d][:, h * GD_DK:(h + 1) * GD_DK]
    k_of = lambda d, h: x_refs[d][:, GD_NQ + h * GD_DK:GD_NQ + (h + 1) * GD_DK]
    v_of = lambda d, h: x_refs[d][:, 2 * GD_NQ + h * GD_DV:2 * GD_NQ + (h + 1) * GD_DV]
    g_all, beta_all = [], []
    for d in range(2):
        ab = ab_refs[d][...]
        g_all.append(-jnp.exp(par_ref[0:1, :]) * _softplus(ab + par_ref[1:2, :]))
        beta_all.append(jax.nn.sigmoid(ab))
    decay, e_cum, e_rest, kb, a = {}, {}, {}, {}, {}
    for d, h in chains:
        j = d * GD_HEADS + h
        g_b = jnp.broadcast_to(g_all[d][:, j:j + 1], (C, 2 * LANES))
        sums = _dot01(tri_ref[d], g_b * sl_ref[d])
        decay[d, h] = jnp.exp(sums[:C, :C])
        e_cum[d, h] = jnp.exp(sums[:C, LANES:])
        e_rest[d, h] = jnp.exp(sums[C:, LANES:])
    for d, h in chains:
        j = d * GD_HEADS + h
        beta = jnp.broadcast_to(beta_all[d][:, 2 * GD_HEADS + j:2 * GD_HEADS + j + 1], (C, LANES))
        k = k_of(d, h)
        kb[d, h] = k * beta
        a[d, h] = strict_ref[d] * decay[d, h] * _bdot_g(kb[d, h], k, NT)
    inv_m = {(d, h): -(msk_ref[d, nlev - 1] * a[d, h]) for d, h in chains}
    for lev in range(nlev - 2, -1, -1):
        a_w = {(d, h): msk_ref[d, lev] * a[d, h] for d, h in chains}
        p = {ch: a_w[ch] + _bdot(inv_m[ch], a_w[ch]) for ch in chains}
        inv_m = {ch: inv_m[ch] - p[ch] - _bdot(p[ch], inv_m[ch]) for ch in chains}
    sol = {}
    for d, h in chains:
        j = d * GD_HEADS + h
        beta = jnp.broadcast_to(beta_all[d][:, 2 * GD_HEADS + j:2 * GD_HEADS + j + 1], (C, LANES))
        rhs = jnp.concatenate([v_of(d, h) * beta, kb[d, h] * e_cum[d, h]], axis=1)
        sol[d, h] = rhs + _dot3(inv_m[d, h], rhs)
    u = {ch: sol[ch][:, :GD_DV] - _bdot_g(sol[ch][:, GD_DV:], st_ref[ch[0], ch[1]], NT) for ch in chains}
    att = {(d, h): incl_ref[d] * decay[d, h] * _bdot_g(q_of(d, h), k_of(d, h), NT) for d, h in chains}
    for d, h in chains:
        o_refs[d][:, h * GD_DV:(h + 1) * GD_DV] = (_bdot_g(q_of(d, h) * e_cum[d, h], st_ref[d, h], NT)
                                                   + _bdot(att[d, h], u[d, h]))
    for d, h in chains:
        e_last = e_cum[d, h][C - 1:C] if d == 0 else e_cum[d, h][0:1]
        st_ref[d, h] = st_ref[d, h] * e_last + _bdot_g(u[d, h], k_of(d, h) * e_rest[d, h], TN)

    @pl.when(c == last_c)
    def _():
        for d in range(2):
            for h in range(GD_HEADS):
                sfin_ref[0, d, h] = st_ref[d, h].T


def _gdn(qkv, z_gd, par, s0, B, L):
    n = L // SCAN_C
    tri, sl, incl, strict = (jnp.asarray(a) for a in _delta_consts())
    tri = tri.astype(BF16)
    masks = jnp.asarray(_gla_consts()[1])
    ab_col = (GD_QKV + GD_HEADS * GD_DV) // LANES
    xblk = (SCAN_C, GD_QKV)
    ablk = (SCAN_C, LANES)
    oblk = (SCAN_C, GD_HEADS * GD_DV)
    st_blk = (1, 2, GD_HEADS, GD_DK, GD_DV)
    fwd = lambda b, c: b * n + c
    bwd = lambda b, c: b * n + n - 1 - c
    const3 = lambda a: pl.BlockSpec(a.shape, lambda b, c: (0, 0, 0))
    in_specs = [pl.BlockSpec(xblk, lambda b, c: (fwd(b, c), 0)),
                pl.BlockSpec(ablk, lambda b, c: (fwd(b, c), ab_col)),
                pl.BlockSpec(xblk, lambda b, c: (bwd(b, c), 0)),
                pl.BlockSpec(ablk, lambda b, c: (bwd(b, c), ab_col)),
                pl.BlockSpec(par.shape, lambda b, c: (0, 0)),
                const3(tri), const3(sl), const3(incl), const3(strict),
                pl.BlockSpec(masks.shape, lambda b, c: (0, 0, 0, 0))]
    args = [qkv, z_gd, qkv, z_gd, par, tri, sl, incl, strict, masks]
    if s0 is not None:
        in_specs.append(pl.BlockSpec(st_blk, lambda b, c: (b, 0, 0, 0, 0)))
        args.append(s0)
    return pl.pallas_call(
        functools.partial(_gdn_kernel, has_state=s0 is not None),
        grid=(B, n),
        in_specs=in_specs,
        out_specs=[pl.BlockSpec(oblk, lambda b, c: (fwd(b, c), 0)),
                   pl.BlockSpec(oblk, lambda b, c: (bwd(b, c), 0)),
                   pl.BlockSpec(st_blk, lambda b, c: (b, 0, 0, 0, 0))],
        out_shape=[jax.ShapeDtypeStruct((B * L, GD_HEADS * GD_DV), F32),
                   jax.ShapeDtypeStruct((B * L, GD_HEADS * GD_DV), F32),
                   jax.ShapeDtypeStruct((B,) + st_blk[1:], F32)],
        scratch_shapes=[pltpu.VMEM((2, GD_HEADS, GD_DV, GD_DK), F32)],
        compiler_params=_cparams("parallel", "arbitrary"),
        name="gdn_scan",
    )(*args)


NA_W = NA_HEADS * NA_DH


def _softmax_pv(s, v):
    m = jnp.max(s, axis=-1, keepdims=True)
    e = jnp.exp(s - m)
    den = jnp.sum(e, axis=-1, keepdims=True)
    return jnp.dot(e.astype(BF16), v, preferred_element_type=F32) / den


def _ctx_attn_kernel(q_ref, k_ref, v_ref, o_ref):
    for h in range(NA_HEADS):
        cs = slice(h * NA_DH, (h + 1) * NA_DH)
        q = (q_ref[:, cs] * (NA_DH ** -0.5)).astype(BF16)
        s = lax.dot_general(q, k_ref[:, cs].astype(BF16), NT, preferred_element_type=F32)
        o_ref[:, cs] = _softmax_pv(s, v_ref[:, cs].astype(BF16)).astype(o_ref.dtype)


def _ctx_attn(z_na, B, L):
    blk = (L, NA_W)
    return pl.pallas_call(
        _ctx_attn_kernel,
        grid=(B,),
        in_specs=[pl.BlockSpec(blk, lambda b: (b, 0)),
                  pl.BlockSpec(blk, lambda b: (b, 1)),
                  pl.BlockSpec(blk, lambda b: (b, 2))],
        out_specs=pl.BlockSpec(blk, lambda b: (b, 0)),
        out_shape=jax.ShapeDtypeStruct((B * L, NA_W), BF16),
        compiler_params=_cparams("parallel"),
        name="ctx_attn",
    )(z_na, z_na, z_na)


def _na_kernel(q_ref, *rest, n_kblk, nkeys_nb):
    k_refs = rest[:n_kblk]
    v_refs = rest[n_kblk:2 * n_kblk]
    kc_ref, vc_ref, bias_ref, o_ref, kbuf, vbuf = rest[2 * n_kblk:]
    qb = q_ref.shape[0]
    for i in range(n_kblk):
        kbuf[i * qb:(i + 1) * qb, :] = k_refs[i][...].astype(BF16)
        vbuf[i * qb:(i + 1) * qb, :] = v_refs[i][...].astype(BF16)
    kbuf[nkeys_nb:, :] = kc_ref[0].astype(BF16)
    vbuf[nkeys_nb:, :] = vc_ref[0].astype(BF16)
    for h in range(NA_HEADS):
        cs = slice(h * NA_DH, (h + 1) * NA_DH)
        q = (q_ref[:, cs] * (NA_DH ** -0.5)).astype(BF16)
        s = lax.dot_general(q, kbuf[:, cs], NT, preferred_element_type=F32)
        s_nb = s[:, :nkeys_nb] + bias_ref[0, h]
        s_cx = s[:, nkeys_nb:]
        m = jnp.maximum(jnp.max(s_nb, axis=-1, keepdims=True), jnp.max(s_cx, axis=-1, keepdims=True))
        e_nb = jnp.exp(s_nb - m)
        e_cx = jnp.exp(s_cx - m)
        den = jnp.sum(e_nb, axis=-1, keepdims=True) + jnp.sum(e_cx, axis=-1, keepdims=True)
        pv = (jnp.dot(e_nb.astype(BF16), vbuf[:nkeys_nb, cs], preferred_element_type=F32)
              + jnp.dot(e_cx.astype(BF16), vbuf[nkeys_nb:, cs], preferred_element_type=F32))
        o_ref[:, cs] = (pv / den).astype(o_ref.dtype)


def _na_bias(rpb, rows):
    qr, qc = np.divmod(np.arange(NA_QROWS * GRID_W), GRID_W)
    kr, kc = np.divmod(np.arange(NA_KROWS * GRID_W), GRID_W)
    nblk = rows // NA_QROWS
    dr_l, dc_l, ok_l = [], [], []
    for m in (0, 1, nblk - 1):
        r = (NA_QROWS * m + qr)[:, None]
        start = np.clip(NA_QROWS * m - NA_KH // 2, 0, rows - NA_KROWS)
        kra = (start + kr)[None, :]
        r0 = np.clip(r - NA_KH // 2, 0, rows - NA_KH)
        row_ok = (kra >= r0) & (kra < r0 + NA_KH)
        col_start = np.clip(qc - NA_KW // 2, 0, GRID_W - NA_KW)[:, None]
        col_ok = (kc[None, :] >= col_start) & (kc[None, :] < col_start + NA_KW)
        dr_l.append(np.clip(kra - r + NA_KH - 1, 0, 2 * NA_KH - 2))
        dc_l.append(np.broadcast_to(np.clip(kc[None, :] - qc[:, None], -(NA_KW - 1), NA_KW - 1) + NA_KW - 1,
                                    row_ok.shape))
        ok_l.append(row_ok & col_ok)
    dr, dc, ok = np.stack(dr_l), np.stack(dc_l), np.stack(ok_l)
    bias = jnp.transpose(rpb[:, dr, dc], (1, 0, 2, 3))
    return jnp.where(ok[:, None], bias.astype(F32), -jnp.inf)


def _na_attn(z_na, k_ctx, v_ctx, bias, B, S):
    rows = S // GRID_W
    qb = NA_QROWS * GRID_W
    nblk = rows // NA_QROWS
    n_kblk = NA_KROWS // NA_QROWS
    lc = k_ctx.shape[1]
    nkeys_nb = NA_KROWS * GRID_W
    kstart = lambda m: jnp.clip(m - 1, 0, nblk - n_kblk)
    variant = lambda m: jnp.where(m == 0, 0, jnp.where(m == nblk - 1, 2, 1))
    kv_specs = lambda col: [pl.BlockSpec((qb, NA_W), functools.partial(
        lambda b, m, i, col: (b * nblk + kstart(m) + i, col), i=i, col=col)) for i in range(n_kblk)]
    return pl.pallas_call(
        functools.partial(_na_kernel, n_kblk=n_kblk, nkeys_nb=nkeys_nb),
        grid=(B, nblk),
        in_specs=([pl.BlockSpec((qb, NA_W), lambda b, m: (b * nblk + m, 0))] + kv_specs(1) + kv_specs(2)
                  + [pl.BlockSpec((1, lc, NA_W), lambda b, m: (b, 0, 0)),
                     pl.BlockSpec((1, lc, NA_W), lambda b, m: (b, 0, 0)),
                     pl.BlockSpec((1,) + bias.shape[1:], lambda b, m: (variant(m), 0, 0, 0))]),
        out_specs=pl.BlockSpec((qb, NA_W), lambda b, m: (b * nblk + m, 0)),
        out_shape=jax.ShapeDtypeStruct((B * S, NA_W), BF16),
        scratch_shapes=[pltpu.VMEM((nkeys_nb + lc, NA_W), BF16), pltpu.VMEM((nkeys_nb + lc, NA_W), BF16)],
        compiler_params=_cparams("parallel", "arbitrary"),
        name="na_attn",
    )(z_na, *([z_na] * (2 * n_kblk)), k_ctx, v_ctx, bias)


def _head_rms(o, g_row):
    parts = []
    for h in range(o.shape[1] // LANES):
        parts.append(_rms(o[:, h * LANES:(h + 1) * LANES]) * g_row)
    return jnp.concatenate(parts, axis=1)


def _merge_kernel(x_ref, oa_ref, hf_ref, hb_ref, hg_ref, gf_ref, gb_ref, gg_ref, od_ref, mg_ref,
                  hn_ref, gn_ref, wb_ref, wo_ref, g1_ref, o_ref):
    o_b = _head_rms(hf_ref[...] + hb_ref[...], hn_ref[...]) * _silu(hg_ref[...])
    o_c = _head_rms(gf_ref[...] + gb_ref[...], gn_ref[...]) * _silu(gg_ref[...])
    branches = (oa_ref[...], o_b.astype(BF16), o_c.astype(BF16), od_ref[...])
    merged = None
    for n_, o_n in enumerate(branches):
        gate = jax.nn.sigmoid(mg_ref[:, n_ * D_MODEL:(n_ + 1) * D_MODEL])
        term = gate * jnp.dot(o_n, wb_ref[n_], preferred_element_type=F32)
        merged = term if merged is None else merged + term
    mix = jnp.dot(merged.astype(BF16), wo_ref[...], preferred_element_type=F32)
    o_ref[...] = x_ref[...] + g1_ref[0] * mix


def _merge(x, o_a, o_hf, o_hb, z_hg, o_gf, o_gb, z_gd, o_d, z_mg, hg_onorm, gd_onorm, wb, wo, mods, mod_row, tm):
    T = x.shape[0]
    row = lambda w, col=0: pl.BlockSpec((tm, w), lambda i: (i, col))
    full = lambda a: pl.BlockSpec(a.shape, lambda i: (0,) * a.ndim)
    hn = hg_onorm.reshape(1, HG_DV)
    gn = gd_onorm.reshape(1, GD_DV)
    return pl.pallas_call(
        _merge_kernel,
        grid=(T // tm,),
        in_specs=[row(D_MODEL), row(BRANCH_W), row(BRANCH_W), row(BRANCH_W), row(BRANCH_W, 4),
                  row(BRANCH_W), row(BRANCH_W), row(BRANCH_W, GD_QKV // BRANCH_W), row(BRANCH_W),
                  row(N_BRANCH * D_MODEL), full(hn), full(gn), full(wb), full(wo),
                  pl.BlockSpec((1, 1, D_MODEL), lambda i: (mod_row(i * tm) * 6 + 2, 0, 0))],
        out_specs=row(D_MODEL),
        out_shape=jax.ShapeDtypeStruct((T, D_MODEL), F32),
        compiler_params=_cparams("parallel"),
        name="merge",
    )(x, o_a, o_hf, o_hb, z_hg, o_gf, o_gb, z_gd, o_d, z_mg, hn, gn, wb, wo, mods)


SC_CORES = 2
SC_SUBCORES = 16
SC_WIN = 32


def _sc_gather(table, idx):
    V, D = table.shape
    N = idx.shape[0]
    nw = SC_CORES * SC_SUBCORES
    per_w = N // nw
    n_win = per_w // SC_WIN
    assert per_w * nw == N and n_win * SC_WIN == per_w
    mesh = plsc.VectorSubcoreMesh(core_axis_name="c", subcore_axis_name="s")

    @functools.partial(
        pl.kernel, mesh=mesh,
        out_type=jax.ShapeDtypeStruct((N, D), table.dtype),
        scratch_types=[pltpu.VMEM((n_win, SC_WIN), jnp.int32),
                       pltpu.VMEM((SC_WIN, D), table.dtype),
                       pltpu.SemaphoreType.DMA],
    )
    def gather_rows(table_hbm, idx_hbm, out_hbm, idx_v, rows_v, sem):
        wid = lax.axis_index("s") * SC_CORES + lax.axis_index("c")
        pltpu.sync_copy(idx_hbm.at[wid], idx_v)

        @pl.loop(0, n_win)
        def _(w):
            pltpu.async_copy(table_hbm.at[idx_v.at[w]], rows_v, sem).wait()
            pltpu.sync_copy(rows_v, out_hbm.at[pl.ds(wid * per_w + w * SC_WIN, SC_WIN)])

    return gather_rows(table, idx.reshape(nw, n_win, SC_WIN))


def _router_kernel(x_ref, g_ref, sc_ref, sh_ref, wr_ref, br_ref, h_ref, e_ref, w_ref):
    h = (_rms(x_ref[...]) * g_ref[...]) * (1.0 + sc_ref[0]) + sh_ref[0]
    h_ref[...] = h.astype(h_ref.dtype)
    logits = _hdot(h, wr_ref[...]) + br_ref[...]
    lane = lax.broadcasted_iota(jnp.int32, logits.shape, 1)
    e_out = jnp.zeros(logits.shape, jnp.int32)
    v_out = jnp.zeros(logits.shape, F32)
    top0 = None
    for k in range(TOP_K):
        m = jnp.max(logits, axis=-1, keepdims=True)
        idx = jnp.min(jnp.where(logits == m, lane, LANES), axis=-1, keepdims=True)
        if k == 0:
            top0 = m
        e_out = jnp.where(lane == k, idx, e_out)
        v_out = jnp.where(lane == k, jnp.exp(m - top0), v_out)
        logits = jnp.where(lane == idx, -jnp.inf, logits)
    e_ref[...] = e_out
    w_ref[...] = v_out / jnp.sum(v_out, axis=-1, keepdims=True)


def _router(x, g, mods, mod_row, w_router, b_router, tm):
    T = x.shape[0]
    wr = jnp.zeros((D_MODEL, LANES), F32).at[:, :N_EXP].set(w_router)
    br = jnp.full((1, LANES), -jnp.inf, F32).at[0, :N_EXP].set(b_router)
    row = lambda w: pl.BlockSpec((tm, w), lambda i: (i, 0))
    return pl.pallas_call(
        _router_kernel,
        grid=(T // tm,),
        in_specs=[row(D_MODEL),
                  pl.BlockSpec((1, D_MODEL), lambda i: (0, 0)),
                  pl.BlockSpec((1, 1, D_MODEL), lambda i: (mod_row(i * tm) * 6 + 4, 0, 0)),
                  pl.BlockSpec((1, 1, D_MODEL), lambda i: (mod_row(i * tm) * 6 + 3, 0, 0)),
                  pl.BlockSpec((D_MODEL, LANES), lambda i: (0, 0)),
                  pl.BlockSpec((1, LANES), lambda i: (0, 0))],
        out_specs=[row(D_MODEL), row(LANES), row(LANES)],
        out_shape=[jax.ShapeDtypeStruct((T, D_MODEL), F32),
                   jax.ShapeDtypeStruct((T, LANES), jnp.int32),
                   jax.ShapeDtypeStruct((T, LANES), F32)],
        compiler_params=_cparams("parallel"),
        name="router",
    )(x, g.reshape(1, D_MODEL), mods, mods, wr, br)


def _expert_kernel(blk_e_ref, x_ref, wgu_ref, bgu_ref, wdn_ref, bdn_ref, o_ref):
    gu = jnp.dot(x_ref[...].astype(BF16), wgu_ref[0], preferred_element_type=F32) + bgu_ref[0]
    a = jnp.minimum(gu[:, :D_FF], SWIGLU_LIMIT)
    lin = jnp.clip(gu[:, D_FF:], -SWIGLU_LIMIT, SWIGLU_LIMIT)
    y = a * jax.nn.sigmoid(SWIGLU_ALPHA * a) * (lin + 1.0)
    o_ref[...] = jnp.dot(y.astype(BF16), wdn_ref[0], preferred_element_type=F32) + bdn_ref[0]


def _experts(xb, blk_e, w_gu, b_gu, w_dn, b_dn):
    n_pad = xb.shape[0]
    n_blocks = n_pad // MOE_BLOCK
    grid_spec = pltpu.PrefetchScalarGridSpec(
        num_scalar_prefetch=1,
        grid=(n_blocks,),
        in_specs=[pl.BlockSpec((MOE_BLOCK, D_MODEL), lambda i, e: (i, 0)),
                  pl.BlockSpec((1, D_MODEL, 2 * D_FF), lambda i, e: (e[i], 0, 0)),
                  pl.BlockSpec((1, 1, 2 * D_FF), lambda i, e: (e[i], 0, 0)),
                  pl.BlockSpec((1, D_FF, D_MODEL), lambda i, e: (e[i], 0, 0)),
                  pl.BlockSpec((1, 1, D_MODEL), lambda i, e: (e[i], 0, 0))],
        out_specs=pl.BlockSpec((MOE_BLOCK, D_MODEL), lambda i, e: (i, 0)),
    )
    return pl.pallas_call(
        _expert_kernel,
        grid_spec=grid_spec,
        out_shape=jax.ShapeDtypeStruct((n_pad, D_MODEL), F32),
        compiler_params=_cparams("arbitrary"),
        name="experts",
    )(blk_e, xb, w_gu, b_gu.reshape(N_EXP, 1, 2 * D_FF), w_dn, b_dn.reshape(N_EXP, 1, D_MODEL))


def _combine_kernel(x_ref, y_ref, w_ref, g2_ref, nf_ref, o_ref, *, final_norm):
    acc = None
    for k in range(TOP_K):
        term = y_ref[k] * w_ref[:, k:k + 1]
        acc = term if acc is None else acc + term
    x = x_ref[...] + g2_ref[0] * acc
    if final_norm:
        x = _rms(x) * nf_ref[...]
    o_ref[...] = x


def _combine(x, yg, wts, mods, mod_row, norm_f, final_norm, tm):
    T = x.shape[0]
    row = lambda w: pl.BlockSpec((tm, w), lambda i: (i, 0))
    return pl.pallas_call(
        functools.partial(_combine_kernel, final_norm=final_norm),
        grid=(T // tm,),
        in_specs=[row(D_MODEL), pl.BlockSpec((TOP_K, tm, D_MODEL), lambda i: (0, i, 0)), row(LANES),
                  pl.BlockSpec((1, 1, D_MODEL), lambda i: (mod_row(i * tm) * 6 + 5, 0, 0)),
                  pl.BlockSpec((1, D_MODEL), lambda i: (0, 0))],
        out_specs=row(D_MODEL),
        out_shape=jax.ShapeDtypeStruct((T, D_MODEL), F32),
        compiler_params=_cparams("parallel"),
        name="combine",
    )(x, yg, wts, mods, norm_f.reshape(1, D_MODEL))


def _route(top_e, T):
    n_assign = T * TOP_K
    n_blocks = n_assign // MOE_BLOCK + N_EXP
    e_flat = top_e.reshape(n_assign)
    onehot = e_flat[:, None] == jnp.arange(N_EXP, dtype=jnp.int32)[None, :]
    counts = jnp.sum(onehot, axis=0, dtype=jnp.int32)
    start = jnp.cumsum(counts) - counts
    padded = (counts + MOE_BLOCK - 1) // MOE_BLOCK * MOE_BLOCK
    pad_end = jnp.cumsum(padded)
    pad_start = pad_end - padded
    iota = jnp.arange(n_assign, dtype=jnp.int32)
    _, order = lax.sort((e_flat, iota), num_keys=1, is_stable=True)
    _, rank = lax.sort((order, iota), num_keys=1)
    pos = rank + jnp.sum(jnp.where(onehot, (pad_start - start)[None, :], 0), axis=1)
    blk_first = jnp.arange(n_blocks, dtype=jnp.int32) * MOE_BLOCK
    blk_e = jnp.minimum(jnp.sum(pad_end[None, :] <= blk_first[:, None], axis=1), N_EXP - 1).astype(jnp.int32)
    r = blk_first[:, None] - pad_start[blk_e][:, None] + jnp.arange(MOE_BLOCK, dtype=jnp.int32)[None, :]
    valid = r < counts[blk_e][:, None]
    src = jnp.clip(start[blk_e][:, None] + r, 0, n_assign - 1)
    tok = jnp.where(valid, order[src] // TOP_K, 0).reshape(n_blocks * MOE_BLOCK).astype(jnp.int32)
    return tok, pos, blk_e


def _prep_layer(l, w_in, sgu_w, w_branch, w_out, w_gu, w_dn, gd_A_log, gd_dt_bias, lb):
    offs = np.cumsum([0, BRANCH_W, BRANCH_W, 512, 512, 512, 512, 512, GD_QKV, 8, 8, 512, 3 * NA_W, N_BRANCH * D_MODEL])
    w = w_in[l]
    seg = lambda i, j: w[:, offs[i]:offs[j]]
    w_gd = jnp.concatenate([seg(7, 8), seg(10, 11), seg(8, 10),
                            jnp.zeros((D_MODEL, LANES - 4 * GD_HEADS), F32)], axis=1)
    par = jnp.zeros((2, LANES), F32)
    par = par.at[0, :2 * GD_HEADS].set(gd_A_log[l].reshape(-1)).at[1, :2 * GD_HEADS].set(gd_dt_bias[l].reshape(-1))
    lb_l = lb[:, l]
    return {
        'w_sgu': seg(0, 2).astype(BF16), 'w_hg': seg(2, 7).astype(BF16), 'w_gd': w_gd.astype(BF16),
        'w_na': seg(11, 12).astype(BF16), 'w_mg': seg(12, 13).astype(BF16),
        'sgu_w': sgu_w[l], 'wb': w_branch[l].astype(BF16), 'wo': w_out[l].astype(BF16),
        'w_gu': w_gu[l].astype(BF16), 'w_dn': w_dn[l].astype(BF16), 'gd_par': par,
        'lbp': jnp.concatenate([jnp.log(lb_l), jnp.log1p(-lb_l), 1.0 - lb_l], axis=0),
    }


def _layer(x, B, L, mods, mod_row, tm, lw, p, ctx, norm_f, final_norm):
    T = B * L
    h = _normmod(x, p['norm1'], mods, mod_row, tm, part_shift=0, part_scale=1)
    z_sgu = _matmul(h, lw['w_sgu'], 1024, 1024)
    z_hg = _matmul(h, lw['w_hg'], 1024, 1280)
    z_gd = _matmul(h, lw['w_gd'], 512, 2176)
    z_na = _matmul(h, lw['w_na'], 1024, 768)
    z_mg = _matmul(h, lw['w_mg'], 1024, 1024)

    o_a = _sgu(z_sgu, p['sgu_norm'], lw['sgu_w'], p['sgu_b'], 256)
    s_hg0 = None if ctx is None else ctx[2]
    s_gd0 = None if ctx is None else ctx[3]
    o_hf, o_hb, s_hg = _hgrn(z_hg, lw['lbp'], s_hg0, B, L)
    qkv = _gdprep(z_gd, p['gd_conv'], L, 256)
    o_gf, o_gb, s_gd = _gdn(qkv, z_gd, lw['gd_par'], s_gd0, B, L)
    if ctx is None:
        o_d = _ctx_attn(z_na, B, L)
    else:
        o_d = _na_attn(z_na, ctx[0], ctx[1], _na_bias(p['na_rpb'], L // GRID_W), B, L)
    x = _merge(x, o_a, o_hf, o_hb, z_hg, o_gf, o_gb, z_gd, o_d, z_mg, p['hg_onorm'], p['gd_onorm'],
               lw['wb'], lw['wo'], mods, mod_row, 256)

    h2, top_e, wts = _router(x, p['norm2'], mods, mod_row, p['w_router'], p['b_router'], 512)
    tok, pos, blk_e = _route(top_e[:, :TOP_K], T)
    xb = _sc_gather(h2, tok)
    yb = _experts(xb, blk_e, lw['w_gu'], p['b_gu'], lw['w_dn'], p['b_dn'])
    yg = _sc_gather(yb, pos.reshape(T, TOP_K).T.reshape(-1)).reshape(TOP_K, T, D_MODEL)
    x = _combine(x, yg, wts, mods, mod_row, norm_f, final_norm, 256)
    side = (z_na[:, NA_W:2 * NA_W].reshape(B, L, NA_HEADS, NA_DH),
            z_na[:, 2 * NA_W:].reshape(B, L, NA_HEADS, NA_DH), s_hg, s_gd)
    return x, side


def kernel(x_prompt, x_sample, c, cache_na_k, cache_na_v, state_hgrn, state_gdn, c_ctx, w_ada, b_ada, norm1, norm2, norm_f, w_in, sgu_norm, sgu_w, sgu_b, hg_lb, hg_onorm, gd_conv, gd_A_log, gd_dt_bias, gd_onorm, na_rpb, w_branch, w_out, w_router, b_router, w_gu, b_gu, w_dn, b_dn):
    Bp, Lp, D = x_prompt.shape
    Bs, Ls, _ = x_sample.shape
    ctx_row = Bs
    cvecs = jnp.zeros((MOD_ROWS, D), F32).at[:Bs].set(c).at[ctx_row].set(c_ctx)
    mods = _modulation(cvecs, w_ada, b_ada)

    cs = jnp.cumsum(jax.nn.softmax(hg_lb.astype(F32), axis=1), axis=1)
    lb = cs - cs[:, :1]

    tm = 1024
    xp = x_prompt.reshape(Bp * Lp, D)
    xs = x_sample.reshape(Bs * Ls, D)
    sides = []
    for l in range(DEPTH):
        lw = _prep_layer(l, w_in, sgu_w, w_branch, w_out, w_gu, w_dn, gd_A_log, gd_dt_bias, lb)
        p = {'norm1': norm1[l], 'norm2': norm2[l], 'sgu_norm': sgu_norm[l], 'sgu_b': sgu_b[l],
             'gd_conv': gd_conv[l], 'hg_onorm': hg_onorm[l], 'gd_onorm': gd_onorm[l], 'na_rpb': na_rpb[l],
             'w_router': w_router[l], 'b_router': b_router[l], 'b_gu': b_gu[l], 'b_dn': b_dn[l]}
        final = l == DEPTH - 1
        xp, side = _layer(xp, Bp, Lp, mods[l], lambda r: ctx_row, tm, lw, p, None, norm_f, final)
        sides.append(side)
        ctx = (cache_na_k[:, l].reshape(Bs, -1, NA_W), cache_na_v[:, l].reshape(Bs, -1, NA_W),
               state_hgrn[:, l], state_gdn[:, l])
        xs, _ = _layer(xs, Bs, Ls, mods[l], lambda r: r // Ls, tm, lw, p, ctx, norm_f, final)

    return (xp.reshape(Bp, Lp, D), xs.reshape(Bs, Ls, D),
            jnp.stack([s[0] for s in sides], axis=1), jnp.stack([s[1] for s in sides], axis=1),
            jnp.stack([s[2] for s in sides], axis=1), jnp.stack([s[3] for s in sides], axis=1))
```

```python
import functools
import math

import numpy as np
import jax
import jax.numpy as jnp
from jax import lax
from jax.experimental import pallas as pl
from jax.experimental.pallas import tpu as pltpu
from jax.experimental.pallas import tpu_sc as plsc

D_MODEL = 1024
DEPTH = 2
GRID_W = 64
BRANCH_W = 512
N_BRANCH = 4
SGU_CHUNK = 128
SGU_GROUPS = 4
HG_HEADS = 4
HG_DK = 128
HG_DV = 128
GD_HEADS = 4
GD_DK = 128
GD_DV = 128
NA_HEADS = 8
NA_DH = 64
NA_KH = 8
NA_KW = 16
N_EXP = 32
TOP_K = 4
D_FF = 1024
SWIGLU_LIMIT = 7.0
SWIGLU_ALPHA = 1.702
EPS = 1e-6

F32 = jnp.float32
BF16 = jnp.bfloat16
HI = lax.Precision.HIGHEST

LANES = 128
MOD_ROWS = 16
SCAN_C = 64
MOE_BLOCK = 256
NA_QROWS = 4
NA_KROWS = NA_QROWS + NA_KH
VMEM_LIMIT = 48 * 1024 * 1024

NT = (((1,), (1,)), ((), ()))
TN = (((0,), (0,)), ((), ()))


def _cparams(*sem):
    return pltpu.CompilerParams(dimension_semantics=sem, vmem_limit_bytes=VMEM_LIMIT)


def _bdot(a, b):
    return jnp.dot(a.astype(BF16), b.astype(BF16), preferred_element_type=F32)


def _bdot_g(a, b, dims):
    return lax.dot_general(a.astype(BF16), b.astype(BF16), dims, preferred_element_type=F32)


def _hdot(a, b):
    return jnp.dot(a, b, precision=HI, preferred_element_type=F32)


def _dot01(m, x):
    hi = x.astype(BF16)
    r1 = x - hi.astype(F32)
    mid = r1.astype(BF16)
    lo = (r1 - mid.astype(F32)).astype(BF16)
    d = lambda t: jnp.dot(m, t, preferred_element_type=F32)
    return (d(lo) + d(mid)) + d(hi)


def _dot3(a, b):
    ah = a.astype(BF16)
    al = (a - ah.astype(F32)).astype(BF16)
    bh = b.astype(BF16)
    bl = (b - bh.astype(F32)).astype(BF16)
    d = lambda x, y: jnp.dot(x, y, preferred_element_type=F32)
    return (d(al, bh) + d(ah, bl)) + d(ah, bh)


def _silu(x):
    return x * jax.nn.sigmoid(x)


def _log_sigmoid(x):
    return jnp.minimum(x, 0.0) - jnp.log1p(jnp.exp(-jnp.abs(x)))


def _logaddexp(a, b):
    return jnp.maximum(a, b) + jnp.log1p(jnp.exp(-jnp.abs(a - b)))


def _softplus(x):
    return jnp.maximum(x, 0.0) + jnp.log1p(jnp.exp(-jnp.abs(x)))


def _ada_kernel(c_ref, w_ref, b_ref, o_ref):
    o_ref[0] = _hdot(_silu(c_ref[...]), w_ref[0]) + b_ref[0]


def _modulation(cvecs, w_ada, b_ada):
    tn = 1536
    out = pl.pallas_call(
        _ada_kernel,
        grid=(DEPTH, 6 * D_MODEL // tn),
        in_specs=[pl.BlockSpec((MOD_ROWS, D_MODEL), lambda l, j: (0, 0)),
                  pl.BlockSpec((1, D_MODEL, tn), lambda l, j: (l, 0, j)),
                  pl.BlockSpec((1, 1, tn), lambda l, j: (l, 0, j))],
        out_specs=pl.BlockSpec((1, MOD_ROWS, tn), lambda l, j: (l, 0, j)),
        out_shape=jax.ShapeDtypeStruct((DEPTH, MOD_ROWS, 6 * D_MODEL), F32),
        compiler_params=_cparams("arbitrary", "arbitrary"),
        name="ada_modulation",
    )(cvecs, w_ada, b_ada.reshape(DEPTH, 1, 6 * D_MODEL))
    return out.reshape(DEPTH, MOD_ROWS * 6, 1, D_MODEL)


def _rms(x):
    return x * lax.rsqrt(jnp.mean(x * x, axis=-1, keepdims=True) + EPS)


def _normmod_kernel(x_ref, g_ref, sc_ref, sh_ref, o_ref):
    h = (_rms(x_ref[...]) * g_ref[...]) * (1.0 + sc_ref[0]) + sh_ref[0]
    o_ref[...] = h.astype(o_ref.dtype)


def _normmod(x, g, mods, mod_row, tm, part_shift, part_scale):
    T = x.shape[0]
    return pl.pallas_call(
        _normmod_kernel,
        grid=(T // tm,),
        in_specs=[pl.BlockSpec((tm, D_MODEL), lambda i: (i, 0)),
                  pl.BlockSpec((1, D_MODEL), lambda i: (0, 0)),
                  pl.BlockSpec((1, 1, D_MODEL), lambda i: (mod_row(i * tm) * 6 + part_scale, 0, 0)),
                  pl.BlockSpec((1, 1, D_MODEL), lambda i: (mod_row(i * tm) * 6 + part_shift, 0, 0))],
        out_specs=pl.BlockSpec((tm, D_MODEL), lambda i: (i, 0)),
        out_shape=jax.ShapeDtypeStruct((T, D_MODEL), BF16),
        compiler_params=_cparams("parallel"),
        name="normmod",
    )(x, g.reshape(1, D_MODEL), mods, mods)


def _mm_kernel(a_ref, w_ref, o_ref):
    o_ref[...] = jnp.dot(a_ref[...], w_ref[...], preferred_element_type=F32).astype(o_ref.dtype)


def _matmul(a, w, tm, tn, out_dtype=F32):
    T, K = a.shape
    N = w.shape[1]
    return pl.pallas_call(
        _mm_kernel,
        grid=(N // tn, T // tm),
        in_specs=[pl.BlockSpec((tm, K), lambda j, i: (i, 0)),
                  pl.BlockSpec((K, tn), lambda j, i: (0, j))],
        out_specs=pl.BlockSpec((tm, tn), lambda j, i: (i, j)),
        out_shape=jax.ShapeDtypeStruct((T, N), out_dtype),
        compiler_params=_cparams("parallel", "parallel"),
        name="in_proj",
    )(a, w)


def _sgu_kernel(u_ref, v_ref, gn_ref, ws_ref, bs_ref, o_ref):
    rows = u_ref.shape[0]
    gw = BRANCH_W // SGU_GROUPS
    u = jax.nn.gelu(u_ref[...])
    v = (_rms(jax.nn.gelu(v_ref[...])) * gn_ref[...]).astype(BF16)
    for n in range(rows // SGU_CHUNK):
        r = slice(n * SGU_CHUNK, (n + 1) * SGU_CHUNK)
        for g in range(SGU_GROUPS):
            cs = slice(g * gw, (g + 1) * gw)
            s = jnp.dot(ws_ref[g], v[r, cs], preferred_element_type=F32) + bs_ref[:, cs]
            o_ref[r, cs] = (u[r, cs] * s).astype(o_ref.dtype)


def _sgu(z_sgu, g_norm, w_s, b_s, rows):
    T = z_sgu.shape[0]
    gw = BRANCH_W // SGU_GROUPS
    b_exp = jnp.repeat(b_s.T, gw, axis=1)
    return pl.pallas_call(
        _sgu_kernel,
        grid=(T // rows,),
        in_specs=[pl.BlockSpec((rows, BRANCH_W), lambda i: (i, 0)),
                  pl.BlockSpec((rows, BRANCH_W), lambda i: (i, 1)),
                  pl.BlockSpec((1, BRANCH_W), lambda i: (0, 0)),
                  pl.BlockSpec((SGU_GROUPS, SGU_CHUNK, SGU_CHUNK), lambda i: (0, 0, 0)),
                  pl.BlockSpec((SGU_CHUNK, BRANCH_W), lambda i: (0, 0))],
        out_specs=pl.BlockSpec((rows, BRANCH_W), lambda i: (i, 0)),
        out_shape=jax.ShapeDtypeStruct((T, BRANCH_W), BF16),
        compiler_params=_cparams("parallel"),
        name="sgu",
    )(z_sgu, z_sgu, g_norm.reshape(1, BRANCH_W), w_s.astype(BF16), b_exp)


def _order(reverse):
    p = np.arange(SCAN_C)
    return SCAN_C - 1 - p if reverse else p


def _gla_consts():
    C = SCAN_C
    nlev = int(math.log2(C))
    mats, masks = [], []
    for reverse in (False, True):
        p = _order(reverse)
        pt, pr = p[:, None], p[None, :]
        m_d, k_d = [], []
        for lev in range(nlev):
            w = C >> (lev + 1)
            parent = p // (2 * w)
            later = (p % (2 * w)) >= w
            anchor = (parent * 2 * w + w - 1)[:, None]
            m = np.where(later[:, None], (pr > anchor) & (pr <= pt), (pr > pt) & (pr <= anchor))
            m_d.append(m)
            k_d.append((parent[:, None] == parent[None, :]) & later[:, None] & ~later[None, :])
        m_d.append(pr <= pt)
        m_d.append(pr > pt)
        k_d.append(np.eye(C, dtype=bool))
        mats.append(np.concatenate(m_d, axis=0))
        masks.append(np.stack(k_d))
    return (np.stack(mats).astype(np.float32), np.stack(masks).astype(np.float32))


def _delta_consts():
    C = SCAN_C
    tri, sl, incl, strict = [], [], [], []
    for reverse in (False, True):
        p = _order(reverse)
        pt, pr = p[:, None], p[None, :]
        tri.append(np.concatenate([pr <= pt, pr > pt], axis=0))
        sl.append(np.concatenate([pt > pr, np.zeros((C, LANES - C), bool), np.ones((C, LANES), bool)], axis=1))
        incl.append(pr <= pt)
        strict.append(pr < pt)
    f = lambda a: np.stack(a).astype(np.float32)
    return f(tri), f(sl), f(incl), f(strict)


def _hgrn_kernel(qf_ref, ff_ref, vf_ref, qb_ref, fb_ref, vb_ref, lb_ref, mat_ref, msk_ref, *rest, has_state):
    if has_state:
        s0_ref, of_ref, ob_ref, sfin_ref, st_ref = rest
    else:
        of_ref, ob_ref, sfin_ref, st_ref = rest
    C = SCAN_C
    nlev = msk_ref.shape[1] - 1
    c = pl.program_id(1)
    last_c = pl.num_programs(1) - 1

    @pl.when(c == 0)
    def _():
        for d in range(2):
            for h in range(HG_HEADS):
                if has_state:
                    st_ref[d, h] = s0_ref[0, d, h].T
                else:
                    st_ref[d, h] = jnp.zeros((HG_DV, HG_DK), F32)

    q_refs, f_refs, v_refs, o_refs = (qf_ref, qb_ref), (ff_ref, fb_ref), (vf_ref, vb_ref), (of_ref, ob_ref)
    chains = [(d, h) for d in range(2) for h in range(HG_HEADS)]
    col = lambda h: slice(h * HG_DK, (h + 1) * HG_DK)
    q, k, fac, att = {}, {}, {}, {}
    for d, h in chains:
        zf = f_refs[d][:, col(h)]
        loglb = lb_ref[d:d + 1, col(h)]
        log1m = lb_ref[2 + d:3 + d, col(h)]
        onem = lb_ref[4 + d:5 + d, col(h)]
        logf = _logaddexp(loglb, log1m + _log_sigmoid(zf))
        k[d, h] = onem * jax.nn.sigmoid(-zf)
        q[d, h] = _silu(q_refs[d][:, col(h)]) * (HG_DK ** -0.5)
        fac[d, h] = jnp.exp(_dot01(mat_ref[d], logf))
    for d, h in chains:
        acc = msk_ref[d, nlev] * _bdot_g(q[d, h], k[d, h], NT)
        for i in range(nlev):
            fi = fac[d, h][i * C:(i + 1) * C]
            acc = acc + msk_ref[d, i] * _bdot_g(q[d, h] * fi, k[d, h] * fi, NT)
        att[d, h] = acc
    for d, h in chains:
        eb = fac[d, h][nlev * C:(nlev + 1) * C]
        o_refs[d][:, col(h)] = (_bdot(att[d, h], v_refs[d][:, col(h)])
                                + _bdot_g(q[d, h] * eb, st_ref[d, h], NT))
    for d, h in chains:
        eb = fac[d, h][nlev * C:(nlev + 1) * C]
        er = fac[d, h][(nlev + 1) * C:]
        e_last = eb[C - 1:C] if d == 0 else eb[0:1]
        st_ref[d, h] = st_ref[d, h] * e_last + _bdot_g(v_refs[d][:, col(h)], k[d, h] * er, TN)

    @pl.when(c == last_c)
    def _():
        for d in range(2):
            for h in range(HG_HEADS):
                sfin_ref[0, d, h] = st_ref[d, h].T


def _hgrn(z_hg, lbp, s0, B, L):
    n = L // SCAN_C
    mats, masks = _gla_consts()
    blk = (SCAN_C, HG_HEADS * HG_DK)
    fwd = lambda col: pl.BlockSpec(blk, lambda b, c: (b * n + c, col))
    bwd = lambda col: pl.BlockSpec(blk, lambda b, c: (b * n + n - 1 - c, col))
    st_blk = (1, 2, HG_HEADS, HG_DK, HG_DV)
    in_specs = [fwd(0), fwd(1), fwd(3), bwd(0), bwd(2), bwd(3),
                pl.BlockSpec(lbp.shape, lambda b, c: (0, 0)),
                pl.BlockSpec(mats.shape, lambda b, c: (0, 0, 0)),
                pl.BlockSpec(masks.shape, lambda b, c: (0, 0, 0, 0))]
    args = [z_hg] * 6 + [lbp, jnp.asarray(mats, BF16), jnp.asarray(masks)]
    if s0 is not None:
        in_specs.append(pl.BlockSpec(st_blk, lambda b, c: (b, 0, 0, 0, 0)))
        args.append(s0)
    return pl.pallas_call(
        functools.partial(_hgrn_kernel, has_state=s0 is not None),
        grid=(B, n),
        in_specs=in_specs,
        out_specs=[pl.BlockSpec(blk, lambda b, c: (b * n + c, 0)),
                   pl.BlockSpec(blk, lambda b, c: (b * n + n - 1 - c, 0)),
                   pl.BlockSpec(st_blk, lambda b, c: (b, 0, 0, 0, 0))],
        out_shape=[jax.ShapeDtypeStruct((B * L, HG_HEADS * HG_DV), F32),
                   jax.ShapeDtypeStruct((B * L, HG_HEADS * HG_DV), F32),
                   jax.ShapeDtypeStruct((B,) + st_blk[1:], F32)],
        scratch_shapes=[pltpu.VMEM((2, HG_HEADS, HG_DV, HG_DK), F32)],
        compiler_params=_cparams("parallel", "arbitrary"),
        name="hgrn_scan",
    )(*args)


GD_NQ = GD_HEADS * GD_DK
GD_QKV = 2 * GD_NQ + GD_HEADS * GD_DV
HALO = 8


def _gdprep_kernel(x_ref, prev_ref, next_ref, w_ref, o_ref, *, tiles_per_seq):
    R = x_ref.shape[0]
    t = pl.program_id(0) % tiles_per_seq
    x = x_ref[...]
    prev_row = jnp.where(t == 0, 0.0, prev_ref[HALO - 1:HALO, :])
    next_row = jnp.where(t == tiles_per_seq - 1, 0.0, next_ref[0:1, :])
    row = lax.broadcasted_iota(jnp.int32, x.shape, 0)
    xm1 = jnp.where(row == 0, prev_row, pltpu.roll(x, 1, 0))
    xp1 = jnp.where(row == R - 1, next_row, pltpu.roll(x, R - 1, 0))
    y = _silu(w_ref[0:1, :] * xm1 + w_ref[1:2, :] * x + w_ref[2:3, :] * xp1)
    for j in range(2 * GD_HEADS):
        cs = slice(j * GD_DK, (j + 1) * GD_DK)
        seg = y[:, cs]
        seg = seg * lax.rsqrt(jnp.sum(seg * seg, axis=-1, keepdims=True) + EPS)
        if j < GD_HEADS:
            seg = seg * (GD_DK ** -0.5)
        o_ref[:, cs] = seg
    o_ref[:, 2 * GD_NQ:] = y[:, 2 * GD_NQ:]


def _gdprep(z_gd, conv_w, L, rows):
    T = z_gd.shape[0]
    tps = L // rows
    hb = rows // HALO
    nhalo = T // HALO
    return pl.pallas_call(
        functools.partial(_gdprep_kernel, tiles_per_seq=tps),
        grid=(T // rows,),
        in_specs=[pl.BlockSpec((rows, GD_QKV), lambda i: (i, 0)),
                  pl.BlockSpec((HALO, GD_QKV), lambda i: (jnp.maximum(i * hb - 1, 0), 0)),
                  pl.BlockSpec((HALO, GD_QKV), lambda i: (jnp.minimum((i + 1) * hb, nhalo - 1), 0)),
                  pl.BlockSpec((3, GD_QKV), lambda i: (0, 0))],
        out_specs=pl.BlockSpec((rows, GD_QKV), lambda i: (i, 0)),
        out_shape=jax.ShapeDtypeStruct((T, GD_QKV), F32),
        compiler_params=_cparams("parallel"),
        name="gdn_prep",
    )(z_gd, z_gd, z_gd, conv_w)


def _gdn_kernel(xf_ref, abf_ref, xb_ref, abb_ref, par_ref, tri_ref, sl_ref, incl_ref, strict_ref, msk_ref, *rest,
                has_state):
    if has_state:
        s0_ref, of_ref, ob_ref, sfin_ref, st_ref = rest
    else:
        of_ref, ob_ref, sfin_ref, st_ref = rest
    C = SCAN_C
    c = pl.program_id(1)
    last_c = pl.num_programs(1) - 1
    nlev = msk_ref.shape[1] - 1

    @pl.when(c == 0)
    def _():
        for d in range(2):
            for h in range(GD_HEADS):
                if has_state:
                    st_ref[d, h] = s0_ref[0, d, h].T
                else:
                    st_ref[d, h] = jnp.zeros((GD_DV, GD_DK), F32)

    x_refs, ab_refs, o_refs = (xf_ref, xb_ref), (abf_ref, abb_ref), (of_ref, ob_ref)
    chains = [(d, h) for d in range(2) for h in range(GD_HEADS)]
    q_of = lambda d, h: x_refs[d][:, h * GD_DK:(h + 1) * GD_DK]
    k_of = lambda d, h: x_refs[d][:, GD_NQ + h * GD_DK:GD_NQ + (h + 1) * GD_DK]
    v_of = lambda d, h: x_refs[d][:, 2 * GD_NQ + h * GD_DV:2 * GD_NQ + (h + 1) * GD_DV]
    g_all, beta_all = [], []
    for d in range(2):
        ab = ab_refs[d][...]
        g_all.append(-jnp.exp(par_ref[0:1, :]) * _softplus(ab + par_ref[1:2, :]))
        beta_all.append(jax.nn.sigmoid(ab))
    decay, e_cum, e_rest, kb, a = {}, {}, {}, {}, {}
    for d, h in chains:
        j = d * GD_HEADS + h
        g_b = jnp.broadcast_to(g_all[d][:, j:j + 1], (C, 2 * LANES))
        sums = _dot01(tri_ref[d], g_b * sl_ref[d])
        decay[d, h] = jnp.exp(sums[:C, :C])
        e_cum[d, h] = jnp.exp(sums[:C, LANES:])
        e_rest[d, h] = jnp.exp(sums[C:, LANES:])
    for d, h in chains:
        j = d * GD_HEADS + h
        beta = jnp.broadcast_to(beta_all[d][:, 2 * GD_HEADS + j:2 * GD_HEADS + j + 1], (C, LANES))
        k = k_of(d, h)
        kb[d, h] = k * beta
        a[d, h] = strict_ref[d] * decay[d, h] * _bdot_g(kb[d, h], k, NT)
    inv_m = {(d, h): -(msk_ref[d, nlev - 1] * a[d, h]) for d, h in chains}
    for lev in range(nlev - 2, -1, -1):
        a_w = {(d, h): msk_ref[d, lev] * a[d, h] for d, h in chains}
        p = {ch: a_w[ch] + _bdot(inv_m[ch], a_w[ch]) for ch in chains}
        inv_m = {ch: inv_m[ch] - p[ch] - _bdot(p[ch], inv_m[ch]) for ch in chains}
    sol = {}
    for d, h in chains:
        j = d * GD_HEADS + h
        beta = jnp.broadcast_to(beta_all[d][:, 2 * GD_HEADS + j:2 * GD_HEADS + j + 1], (C, LANES))
        rhs = jnp.concatenate([v_of(d, h) * beta, kb[d, h] * e_cum[d, h]], axis=1)
        sol[d, h] = rhs + _dot3(inv_m[d, h], rhs)
    u = {ch: sol[ch][:, :GD_DV] - _bdot_g(sol[ch][:, GD_DV:], st_ref[ch[0], ch[1]], NT) for ch in chains}
    att = {(d, h): incl_ref[d] * decay[d, h] * _bdot_g(q_of(d, h), k_of(d, h), NT) for d, h in chains}
    for d, h in chains:
        o_refs[d][:, h * GD_DV:(h + 1) * GD_DV] = (_bdot_g(q_of(d, h) * e_cum[d, h], st_ref[d, h], NT)
                                                   + _bdot(att[d, h], u[d, h]))
    for d, h in chains:
        e_last = e_cum[d, h][C - 1:C] if d == 0 else e_cum[d, h][0:1]
        st_ref[d, h] = st_ref[d, h] * e_last + _bdot_g(u[d, h], k_of(d, h) * e_rest[d, h], TN)

    @pl.when(c == last_c)
    def _():
        for d in range(2):
            for h in range(GD_HEADS):
                sfin_ref[0, d, h] = st_ref[d, h].T


def _gdn(qkv, z_gd, par, s0, B, L):
    n = L // SCAN_C
    tri, sl, incl, strict = (jnp.asarray(a) for a in _delta_consts())
    tri = tri.astype(BF16)
    masks = jnp.asarray(_gla_consts()[1])
    ab_col = (GD_QKV + GD_HEADS * GD_DV) // LANES
    xblk = (SCAN_C, GD_QKV)
    ablk = (SCAN_C, LANES)
    oblk = (SCAN_C, GD_HEADS * GD_DV)
    st_blk = (1, 2, GD_HEADS, GD_DK, GD_DV)
    fwd = lambda b, c: b * n + c
    bwd = lambda b, c: b * n + n - 1 - c
    const3 = lambda a: pl.BlockSpec(a.shape, lambda b, c: (0, 0, 0))
    in_specs = [pl.BlockSpec(xblk, lambda b, c: (fwd(b, c), 0)),
                pl.BlockSpec(ablk, lambda b, c: (fwd(b, c), ab_col)),
                pl.BlockSpec(xblk, lambda b, c: (bwd(b, c), 0)),
                pl.BlockSpec(ablk, lambda b, c: (bwd(b, c), ab_col)),
                pl.BlockSpec(par.shape, lambda b, c: (0, 0)),
                const3(tri), const3(sl), const3(incl), const3(strict),
                pl.BlockSpec(masks.shape, lambda b, c: (0, 0, 0, 0))]
    args = [qkv, z_gd, qkv, z_gd, par, tri, sl, incl, strict, masks]
    if s0 is not None:
        in_specs.append(pl.BlockSpec(st_blk, lambda b, c: (b, 0, 0, 0, 0)))
        args.append(s0)
    return pl.pallas_call(
        functools.partial(_gdn_kernel, has_state=s0 is not None),
        grid=(B, n),
        in_specs=in_specs,
        out_specs=[pl.BlockSpec(oblk, lambda b, c: (fwd(b, c), 0)),
                   pl.BlockSpec(oblk, lambda b, c: (bwd(b, c), 0)),
                   pl.BlockSpec(st_blk, lambda b, c: (b, 0, 0, 0, 0))],
        out_shape=[jax.ShapeDtypeStruct((B * L, GD_HEADS * GD_DV), F32),
                   jax.ShapeDtypeStruct((B * L, GD_HEADS * GD_DV), F32),
                   jax.ShapeDtypeStruct((B,) + st_blk[1:], F32)],
        scratch_shapes=[pltpu.VMEM((2, GD_HEADS, GD_DV, GD_DK), F32)],
        compiler_params=_cparams("parallel", "arbitrary"),
        name="gdn_scan",
    )(*args)


NA_W = NA_HEADS * NA_DH


def _softmax_pv(s, v):
    m = jnp.max(s, axis=-1, keepdims=True)
    e = jnp.exp(s - m)
    den = jnp.sum(e, axis=-1, keepdims=True)
    return jnp.dot(e.astype(BF16), v, preferred_element_type=F32) / den


def _ctx_attn_kernel(q_ref, k_ref, v_ref, o_ref):
    for h in range(NA_HEADS):
        cs = slice(h * NA_DH, (h + 1) * NA_DH)
        q = (q_ref[:, cs] * (NA_DH ** -0.5)).astype(BF16)
        s = lax.dot_general(q, k_ref[:, cs].astype(BF16), NT, preferred_element_type=F32)
        o_ref[:, cs] = _softmax_pv(s, v_ref[:, cs].astype(BF16)).astype(o_ref.dtype)


def _ctx_attn(z_na, B, L):
    blk = (L, NA_W)
    return pl.pallas_call(
        _ctx_attn_kernel,
        grid=(B,),
        in_specs=[pl.BlockSpec(blk, lambda b: (b, 0)),
                  pl.BlockSpec(blk, lambda b: (b, 1)),
                  pl.BlockSpec(blk, lambda b: (b, 2))],
        out_specs=pl.BlockSpec(blk, lambda b: (b, 0)),
        out_shape=jax.ShapeDtypeStruct((B * L, NA_W), BF16),
        compiler_params=_cparams("parallel"),
        name="ctx_attn",
    )(z_na, z_na, z_na)


def _na_kernel(q_ref, *rest, n_kblk, nkeys_nb):
    k_refs = rest[:n_kblk]
    v_refs = rest[n_kblk:2 * n_kblk]
    kc_ref, vc_ref, bias_ref, o_ref, kbuf, vbuf = rest[2 * n_kblk:]
    qb = q_ref.shape[0]
    for i in range(n_kblk):
        kbuf[i * qb:(i + 1) * qb, :] = k_refs[i][...].astype(BF16)
        vbuf[i * qb:(i + 1) * qb, :] = v_refs[i][...].astype(BF16)
    kbuf[nkeys_nb:, :] = kc_ref[0].astype(BF16)
    vbuf[nkeys_nb:, :] = vc_ref[0].astype(BF16)
    for h in range(NA_HEADS):
        cs = slice(h * NA_DH, (h + 1) * NA_DH)
        q = (q_ref[:, cs] * (NA_DH ** -0.5)).astype(BF16)
        s = lax.dot_general(q, kbuf[:, cs], NT, preferred_element_type=F32)
        s_nb = s[:, :nkeys_nb] + bias_ref[0, h]
        s_cx = s[:, nkeys_nb:]
        m = jnp.maximum(jnp.max(s_nb, axis=-1, keepdims=True), jnp.max(s_cx, axis=-1, keepdims=True))
        e_nb = jnp.exp(s_nb - m)
        e_cx = jnp.exp(s_cx - m)
        den = jnp.sum(e_nb, axis=-1, keepdims=True) + jnp.sum(e_cx, axis=-1, keepdims=True)
        pv = (jnp.dot(e_nb.astype(BF16), vbuf[:nkeys_nb, cs], preferred_element_type=F32)
              + jnp.dot(e_cx.astype(BF16), vbuf[nkeys_nb:, cs], preferred_element_type=F32))
        o_ref[:, cs] = (pv / den).astype(o_ref.dtype)


def _na_bias(rpb, rows):
    qr, kr, col = np.arange(NA_QROWS), np.arange(NA_KROWS), np.arange(GRID_W)
    nblk = rows // NA_QROWS
    ndr, ndc = 2 * NA_KH - 1, 2 * NA_KW - 1
    sel_r, row_ok = [], []
    for m in (0, 1, nblk - 1):
        r = (NA_QROWS * m + qr)[:, None]
        start = np.clip(NA_QROWS * m - NA_KH // 2, 0, rows - NA_KROWS)
        kra = (start + kr)[None, :]
        r0 = np.clip(r - NA_KH // 2, 0, rows - NA_KH)
        row_ok.append((kra >= r0) & (kra < r0 + NA_KH))
        dr = np.clip(kra - r + NA_KH - 1, 0, ndr - 1)
        sel_r.append(dr[..., None] == np.arange(ndr))
    sel_r = np.stack(sel_r).astype(np.float32)
    row_ok = np.stack(row_ok)
    col_start = np.clip(col - NA_KW // 2, 0, GRID_W - NA_KW)[:, None]
    col_ok = (col[None, :] >= col_start) & (col[None, :] < col_start + NA_KW)
    dc = np.clip(col[None, :] - col[:, None], -(NA_KW - 1), NA_KW - 1) + NA_KW - 1
    sel_c = (dc[..., None] == np.arange(ndc)).astype(np.float32)
    bias = jnp.einsum('hab,vqka,xyb->vhqxky', rpb.astype(F32), sel_r, sel_c, precision=HI)
    ok = row_ok[:, None, :, None, :, None] & col_ok[None, None, None, :, None, :]
    bias = jnp.where(ok, bias, -jnp.inf)
    return bias.reshape(3, NA_HEADS, NA_QROWS * GRID_W, NA_KROWS * GRID_W)


def _na_attn(z_na, k_ctx, v_ctx, bias, B, S):
    rows = S // GRID_W
    qb = NA_QROWS * GRID_W
    nblk = rows // NA_QROWS
    n_kblk = NA_KROWS // NA_QROWS
    lc = k_ctx.shape[1]
    nkeys_nb = NA_KROWS * GRID_W
    kstart = lambda m: jnp.clip(m - 1, 0, nblk - n_kblk)
    variant = lambda m: jnp.where(m == 0, 0, jnp.where(m == nblk - 1, 2, 1))
    kv_specs = lambda col: [pl.BlockSpec((qb, NA_W), functools.partial(
        lambda b, m, i, col: (b * nblk + kstart(m) + i, col), i=i, col=col)) for i in range(n_kblk)]
    return pl.pallas_call(
        functools.partial(_na_kernel, n_kblk=n_kblk, nkeys_nb=nkeys_nb),
        grid=(B, nblk),
        in_specs=([pl.BlockSpec((qb, NA_W), lambda b, m: (b * nblk + m, 0))] + kv_specs(1) + kv_specs(2)
                  + [pl.BlockSpec((1, lc, NA_W), lambda b, m: (b, 0, 0)),
                     pl.BlockSpec((1, lc, NA_W), lambda b, m: (b, 0, 0)),
                     pl.BlockSpec((1,) + bias.shape[1:], lambda b, m: (variant(m), 0, 0, 0))]),
        out_specs=pl.BlockSpec((qb, NA_W), lambda b, m: (b * nblk + m, 0)),
        out_shape=jax.ShapeDtypeStruct((B * S, NA_W), BF16),
        scratch_shapes=[pltpu.VMEM((nkeys_nb + lc, NA_W), BF16), pltpu.VMEM((nkeys_nb + lc, NA_W), BF16)],
        compiler_params=_cparams("parallel", "arbitrary"),
        name="na_attn",
    )(z_na, *([z_na] * (2 * n_kblk)), k_ctx, v_ctx, bias)


def _head_rms(o, g_row):
    parts = []
    for h in range(o.shape[1] // LANES):
        parts.append(_rms(o[:, h * LANES:(h + 1) * LANES]) * g_row)
    return jnp.concatenate(parts, axis=1)


def _merge_kernel(x_ref, oa_ref, hf_ref, hb_ref, hg_ref, gf_ref, gb_ref, gg_ref, od_ref, mg_ref,
                  hn_ref, gn_ref, wb_ref, wo_ref, g1_ref, o_ref):
    o_b = _head_rms(hf_ref[...] + hb_ref[...], hn_ref[...]) * _silu(hg_ref[...])
    o_c = _head_rms(gf_ref[...] + gb_ref[...], gn_ref[...]) * _silu(gg_ref[...])
    branches = (oa_ref[...], o_b.astype(BF16), o_c.astype(BF16), od_ref[...])
    merged = None
    for n_, o_n in enumerate(branches):
        gate = jax.nn.sigmoid(mg_ref[:, n_ * D_MODEL:(n_ + 1) * D_MODEL])
        term = gate * jnp.dot(o_n, wb_ref[n_], preferred_element_type=F32)
        merged = term if merged is None else merged + term
    mix = jnp.dot(merged.astype(BF16), wo_ref[...], preferred_element_type=F32)
    o_ref[...] = x_ref[...] + g1_ref[0] * mix


def _merge(x, o_a, o_hf, o_hb, z_hg, o_gf, o_gb, z_gd, o_d, z_mg, hg_onorm, gd_onorm, wb, wo, mods, mod_row, tm):
    T = x.shape[0]
    row = lambda w, col=0: pl.BlockSpec((tm, w), lambda i: (i, col))
    full = lambda a: pl.BlockSpec(a.shape, lambda i: (0,) * a.ndim)
    hn = hg_onorm.reshape(1, HG_DV)
    gn = gd_onorm.reshape(1, GD_DV)
    return pl.pallas_call(
        _merge_kernel,
        grid=(T // tm,),
        in_specs=[row(D_MODEL), row(BRANCH_W), row(BRANCH_W), row(BRANCH_W), row(BRANCH_W, 4),
                  row(BRANCH_W), row(BRANCH_W), row(BRANCH_W, GD_QKV // BRANCH_W), row(BRANCH_W),
                  row(N_BRANCH * D_MODEL), full(hn), full(gn), full(wb), full(wo),
                  pl.BlockSpec((1, 1, D_MODEL), lambda i: (mod_row(i * tm) * 6 + 2, 0, 0))],
        out_specs=row(D_MODEL),
        out_shape=jax.ShapeDtypeStruct((T, D_MODEL), F32),
        compiler_params=_cparams("parallel"),
        name="merge",
    )(x, o_a, o_hf, o_hb, z_hg, o_gf, o_gb, z_gd, o_d, z_mg, hn, gn, wb, wo, mods)


SC_CORES = 2
SC_SUBCORES = 16
SC_WIN = 32


def _sc_gather(table, idx):
    V, D = table.shape
    N = idx.shape[0]
    nw = SC_CORES * SC_SUBCORES
    per_w = N // nw
    n_win = per_w // SC_WIN
    assert per_w * nw == N and n_win * SC_WIN == per_w
    mesh = plsc.VectorSubcoreMesh(core_axis_name="c", subcore_axis_name="s")

    @functools.partial(
        pl.kernel, mesh=mesh,
        out_type=jax.ShapeDtypeStruct((N, D), table.dtype),
        scratch_types=[pltpu.VMEM((n_win, SC_WIN), jnp.int32),
                       pltpu.VMEM((SC_WIN, D), table.dtype),
                       pltpu.SemaphoreType.DMA],
    )
    def gather_rows(table_hbm, idx_hbm, out_hbm, idx_v, rows_v, sem):
        wid = lax.axis_index("s") * SC_CORES + lax.axis_index("c")
        pltpu.sync_copy(idx_hbm.at[wid], idx_v)

        @pl.loop(0, n_win)
        def _(w):
            pltpu.async_copy(table_hbm.at[idx_v.at[w]], rows_v, sem).wait()
            pltpu.sync_copy(rows_v, out_hbm.at[pl.ds(wid * per_w + w * SC_WIN, SC_WIN)])

    return gather_rows(table, idx.reshape(nw, n_win, SC_WIN))


def _router_kernel(x_ref, g_ref, sc_ref, sh_ref, wr_ref, br_ref, h_ref, e_ref, w_ref):
    h = (_rms(x_ref[...]) * g_ref[...]) * (1.0 + sc_ref[0]) + sh_ref[0]
    h_ref[...] = h.astype(h_ref.dtype)
    logits = _hdot(h, wr_ref[...]) + br_ref[...]
    lane = lax.broadcasted_iota(jnp.int32, logits.shape, 1)
    e_out = jnp.zeros(logits.shape, jnp.int32)
    v_out = jnp.zeros(logits.shape, F32)
    top0 = None
    for k in range(TOP_K):
        m = jnp.max(logits, axis=-1, keepdims=True)
        idx = jnp.min(jnp.where(logits == m, lane, LANES), axis=-1, keepdims=True)
        if k == 0:
            top0 = m
        e_out = jnp.where(lane == k, idx, e_out)
        v_out = jnp.where(lane == k, jnp.exp(m - top0), v_out)
        logits = jnp.where(lane == idx, -jnp.inf, logits)
    e_ref[...] = e_out
    w_ref[...] = v_out / jnp.sum(v_out, axis=-1, keepdims=True)


def _router(x, g, mods, mod_row, w_router, b_router, tm):
    T = x.shape[0]
    wr = jnp.zeros((D_MODEL, LANES), F32).at[:, :N_EXP].set(w_router)
    br = jnp.full((1, LANES), -jnp.inf, F32).at[0, :N_EXP].set(b_router)
    row = lambda w: pl.BlockSpec((tm, w), lambda i: (i, 0))
    return pl.pallas_call(
        _router_kernel,
        grid=(T // tm,),
        in_specs=[row(D_MODEL),
                  pl.BlockSpec((1, D_MODEL), lambda i: (0, 0)),
                  pl.BlockSpec((1, 1, D_MODEL), lambda i: (mod_row(i * tm) * 6 + 4, 0, 0)),
                  pl.BlockSpec((1, 1, D_MODEL), lambda i: (mod_row(i * tm) * 6 + 3, 0, 0)),
                  pl.BlockSpec((D_MODEL, LANES), lambda i: (0, 0)),
                  pl.BlockSpec((1, LANES), lambda i: (0, 0))],
        out_specs=[row(D_MODEL), row(LANES), row(LANES)],
        out_shape=[jax.ShapeDtypeStruct((T, D_MODEL), F32),
                   jax.ShapeDtypeStruct((T, LANES), jnp.int32),
                   jax.ShapeDtypeStruct((T, LANES), F32)],
        compiler_params=_cparams("parallel"),
        name="router",
    )(x, g.reshape(1, D_MODEL), mods, mods, wr, br)


def _expert_kernel(blk_e_ref, x_ref, wgu_ref, bgu_ref, wdn_ref, bdn_ref, o_ref):
    gu = jnp.dot(x_ref[...].astype(BF16), wgu_ref[0], preferred_element_type=F32) + bgu_ref[0]
    a = jnp.minimum(gu[:, :D_FF], SWIGLU_LIMIT)
    lin = jnp.clip(gu[:, D_FF:], -SWIGLU_LIMIT, SWIGLU_LIMIT)
    y = a * jax.nn.sigmoid(SWIGLU_ALPHA * a) * (lin + 1.0)
    o_ref[...] = jnp.dot(y.astype(BF16), wdn_ref[0], preferred_element_type=F32) + bdn_ref[0]


def _experts(xb, blk_e, w_gu, b_gu, w_dn, b_dn):
    n_pad = xb.shape[0]
    n_blocks = n_pad // MOE_BLOCK
    grid_spec = pltpu.PrefetchScalarGridSpec(
        num_scalar_prefetch=1,
        grid=(n_blocks,),
        in_specs=[pl.BlockSpec((MOE_BLOCK, D_MODEL), lambda i, e: (i, 0)),
                  pl.BlockSpec((1, D_MODEL, 2 * D_FF), lambda i, e: (e[i], 0, 0)),
                  pl.BlockSpec((1, 1, 2 * D_FF), lambda i, e: (e[i], 0, 0)),
                  pl.BlockSpec((1, D_FF, D_MODEL), lambda i, e: (e[i], 0, 0)),
                  pl.BlockSpec((1, 1, D_MODEL), lambda i, e: (e[i], 0, 0))],
        out_specs=pl.BlockSpec((MOE_BLOCK, D_MODEL), lambda i, e: (i, 0)),
    )
    return pl.pallas_call(
        _expert_kernel,
        grid_spec=grid_spec,
        out_shape=jax.ShapeDtypeStruct((n_pad, D_MODEL), F32),
        compiler_params=_cparams("arbitrary"),
        name="experts",
    )(blk_e, xb, w_gu, b_gu.reshape(N_EXP, 1, 2 * D_FF), w_dn, b_dn.reshape(N_EXP, 1, D_MODEL))


def _combine_kernel(x_ref, y_ref, w_ref, g2_ref, nf_ref, o_ref, *, final_norm):
    acc = None
    for k in range(TOP_K):
        term = y_ref[k] * w_ref[:, k:k + 1]
        acc = term if acc is None else acc + term
    x = x_ref[...] + g2_ref[0] * acc
    if final_norm:
        x = _rms(x) * nf_ref[...]
    o_ref[...] = x


def _combine(x, yg, wts, mods, mod_row, norm_f, final_norm, tm):
    T = x.shape[0]
    row = lambda w: pl.BlockSpec((tm, w), lambda i: (i, 0))
    return pl.pallas_call(
        functools.partial(_combine_kernel, final_norm=final_norm),
        grid=(T // tm,),
        in_specs=[row(D_MODEL), pl.BlockSpec((TOP_K, tm, D_MODEL), lambda i: (0, i, 0)), row(LANES),
                  pl.BlockSpec((1, 1, D_MODEL), lambda i: (mod_row(i * tm) * 6 + 5, 0, 0)),
                  pl.BlockSpec((1, D_MODEL), lambda i: (0, 0))],
        out_specs=row(D_MODEL),
        out_shape=jax.ShapeDtypeStruct((T, D_MODEL), F32),
        compiler_params=_cparams("parallel"),
        name="combine",
    )(x, yg, wts, mods, norm_f.reshape(1, D_MODEL))


def _route(top_e, T):
    n_assign = T * TOP_K
    n_blocks = n_assign // MOE_BLOCK + N_EXP
    e_flat = top_e.reshape(n_assign)
    onehot = e_flat[:, None] == jnp.arange(N_EXP, dtype=jnp.int32)[None, :]
    counts = jnp.sum(onehot, axis=0, dtype=jnp.int32)
    start = jnp.cumsum(counts) - counts
    padded = (counts + MOE_BLOCK - 1) // MOE_BLOCK * MOE_BLOCK
    pad_end = jnp.cumsum(padded)
    pad_start = pad_end - padded
    iota = jnp.arange(n_assign, dtype=jnp.int32)
    _, order = lax.sort((e_flat, iota), num_keys=1, is_stable=True)
    _, rank = lax.sort((order, iota), num_keys=1)
    pos = rank + jnp.sum(jnp.where(onehot, (pad_start - start)[None, :], 0), axis=1)
    blk_first = jnp.arange(n_blocks, dtype=jnp.int32) * MOE_BLOCK
    blk_e = jnp.minimum(jnp.sum(pad_end[None, :] <= blk_first[:, None], axis=1), N_EXP - 1).astype(jnp.int32)
    r = blk_first[:, None] - pad_start[blk_e][:, None] + jnp.arange(MOE_BLOCK, dtype=jnp.int32)[None, :]
    valid = r < counts[blk_e][:, None]
    src = jnp.clip(start[blk_e][:, None] + r, 0, n_assign - 1)
    tok = jnp.where(valid, order[src] // TOP_K, 0).reshape(n_blocks * MOE_BLOCK).astype(jnp.int32)
    return tok, pos, blk_e


def _prep_layer(l, w_in, sgu_w, w_branch, w_out, w_gu, w_dn, gd_A_log, gd_dt_bias, lb):
    offs = np.cumsum([0, BRANCH_W, BRANCH_W, 512, 512, 512, 512, 512, GD_QKV, 8, 8, 512, 3 * NA_W, N_BRANCH * D_MODEL])
    w = w_in[l]
    seg = lambda i, j: w[:, offs[i]:offs[j]]
    w_gd = jnp.concatenate([seg(7, 8), seg(10, 11), seg(8, 10),
                            jnp.zeros((D_MODEL, LANES - 4 * GD_HEADS), F32)], axis=1)
    par = jnp.zeros((2, LANES), F32)
    par = par.at[0, :2 * GD_HEADS].set(gd_A_log[l].reshape(-1)).at[1, :2 * GD_HEADS].set(gd_dt_bias[l].reshape(-1))
    lb_l = lb[:, l]
    return {
        'w_sgu': seg(0, 2).astype(BF16), 'w_hg': seg(2, 7).astype(BF16), 'w_gd': w_gd.astype(BF16),
        'w_na': seg(11, 12).astype(BF16), 'w_mg': seg(12, 13).astype(BF16),
        'sgu_w': sgu_w[l], 'wb': w_branch[l].astype(BF16), 'wo': w_out[l].astype(BF16),
        'w_gu': w_gu[l].astype(BF16), 'w_dn': w_dn[l].astype(BF16), 'gd_par': par,
        'lbp': jnp.concatenate([jnp.log(lb_l), jnp.log1p(-lb_l), 1.0 - lb_l], axis=0),
    }


def _layer(x, B, L, mods, mod_row, tm, lw, p, ctx, norm_f, final_norm):
    T = B * L
    h = _normmod(x, p['norm1'], mods, mod_row, tm, part_shift=0, part_scale=1)
    z_sgu = _matmul(h, lw['w_sgu'], 1024, 1024)
    z_hg = _matmul(h, lw['w_hg'], 1024, 1280)
    z_gd = _matmul(h, lw['w_gd'], 512, 2176)
    z_na = _matmul(h, lw['w_na'], 1024, 768)
    z_mg = _matmul(h, lw['w_mg'], 1024, 1024)

    o_a = _sgu(z_sgu, p['sgu_norm'], lw['sgu_w'], p['sgu_b'], 256)
    s_hg0 = None if ctx is None else ctx[2]
    s_gd0 = None if ctx is None else ctx[3]
    o_hf, o_hb, s_hg = _hgrn(z_hg, lw['lbp'], s_hg0, B, L)
    qkv = _gdprep(z_gd, p['gd_conv'], L, 256)
    o_gf, o_gb, s_gd = _gdn(qkv, z_gd, lw['gd_par'], s_gd0, B, L)
    if ctx is None:
        o_d = _ctx_attn(z_na, B, L)
    else:
        o_d = _na_attn(z_na, ctx[0], ctx[1], _na_bias(p['na_rpb'], L // GRID_W), B, L)
    x = _merge(x, o_a, o_hf, o_hb, z_hg, o_gf, o_gb, z_gd, o_d, z_mg, p['hg_onorm'], p['gd_onorm'],
               lw['wb'], lw['wo'], mods, mod_row, 256)

    h2, top_e, wts = _router(x, p['norm2'], mods, mod_row, p['w_router'], p['b_router'], 512)
    tok, pos, blk_e = _route(top_e[:, :TOP_K], T)
    xb = _sc_gather(h2, tok)
    yb = _experts(xb, blk_e, lw['w_gu'], p['b_gu'], lw['w_dn'], p['b_dn'])
    yg = _sc_gather(yb, pos.reshape(T, TOP_K).T.reshape(-1)).reshape(TOP_K, T, D_MODEL)
    x = _combine(x, yg, wts, mods, mod_row, norm_f, final_norm, 256)
    side = (z_na[:, NA_W:2 * NA_W].reshape(B, L, NA_HEADS, NA_DH),
            z_na[:, 2 * NA_W:].reshape(B, L, NA_HEADS, NA_DH), s_hg, s_gd)
    return x, side


def kernel(x_prompt, x_sample, c, cache_na_k, cache_na_v, state_hgrn, state_gdn, c_ctx, w_ada, b_ada, norm1, norm2, norm_f, w_in, sgu_norm, sgu_w, sgu_b, hg_lb, hg_onorm, gd_conv, gd_A_log, gd_dt_bias, gd_onorm, na_rpb, w_branch, w_out, w_router, b_router, w_gu, b_gu, w_dn, b_dn):
    Bp, Lp, D = x_prompt.shape
    Bs, Ls, _ = x_sample.shape
    ctx_row = Bs
    cvecs = jnp.zeros((MOD_ROWS, D), F32).at[:Bs].set(c).at[ctx_row].set(c_ctx)
    mods = _modulation(cvecs, w_ada, b_ada)

    cs = jnp.cumsum(jax.nn.softmax(hg_lb.astype(F32), axis=1), axis=1)
    lb = cs - cs[:, :1]

    tm = 1024
    xp = x_prompt.reshape(Bp * Lp, D)
    xs = x_sample.reshape(Bs * Ls, D)
    sides = []
    for l in range(DEPTH):
        lw = _prep_layer(l, w_in, sgu_w, w_branch, w_out, w_gu, w_dn, gd_A_log, gd_dt_bias, lb)
        p = {'norm1': norm1[l], 'norm2': norm2[l], 'sgu_norm': sgu_norm[l], 'sgu_b': sgu_b[l],
             'gd_conv': gd_conv[l], 'hg_onorm': hg_onorm[l], 'gd_onorm': gd_onorm[l], 'na_rpb': na_rpb[l],
             'w_router': w_router[l], 'b_router': b_router[l], 'b_gu': b_gu[l], 'b_dn': b_dn[l]}
        final = l == DEPTH - 1
        xp, side = _layer(xp, Bp, Lp, mods[l], lambda r: ctx_row, tm, lw, p, None, norm_f, final)
        sides.append(side)
        ctx = (cache_na_k[:, l].reshape(Bs, -1, NA_W), cache_na_v[:, l].reshape(Bs, -1, NA_W),
               state_hgrn[:, l], state_gdn[:, l])
        xs, _ = _layer(xs, Bs, Ls, mods[l], lambda r: r // Ls, tm, lw, p, ctx, norm_f, final)

    return (xp.reshape(Bp, Lp, D), xs.reshape(Bs, Ls, D),
            jnp.stack([s[0] for s in sides], axis=1), jnp.stack([s[1] for s in sides], axis=1),
            jnp.stack([s[2] for s in sides], axis=1), jnp.stack([s[3] for s in sides], axis=1))
```

```python
import functools
import math

import numpy as np
import jax
import jax.numpy as jnp
from jax import lax
from jax.experimental import pallas as pl
from jax.experimental.pallas import tpu as pltpu
from jax.experimental.pallas import tpu_sc as plsc

D_MODEL = 1024
DEPTH = 2
GRID_W = 64
BRANCH_W = 512
N_BRANCH = 4
SGU_CHUNK = 128
SGU_GROUPS = 4
HG_HEADS = 4
HG_DK = 128
HG_DV = 128
GD_HEADS = 4
GD_DK = 128
GD_DV = 128
NA_HEADS = 8
NA_DH = 64
NA_KH = 8
NA_KW = 16
N_EXP = 32
TOP_K = 4
D_FF = 1024
SWIGLU_LIMIT = 7.0
SWIGLU_ALPHA = 1.702
EPS = 1e-6

F32 = jnp.float32
BF16 = jnp.bfloat16
HI = lax.Precision.HIGHEST

LANES = 128
MOD_ROWS = 16
SCAN_C = 64
MOE_BLOCK = 256
NA_QROWS = 4
NA_KROWS = NA_QROWS + NA_KH
VMEM_LIMIT = 48 * 1024 * 1024

NT = (((1,), (1,)), ((), ()))
TN = (((0,), (0,)), ((), ()))


def _cparams(*sem):
    return pltpu.CompilerParams(dimension_semantics=sem, vmem_limit_bytes=VMEM_LIMIT)


def _bdot(a, b):
    return jnp.dot(a.astype(BF16), b.astype(BF16), preferred_element_type=F32)


def _bdot_g(a, b, dims):
    return lax.dot_general(a.astype(BF16), b.astype(BF16), dims, preferred_element_type=F32)


def _hdot(a, b):
    return jnp.dot(a, b, precision=HI, preferred_element_type=F32)


def _dot01(m, x):
    hi = x.astype(BF16)
    r1 = x - hi.astype(F32)
    mid = r1.astype(BF16)
    lo = (r1 - mid.astype(F32)).astype(BF16)
    d = lambda t: jnp.dot(m, t, preferred_element_type=F32)
    return (d(lo) + d(mid)) + d(hi)


def _dot3(a, b):
    ah = a.astype(BF16)
    al = (a - ah.astype(F32)).astype(BF16)
    bh = b.astype(BF16)
    bl = (b - bh.astype(F32)).astype(BF16)
    d = lambda x, y: jnp.dot(x, y, preferred_element_type=F32)
    return (d(al, bh) + d(ah, bl)) + d(ah, bh)


def _silu(x):
    return x * jax.nn.sigmoid(x)


def _log_sigmoid(x):
    return jnp.minimum(x, 0.0) - jnp.log1p(jnp.exp(-jnp.abs(x)))


def _logaddexp(a, b):
    return jnp.maximum(a, b) + jnp.log1p(jnp.exp(-jnp.abs(a - b)))


def _softplus(x):
    return jnp.maximum(x, 0.0) + jnp.log1p(jnp.exp(-jnp.abs(x)))


def _ada_kernel(c_ref, w_ref, b_ref, o_ref):
    o_ref[0] = _hdot(_silu(c_ref[...]), w_ref[0]) + b_ref[0]


def _modulation(cvecs, w_ada, b_ada):
    tn = 1536
    out = pl.pallas_call(
        _ada_kernel,
        grid=(DEPTH, 6 * D_MODEL // tn),
        in_specs=[pl.BlockSpec((MOD_ROWS, D_MODEL), lambda l, j: (0, 0)),
                  pl.BlockSpec((1, D_MODEL, tn), lambda l, j: (l, 0, j)),
                  pl.BlockSpec((1, 1, tn), lambda l, j: (l, 0, j))],
        out_specs=pl.BlockSpec((1, MOD_ROWS, tn), lambda l, j: (l, 0, j)),
        out_shape=jax.ShapeDtypeStruct((DEPTH, MOD_ROWS, 6 * D_MODEL), F32),
        compiler_params=_cparams("arbitrary", "arbitrary"),
        name="ada_modulation",
    )(cvecs, w_ada, b_ada.reshape(DEPTH, 1, 6 * D_MODEL))
    return out.reshape(DEPTH, MOD_ROWS * 6, 1, D_MODEL)


def _rms(x):
    return x * lax.rsqrt(jnp.mean(x * x, axis=-1, keepdims=True) + EPS)


def _normmod_kernel(x_ref, g_ref, sc_ref, sh_ref, o_ref):
    h = (_rms(x_ref[...]) * g_ref[...]) * (1.0 + sc_ref[0]) + sh_ref[0]
    o_ref[...] = h.astype(o_ref.dtype)


def _normmod(x, g, mods, mod_row, tm, part_shift, part_scale):
    T = x.shape[0]
    return pl.pallas_call(
        _normmod_kernel,
        grid=(T // tm,),
        in_specs=[pl.BlockSpec((tm, D_MODEL), lambda i: (i, 0)),
                  pl.BlockSpec((1, D_MODEL), lambda i: (0, 0)),
                  pl.BlockSpec((1, 1, D_MODEL), lambda i: (mod_row(i * tm) * 6 + part_scale, 0, 0)),
                  pl.BlockSpec((1, 1, D_MODEL), lambda i: (mod_row(i * tm) * 6 + part_shift, 0, 0))],
        out_specs=pl.BlockSpec((tm, D_MODEL), lambda i: (i, 0)),
        out_shape=jax.ShapeDtypeStruct((T, D_MODEL), BF16),
        compiler_params=_cparams("parallel"),
        name="normmod",
    )(x, g.reshape(1, D_MODEL), mods, mods)


def _mm_kernel(a_ref, w_ref, o_ref):
    o_ref[...] = jnp.dot(a_ref[...], w_ref[...], preferred_element_type=F32).astype(o_ref.dtype)


def _matmul(a, w, tm, tn, out_dtype=F32):
    T, K = a.shape
    N = w.shape[1]
    return pl.pallas_call(
        _mm_kernel,
        grid=(N // tn, T // tm),
        in_specs=[pl.BlockSpec((tm, K), lambda j, i: (i, 0)),
                  pl.BlockSpec((K, tn), lambda j, i: (0, j))],
        out_specs=pl.BlockSpec((tm, tn), lambda j, i: (i, j)),
        out_shape=jax.ShapeDtypeStruct((T, N), out_dtype),
        compiler_params=_cparams("parallel", "parallel"),
        name="in_proj",
    )(a, w)


def _sgu_kernel(u_ref, v_ref, gn_ref, ws_ref, bs_ref, o_ref):
    rows = u_ref.shape[0]
    gw = BRANCH_W // SGU_GROUPS
    u = jax.nn.gelu(u_ref[...])
    v = (_rms(jax.nn.gelu(v_ref[...])) * gn_ref[...]).astype(BF16)
    for n in range(rows // SGU_CHUNK):
        r = slice(n * SGU_CHUNK, (n + 1) * SGU_CHUNK)
        for g in range(SGU_GROUPS):
            cs = slice(g * gw, (g + 1) * gw)
            s = jnp.dot(ws_ref[g], v[r, cs], preferred_element_type=F32) + bs_ref[:, cs]
            o_ref[r, cs] = (u[r, cs] * s).astype(o_ref.dtype)


def _sgu(z_sgu, g_norm, w_s, b_s, rows, row0, T):
    gw = BRANCH_W // SGU_GROUPS
    b_exp = jnp.repeat(b_s.T, gw, axis=1)
    i0 = row0 // rows
    return pl.pallas_call(
        _sgu_kernel,
        grid=(T // rows,),
        in_specs=[pl.BlockSpec((rows, BRANCH_W), lambda i: (i0 + i, 0)),
                  pl.BlockSpec((rows, BRANCH_W), lambda i: (i0 + i, 1)),
                  pl.BlockSpec((1, BRANCH_W), lambda i: (0, 0)),
                  pl.BlockSpec((SGU_GROUPS, SGU_CHUNK, SGU_CHUNK), lambda i: (0, 0, 0)),
                  pl.BlockSpec((SGU_CHUNK, BRANCH_W), lambda i: (0, 0))],
        out_specs=pl.BlockSpec((rows, BRANCH_W), lambda i: (i, 0)),
        out_shape=jax.ShapeDtypeStruct((T, BRANCH_W), BF16),
        compiler_params=_cparams("parallel"),
        name="sgu",
    )(z_sgu, z_sgu, g_norm.reshape(1, BRANCH_W), w_s.astype(BF16), b_exp)


def _order(reverse):
    p = np.arange(SCAN_C)
    return SCAN_C - 1 - p if reverse else p


def _gla_consts():
    C = SCAN_C
    nlev = int(math.log2(C))
    mats, masks = [], []
    for reverse in (False, True):
        p = _order(reverse)
        pt, pr = p[:, None], p[None, :]
        m_d, k_d = [], []
        for lev in range(nlev):
            w = C >> (lev + 1)
            parent = p // (2 * w)
            later = (p % (2 * w)) >= w
            anchor = (parent * 2 * w + w - 1)[:, None]
            m = np.where(later[:, None], (pr > anchor) & (pr <= pt), (pr > pt) & (pr <= anchor))
            m_d.append(m)
            k_d.append((parent[:, None] == parent[None, :]) & later[:, None] & ~later[None, :])
        m_d.append(pr <= pt)
        m_d.append(pr > pt)
        k_d.append(np.eye(C, dtype=bool))
        mats.append(np.concatenate(m_d, axis=0))
        masks.append(np.stack(k_d))
    return (np.stack(mats).astype(np.float32), np.stack(masks).astype(np.float32))


def _delta_consts():
    C = SCAN_C
    tri, sl, incl, strict = [], [], [], []
    for reverse in (False, True):
        p = _order(reverse)
        pt, pr = p[:, None], p[None, :]
        tri.append(np.concatenate([pr <= pt, pr > pt], axis=0))
        sl.append(np.concatenate([pt > pr, np.zeros((C, LANES - C), bool), np.ones((C, LANES), bool)], axis=1))
        incl.append(pr <= pt)
        strict.append(pr < pt)
    f = lambda a: np.stack(a).astype(np.float32)
    return f(tri), f(sl), f(incl), f(strict)


def _hgrn_kernel(qf_ref, ff_ref, vf_ref, qb_ref, fb_ref, vb_ref, lb_ref, mat_ref, msk_ref, *rest, has_state):
    if has_state:
        s0_ref, of_ref, ob_ref, sfin_ref, st_ref = rest
    else:
        of_ref, ob_ref, sfin_ref, st_ref = rest
    C = SCAN_C
    nlev = msk_ref.shape[1] - 1
    c = pl.program_id(1)
    last_c = pl.num_programs(1) - 1

    @pl.when(c == 0)
    def _():
        for d in range(2):
            for h in range(HG_HEADS):
                if has_state:
                    st_ref[d, h] = s0_ref[0, d, h].T
                else:
                    st_ref[d, h] = jnp.zeros((HG_DV, HG_DK), F32)

    q_refs, f_refs, v_refs, o_refs = (qf_ref, qb_ref), (ff_ref, fb_ref), (vf_ref, vb_ref), (of_ref, ob_ref)
    chains = [(d, h) for d in range(2) for h in range(HG_HEADS)]
    col = lambda h: slice(h * HG_DK, (h + 1) * HG_DK)
    q, k, fac, att = {}, {}, {}, {}
    for d, h in chains:
        zf = f_refs[d][:, col(h)]
        loglb = lb_ref[d:d + 1, col(h)]
        log1m = lb_ref[2 + d:3 + d, col(h)]
        onem = lb_ref[4 + d:5 + d, col(h)]
        logf = _logaddexp(loglb, log1m + _log_sigmoid(zf))
        k[d, h] = onem * jax.nn.sigmoid(-zf)
        q[d, h] = _silu(q_refs[d][:, col(h)]) * (HG_DK ** -0.5)
        fac[d, h] = jnp.exp(_dot01(mat_ref[d], logf))
    for d, h in chains:
        acc = msk_ref[d, nlev] * _bdot_g(q[d, h], k[d, h], NT)
        for i in range(nlev):
            fi = fac[d, h][i * C:(i + 1) * C]
            acc = acc + msk_ref[d, i] * _bdot_g(q[d, h] * fi, k[d, h] * fi, NT)
        att[d, h] = acc
    for d, h in chains:
        eb = fac[d, h][nlev * C:(nlev + 1) * C]
        o_refs[d][:, col(h)] = (_bdot(att[d, h], v_refs[d][:, col(h)])
                                + _bdot_g(q[d, h] * eb, st_ref[d, h], NT))
    for d, h in chains:
        eb = fac[d, h][nlev * C:(nlev + 1) * C]
        er = fac[d, h][(nlev + 1) * C:]
        e_last = eb[C - 1:C] if d == 0 else eb[0:1]
        st_ref[d, h] = st_ref[d, h] * e_last + _bdot_g(v_refs[d][:, col(h)], k[d, h] * er, TN)

    @pl.when(c == last_c)
    def _():
        for d in range(2):
            for h in range(HG_HEADS):
                sfin_ref[0, d, h] = st_ref[d, h].T


def _hgrn(z_hg, lbp, s0, B, L, row0):
    n = L // SCAN_C
    c0 = row0 // SCAN_C
    mats, masks = _gla_consts()
    blk = (SCAN_C, HG_HEADS * HG_DK)
    fwd = lambda col: pl.BlockSpec(blk, lambda b, c: (c0 + b * n + c, col))
    bwd = lambda col: pl.BlockSpec(blk, lambda b, c: (c0 + b * n + n - 1 - c, col))
    st_blk = (1, 2, HG_HEADS, HG_DK, HG_DV)
    in_specs = [fwd(0), fwd(1), fwd(3), bwd(0), bwd(2), bwd(3),
                pl.BlockSpec(lbp.shape, lambda b, c: (0, 0)),
                pl.BlockSpec(mats.shape, lambda b, c: (0, 0, 0)),
                pl.BlockSpec(masks.shape, lambda b, c: (0, 0, 0, 0))]
    args = [z_hg] * 6 + [lbp, jnp.asarray(mats, BF16), jnp.asarray(masks)]
    if s0 is not None:
        in_specs.append(pl.BlockSpec(st_blk, lambda b, c: (b, 0, 0, 0, 0)))
        args.append(s0)
    return pl.pallas_call(
        functools.partial(_hgrn_kernel, has_state=s0 is not None),
        grid=(B, n),
        in_specs=in_specs,
        out_specs=[pl.BlockSpec(blk, lambda b, c: (b * n + c, 0)),
                   pl.BlockSpec(blk, lambda b, c: (b * n + n - 1 - c, 0)),
                   pl.BlockSpec(st_blk, lambda b, c: (b, 0, 0, 0, 0))],
        out_shape=[jax.ShapeDtypeStruct((B * L, HG_HEADS * HG_DV), F32),
                   jax.ShapeDtypeStruct((B * L, HG_HEADS * HG_DV), F32),
                   jax.ShapeDtypeStruct((B,) + st_blk[1:], F32)],
        scratch_shapes=[pltpu.VMEM((2, HG_HEADS, HG_DV, HG_DK), F32)],
        compiler_params=_cparams("parallel", "arbitrary"),
        name="hgrn_scan",
    )(*args)


GD_NQ = GD_HEADS * GD_DK
GD_QKV = 2 * GD_NQ + GD_HEADS * GD_DV
HALO = 8


def _gdprep_kernel(x_ref, prev_ref, next_ref, w_ref, o_ref, *, tiles_per_seq):
    R = x_ref.shape[0]
    t = pl.program_id(0) % tiles_per_seq
    x = x_ref[...]
    prev_row = jnp.where(t == 0, 0.0, prev_ref[HALO - 1:HALO, :])
    next_row = jnp.where(t == tiles_per_seq - 1, 0.0, next_ref[0:1, :])
    row = lax.broadcasted_iota(jnp.int32, x.shape, 0)
    xm1 = jnp.where(row == 0, prev_row, pltpu.roll(x, 1, 0))
    xp1 = jnp.where(row == R - 1, next_row, pltpu.roll(x, R - 1, 0))
    y = _silu(w_ref[0:1, :] * xm1 + w_ref[1:2, :] * x + w_ref[2:3, :] * xp1)
    for j in range(2 * GD_HEADS):
        cs = slice(j * GD_DK, (j + 1) * GD_DK)
        seg = y[:, cs]
        seg = seg * lax.rsqrt(jnp.sum(seg * seg, axis=-1, keepdims=True) + EPS)
        if j < GD_HEADS:
            seg = seg * (GD_DK ** -0.5)
        o_ref[:, cs] = seg
    o_ref[:, 2 * GD_NQ:] = y[:, 2 * GD_NQ:]


def _gdprep(z_gd, conv_w, L, rows, row0, T):
    tps = L // rows
    hb = rows // HALO
    nhalo = z_gd.shape[0] // HALO
    i0 = row0 // rows
    return pl.pallas_call(
        functools.partial(_gdprep_kernel, tiles_per_seq=tps),
        grid=(T // rows,),
        in_specs=[pl.BlockSpec((rows, GD_QKV), lambda i: (i0 + i, 0)),
                  pl.BlockSpec((HALO, GD_QKV), lambda i: (jnp.maximum((i0 + i) * hb - 1, 0), 0)),
                  pl.BlockSpec((HALO, GD_QKV), lambda i: (jnp.minimum((i0 + i + 1) * hb, nhalo - 1), 0)),
                  pl.BlockSpec((3, GD_QKV), lambda i: (0, 0))],
        out_specs=pl.BlockSpec((rows, GD_QKV), lambda i: (i, 0)),
        out_shape=jax.ShapeDtypeStruct((T, GD_QKV), F32),
        compiler_params=_cparams("parallel"),
        name="gdn_prep",
    )(z_gd, z_gd, z_gd, conv_w)


def _gdn_kernel(xf_ref, abf_ref, xb_ref, abb_ref, par_ref, tri_ref, sl_ref, incl_ref, strict_ref, msk_ref, *rest,
                has_state):
    if has_state:
        s0_ref, of_ref, ob_ref, sfin_ref, st_ref = rest
    else:
        of_ref, ob_ref, sfin_ref, st_ref = rest
    C = SCAN_C
    c = pl.program_id(1)
    last_c = pl.num_programs(1) - 1
    nlev = msk_ref.shape[1] - 1

    @pl.when(c == 0)
    def _():
        for d in range(2):
            for h in range(GD_HEADS):
                if has_state:
                    st_ref[d, h] = s0_ref[0, d, h].T
                else:
                    st_ref[d, h] = jnp.zeros((GD_DV, GD_DK), F32)

    x_refs, ab_refs, o_refs = (xf_ref, xb_ref), (abf_ref, abb_ref), (of_ref, ob_ref)
    chains = [(d, h) for d in range(2) for h in range(GD_HEADS)]
    q_of = lambda d, h: x_refs[d][:, h * GD_DK:(h + 1) * GD_DK]
    k_of = lambda d, h: x_refs[d][:, GD_NQ + h * GD_DK:GD_NQ + (h + 1) * GD_DK]
    v_of = lambda d, h: x_refs[d][:, 2 * GD_NQ + h * GD_DV:2 * GD_NQ + (h + 1) * GD_DV]
    g_all, beta_all = [], []
    for d in range(2):
        ab = ab_refs[d][...]
        g_all.append(-jnp.exp(par_ref[0:1, :]) * _softplus(ab + par_ref[1:2, :]))
        beta_all.append(jax.nn.sigmoid(ab))
    decay, e_cum, e_rest, kb, a = {}, {}, {}, {}, {}
    for d, h in chains:
        j = d * GD_HEADS + h
        g_b = jnp.broadcast_to(g_all[d][:, j:j + 1], (C, 2 * LANES))
        sums = _dot01(tri_ref[d], g_b * sl_ref[d])
        decay[d, h] = jnp.exp(sums[:C, :C])
        e_cum[d, h] = jnp.exp(sums[:C, LANES:])
        e_rest[d, h] = jnp.exp(sums[C:, LANES:])
    for d, h in chains:
        j = d * GD_HEADS + h
        beta = jnp.broadcast_to(beta_all[d][:, 2 * GD_HEADS + j:2 * GD_HEADS + j + 1], (C, LANES))
        k = k_of(d, h)
        kb[d, h] = k * beta
        a[d, h] = strict_ref[d] * decay[d, h] * _bdot_g(kb[d, h], k, NT)
    inv_m = {(d, h): -(msk_ref[d, nlev - 1] * a[d, h]) for d, h in chains}
    for lev in range(nlev - 2, -1, -1):
        a_w = {(d, h): msk_ref[d, lev] * a[d, h] for d, h in chains}
        p = {ch: a_w[ch] + _bdot(inv_m[ch], a_w[ch]) for ch in chains}
        inv_m = {ch: inv_m[ch] - p[ch] - _bdot(p[ch], inv_m[ch]) for ch in chains}
    sol = {}
    for d, h in chains:
        j = d * GD_HEADS + h
        beta = jnp.broadcast_to(beta_all[d][:, 2 * GD_HEADS + j:2 * GD_HEADS + j + 1], (C, LANES))
        rhs = jnp.concatenate([v_of(d, h) * beta, kb[d, h] * e_cum[d, h]], axis=1)
        sol[d, h] = rhs + _dot3(inv_m[d, h], rhs)
    u = {ch: sol[ch][:, :GD_DV] - _bdot_g(sol[ch][:, GD_DV:], st_ref[ch[0], ch[1]], NT) for ch in chains}
    att = {(d, h): incl_ref[d] * decay[d, h] * _bdot_g(q_of(d, h), k_of(d, h), NT) for d, h in chains}
    for d, h in chains:
        o_refs[d][:, h * GD_DV:(h + 1) * GD_DV] = (_bdot_g(q_of(d, h) * e_cum[d, h], st_ref[d, h], NT)
                                                   + _bdot(att[d, h], u[d, h]))
    for d, h in chains:
        e_last = e_cum[d, h][C - 1:C] if d == 0 else e_cum[d, h][0:1]
        st_ref[d, h] = st_ref[d, h] * e_last + _bdot_g(u[d, h], k_of(d, h) * e_rest[d, h], TN)

    @pl.when(c == last_c)
    def _():
        for d in range(2):
            for h in range(GD_HEADS):
                sfin_ref[0, d, h] = st_ref[d, h].T


def _gdn(qkv, z_gd, par, s0, B, L, row0):
    n = L // SCAN_C
    c0 = row0 // SCAN_C
    tri, sl, incl, strict = (jnp.asarray(a) for a in _delta_consts())
    tri = tri.astype(BF16)
    masks = jnp.asarray(_gla_consts()[1])
    ab_col = (GD_QKV + GD_HEADS * GD_DV) // LANES
    xblk = (SCAN_C, GD_QKV)
    ablk = (SCAN_C, LANES)
    oblk = (SCAN_C, GD_HEADS * GD_DV)
    st_blk = (1, 2, GD_HEADS, GD_DK, GD_DV)
    fwd = lambda b, c: b * n + c
    bwd = lambda b, c: b * n + n - 1 - c
    const3 = lambda a: pl.BlockSpec(a.shape, lambda b, c: (0, 0, 0))
    in_specs = [pl.BlockSpec(xblk, lambda b, c: (fwd(b, c), 0)),
                pl.BlockSpec(ablk, lambda b, c: (c0 + fwd(b, c), ab_col)),
                pl.BlockSpec(xblk, lambda b, c: (bwd(b, c), 0)),
                pl.BlockSpec(ablk, lambda b, c: (c0 + bwd(b, c), ab_col)),
                pl.BlockSpec(par.shape, lambda b, c: (0, 0)),
                const3(tri), const3(sl), const3(incl), const3(strict),
                pl.BlockSpec(masks.shape, lambda b, c: (0, 0, 0, 0))]
    args = [qkv, z_gd, qkv, z_gd, par, tri, sl, incl, strict, masks]
    if s0 is not None:
        in_specs.append(pl.BlockSpec(st_blk, lambda b, c: (b, 0, 0, 0, 0)))
        args.append(s0)
    return pl.pallas_call(
        functools.partial(_gdn_kernel, has_state=s0 is not None),
        grid=(B, n),
        in_specs=in_specs,
        out_specs=[pl.BlockSpec(oblk, lambda b, c: (fwd(b, c), 0)),
                   pl.BlockSpec(oblk, lambda b, c: (bwd(b, c), 0)),
                   pl.BlockSpec(st_blk, lambda b, c: (b, 0, 0, 0, 0))],
        out_shape=[jax.ShapeDtypeStruct((B * L, GD_HEADS * GD_DV), F32),
                   jax.ShapeDtypeStruct((B * L, GD_HEADS * GD_DV), F32),
                   jax.ShapeDtypeStruct((B,) + st_blk[1:], F32)],
        scratch_shapes=[pltpu.VMEM((2, GD_HEADS, GD_DV, GD_DK), F32)],
        compiler_params=_cparams("parallel", "arbitrary"),
        name="gdn_scan",
    )(*args)


NA_W = NA_HEADS * NA_DH


def _softmax_pv(s, v):
    m = jnp.max(s, axis=-1, keepdims=True)
    e = jnp.exp(s - m)
    den = jnp.sum(e, axis=-1, keepdims=True)
    return jnp.dot(e.astype(BF16), v, preferred_element_type=F32) / den


def _ctx_attn_kernel(q_ref, k_ref, v_ref, o_ref):
    for h in range(NA_HEADS):
        cs = slice(h * NA_DH, (h + 1) * NA_DH)
        q = (q_ref[:, cs] * (NA_DH ** -0.5)).astype(BF16)
        s = lax.dot_general(q, k_ref[:, cs].astype(BF16), NT, preferred_element_type=F32)
        o_ref[:, cs] = _softmax_pv(s, v_ref[:, cs].astype(BF16)).astype(o_ref.dtype)


def _ctx_attn(z_na, B, L, row0):
    blk = (L, NA_W)
    b0 = row0 // L
    return pl.pallas_call(
        _ctx_attn_kernel,
        grid=(B,),
        in_specs=[pl.BlockSpec(blk, lambda b: (b0 + b, 0)),
                  pl.BlockSpec(blk, lambda b: (b0 + b, 1)),
                  pl.BlockSpec(blk, lambda b: (b0 + b, 2))],
        out_specs=pl.BlockSpec(blk, lambda b: (b, 0)),
        out_shape=jax.ShapeDtypeStruct((B * L, NA_W), BF16),
        compiler_params=_cparams("parallel"),
        name="ctx_attn",
    )(z_na, z_na, z_na)


def _na_kernel(q_ref, *rest, n_kblk, nkeys_nb):
    k_refs = rest[:n_kblk]
    v_refs = rest[n_kblk:2 * n_kblk]
    kc_ref, vc_ref, bias_ref, o_ref, kbuf, vbuf = rest[2 * n_kblk:]
    qb = q_ref.shape[0]
    for i in range(n_kblk):
        kbuf[i * qb:(i + 1) * qb, :] = k_refs[i][...].astype(BF16)
        vbuf[i * qb:(i + 1) * qb, :] = v_refs[i][...].astype(BF16)
    kbuf[nkeys_nb:, :] = kc_ref[0].astype(BF16)
    vbuf[nkeys_nb:, :] = vc_ref[0].astype(BF16)
    for h in range(NA_HEADS):
        cs = slice(h * NA_DH, (h + 1) * NA_DH)
        q = (q_ref[:, cs] * (NA_DH ** -0.5)).astype(BF16)
        s = lax.dot_general(q, kbuf[:, cs], NT, preferred_element_type=F32)
        s_nb = s[:, :nkeys_nb] + bias_ref[0, h]
        s_cx = s[:, nkeys_nb:]
        m = jnp.maximum(jnp.max(s_nb, axis=-1, keepdims=True), jnp.max(s_cx, axis=-1, keepdims=True))
        e_nb = jnp.exp(s_nb - m)
        e_cx = jnp.exp(s_cx - m)
        den = jnp.sum(e_nb, axis=-1, keepdims=True) + jnp.sum(e_cx, axis=-1, keepdims=True)
        pv = (jnp.dot(e_nb.astype(BF16), vbuf[:nkeys_nb, cs], preferred_element_type=F32)
              + jnp.dot(e_cx.astype(BF16), vbuf[nkeys_nb:, cs], preferred_element_type=F32))
        o_ref[:, cs] = (pv / den).astype(o_ref.dtype)


def _na_bias(rpb, rows):
    qr, kr, col = np.arange(NA_QROWS), np.arange(NA_KROWS), np.arange(GRID_W)
    nblk = rows // NA_QROWS
    ndr, ndc = 2 * NA_KH - 1, 2 * NA_KW - 1
    sel_r, row_ok = [], []
    for m in (0, 1, nblk - 1):
        r = (NA_QROWS * m + qr)[:, None]
        start = np.clip(NA_QROWS * m - NA_KH // 2, 0, rows - NA_KROWS)
        kra = (start + kr)[None, :]
        r0 = np.clip(r - NA_KH // 2, 0, rows - NA_KH)
        row_ok.append((kra >= r0) & (kra < r0 + NA_KH))
        dr = np.clip(kra - r + NA_KH - 1, 0, ndr - 1)
        sel_r.append(dr[..., None] == np.arange(ndr))
    sel_r = np.stack(sel_r).astype(np.float32)
    row_ok = np.stack(row_ok)
    col_start = np.clip(col - NA_KW // 2, 0, GRID_W - NA_KW)[:, None]
    col_ok = (col[None, :] >= col_start) & (col[None, :] < col_start + NA_KW)
    dc = np.clip(col[None, :] - col[:, None], -(NA_KW - 1), NA_KW - 1) + NA_KW - 1
    sel_c = (dc[..., None] == np.arange(ndc)).astype(np.float32)
    bias = jnp.einsum('hab,vqka,xyb->vhqxky', rpb.astype(F32), sel_r, sel_c, precision=HI)
    ok = row_ok[:, None, :, None, :, None] & col_ok[None, None, None, :, None, :]
    bias = jnp.where(ok, bias, -jnp.inf)
    return bias.reshape(3, NA_HEADS, NA_QROWS * GRID_W, NA_KROWS * GRID_W)


def _na_attn(z_na, k_ctx, v_ctx, bias, B, S, row0):
    rows = S // GRID_W
    qb = NA_QROWS * GRID_W
    m0 = row0 // qb
    nblk = rows // NA_QROWS
    n_kblk = NA_KROWS // NA_QROWS
    lc = k_ctx.shape[1]
    nkeys_nb = NA_KROWS * GRID_W
    kstart = lambda m: jnp.clip(m - 1, 0, nblk - n_kblk)
    variant = lambda m: jnp.where(m == 0, 0, jnp.where(m == nblk - 1, 2, 1))
    kv_specs = lambda col: [pl.BlockSpec((qb, NA_W), functools.partial(
        lambda b, m, i, col: (m0 + b * nblk + kstart(m) + i, col), i=i, col=col)) for i in range(n_kblk)]
    return pl.pallas_call(
        functools.partial(_na_kernel, n_kblk=n_kblk, nkeys_nb=nkeys_nb),
        grid=(B, nblk),
        in_specs=([pl.BlockSpec((qb, NA_W), lambda b, m: (m0 + b * nblk + m, 0))] + kv_specs(1) + kv_specs(2)
                  + [pl.BlockSpec((1, lc, NA_W), lambda b, m: (b, 0, 0)),
                     pl.BlockSpec((1, lc, NA_W), lambda b, m: (b, 0, 0)),
                     pl.BlockSpec((1,) + bias.shape[1:], lambda b, m: (variant(m), 0, 0, 0))]),
        out_specs=pl.BlockSpec((qb, NA_W), lambda b, m: (b * nblk + m, 0)),
        out_shape=jax.ShapeDtypeStruct((B * S, NA_W), BF16),
        scratch_shapes=[pltpu.VMEM((nkeys_nb + lc, NA_W), BF16), pltpu.VMEM((nkeys_nb + lc, NA_W), BF16)],
        compiler_params=_cparams("parallel", "arbitrary"),
        name="na_attn",
    )(z_na, *([z_na] * (2 * n_kblk)), k_ctx, v_ctx, bias)


def _head_rms(o, g_row):
    parts = []
    for h in range(o.shape[1] // LANES):
        parts.append(_rms(o[:, h * LANES:(h + 1) * LANES]) * g_row)
    return jnp.concatenate(parts, axis=1)


def _merge_kernel(x_ref, oa_ref, hf_ref, hb_ref, hg_ref, gf_ref, gb_ref, gg_ref, od_ref, mg_ref,
                  hn_ref, gn_ref, wb_ref, wo_ref, g1_ref, *rest):
    o_ref = rest[-1]
    o_b = _head_rms(hf_ref[...] + hb_ref[...], hn_ref[...]) * _silu(hg_ref[...])
    o_c = _head_rms(gf_ref[...] + gb_ref[...], gn_ref[...]) * _silu(gg_ref[...])
    branches = (oa_ref[...], o_b.astype(BF16), o_c.astype(BF16), od_ref[...])
    merged = None
    for n_, o_n in enumerate(branches):
        gate = jax.nn.sigmoid(mg_ref[:, n_ * D_MODEL:(n_ + 1) * D_MODEL])
        term = gate * jnp.dot(o_n, wb_ref[n_], preferred_element_type=F32)
        merged = term if merged is None else merged + term
    mix = jnp.dot(merged.astype(BF16), wo_ref[...], preferred_element_type=F32)
    o_ref[...] = x_ref[...] + g1_ref[0] * mix


def _merge(x, o_a, o_hf, o_hb, z_hg, o_gf, o_gb, z_gd, o_d, z_mg, hg_onorm, gd_onorm, wb, wo, mods, mod_row, tm,
           row0, T, x_acc):
    i0 = row0 // tm
    glob = lambda w, col=0: pl.BlockSpec((tm, w), lambda i: (i0 + i, col))
    loc = lambda w: pl.BlockSpec((tm, w), lambda i: (i, 0))
    full = lambda a: pl.BlockSpec(a.shape, lambda i: (0,) * a.ndim)
    hn = hg_onorm.reshape(1, HG_DV)
    gn = gd_onorm.reshape(1, GD_DV)
    in_specs = [glob(D_MODEL), loc(BRANCH_W), loc(BRANCH_W), loc(BRANCH_W), glob(BRANCH_W, 4),
                loc(BRANCH_W), loc(BRANCH_W), glob(BRANCH_W, GD_QKV // BRANCH_W), loc(BRANCH_W),
                glob(N_BRANCH * D_MODEL), full(hn), full(gn), full(wb), full(wo),
                pl.BlockSpec((1, 1, D_MODEL), lambda i: (mod_row(row0 + i * tm) * 6 + 2, 0, 0))]
    args = [x, o_a, o_hf, o_hb, z_hg, o_gf, o_gb, z_gd, o_d, z_mg, hn, gn, wb, wo, mods]
    aliases = {}
    if x_acc is not None:
        in_specs.append(pl.BlockSpec(memory_space=pl.ANY))
        aliases = {len(args): 0}
        args.append(x_acc)
    return pl.pallas_call(
        _merge_kernel,
        grid=(T // tm,),
        in_specs=in_specs,
        out_specs=glob(D_MODEL),
        out_shape=jax.ShapeDtypeStruct(x.shape, F32),
        input_output_aliases=aliases,
        compiler_params=_cparams("parallel"),
        name="merge",
    )(*args)


SC_CORES = 2
SC_SUBCORES = 16
SC_WIN = 32


def _sc_gather(table, idx):
    V, D = table.shape
    N = idx.shape[0]
    nw = SC_CORES * SC_SUBCORES
    per_w = N // nw
    n_win = per_w // SC_WIN
    assert per_w * nw == N and n_win * SC_WIN == per_w
    mesh = plsc.VectorSubcoreMesh(core_axis_name="c", subcore_axis_name="s")

    @functools.partial(
        pl.kernel, mesh=mesh,
        out_type=jax.ShapeDtypeStruct((N, D), table.dtype),
        scratch_types=[pltpu.VMEM((n_win, SC_WIN), jnp.int32),
                       pltpu.VMEM((SC_WIN, D), table.dtype),
                       pltpu.SemaphoreType.DMA],
    )
    def gather_rows(table_hbm, idx_hbm, out_hbm, idx_v, rows_v, sem):
        wid = lax.axis_index("s") * SC_CORES + lax.axis_index("c")
        pltpu.sync_copy(idx_hbm.at[wid], idx_v)

        @pl.loop(0, n_win)
        def _(w):
            pltpu.async_copy(table_hbm.at[idx_v.at[w]], rows_v, sem).wait()
            pltpu.sync_copy(rows_v, out_hbm.at[pl.ds(wid * per_w + w * SC_WIN, SC_WIN)])

    return gather_rows(table, idx.reshape(nw, n_win, SC_WIN))


def _router_kernel(x_ref, g_ref, sc_ref, sh_ref, wr_ref, br_ref, h_ref, e_ref, w_ref):
    h = (_rms(x_ref[...]) * g_ref[...]) * (1.0 + sc_ref[0]) + sh_ref[0]
    h_ref[...] = h.astype(h_ref.dtype)
    logits = _hdot(h, wr_ref[...]) + br_ref[...]
    lane = lax.broadcasted_iota(jnp.int32, logits.shape, 1)
    e_out = jnp.zeros(logits.shape, jnp.int32)
    v_out = jnp.zeros(logits.shape, F32)
    top0 = None
    for k in range(TOP_K):
        m = jnp.max(logits, axis=-1, keepdims=True)
        idx = jnp.min(jnp.where(logits == m, lane, LANES), axis=-1, keepdims=True)
        if k == 0:
            top0 = m
        e_out = jnp.where(lane == k, idx, e_out)
        v_out = jnp.where(lane == k, jnp.exp(m - top0), v_out)
        logits = jnp.where(lane == idx, -jnp.inf, logits)
    e_ref[...] = e_out
    w_ref[...] = v_out / jnp.sum(v_out, axis=-1, keepdims=True)


def _router(x, g, mods, mod_row, w_router, b_router, tm):
    T = x.shape[0]
    wr = jnp.zeros((D_MODEL, LANES), F32).at[:, :N_EXP].set(w_router)
    br = jnp.full((1, LANES), -jnp.inf, F32).at[0, :N_EXP].set(b_router)
    row = lambda w: pl.BlockSpec((tm, w), lambda i: (i, 0))
    return pl.pallas_call(
        _router_kernel,
        grid=(T // tm,),
        in_specs=[row(D_MODEL),
                  pl.BlockSpec((1, D_MODEL), lambda i: (0, 0)),
                  pl.BlockSpec((1, 1, D_MODEL), lambda i: (mod_row(i * tm) * 6 + 4, 0, 0)),
                  pl.BlockSpec((1, 1, D_MODEL), lambda i: (mod_row(i * tm) * 6 + 3, 0, 0)),
                  pl.BlockSpec((D_MODEL, LANES), lambda i: (0, 0)),
                  pl.BlockSpec((1, LANES), lambda i: (0, 0))],
        out_specs=[row(D_MODEL), row(LANES), row(LANES)],
        out_shape=[jax.ShapeDtypeStruct((T, D_MODEL), F32),
                   jax.ShapeDtypeStruct((T, LANES), jnp.int32),
                   jax.ShapeDtypeStruct((T, LANES), F32)],
        compiler_params=_cparams("parallel"),
        name="router",
    )(x, g.reshape(1, D_MODEL), mods, mods, wr, br)


def _expert_kernel(blk_e_ref, x_ref, wgu_ref, bgu_ref, wdn_ref, bdn_ref, o_ref, wgu_bf, wdn_bf):
    i = pl.program_id(0)
    new_expert = jnp.logical_or(i == 0, blk_e_ref[i] != blk_e_ref[jnp.maximum(i - 1, 0)])

    @pl.when(new_expert)
    def _():
        wgu_bf[...] = wgu_ref[0].astype(BF16)
        wdn_bf[...] = wdn_ref[0].astype(BF16)

    gu = jnp.dot(x_ref[...].astype(BF16), wgu_bf[...], preferred_element_type=F32) + bgu_ref[0]
    a = jnp.minimum(gu[:, :D_FF], SWIGLU_LIMIT)
    lin = jnp.clip(gu[:, D_FF:], -SWIGLU_LIMIT, SWIGLU_LIMIT)
    y = a * jax.nn.sigmoid(SWIGLU_ALPHA * a) * (lin + 1.0)
    o_ref[...] = jnp.dot(y.astype(BF16), wdn_bf[...], preferred_element_type=F32) + bdn_ref[0]


def _experts(xb, blk_e, w_gu, b_gu, w_dn, b_dn, layer):
    n_pad = xb.shape[0]
    n_blocks = n_pad // MOE_BLOCK
    e0 = layer * N_EXP
    w_gu = w_gu.reshape(DEPTH * N_EXP, D_MODEL, 2 * D_FF)
    w_dn = w_dn.reshape(DEPTH * N_EXP, D_FF, D_MODEL)
    grid_spec = pltpu.PrefetchScalarGridSpec(
        num_scalar_prefetch=1,
        grid=(n_blocks,),
        in_specs=[pl.BlockSpec((MOE_BLOCK, D_MODEL), lambda i, e: (i, 0)),
                  pl.BlockSpec((1, D_MODEL, 2 * D_FF), lambda i, e: (e0 + e[i], 0, 0)),
                  pl.BlockSpec((1, 1, 2 * D_FF), lambda i, e: (e0 + e[i], 0, 0)),
                  pl.BlockSpec((1, D_FF, D_MODEL), lambda i, e: (e0 + e[i], 0, 0)),
                  pl.BlockSpec((1, 1, D_MODEL), lambda i, e: (e0 + e[i], 0, 0))],
        out_specs=pl.BlockSpec((MOE_BLOCK, D_MODEL), lambda i, e: (i, 0)),
        scratch_shapes=[pltpu.VMEM((D_MODEL, 2 * D_FF), BF16), pltpu.VMEM((D_FF, D_MODEL), BF16)],
    )
    return pl.pallas_call(
        _expert_kernel,
        grid_spec=grid_spec,
        out_shape=jax.ShapeDtypeStruct((n_pad, D_MODEL), F32),
        compiler_params=_cparams("arbitrary"),
        name="experts",
    )(blk_e, xb, w_gu, b_gu.reshape(DEPTH * N_EXP, 1, 2 * D_FF), w_dn, b_dn.reshape(DEPTH * N_EXP, 1, D_MODEL))


def _combine_kernel(x_ref, y_ref, w_ref, g2_ref, nf_ref, o_ref, *, final_norm):
    acc = None
    for k in range(TOP_K):
        term = y_ref[k] * w_ref[:, k:k + 1]
        acc = term if acc is None else acc + term
    x = x_ref[...] + g2_ref[0] * acc
    if final_norm:
        x = _rms(x) * nf_ref[...]
    o_ref[...] = x


def _combine(x, yg, wts, mods, mod_row, norm_f, final_norm, tm):
    T = x.shape[0]
    row = lambda w: pl.BlockSpec((tm, w), lambda i: (i, 0))
    return pl.pallas_call(
        functools.partial(_combine_kernel, final_norm=final_norm),
        grid=(T // tm,),
        in_specs=[row(D_MODEL), pl.BlockSpec((TOP_K, tm, D_MODEL), lambda i: (0, i, 0)), row(LANES),
                  pl.BlockSpec((1, 1, D_MODEL), lambda i: (mod_row(i * tm) * 6 + 5, 0, 0)),
                  pl.BlockSpec((1, D_MODEL), lambda i: (0, 0))],
        out_specs=row(D_MODEL),
        out_shape=jax.ShapeDtypeStruct((T, D_MODEL), F32),
        compiler_params=_cparams("parallel"),
        name="combine",
    )(x, yg, wts, mods, norm_f.reshape(1, D_MODEL))


def _route(top_e, T):
    n_assign = T * TOP_K
    n_blocks = n_assign // MOE_BLOCK + N_EXP
    e_flat = top_e.reshape(n_assign)
    onehot = e_flat[:, None] == jnp.arange(N_EXP, dtype=jnp.int32)[None, :]
    counts = jnp.sum(onehot, axis=0, dtype=jnp.int32)
    start = jnp.cumsum(counts) - counts
    padded = (counts + MOE_BLOCK - 1) // MOE_BLOCK * MOE_BLOCK
    pad_end = jnp.cumsum(padded)
    pad_start = pad_end - padded
    iota = jnp.arange(n_assign, dtype=jnp.int32)
    _, order = lax.sort((e_flat, iota), num_keys=1, is_stable=True)
    _, rank = lax.sort((order, iota), num_keys=1)
    pos = rank + jnp.sum(jnp.where(onehot, (pad_start - start)[None, :], 0), axis=1)
    blk_first = jnp.arange(n_blocks, dtype=jnp.int32) * MOE_BLOCK
    blk_e = jnp.minimum(jnp.sum(pad_end[None, :] <= blk_first[:, None], axis=1), N_EXP - 1).astype(jnp.int32)
    r = blk_first[:, None] - pad_start[blk_e][:, None] + jnp.arange(MOE_BLOCK, dtype=jnp.int32)[None, :]
    valid = r < counts[blk_e][:, None]
    src = jnp.clip(start[blk_e][:, None] + r, 0, n_assign - 1)
    tok = jnp.where(valid, order[src] // TOP_K, 0).reshape(n_blocks * MOE_BLOCK).astype(jnp.int32)
    return tok, pos, blk_e


def _prep_layer(l, w_in, sgu_w, w_branch, w_out, w_gu, w_dn, gd_A_log, gd_dt_bias, lb):
    offs = np.cumsum([0, BRANCH_W, BRANCH_W, 512, 512, 512, 512, 512, GD_QKV, 8, 8, 512, 3 * NA_W, N_BRANCH * D_MODEL])
    w = w_in[l]
    seg = lambda i, j: w[:, offs[i]:offs[j]]
    w_gd = jnp.concatenate([seg(7, 8), seg(10, 11), seg(8, 10),
                            jnp.zeros((D_MODEL, LANES - 4 * GD_HEADS), F32)], axis=1)
    par = jnp.zeros((2, LANES), F32)
    par = par.at[0, :2 * GD_HEADS].set(gd_A_log[l].reshape(-1)).at[1, :2 * GD_HEADS].set(gd_dt_bias[l].reshape(-1))
    lb_l = lb[:, l]
    return {
        'w_sgu': seg(0, 2).astype(BF16), 'w_hg': seg(2, 7).astype(BF16), 'w_gd': w_gd.astype(BF16),
        'w_na': seg(11, 12).astype(BF16), 'w_mg': seg(12, 13).astype(BF16),
        'sgu_w': sgu_w[l], 'wb': w_branch[l].astype(BF16), 'wo': w_out[l].astype(BF16),
        'w_gu': w_gu, 'w_dn': w_dn, 'layer': l, 'gd_par': par,
        'lbp': jnp.concatenate([jnp.log(lb_l), jnp.log1p(-lb_l), 1.0 - lb_l], axis=0),
    }


def _layer(x, groups, mods, mod_row, lw, p, norm_f, final_norm):
    T = x.shape[0]
    h = _normmod(x, p['norm1'], mods, mod_row, 1024, part_shift=0, part_scale=1)
    z_sgu = _matmul(h, lw['w_sgu'], 1024, 1024)
    z_hg = _matmul(h, lw['w_hg'], 1024, 1280)
    z_gd = _matmul(h, lw['w_gd'], 512, 2176)
    z_na = _matmul(h, lw['w_na'], 1024, 768)
    z_mg = _matmul(h, lw['w_mg'], 1024, 1024)

    x_mix, states = None, []
    for row0, B, L, ctx in groups:
        Tg = B * L
        o_a = _sgu(z_sgu, p['sgu_norm'], lw['sgu_w'], p['sgu_b'], 256, row0, Tg)
        s_hg0 = None if ctx is None else ctx[2]
        s_gd0 = None if ctx is None else ctx[3]
        o_hf, o_hb, s_hg = _hgrn(z_hg, lw['lbp'], s_hg0, B, L, row0)
        qkv = _gdprep(z_gd, p['gd_conv'], L, 256, row0, Tg)
        o_gf, o_gb, s_gd = _gdn(qkv, z_gd, lw['gd_par'], s_gd0, B, L, row0)
        if ctx is None:
            o_d = _ctx_attn(z_na, B, L, row0)
        else:
            o_d = _na_attn(z_na, ctx[0], ctx[1], _na_bias(p['na_rpb'], L // GRID_W), B, L, row0)
        x_mix = _merge(x, o_a, o_hf, o_hb, z_hg, o_gf, o_gb, z_gd, o_d, z_mg, p['hg_onorm'], p['gd_onorm'],
                       lw['wb'], lw['wo'], mods, mod_row, 256, row0, Tg, x_mix)
        states.append((s_hg, s_gd))
    x = x_mix

    h2, top_e, wts = _router(x, p['norm2'], mods, mod_row, p['w_router'], p['b_router'], 512)
    tok, pos, blk_e = _route(top_e[:, :TOP_K], T)
    xb = _sc_gather(h2, tok)
    yb = _experts(xb, blk_e, lw['w_gu'], p['b_gu'], lw['w_dn'], p['b_dn'], lw['layer'])
    yg = _sc_gather(yb, pos.reshape(T, TOP_K).T.reshape(-1)).reshape(TOP_K, T, D_MODEL)
    x = _combine(x, yg, wts, mods, mod_row, norm_f, final_norm, 256)
    return x, z_na, states


def kernel(x_prompt, x_sample, c, cache_na_k, cache_na_v, state_hgrn, state_gdn, c_ctx, w_ada, b_ada, norm1, norm2, norm_f, w_in, sgu_norm, sgu_w, sgu_b, hg_lb, hg_onorm, gd_conv, gd_A_log, gd_dt_bias, gd_onorm, na_rpb, w_branch, w_out, w_router, b_router, w_gu, b_gu, w_dn, b_dn):
    Bp, Lp, D = x_prompt.shape
    Bs, Ls, _ = x_sample.shape
    ctx_row = Bs
    cvecs = jnp.zeros((MOD_ROWS, D), F32).at[:Bs].set(c).at[ctx_row].set(c_ctx)
    mods = _modulation(cvecs, w_ada, b_ada)

    cs = jnp.cumsum(jax.nn.softmax(hg_lb.astype(F32), axis=1), axis=1)
    lb = cs - cs[:, :1]

    Tp, Ts = Bp * Lp, Bs * Ls
    x = jnp.concatenate([x_prompt.reshape(Tp, D), x_sample.reshape(Ts, D)], axis=0)
    mod_row = lambda r: jnp.where(r < Tp, ctx_row, (r - Tp) // Ls)
    ks_, vs_, hs_, gs_ = [], [], [], []
    for l in range(DEPTH):
        lw = _prep_layer(l, w_in, sgu_w, w_branch, w_out, w_gu, w_dn, gd_A_log, gd_dt_bias, lb)
        p = {'norm1': norm1[l], 'norm2': norm2[l], 'sgu_norm': sgu_norm[l], 'sgu_b': sgu_b[l],
             'gd_conv': gd_conv[l], 'hg_onorm': hg_onorm[l], 'gd_onorm': gd_onorm[l], 'na_rpb': na_rpb[l],
             'w_router': w_router[l], 'b_router': b_router[l], 'b_gu': b_gu, 'b_dn': b_dn}
        ctx = (cache_na_k[:, l].reshape(Bs, -1, NA_W), cache_na_v[:, l].reshape(Bs, -1, NA_W),
               state_hgrn[:, l], state_gdn[:, l])
        groups = [(0, Bp, Lp, None), (Tp, Bs, Ls, ctx)]
        x, z_na, states = _layer(x, groups, mods[l], mod_row, lw, p, norm_f, l == DEPTH - 1)
        ks_.append(z_na[:Tp, NA_W:2 * NA_W].reshape(Bp, Lp, NA_HEADS, NA_DH))
        vs_.append(z_na[:Tp, 2 * NA_W:].reshape(Bp, Lp, NA_HEADS, NA_DH))
        hs_.append(states[0][0])
        gs_.append(states[0][1])

    return (x[:Tp].reshape(Bp, Lp, D), x[Tp:].reshape(Bs, Ls, D),
            jnp.stack(ks_, axis=1), jnp.stack(vs_, axis=1), jnp.stack(hs_, axis=1), jnp.stack(gs_, axis=1))
```

```python
import functools
import math

import numpy as np
import jax
import jax.numpy as jnp
from jax import lax
from jax.experimental import pallas as pl
from jax.experimental.pallas import tpu as pltpu
from jax.experimental.pallas import tpu_sc as plsc

D_MODEL = 1024
DEPTH = 2
GRID_W = 64
BRANCH_W = 512
N_BRANCH = 4
SGU_CHUNK = 128
SGU_GROUPS = 4
HG_HEADS = 4
HG_DK = 128
HG_DV = 128
GD_HEADS = 4
GD_DK = 128
GD_DV = 128
NA_HEADS = 8
NA_DH = 64
NA_KH = 8
NA_KW = 16
N_EXP = 32
TOP_K = 4
D_FF = 1024
SWIGLU_LIMIT = 7.0
SWIGLU_ALPHA = 1.702
EPS = 1e-6

F32 = jnp.float32
BF16 = jnp.bfloat16
HI = lax.Precision.HIGHEST

LANES = 128
MOD_ROWS = 16
SCAN_C = 64
MOE_BLOCK = 256
MOE_SPLIT = 2
NA_QROWS = 4
NA_KROWS = NA_QROWS + NA_KH
VMEM_LIMIT = 48 * 1024 * 1024

NT = (((1,), (1,)), ((), ()))
TN = (((0,), (0,)), ((), ()))


def _cparams(*sem):
    return pltpu.CompilerParams(dimension_semantics=sem, vmem_limit_bytes=VMEM_LIMIT)


def _bdot(a, b):
    return jnp.dot(a.astype(BF16), b.astype(BF16), preferred_element_type=F32)


def _bdot_g(a, b, dims):
    return lax.dot_general(a.astype(BF16), b.astype(BF16), dims, preferred_element_type=F32)


def _hdot(a, b):
    return jnp.dot(a, b, precision=HI, preferred_element_type=F32)


def _dot01(m, x):
    hi = x.astype(BF16)
    r1 = x - hi.astype(F32)
    mid = r1.astype(BF16)
    lo = (r1 - mid.astype(F32)).astype(BF16)
    d = lambda t: jnp.dot(m, t, preferred_element_type=F32)
    return (d(lo) + d(mid)) + d(hi)


def _dot3(a, b):
    ah = a.astype(BF16)
    al = (a - ah.astype(F32)).astype(BF16)
    bh = b.astype(BF16)
    bl = (b - bh.astype(F32)).astype(BF16)
    d = lambda x, y: jnp.dot(x, y, preferred_element_type=F32)
    return (d(al, bh) + d(ah, bl)) + d(ah, bh)


def _silu(x):
    return x * jax.nn.sigmoid(x)


def _log_sigmoid(x):
    return jnp.minimum(x, 0.0) - jnp.log1p(jnp.exp(-jnp.abs(x)))


def _logaddexp(a, b):
    return jnp.maximum(a, b) + jnp.log1p(jnp.exp(-jnp.abs(a - b)))


def _softplus(x):
    return jnp.maximum(x, 0.0) + jnp.log1p(jnp.exp(-jnp.abs(x)))


def _ada_kernel(c_ref, w_ref, b_ref, o_ref):
    o_ref[0] = _hdot(_silu(c_ref[...]), w_ref[0]) + b_ref[0]


def _modulation(cvecs, w_ada, b_ada):
    tn = 1536
    out = pl.pallas_call(
        _ada_kernel,
        grid=(DEPTH, 6 * D_MODEL // tn),
        in_specs=[pl.BlockSpec((MOD_ROWS, D_MODEL), lambda l, j: (0, 0)),
                  pl.BlockSpec((1, D_MODEL, tn), lambda l, j: (l, 0, j)),
                  pl.BlockSpec((1, 1, tn), lambda l, j: (l, 0, j))],
        out_specs=pl.BlockSpec((1, MOD_ROWS, tn), lambda l, j: (l, 0, j)),
        out_shape=jax.ShapeDtypeStruct((DEPTH, MOD_ROWS, 6 * D_MODEL), F32),
        compiler_params=_cparams("arbitrary", "arbitrary"),
        name="ada_modulation",
    )(cvecs, w_ada, b_ada.reshape(DEPTH, 1, 6 * D_MODEL))
    return out.reshape(DEPTH, MOD_ROWS * 6, 1, D_MODEL)


def _rms(x):
    return x * lax.rsqrt(jnp.mean(x * x, axis=-1, keepdims=True) + EPS)


def _normmod_kernel(x_ref, g_ref, sc_ref, sh_ref, o_ref):
    h = (_rms(x_ref[...]) * g_ref[...]) * (1.0 + sc_ref[0]) + sh_ref[0]
    o_ref[...] = h.astype(o_ref.dtype)


def _normmod(x, g, mods, mod_row, tm, part_shift, part_scale):
    T = x.shape[0]
    return pl.pallas_call(
        _normmod_kernel,
        grid=(T // tm,),
        in_specs=[pl.BlockSpec((tm, D_MODEL), lambda i: (i, 0)),
                  pl.BlockSpec((1, D_MODEL), lambda i: (0, 0)),
                  pl.BlockSpec((1, 1, D_MODEL), lambda i: (mod_row(i * tm) * 6 + part_scale, 0, 0)),
                  pl.BlockSpec((1, 1, D_MODEL), lambda i: (mod_row(i * tm) * 6 + part_shift, 0, 0))],
        out_specs=pl.BlockSpec((tm, D_MODEL), lambda i: (i, 0)),
        out_shape=jax.ShapeDtypeStruct((T, D_MODEL), BF16),
        compiler_params=_cparams("parallel"),
        name="normmod",
    )(x, g.reshape(1, D_MODEL), mods, mods)


def _mm_kernel(a_ref, w_ref, o_ref):
    o_ref[...] = jnp.dot(a_ref[...], w_ref[...], preferred_element_type=F32).astype(o_ref.dtype)


def _matmul(a, w, tm, tn, out_dtype=F32):
    T, K = a.shape
    N = w.shape[1]
    return pl.pallas_call(
        _mm_kernel,
        grid=(N // tn, T // tm),
        in_specs=[pl.BlockSpec((tm, K), lambda j, i: (i, 0)),
                  pl.BlockSpec((K, tn), lambda j, i: (0, j))],
        out_specs=pl.BlockSpec((tm, tn), lambda j, i: (i, j)),
        out_shape=jax.ShapeDtypeStruct((T, N), out_dtype),
        compiler_params=_cparams("parallel", "parallel"),
        name="in_proj",
    )(a, w)


def _sgu_kernel(u_ref, v_ref, gn_ref, ws_ref, bs_ref, o_ref):
    rows = u_ref.shape[0]
    gw = BRANCH_W // SGU_GROUPS
    u = jax.nn.gelu(u_ref[...])
    v = (_rms(jax.nn.gelu(v_ref[...])) * gn_ref[...]).astype(BF16)
    for n in range(rows // SGU_CHUNK):
        r = slice(n * SGU_CHUNK, (n + 1) * SGU_CHUNK)
        for g in range(SGU_GROUPS):
            cs = slice(g * gw, (g + 1) * gw)
            s = jnp.dot(ws_ref[g], v[r, cs], preferred_element_type=F32) + bs_ref[:, cs]
            o_ref[r, cs] = (u[r, cs] * s).astype(o_ref.dtype)


def _sgu(z_sgu, g_norm, w_s, b_s, rows, row0, T):
    gw = BRANCH_W // SGU_GROUPS
    b_exp = jnp.repeat(b_s.T, gw, axis=1)
    i0 = row0 // rows
    return pl.pallas_call(
        _sgu_kernel,
        grid=(T // rows,),
        in_specs=[pl.BlockSpec((rows, BRANCH_W), lambda i: (i0 + i, 0)),
                  pl.BlockSpec((rows, BRANCH_W), lambda i: (i0 + i, 1)),
                  pl.BlockSpec((1, BRANCH_W), lambda i: (0, 0)),
                  pl.BlockSpec((SGU_GROUPS, SGU_CHUNK, SGU_CHUNK), lambda i: (0, 0, 0)),
                  pl.BlockSpec((SGU_CHUNK, BRANCH_W), lambda i: (0, 0))],
        out_specs=pl.BlockSpec((rows, BRANCH_W), lambda i: (i, 0)),
        out_shape=jax.ShapeDtypeStruct((T, BRANCH_W), BF16),
        compiler_params=_cparams("parallel"),
        name="sgu",
    )(z_sgu, z_sgu, g_norm.reshape(1, BRANCH_W), w_s.astype(BF16), b_exp)


def _order(reverse):
    p = np.arange(SCAN_C)
    return SCAN_C - 1 - p if reverse else p


def _gla_consts():
    C = SCAN_C
    nlev = int(math.log2(C))
    mats, masks = [], []
    for reverse in (False, True):
        p = _order(reverse)
        pt, pr = p[:, None], p[None, :]
        m_d, k_d = [], []
        for lev in range(nlev):
            w = C >> (lev + 1)
            parent = p // (2 * w)
            later = (p % (2 * w)) >= w
            anchor = (parent * 2 * w + w - 1)[:, None]
            m = np.where(later[:, None], (pr > anchor) & (pr <= pt), (pr > pt) & (pr <= anchor))
            m_d.append(m)
            k_d.append((parent[:, None] == parent[None, :]) & later[:, None] & ~later[None, :])
        m_d.append(pr <= pt)
        m_d.append(pr > pt)
        k_d.append(np.eye(C, dtype=bool))
        mats.append(np.concatenate(m_d, axis=0))
        masks.append(np.stack(k_d))
    return (np.stack(mats).astype(np.float32), np.stack(masks).astype(np.float32))


def _delta_consts():
    C = SCAN_C
    tri, sl, incl, strict = [], [], [], []
    for reverse in (False, True):
        p = _order(reverse)
        pt, pr = p[:, None], p[None, :]
        tri.append(np.concatenate([pr <= pt, pr > pt], axis=0))
        sl.append(np.concatenate([pt > pr, np.zeros((C, LANES - C), bool), np.ones((C, LANES), bool)], axis=1))
        incl.append(pr <= pt)
        strict.append(pr < pt)
    f = lambda a: np.stack(a).astype(np.float32)
    return f(tri), f(sl), f(incl), f(strict)


def _hgrn_kernel(qf_ref, ff_ref, vf_ref, qb_ref, fb_ref, vb_ref, lb_ref, mat_ref, msk_ref, *rest, has_state):
    if has_state:
        s0_ref, of_ref, ob_ref, sfin_ref, st_ref = rest
    else:
        of_ref, ob_ref, sfin_ref, st_ref = rest
    C = SCAN_C
    nlev = msk_ref.shape[1] - 1
    c = pl.program_id(1)
    last_c = pl.num_programs(1) - 1

    @pl.when(c == 0)
    def _():
        for d in range(2):
            for h in range(HG_HEADS):
                if has_state:
                    st_ref[d, h] = s0_ref[0, d, h].T
                else:
                    st_ref[d, h] = jnp.zeros((HG_DV, HG_DK), F32)

    q_refs, f_refs, v_refs, o_refs = (qf_ref, qb_ref), (ff_ref, fb_ref), (vf_ref, vb_ref), (of_ref, ob_ref)
    chains = [(d, h) for d in range(2) for h in range(HG_HEADS)]
    col = lambda h: slice(h * HG_DK, (h + 1) * HG_DK)
    q, k, fac, att = {}, {}, {}, {}
    for d, h in chains:
        zf = f_refs[d][:, col(h)]
        loglb = lb_ref[d:d + 1, col(h)]
        log1m = lb_ref[2 + d:3 + d, col(h)]
        onem = lb_ref[4 + d:5 + d, col(h)]
        logf = _logaddexp(loglb, log1m + _log_sigmoid(zf))
        k[d, h] = onem * jax.nn.sigmoid(-zf)
        q[d, h] = _silu(q_refs[d][:, col(h)]) * (HG_DK ** -0.5)
        fac[d, h] = jnp.exp(_dot01(mat_ref[d], logf))
    for d, h in chains:
        acc = msk_ref[d, nlev] * _bdot_g(q[d, h], k[d, h], NT)
        for i in range(nlev):
            fi = fac[d, h][i * C:(i + 1) * C]
            acc = acc + msk_ref[d, i] * _bdot_g(q[d, h] * fi, k[d, h] * fi, NT)
        att[d, h] = acc
    for d, h in chains:
        eb = fac[d, h][nlev * C:(nlev + 1) * C]
        o_refs[d][:, col(h)] = (_bdot(att[d, h], v_refs[d][:, col(h)])
                                + _bdot_g(q[d, h] * eb, st_ref[d, h], NT))
    for d, h in chains:
        eb = fac[d, h][nlev * C:(nlev + 1) * C]
        er = fac[d, h][(nlev + 1) * C:]
        e_last = eb[C - 1:C] if d == 0 else eb[0:1]
        st_ref[d, h] = st_ref[d, h] * e_last + _bdot_g(v_refs[d][:, col(h)], k[d, h] * er, TN)

    @pl.when(c == last_c)
    def _():
        for d in range(2):
            for h in range(HG_HEADS):
                sfin_ref[0, d, h] = st_ref[d, h].T


def _hgrn(z_hg, lbp, s0, B, L, row0):
    n = L // SCAN_C
    c0 = row0 // SCAN_C
    mats, masks = _gla_consts()
    blk = (SCAN_C, HG_HEADS * HG_DK)
    fwd = lambda col: pl.BlockSpec(blk, lambda b, c: (c0 + b * n + c, col))
    bwd = lambda col: pl.BlockSpec(blk, lambda b, c: (c0 + b * n + n - 1 - c, col))
    st_blk = (1, 2, HG_HEADS, HG_DK, HG_DV)
    in_specs = [fwd(0), fwd(1), fwd(3), bwd(0), bwd(2), bwd(3),
                pl.BlockSpec(lbp.shape, lambda b, c: (0, 0)),
                pl.BlockSpec(mats.shape, lambda b, c: (0, 0, 0)),
                pl.BlockSpec(masks.shape, lambda b, c: (0, 0, 0, 0))]
    args = [z_hg] * 6 + [lbp, jnp.asarray(mats, BF16), jnp.asarray(masks)]
    if s0 is not None:
        in_specs.append(pl.BlockSpec(st_blk, lambda b, c: (b, 0, 0, 0, 0)))
        args.append(s0)
    return pl.pallas_call(
        functools.partial(_hgrn_kernel, has_state=s0 is not None),
        grid=(B, n),
        in_specs=in_specs,
        out_specs=[pl.BlockSpec(blk, lambda b, c: (b * n + c, 0)),
                   pl.BlockSpec(blk, lambda b, c: (b * n + n - 1 - c, 0)),
                   pl.BlockSpec(st_blk, lambda b, c: (b, 0, 0, 0, 0))],
        out_shape=[jax.ShapeDtypeStruct((B * L, HG_HEADS * HG_DV), F32),
                   jax.ShapeDtypeStruct((B * L, HG_HEADS * HG_DV), F32),
                   jax.ShapeDtypeStruct((B,) + st_blk[1:], F32)],
        scratch_shapes=[pltpu.VMEM((2, HG_HEADS, HG_DV, HG_DK), F32)],
        compiler_params=_cparams("parallel", "arbitrary"),
        name="hgrn_scan",
    )(*args)


GD_NQ = GD_HEADS * GD_DK
GD_QKV = 2 * GD_NQ + GD_HEADS * GD_DV
HALO = 8


def _gdprep_kernel(x_ref, prev_ref, next_ref, w_ref, o_ref, *, tiles_per_seq):
    R = x_ref.shape[0]
    t = pl.program_id(0) % tiles_per_seq
    x = x_ref[...]
    prev_row = jnp.where(t == 0, 0.0, prev_ref[HALO - 1:HALO, :])
    next_row = jnp.where(t == tiles_per_seq - 1, 0.0, next_ref[0:1, :])
    row = lax.broadcasted_iota(jnp.int32, x.shape, 0)
    xm1 = jnp.where(row == 0, prev_row, pltpu.roll(x, 1, 0))
    xp1 = jnp.where(row == R - 1, next_row, pltpu.roll(x, R - 1, 0))
    y = _silu(w_ref[0:1, :] * xm1 + w_ref[1:2, :] * x + w_ref[2:3, :] * xp1)
    for j in range(2 * GD_HEADS):
        cs = slice(j * GD_DK, (j + 1) * GD_DK)
        seg = y[:, cs]
        seg = seg * lax.rsqrt(jnp.sum(seg * seg, axis=-1, keepdims=True) + EPS)
        if j < GD_HEADS:
            seg = seg * (GD_DK ** -0.5)
        o_ref[:, cs] = seg
    o_ref[:, 2 * GD_NQ:] = y[:, 2 * GD_NQ:]


def _gdprep(z_gd, conv_w, L, rows, row0, T):
    tps = L // rows
    hb = rows // HALO
    nhalo = z_gd.shape[0] // HALO
    i0 = row0 // rows
    return pl.pallas_call(
        functools.partial(_gdprep_kernel, tiles_per_seq=tps),
        grid=(T // rows,),
        in_specs=[pl.BlockSpec((rows, GD_QKV), lambda i: (i0 + i, 0)),
                  pl.BlockSpec((HALO, GD_QKV), lambda i: (jnp.maximum((i0 + i) * hb - 1, 0), 0)),
                  pl.BlockSpec((HALO, GD_QKV), lambda i: (jnp.minimum((i0 + i + 1) * hb, nhalo - 1), 0)),
                  pl.BlockSpec((3, GD_QKV), lambda i: (0, 0))],
        out_specs=pl.BlockSpec((rows, GD_QKV), lambda i: (i, 0)),
        out_shape=jax.ShapeDtypeStruct((T, GD_QKV), F32),
        compiler_params=_cparams("parallel"),
        name="gdn_prep",
    )(z_gd, z_gd, z_gd, conv_w)


def _gdn_kernel(xf_ref, abf_ref, xb_ref, abb_ref, par_ref, tri_ref, sl_ref, incl_ref, strict_ref, msk_ref, *rest,
                has_state):
    if has_state:
        s0_ref, of_ref, ob_ref, sfin_ref, st_ref = rest
    else:
        of_ref, ob_ref, sfin_ref, st_ref = rest
    C = SCAN_C
    c = pl.program_id(1)
    last_c = pl.num_programs(1) - 1
    nlev = msk_ref.shape[1] - 1

    @pl.when(c == 0)
    def _():
        for d in range(2):
            for h in range(GD_HEADS):
                if has_state:
                    st_ref[d, h] = s0_ref[0, d, h].T
                else:
                    st_ref[d, h] = jnp.zeros((GD_DV, GD_DK), F32)

    x_refs, ab_refs, o_refs = (xf_ref, xb_ref), (abf_ref, abb_ref), (of_ref, ob_ref)
    chains = [(d, h) for d in range(2) for h in range(GD_HEADS)]
    q_of = lambda d, h: x_refs[d][:, h * GD_DK:(h + 1) * GD_DK]
    k_of = lambda d, h: x_refs[d][:, GD_NQ + h * GD_DK:GD_NQ + (h + 1) * GD_DK]
    v_of = lambda d, h: x_refs[d][:, 2 * GD_NQ + h * GD_DV:2 * GD_NQ + (h + 1) * GD_DV]
    g_all, beta_all = [], []
    for d in range(2):
        ab = ab_refs[d][...]
        g_all.append(-jnp.exp(par_ref[0:1, :]) * _softplus(ab + par_ref[1:2, :]))
        beta_all.append(jax.nn.sigmoid(ab))
    decay, e_cum, e_rest, kb, a = {}, {}, {}, {}, {}
    for d, h in chains:
        j = d * GD_HEADS + h
        g_b = jnp.broadcast_to(g_all[d][:, j:j + 1], (C, 2 * LANES))
        sums = _dot01(tri_ref[d], g_b * sl_ref[d])
        decay[d, h] = jnp.exp(sums[:C, :C])
        e_cum[d, h] = jnp.exp(sums[:C, LANES:])
        e_rest[d, h] = jnp.exp(sums[C:, LANES:])
    for d, h in chains:
        j = d * GD_HEADS + h
        beta = jnp.broadcast_to(beta_all[d][:, 2 * GD_HEADS + j:2 * GD_HEADS + j + 1], (C, LANES))
        k = k_of(d, h)
        kb[d, h] = k * beta
        a[d, h] = strict_ref[d] * decay[d, h] * _bdot_g(kb[d, h], k, NT)
    inv_m = {(d, h): -(msk_ref[d, nlev - 1] * a[d, h]) for d, h in chains}
    for lev in range(nlev - 2, -1, -1):
        a_w = {(d, h): msk_ref[d, lev] * a[d, h] for d, h in chains}
        p = {ch: a_w[ch] + _bdot(inv_m[ch], a_w[ch]) for ch in chains}
        inv_m = {ch: inv_m[ch] - p[ch] - _bdot(p[ch], inv_m[ch]) for ch in chains}
    sol = {}
    for d, h in chains:
        j = d * GD_HEADS + h
        beta = jnp.broadcast_to(beta_all[d][:, 2 * GD_HEADS + j:2 * GD_HEADS + j + 1], (C, LANES))
        rhs = jnp.concatenate([v_of(d, h) * beta, kb[d, h] * e_cum[d, h]], axis=1)
        sol[d, h] = rhs + _dot3(inv_m[d, h], rhs)
    u = {ch: sol[ch][:, :GD_DV] - _bdot_g(sol[ch][:, GD_DV:], st_ref[ch[0], ch[1]], NT) for ch in chains}
    att = {(d, h): incl_ref[d] * decay[d, h] * _bdot_g(q_of(d, h), k_of(d, h), NT) for d, h in chains}
    for d, h in chains:
        o_refs[d][:, h * GD_DV:(h + 1) * GD_DV] = (_bdot_g(q_of(d, h) * e_cum[d, h], st_ref[d, h], NT)
                                                   + _bdot(att[d, h], u[d, h]))
    for d, h in chains:
        e_last = e_cum[d, h][C - 1:C] if d == 0 else e_cum[d, h][0:1]
        st_ref[d, h] = st_ref[d, h] * e_last + _bdot_g(u[d, h], k_of(d, h) * e_rest[d, h], TN)

    @pl.when(c == last_c)
    def _():
        for d in range(2):
            for h in range(GD_HEADS):
                sfin_ref[0, d, h] = st_ref[d, h].T


def _gdn(qkv, z_gd, par, s0, B, L, row0):
    n = L // SCAN_C
    c0 = row0 // SCAN_C
    tri, sl, incl, strict = (jnp.asarray(a) for a in _delta_consts())
    tri = tri.astype(BF16)
    masks = jnp.asarray(_gla_consts()[1])
    ab_col = (GD_QKV + GD_HEADS * GD_DV) // LANES
    xblk = (SCAN_C, GD_QKV)
    ablk = (SCAN_C, LANES)
    oblk = (SCAN_C, GD_HEADS * GD_DV)
    st_blk = (1, 2, GD_HEADS, GD_DK, GD_DV)
    fwd = lambda b, c: b * n + c
    bwd = lambda b, c: b * n + n - 1 - c
    const3 = lambda a: pl.BlockSpec(a.shape, lambda b, c: (0, 0, 0))
    in_specs = [pl.BlockSpec(xblk, lambda b, c: (fwd(b, c), 0)),
                pl.BlockSpec(ablk, lambda b, c: (c0 + fwd(b, c), ab_col)),
                pl.BlockSpec(xblk, lambda b, c: (bwd(b, c), 0)),
                pl.BlockSpec(ablk, lambda b, c: (c0 + bwd(b, c), ab_col)),
                pl.BlockSpec(par.shape, lambda b, c: (0, 0)),
                const3(tri), const3(sl), const3(incl), const3(strict),
                pl.BlockSpec(masks.shape, lambda b, c: (0, 0, 0, 0))]
    args = [qkv, z_gd, qkv, z_gd, par, tri, sl, incl, strict, masks]
    if s0 is not None:
        in_specs.append(pl.BlockSpec(st_blk, lambda b, c: (b, 0, 0, 0, 0)))
        args.append(s0)
    return pl.pallas_call(
        functools.partial(_gdn_kernel, has_state=s0 is not None),
        grid=(B, n),
        in_specs=in_specs,
        out_specs=[pl.BlockSpec(oblk, lambda b, c: (fwd(b, c), 0)),
                   pl.BlockSpec(oblk, lambda b, c: (bwd(b, c), 0)),
                   pl.BlockSpec(st_blk, lambda b, c: (b, 0, 0, 0, 0))],
        out_shape=[jax.ShapeDtypeStruct((B * L, GD_HEADS * GD_DV), F32),
                   jax.ShapeDtypeStruct((B * L, GD_HEADS * GD_DV), F32),
                   jax.ShapeDtypeStruct((B,) + st_blk[1:], F32)],
        scratch_shapes=[pltpu.VMEM((2, GD_HEADS, GD_DV, GD_DK), F32)],
        compiler_params=_cparams("parallel", "arbitrary"),
        name="gdn_scan",
    )(*args)


NA_W = NA_HEADS * NA_DH


def _softmax_pv(s, v):
    m = jnp.max(s, axis=-1, keepdims=True)
    e = jnp.exp(s - m)
    den = jnp.sum(e, axis=-1, keepdims=True)
    return jnp.dot(e.astype(BF16), v, preferred_element_type=F32) / den


def _ctx_attn_kernel(q_ref, k_ref, v_ref, o_ref):
    for h in range(NA_HEADS):
        cs = slice(h * NA_DH, (h + 1) * NA_DH)
        q = (q_ref[:, cs] * (NA_DH ** -0.5)).astype(BF16)
        s = lax.dot_general(q, k_ref[:, cs].astype(BF16), NT, preferred_element_type=F32)
        o_ref[:, cs] = _softmax_pv(s, v_ref[:, cs].astype(BF16)).astype(o_ref.dtype)


def _ctx_attn(z_na, B, L, row0):
    blk = (L, NA_W)
    b0 = row0 // L
    return pl.pallas_call(
        _ctx_attn_kernel,
        grid=(B,),
        in_specs=[pl.BlockSpec(blk, lambda b: (b0 + b, 0)),
                  pl.BlockSpec(blk, lambda b: (b0 + b, 1)),
                  pl.BlockSpec(blk, lambda b: (b0 + b, 2))],
        out_specs=pl.BlockSpec(blk, lambda b: (b, 0)),
        out_shape=jax.ShapeDtypeStruct((B * L, NA_W), BF16),
        compiler_params=_cparams("parallel"),
        name="ctx_attn",
    )(z_na, z_na, z_na)


def _na_kernel(q_ref, *rest, n_kblk, nkeys_nb):
    k_refs = rest[:n_kblk]
    v_refs = rest[n_kblk:2 * n_kblk]
    kc_ref, vc_ref, bias_ref, o_ref, kbuf, vbuf = rest[2 * n_kblk:]
    qb = q_ref.shape[0]
    for i in range(n_kblk):
        kbuf[i * qb:(i + 1) * qb, :] = k_refs[i][...].astype(BF16)
        vbuf[i * qb:(i + 1) * qb, :] = v_refs[i][...].astype(BF16)
    kbuf[nkeys_nb:, :] = kc_ref[0].astype(BF16)
    vbuf[nkeys_nb:, :] = vc_ref[0].astype(BF16)
    for h in range(NA_HEADS):
        cs = slice(h * NA_DH, (h + 1) * NA_DH)
        q = (q_ref[:, cs] * (NA_DH ** -0.5)).astype(BF16)
        s = lax.dot_general(q, kbuf[:, cs], NT, preferred_element_type=F32)
        s_nb = s[:, :nkeys_nb] + bias_ref[0, h]
        s_cx = s[:, nkeys_nb:]
        m = jnp.maximum(jnp.max(s_nb, axis=-1, keepdims=True), jnp.max(s_cx, axis=-1, keepdims=True))
        e_nb = jnp.exp(s_nb - m)
        e_cx = jnp.exp(s_cx - m)
        den = jnp.sum(e_nb, axis=-1, keepdims=True) + jnp.sum(e_cx, axis=-1, keepdims=True)
        pv = (jnp.dot(e_nb.astype(BF16), vbuf[:nkeys_nb, cs], preferred_element_type=F32)
              + jnp.dot(e_cx.astype(BF16), vbuf[nkeys_nb:, cs], preferred_element_type=F32))
        o_ref[:, cs] = (pv / den).astype(o_ref.dtype)


def _na_bias(rpb, rows):
    qr, kr, col = np.arange(NA_QROWS), np.arange(NA_KROWS), np.arange(GRID_W)
    nblk = rows // NA_QROWS
    ndr, ndc = 2 * NA_KH - 1, 2 * NA_KW - 1
    sel_r, row_ok = [], []
    for m in (0, 1, nblk - 1):
        r = (NA_QROWS * m + qr)[:, None]
        start = np.clip(NA_QROWS * m - NA_KH // 2, 0, rows - NA_KROWS)
        kra = (start + kr)[None, :]
        r0 = np.clip(r - NA_KH // 2, 0, rows - NA_KH)
        row_ok.append((kra >= r0) & (kra < r0 + NA_KH))
        dr = np.clip(kra - r + NA_KH - 1, 0, ndr - 1)
        sel_r.append(dr[..., None] == np.arange(ndr))
    sel_r = np.stack(sel_r).astype(np.float32)
    row_ok = np.stack(row_ok)
    col_start = np.clip(col - NA_KW // 2, 0, GRID_W - NA_KW)[:, None]
    col_ok = (col[None, :] >= col_start) & (col[None, :] < col_start + NA_KW)
    dc = np.clip(col[None, :] - col[:, None], -(NA_KW - 1), NA_KW - 1) + NA_KW - 1
    sel_c = (dc[..., None] == np.arange(ndc)).astype(np.float32)
    bias = jnp.einsum('hab,vqka,xyb->vhqxky', rpb.astype(F32), sel_r, sel_c, precision=HI)
    ok = row_ok[:, None, :, None, :, None] & col_ok[None, None, None, :, None, :]
    bias = jnp.where(ok, bias, -jnp.inf)
    return bias.reshape(3, NA_HEADS, NA_QROWS * GRID_W, NA_KROWS * GRID_W)


def _na_attn(z_na, k_ctx, v_ctx, bias, B, S, row0):
    rows = S // GRID_W
    qb = NA_QROWS * GRID_W
    m0 = row0 // qb
    nblk = rows // NA_QROWS
    n_kblk = NA_KROWS // NA_QROWS
    lc = k_ctx.shape[1]
    nkeys_nb = NA_KROWS * GRID_W
    kstart = lambda m: jnp.clip(m - 1, 0, nblk - n_kblk)
    variant = lambda m: jnp.where(m == 0, 0, jnp.where(m == nblk - 1, 2, 1))
    kv_specs = lambda col: [pl.BlockSpec((qb, NA_W), functools.partial(
        lambda b, m, i, col: (m0 + b * nblk + kstart(m) + i, col), i=i, col=col)) for i in range(n_kblk)]
    return pl.pallas_call(
        functools.partial(_na_kernel, n_kblk=n_kblk, nkeys_nb=nkeys_nb),
        grid=(B, nblk),
        in_specs=([pl.BlockSpec((qb, NA_W), lambda b, m: (m0 + b * nblk + m, 0))] + kv_specs(1) + kv_specs(2)
                  + [pl.BlockSpec((1, lc, NA_W), lambda b, m: (b, 0, 0)),
                     pl.BlockSpec((1, lc, NA_W), lambda b, m: (b, 0, 0)),
                     pl.BlockSpec((1,) + bias.shape[1:], lambda b, m: (variant(m), 0, 0, 0))]),
        out_specs=pl.BlockSpec((qb, NA_W), lambda b, m: (b * nblk + m, 0)),
        out_shape=jax.ShapeDtypeStruct((B * S, NA_W), BF16),
        scratch_shapes=[pltpu.VMEM((nkeys_nb + lc, NA_W), BF16), pltpu.VMEM((nkeys_nb + lc, NA_W), BF16)],
        compiler_params=_cparams("parallel", "arbitrary"),
        name="na_attn",
    )(z_na, *([z_na] * (2 * n_kblk)), k_ctx, v_ctx, bias)


def _head_rms(o, g_row):
    parts = []
    for h in range(o.shape[1] // LANES):
        parts.append(_rms(o[:, h * LANES:(h + 1) * LANES]) * g_row)
    return jnp.concatenate(parts, axis=1)


def _merge_kernel(x_ref, oa_ref, hf_ref, hb_ref, hg_ref, gf_ref, gb_ref, gg_ref, od_ref, mg_ref,
                  hn_ref, gn_ref, wb_ref, wo_ref, g1_ref, *rest):
    o_ref = rest[-1]
    o_b = _head_rms(hf_ref[...] + hb_ref[...], hn_ref[...]) * _silu(hg_ref[...])
    o_c = _head_rms(gf_ref[...] + gb_ref[...], gn_ref[...]) * _silu(gg_ref[...])
    branches = (oa_ref[...], o_b.astype(BF16), o_c.astype(BF16), od_ref[...])
    merged = None
    for n_, o_n in enumerate(branches):
        gate = jax.nn.sigmoid(mg_ref[:, n_ * D_MODEL:(n_ + 1) * D_MODEL])
        term = gate * jnp.dot(o_n, wb_ref[n_], preferred_element_type=F32)
        merged = term if merged is None else merged + term
    mix = jnp.dot(merged.astype(BF16), wo_ref[...], preferred_element_type=F32)
    o_ref[...] = x_ref[...] + g1_ref[0] * mix


def _merge(x, o_a, o_hf, o_hb, z_hg, o_gf, o_gb, z_gd, o_d, z_mg, hg_onorm, gd_onorm, wb, wo, mods, mod_row, tm,
           row0, T, x_acc):
    i0 = row0 // tm
    glob = lambda w, col=0: pl.BlockSpec((tm, w), lambda i: (i0 + i, col))
    loc = lambda w: pl.BlockSpec((tm, w), lambda i: (i, 0))
    full = lambda a: pl.BlockSpec(a.shape, lambda i: (0,) * a.ndim)
    hn = hg_onorm.reshape(1, HG_DV)
    gn = gd_onorm.reshape(1, GD_DV)
    in_specs = [glob(D_MODEL), loc(BRANCH_W), loc(BRANCH_W), loc(BRANCH_W), glob(BRANCH_W, 4),
                loc(BRANCH_W), loc(BRANCH_W), glob(BRANCH_W, GD_QKV // BRANCH_W), loc(BRANCH_W),
                glob(N_BRANCH * D_MODEL), full(hn), full(gn), full(wb), full(wo),
                pl.BlockSpec((1, 1, D_MODEL), lambda i: (mod_row(row0 + i * tm) * 6 + 2, 0, 0))]
    args = [x, o_a, o_hf, o_hb, z_hg, o_gf, o_gb, z_gd, o_d, z_mg, hn, gn, wb, wo, mods]
    aliases = {}
    if x_acc is not None:
        in_specs.append(pl.BlockSpec(memory_space=pl.ANY))
        aliases = {len(args): 0}
        args.append(x_acc)
    return pl.pallas_call(
        _merge_kernel,
        grid=(T // tm,),
        in_specs=in_specs,
        out_specs=glob(D_MODEL),
        out_shape=jax.ShapeDtypeStruct(x.shape, F32),
        input_output_aliases=aliases,
        compiler_params=_cparams("parallel"),
        name="merge",
    )(*args)


SC_CORES = 2
SC_SUBCORES = 16
SC_WIN = 32


def _sc_gather(table, idx):
    V, D = table.shape
    N = idx.shape[0]
    nw = SC_CORES * SC_SUBCORES
    per_w = N // nw
    n_win = per_w // SC_WIN
    assert per_w * nw == N and n_win * SC_WIN == per_w
    mesh = plsc.VectorSubcoreMesh(core_axis_name="c", subcore_axis_name="s")

    @functools.partial(
        pl.kernel, mesh=mesh,
        out_type=jax.ShapeDtypeStruct((N, D), table.dtype),
        scratch_types=[pltpu.VMEM((n_win, SC_WIN), jnp.int32),
                       pltpu.VMEM((SC_WIN, D), table.dtype),
                       pltpu.SemaphoreType.DMA],
    )
    def gather_rows(table_hbm, idx_hbm, out_hbm, idx_v, rows_v, sem):
        wid = lax.axis_index("s") * SC_CORES + lax.axis_index("c")
        pltpu.sync_copy(idx_hbm.at[wid], idx_v)

        @pl.loop(0, n_win)
        def _(w):
            pltpu.async_copy(table_hbm.at[idx_v.at[w]], rows_v, sem).wait()
            pltpu.sync_copy(rows_v, out_hbm.at[pl.ds(wid * per_w + w * SC_WIN, SC_WIN)])

    return gather_rows(table, idx.reshape(nw, n_win, SC_WIN))


def _router_kernel(x_ref, g_ref, sc_ref, sh_ref, wr_ref, br_ref, h_ref, e_ref, w_ref):
    h = (_rms(x_ref[...]) * g_ref[...]) * (1.0 + sc_ref[0]) + sh_ref[0]
    h_ref[...] = h.astype(h_ref.dtype)
    logits = _hdot(h, wr_ref[...]) + br_ref[...]
    lane = lax.broadcasted_iota(jnp.int32, logits.shape, 1)
    e_out = jnp.zeros(logits.shape, jnp.int32)
    v_out = jnp.zeros(logits.shape, F32)
    top0 = None
    for k in range(TOP_K):
        m = jnp.max(logits, axis=-1, keepdims=True)
        idx = jnp.min(jnp.where(logits == m, lane, LANES), axis=-1, keepdims=True)
        if k == 0:
            top0 = m
        e_out = jnp.where(lane == k, idx, e_out)
        v_out = jnp.where(lane == k, jnp.exp(m - top0), v_out)
        logits = jnp.where(lane == idx, -jnp.inf, logits)
    e_ref[...] = e_out
    w_ref[...] = v_out / jnp.sum(v_out, axis=-1, keepdims=True)


def _router(x, g, mods, mod_row, w_router, b_router, tm):
    T = x.shape[0]
    wr = jnp.zeros((D_MODEL, LANES), F32).at[:, :N_EXP].set(w_router)
    br = jnp.full((1, LANES), -jnp.inf, F32).at[0, :N_EXP].set(b_router)
    row = lambda w: pl.BlockSpec((tm, w), lambda i: (i, 0))
    return pl.pallas_call(
        _router_kernel,
        grid=(T // tm,),
        in_specs=[row(D_MODEL),
                  pl.BlockSpec((1, D_MODEL), lambda i: (0, 0)),
                  pl.BlockSpec((1, 1, D_MODEL), lambda i: (mod_row(i * tm) * 6 + 4, 0, 0)),
                  pl.BlockSpec((1, 1, D_MODEL), lambda i: (mod_row(i * tm) * 6 + 3, 0, 0)),
                  pl.BlockSpec((D_MODEL, LANES), lambda i: (0, 0)),
                  pl.BlockSpec((1, LANES), lambda i: (0, 0))],
        out_specs=[row(D_MODEL), row(LANES), row(LANES)],
        out_shape=[jax.ShapeDtypeStruct((T, D_MODEL), F32),
                   jax.ShapeDtypeStruct((T, LANES), jnp.int32),
                   jax.ShapeDtypeStruct((T, LANES), F32)],
        compiler_params=_cparams("parallel"),
        name="router",
    )(x, g.reshape(1, D_MODEL), mods, mods, wr, br)


def _expert_kernel(blk_e_ref, x_ref, wgu_ref, bgu_ref, wdn_ref, bdn_ref, o_ref, wgu_bf, wdn_bf):
    i = pl.program_id(0)
    new_expert = jnp.logical_or(i == 0, blk_e_ref[i] != blk_e_ref[jnp.maximum(i - 1, 0)])

    @pl.when(new_expert)
    def _():
        wgu_bf[...] = wgu_ref[0].astype(BF16)
        wdn_bf[...] = wdn_ref[0].astype(BF16)

    gu = jnp.dot(x_ref[...].astype(BF16), wgu_bf[...], preferred_element_type=F32) + bgu_ref[0]
    a = jnp.minimum(gu[:, :D_FF], SWIGLU_LIMIT)
    lin = jnp.clip(gu[:, D_FF:], -SWIGLU_LIMIT, SWIGLU_LIMIT)
    y = a * jax.nn.sigmoid(SWIGLU_ALPHA * a) * (lin + 1.0)
    o_ref[...] = jnp.dot(y.astype(BF16), wdn_bf[...], preferred_element_type=F32) + bdn_ref[0]


def _experts(xb, blk_e, w_gu, b_gu, w_dn, b_dn, layer):
    n_pad = xb.shape[0]
    n_blocks = n_pad // MOE_BLOCK
    e0 = layer * N_EXP
    w_gu = w_gu.reshape(DEPTH * N_EXP, D_MODEL, 2 * D_FF)
    w_dn = w_dn.reshape(DEPTH * N_EXP, D_FF, D_MODEL)
    grid_spec = pltpu.PrefetchScalarGridSpec(
        num_scalar_prefetch=1,
        grid=(n_blocks,),
        in_specs=[pl.BlockSpec((MOE_BLOCK, D_MODEL), lambda i, e: (i, 0)),
                  pl.BlockSpec((1, D_MODEL, 2 * D_FF), lambda i, e: (e0 + e[i], 0, 0)),
                  pl.BlockSpec((1, 1, 2 * D_FF), lambda i, e: (e0 + e[i], 0, 0)),
                  pl.BlockSpec((1, D_FF, D_MODEL), lambda i, e: (e0 + e[i], 0, 0)),
                  pl.BlockSpec((1, 1, D_MODEL), lambda i, e: (e0 + e[i], 0, 0))],
        out_specs=pl.BlockSpec((MOE_BLOCK, D_MODEL), lambda i, e: (i, 0)),
        scratch_shapes=[pltpu.VMEM((D_MODEL, 2 * D_FF), BF16), pltpu.VMEM((D_FF, D_MODEL), BF16)],
    )
    return pl.pallas_call(
        _expert_kernel,
        grid_spec=grid_spec,
        out_shape=jax.ShapeDtypeStruct((n_pad, D_MODEL), F32),
        compiler_params=_cparams("arbitrary"),
        name="experts",
    )(blk_e, xb, w_gu, b_gu.reshape(DEPTH * N_EXP, 1, 2 * D_FF), w_dn, b_dn.reshape(DEPTH * N_EXP, 1, D_MODEL))


def _combine_kernel(x_ref, y_ref, w_ref, g2_ref, nf_ref, *rest, final_norm):
    o_ref = rest[-1]
    acc = None
    for k in range(TOP_K):
        term = y_ref[k] * w_ref[:, k:k + 1]
        acc = term if acc is None else acc + term
    x = x_ref[...] + g2_ref[0] * acc
    if final_norm:
        x = _rms(x) * nf_ref[...]
    o_ref[...] = x


def _combine(x, yg, wts, mods, mod_row, norm_f, final_norm, tm, row0, x_acc):
    T = yg.shape[1]
    i0 = row0 // tm
    glob = lambda w: pl.BlockSpec((tm, w), lambda i: (i0 + i, 0))
    in_specs = [glob(D_MODEL), pl.BlockSpec((TOP_K, tm, D_MODEL), lambda i: (0, i, 0)), glob(LANES),
                pl.BlockSpec((1, 1, D_MODEL), lambda i: (mod_row(row0 + i * tm) * 6 + 5, 0, 0)),
                pl.BlockSpec((1, D_MODEL), lambda i: (0, 0))]
    args = [x, yg, wts, mods, norm_f.reshape(1, D_MODEL)]
    aliases = {}
    if x_acc is not None:
        in_specs.append(pl.BlockSpec(memory_space=pl.ANY))
        aliases = {len(args): 0}
        args.append(x_acc)
    return pl.pallas_call(
        functools.partial(_combine_kernel, final_norm=final_norm),
        grid=(T // tm,),
        in_specs=in_specs,
        out_specs=glob(D_MODEL),
        out_shape=jax.ShapeDtypeStruct(x.shape, F32),
        input_output_aliases=aliases,
        compiler_params=_cparams("parallel"),
        name="combine",
    )(*args)


def _route(top_e, T):
    n_assign = T * TOP_K
    n_blocks = n_assign // MOE_BLOCK + N_EXP
    e_flat = top_e.reshape(n_assign)
    onehot = e_flat[:, None] == jnp.arange(N_EXP, dtype=jnp.int32)[None, :]
    counts = jnp.sum(onehot, axis=0, dtype=jnp.int32)
    start = jnp.cumsum(counts) - counts
    padded = (counts + MOE_BLOCK - 1) // MOE_BLOCK * MOE_BLOCK
    pad_end = jnp.cumsum(padded)
    pad_start = pad_end - padded
    iota = jnp.arange(n_assign, dtype=jnp.int32)
    _, order = lax.sort((e_flat, iota), num_keys=1, is_stable=True)
    _, rank = lax.sort((order, iota), num_keys=1)
    pos = rank + jnp.sum(jnp.where(onehot, (pad_start - start)[None, :], 0), axis=1)
    blk_first = jnp.arange(n_blocks, dtype=jnp.int32) * MOE_BLOCK
    blk_e = jnp.minimum(jnp.sum(pad_end[None, :] <= blk_first[:, None], axis=1), N_EXP - 1).astype(jnp.int32)
    r = blk_first[:, None] - pad_start[blk_e][:, None] + jnp.arange(MOE_BLOCK, dtype=jnp.int32)[None, :]
    valid = r < counts[blk_e][:, None]
    src = jnp.clip(start[blk_e][:, None] + r, 0, n_assign - 1)
    filler = (blk_first[:, None] + jnp.arange(MOE_BLOCK, dtype=jnp.int32)[None, :]) % T
    tok = jnp.where(valid, order[src] // TOP_K, filler).reshape(n_blocks * MOE_BLOCK).astype(jnp.int32)
    return tok, pos, blk_e


def _prep_layer(l, w_in, sgu_w, w_branch, w_out, w_gu, w_dn, gd_A_log, gd_dt_bias, lb):
    offs = np.cumsum([0, BRANCH_W, BRANCH_W, 512, 512, 512, 512, 512, GD_QKV, 8, 8, 512, 3 * NA_W, N_BRANCH * D_MODEL])
    w = w_in[l]
    seg = lambda i, j: w[:, offs[i]:offs[j]]
    w_gd = jnp.concatenate([seg(7, 8), seg(10, 11), seg(8, 10),
                            jnp.zeros((D_MODEL, LANES - 4 * GD_HEADS), F32)], axis=1)
    par = jnp.zeros((2, LANES), F32)
    par = par.at[0, :2 * GD_HEADS].set(gd_A_log[l].reshape(-1)).at[1, :2 * GD_HEADS].set(gd_dt_bias[l].reshape(-1))
    lb_l = lb[:, l]
    return {
        'w_sgu': seg(0, 2).astype(BF16), 'w_hg': seg(2, 7).astype(BF16), 'w_gd': w_gd.astype(BF16),
        'w_na': seg(11, 12).astype(BF16), 'w_mg': seg(12, 13).astype(BF16),
        'sgu_w': sgu_w[l], 'wb': w_branch[l].astype(BF16), 'wo': w_out[l].astype(BF16),
        'w_gu': w_gu, 'w_dn': w_dn, 'layer': l, 'gd_par': par,
        'lbp': jnp.concatenate([jnp.log(lb_l), jnp.log1p(-lb_l), 1.0 - lb_l], axis=0),
    }


def _layer(x, groups, mods, mod_row, lw, p, norm_f, final_norm):
    T = x.shape[0]
    h = _normmod(x, p['norm1'], mods, mod_row, 1024, part_shift=0, part_scale=1)
    z_sgu = _matmul(h, lw['w_sgu'], 1024, 1024)
    z_hg = _matmul(h, lw['w_hg'], 1024, 1280)
    z_gd = _matmul(h, lw['w_gd'], 512, 2176)
    z_na = _matmul(h, lw['w_na'], 1024, 768)
    z_mg = _matmul(h, lw['w_mg'], 1024, 1024)

    x_mix, states = None, []
    for row0, B, L, ctx in groups:
        Tg = B * L
        o_a = _sgu(z_sgu, p['sgu_norm'], lw['sgu_w'], p['sgu_b'], 256, row0, Tg)
        s_hg0 = None if ctx is None else ctx[2]
        s_gd0 = None if ctx is None else ctx[3]
        o_hf, o_hb, s_hg = _hgrn(z_hg, lw['lbp'], s_hg0, B, L, row0)
        qkv = _gdprep(z_gd, p['gd_conv'], L, 256, row0, Tg)
        o_gf, o_gb, s_gd = _gdn(qkv, z_gd, lw['gd_par'], s_gd0, B, L, row0)
        if ctx is None:
            o_d = _ctx_attn(z_na, B, L, row0)
        else:
            o_d = _na_attn(z_na, ctx[0], ctx[1], _na_bias(p['na_rpb'], L // GRID_W), B, L, row0)
        x_mix = _merge(x, o_a, o_hf, o_hb, z_hg, o_gf, o_gb, z_gd, o_d, z_mg, p['hg_onorm'], p['gd_onorm'],
                       lw['wb'], lw['wo'], mods, mod_row, 256, row0, Tg, x_mix)
        states.append((s_hg, s_gd))
    x = x_mix

    h2, top_e, wts = _router(x, p['norm2'], mods, mod_row, p['w_router'], p['b_router'], 512)
    th = T // MOE_SPLIT
    x_out = None
    for j in range(MOE_SPLIT):
        r0 = j * th
        tok, pos, blk_e = _route(top_e[r0:r0 + th, :TOP_K], th)
        xb = _sc_gather(h2, tok + r0)
        yb = _experts(xb, blk_e, lw['w_gu'], p['b_gu'], lw['w_dn'], p['b_dn'], lw['layer'])
        yg = _sc_gather(yb, pos.reshape(th, TOP_K).T.reshape(-1)).reshape(TOP_K, th, D_MODEL)
        x_out = _combine(x, yg, wts, mods, mod_row, norm_f, final_norm, 256, r0, x_out)
    return x_out, z_na, states


def kernel(x_prompt, x_sample, c, cache_na_k, cache_na_v, state_hgrn, state_gdn, c_ctx, w_ada, b_ada, norm1, norm2, norm_f, w_in, sgu_norm, sgu_w, sgu_b, hg_lb, hg_onorm, gd_conv, gd_A_log, gd_dt_bias, gd_onorm, na_rpb, w_branch, w_out, w_router, b_router, w_gu, b_gu, w_dn, b_dn):
    Bp, Lp, D = x_prompt.shape
    Bs, Ls, _ = x_sample.shape
    ctx_row = Bs
    cvecs = jnp.zeros((MOD_ROWS, D), F32).at[:Bs].set(c).at[ctx_row].set(c_ctx)
    mods = _modulation(cvecs, w_ada, b_ada)

    cs = jnp.cumsum(jax.nn.softmax(hg_lb.astype(F32), axis=1), axis=1)
    lb = cs - cs[:, :1]

    Tp, Ts = Bp * Lp, Bs * Ls
    x = jnp.concatenate([x_prompt.reshape(Tp, D), x_sample.reshape(Ts, D)], axis=0)
    mod_row = lambda r: jnp.where(r < Tp, ctx_row, (r - Tp) // Ls)
    ks_, vs_, hs_, gs_ = [], [], [], []
    for l in range(DEPTH):
        lw = _prep_layer(l, w_in, sgu_w, w_branch, w_out, w_gu, w_dn, gd_A_log, gd_dt_bias, lb)
        p = {'norm1': norm1[l], 'norm2': norm2[l], 'sgu_norm': sgu_norm[l], 'sgu_b': sgu_b[l],
             'gd_conv': gd_conv[l], 'hg_onorm': hg_onorm[l], 'gd_onorm': gd_onorm[l], 'na_rpb': na_rpb[l],
             'w_router': w_router[l], 'b_router': b_router[l], 'b_gu': b_gu, 'b_dn': b_dn}
        ctx = (cache_na_k[:, l].reshape(Bs, -1, NA_W), cache_na_v[:, l].reshape(Bs, -1, NA_W),
               state_hgrn[:, l], state_gdn[:, l])
        groups = [(0, Bp, Lp, None), (Tp, Bs, Ls, ctx)]
        x, z_na, states = _layer(x, groups, mods[l], mod_row, lw, p, norm_f, l == DEPTH - 1)
        ks_.append(z_na[:Tp, NA_W:2 * NA_W].reshape(Bp, Lp, NA_HEADS, NA_DH))
        vs_.append(z_na[:Tp, 2 * NA_W:].reshape(Bp, Lp, NA_HEADS, NA_DH))
        hs_.append(states[0][0])
        gs_.append(states[0][1])

    return (x[:Tp].reshape(Bp, Lp, D), x[Tp:].reshape(Bs, Ls, D),
            jnp.stack(ks_, axis=1), jnp.stack(vs_, axis=1), jnp.stack(hs_, axis=1), jnp.stack(gs_, axis=1))
```

```python
import functools
import math

import numpy as np
import jax
import jax.numpy as jnp
from jax import lax
from jax.experimental import pallas as pl
from jax.experimental.pallas import tpu as pltpu
from jax.experimental.pallas import tpu_sc as plsc

D_MODEL = 1024
DEPTH = 2
GRID_W = 64
BRANCH_W = 512
N_BRANCH = 4
SGU_CHUNK = 128
SGU_GROUPS = 4
HG_HEADS = 4
HG_DK = 128
HG_DV = 128
GD_HEADS = 4
GD_DK = 128
GD_DV = 128
NA_HEADS = 8
NA_DH = 64
NA_KH = 8
NA_KW = 16
N_EXP = 32
TOP_K = 4
D_FF = 1024
SWIGLU_LIMIT = 7.0
SWIGLU_ALPHA = 1.702
EPS = 1e-6

F32 = jnp.float32
BF16 = jnp.bfloat16
HI = lax.Precision.HIGHEST

LANES = 128
MOD_ROWS = 16
SCAN_C = 64
SCAN_ROWS = 2 * SCAN_C
MOE_BLOCK = 256
MOE_SPLIT = 2
NA_QROWS = 4
NA_KROWS = NA_QROWS + NA_KH
VMEM_LIMIT = 48 * 1024 * 1024

NT = (((1,), (1,)), ((), ()))
TN = (((0,), (0,)), ((), ()))


def _cparams(*sem):
    return pltpu.CompilerParams(dimension_semantics=sem, vmem_limit_bytes=VMEM_LIMIT)


def _bdot(a, b):
    return jnp.dot(a.astype(BF16), b.astype(BF16), preferred_element_type=F32)


def _bdot_g(a, b, dims):
    return lax.dot_general(a.astype(BF16), b.astype(BF16), dims, preferred_element_type=F32)


def _hdot(a, b):
    return jnp.dot(a, b, precision=HI, preferred_element_type=F32)


def _dot01(m3, x):
    hi = x.astype(BF16)
    r1 = x - hi.astype(F32)
    mid = r1.astype(BF16)
    lo = (r1 - mid.astype(F32)).astype(BF16)
    return jnp.dot(m3, jnp.concatenate([hi, mid, lo], axis=0), preferred_element_type=F32)


def _dot3(a, b):
    ah = a.astype(BF16)
    al = (a - ah.astype(F32)).astype(BF16)
    bh = b.astype(BF16)
    bl = (b - bh.astype(F32)).astype(BF16)
    return jnp.dot(jnp.concatenate([al, ah, ah], axis=1), jnp.concatenate([bh, bl, bh], axis=0),
                   preferred_element_type=F32)


def _silu(x):
    return x * jax.nn.sigmoid(x)


def _log_sigmoid(x):
    return jnp.minimum(x, 0.0) - jnp.log1p(jnp.exp(-jnp.abs(x)))


def _logaddexp(a, b):
    return jnp.maximum(a, b) + jnp.log1p(jnp.exp(-jnp.abs(a - b)))


def _softplus(x):
    return jnp.maximum(x, 0.0) + jnp.log1p(jnp.exp(-jnp.abs(x)))


def _ada_kernel(c_ref, w_ref, b_ref, o_ref):
    o_ref[0] = _hdot(_silu(c_ref[...]), w_ref[0]) + b_ref[0]


def _modulation(cvecs, w_ada, b_ada):
    tn = 1536
    out = pl.pallas_call(
        _ada_kernel,
        grid=(DEPTH, 6 * D_MODEL // tn),
        in_specs=[pl.BlockSpec((MOD_ROWS, D_MODEL), lambda l, j: (0, 0)),
                  pl.BlockSpec((1, D_MODEL, tn), lambda l, j: (l, 0, j)),
                  pl.BlockSpec((1, 1, tn), lambda l, j: (l, 0, j))],
        out_specs=pl.BlockSpec((1, MOD_ROWS, tn), lambda l, j: (l, 0, j)),
        out_shape=jax.ShapeDtypeStruct((DEPTH, MOD_ROWS, 6 * D_MODEL), F32),
        compiler_params=_cparams("arbitrary", "arbitrary"),
        name="ada_modulation",
    )(cvecs, w_ada, b_ada.reshape(DEPTH, 1, 6 * D_MODEL))
    return out.reshape(DEPTH, MOD_ROWS * 6, 1, D_MODEL)


def _rms(x):
    return x * lax.rsqrt(jnp.mean(x * x, axis=-1, keepdims=True) + EPS)


def _normmod_kernel(x_ref, g_ref, sc_ref, sh_ref, o_ref):
    h = (_rms(x_ref[...]) * g_ref[...]) * (1.0 + sc_ref[0]) + sh_ref[0]
    o_ref[...] = h.astype(o_ref.dtype)


def _normmod(x, g, mods, mod_row, tm, part_shift, part_scale):
    T = x.shape[0]
    return pl.pallas_call(
        _normmod_kernel,
        grid=(T // tm,),
        in_specs=[pl.BlockSpec((tm, D_MODEL), lambda i: (i, 0)),
                  pl.BlockSpec((1, D_MODEL), lambda i: (0, 0)),
                  pl.BlockSpec((1, 1, D_MODEL), lambda i: (mod_row(i * tm) * 6 + part_scale, 0, 0)),
                  pl.BlockSpec((1, 1, D_MODEL), lambda i: (mod_row(i * tm) * 6 + part_shift, 0, 0))],
        out_specs=pl.BlockSpec((tm, D_MODEL), lambda i: (i, 0)),
        out_shape=jax.ShapeDtypeStruct((T, D_MODEL), BF16),
        compiler_params=_cparams("parallel"),
        name="normmod",
    )(x, g.reshape(1, D_MODEL), mods, mods)


def _mm_kernel(a_ref, w_ref, o_ref):
    o_ref[...] = jnp.dot(a_ref[...], w_ref[...], preferred_element_type=F32).astype(o_ref.dtype)


def _matmul(a, w, tm, tn, out_dtype=F32):
    T, K = a.shape
    N = w.shape[1]
    return pl.pallas_call(
        _mm_kernel,
        grid=(N // tn, T // tm),
        in_specs=[pl.BlockSpec((tm, K), lambda j, i: (i, 0)),
                  pl.BlockSpec((K, tn), lambda j, i: (0, j))],
        out_specs=pl.BlockSpec((tm, tn), lambda j, i: (i, j)),
        out_shape=jax.ShapeDtypeStruct((T, N), out_dtype),
        compiler_params=_cparams("parallel", "parallel"),
        name="in_proj",
    )(a, w)


def _sgu_kernel(u_ref, v_ref, gn_ref, ws_ref, bs_ref, o_ref):
    rows = u_ref.shape[0]
    gw = BRANCH_W // SGU_GROUPS
    u = jax.nn.gelu(u_ref[...])
    v = (_rms(jax.nn.gelu(v_ref[...])) * gn_ref[...]).astype(BF16)
    for n in range(rows // SGU_CHUNK):
        r = slice(n * SGU_CHUNK, (n + 1) * SGU_CHUNK)
        for g in range(SGU_GROUPS):
            cs = slice(g * gw, (g + 1) * gw)
            s = jnp.dot(ws_ref[g], v[r, cs], preferred_element_type=F32) + bs_ref[:, cs]
            o_ref[r, cs] = (u[r, cs] * s).astype(o_ref.dtype)


def _sgu(z_sgu, g_norm, w_s, b_s, rows, row0, T):
    gw = BRANCH_W // SGU_GROUPS
    b_exp = jnp.repeat(b_s.T, gw, axis=1)
    i0 = row0 // rows
    return pl.pallas_call(
        _sgu_kernel,
        grid=(T // rows,),
        in_specs=[pl.BlockSpec((rows, BRANCH_W), lambda i: (i0 + i, 0)),
                  pl.BlockSpec((rows, BRANCH_W), lambda i: (i0 + i, 1)),
                  pl.BlockSpec((1, BRANCH_W), lambda i: (0, 0)),
                  pl.BlockSpec((SGU_GROUPS, SGU_CHUNK, SGU_CHUNK), lambda i: (0, 0, 0)),
                  pl.BlockSpec((SGU_CHUNK, BRANCH_W), lambda i: (0, 0))],
        out_specs=pl.BlockSpec((rows, BRANCH_W), lambda i: (i, 0)),
        out_shape=jax.ShapeDtypeStruct((T, BRANCH_W), BF16),
        compiler_params=_cparams("parallel"),
        name="sgu",
    )(z_sgu, z_sgu, g_norm.reshape(1, BRANCH_W), w_s.astype(BF16), b_exp)


def _order(reverse):
    p = np.arange(SCAN_C)
    return SCAN_C - 1 - p if reverse else p


def _gla_consts():
    C = SCAN_C
    nlev = int(math.log2(C))
    mats, masks = [], []
    for reverse in (False, True):
        p = _order(reverse)
        pt, pr = p[:, None], p[None, :]
        m_d, k_d = [], []
        for lev in range(nlev):
            w = C >> (lev + 1)
            parent = p // (2 * w)
            later = (p % (2 * w)) >= w
            anchor = (parent * 2 * w + w - 1)[:, None]
            m = np.where(later[:, None], (pr > anchor) & (pr <= pt), (pr > pt) & (pr <= anchor))
            m_d.append(m)
            k_d.append((parent[:, None] == parent[None, :]) & later[:, None] & ~later[None, :])
        m_d.append(pr <= pt)
        m_d.append(pr > pt)
        k_d.append(np.eye(C, dtype=bool))
        mats.append(np.concatenate(m_d, axis=0))
        masks.append(np.stack(k_d))
    return (np.stack(mats).astype(np.float32), np.stack(masks).astype(np.float32))


def _delta_consts():
    C = SCAN_C
    tri, sl, incl, strict = [], [], [], []
    for reverse in (False, True):
        p = _order(reverse)
        pt, pr = p[:, None], p[None, :]
        tri.append(np.concatenate([pr <= pt, pr > pt], axis=0))
        sl.append(np.concatenate([pt > pr, np.zeros((C, LANES - C), bool), np.ones((C, LANES), bool)], axis=1))
        incl.append(pr <= pt)
        strict.append(pr < pt)
    f = lambda a: np.stack(a).astype(np.float32)
    return f(tri), f(sl), f(incl), f(strict)


def _hgrn_kernel(qf_ref, ff_ref, vf_ref, qb_ref, fb_ref, vb_ref, lb_ref, mat_ref, msk_ref, *rest, has_state):
    if has_state:
        s0_ref, of_ref, ob_ref, sfin_ref, st_ref = rest
    else:
        of_ref, ob_ref, sfin_ref, st_ref = rest
    C = SCAN_C
    nlev = msk_ref.shape[1] - 1
    c = pl.program_id(1)
    last_c = pl.num_programs(1) - 1

    @pl.when(c == 0)
    def _():
        for d in range(2):
            for h in range(HG_HEADS):
                if has_state:
                    st_ref[d, h] = s0_ref[0, d, h].T
                else:
                    st_ref[d, h] = jnp.zeros((HG_DV, HG_DK), F32)

    q_refs, f_refs, v_refs, o_refs = (qf_ref, qb_ref), (ff_ref, fb_ref), (vf_ref, vb_ref), (of_ref, ob_ref)
    sub = qf_ref.shape[0] // C
    rows = lambda j: slice(j * C, (j + 1) * C)
    col = lambda h: slice(h * HG_DK, (h + 1) * HG_DK)
    parts = [(d, j) for d in range(2) for j in range(sub)]
    q, k, fac, att = {}, {}, {}, {}
    for d, j in parts:
        zf = f_refs[d][rows(j), :]
        logf = _logaddexp(lb_ref[d:d + 1, :], lb_ref[2 + d:3 + d, :] + _log_sigmoid(zf))
        k[d, j] = lb_ref[4 + d:5 + d, :] * jax.nn.sigmoid(-zf)
        q[d, j] = _silu(q_refs[d][rows(j), :]) * (HG_DK ** -0.5)
        fac[d, j] = jnp.exp(_dot01(mat_ref[d], logf))
    for d, j in parts:
        for h in range(HG_HEADS):
            qh, kh = q[d, j][:, col(h)], k[d, j][:, col(h)]
            acc = msk_ref[d, nlev] * _bdot_g(qh, kh, NT)
            for i in range(nlev):
                fi = fac[d, j][i * C:(i + 1) * C, col(h)]
                acc = acc + msk_ref[d, i] * _bdot_g(qh * fi, kh * fi, NT)
            att[d, j, h] = acc
    chains = [(d, h) for d in range(2) for h in range(HG_HEADS)]
    for t in range(sub):
        jd = (t, sub - 1 - t)
        for d, h in chains:
            j = jd[d]
            eb = fac[d, j][nlev * C:(nlev + 1) * C, col(h)]
            o_refs[d][rows(j), col(h)] = (_bdot(att[d, j, h], v_refs[d][rows(j), col(h)])
                                          + _bdot_g(q[d, j][:, col(h)] * eb, st_ref[d, h], NT))
        for d, h in chains:
            j = jd[d]
            eb = fac[d, j][nlev * C:(nlev + 1) * C, col(h)]
            er = fac[d, j][(nlev + 1) * C:, col(h)]
            e_last = eb[C - 1:C] if d == 0 else eb[0:1]
            st_ref[d, h] = st_ref[d, h] * e_last + _bdot_g(v_refs[d][rows(j), col(h)], k[d, j][:, col(h)] * er, TN)

    @pl.when(c == last_c)
    def _():
        for d in range(2):
            for h in range(HG_HEADS):
                sfin_ref[0, d, h] = st_ref[d, h].T


def _hgrn(z_hg, lbp, s0, B, L, row0):
    n = L // SCAN_ROWS
    c0 = row0 // SCAN_ROWS
    mats, masks = _gla_consts()
    mats = np.tile(mats, (1, 1, 3))
    blk = (SCAN_ROWS, HG_HEADS * HG_DK)
    fwd = lambda col: pl.BlockSpec(blk, lambda b, c: (c0 + b * n + c, col))
    bwd = lambda col: pl.BlockSpec(blk, lambda b, c: (c0 + b * n + n - 1 - c, col))
    st_blk = (1, 2, HG_HEADS, HG_DK, HG_DV)
    in_specs = [fwd(0), fwd(1), fwd(3), bwd(0), bwd(2), bwd(3),
                pl.BlockSpec(lbp.shape, lambda b, c: (0, 0)),
                pl.BlockSpec(mats.shape, lambda b, c: (0, 0, 0)),
                pl.BlockSpec(masks.shape, lambda b, c: (0, 0, 0, 0))]
    args = [z_hg] * 6 + [lbp, jnp.asarray(mats, BF16), jnp.asarray(masks)]
    if s0 is not None:
        in_specs.append(pl.BlockSpec(st_blk, lambda b, c: (b, 0, 0, 0, 0)))
        args.append(s0)
    return pl.pallas_call(
        functools.partial(_hgrn_kernel, has_state=s0 is not None),
        grid=(B, n),
        in_specs=in_specs,
        out_specs=[pl.BlockSpec(blk, lambda b, c: (b * n + c, 0)),
                   pl.BlockSpec(blk, lambda b, c: (b * n + n - 1 - c, 0)),
                   pl.BlockSpec(st_blk, lambda b, c: (b, 0, 0, 0, 0))],
        out_shape=[jax.ShapeDtypeStruct((B * L, HG_HEADS * HG_DV), F32),
                   jax.ShapeDtypeStruct((B * L, HG_HEADS * HG_DV), F32),
                   jax.ShapeDtypeStruct((B,) + st_blk[1:], F32)],
        scratch_shapes=[pltpu.VMEM((2, HG_HEADS, HG_DV, HG_DK), F32)],
        compiler_params=_cparams("parallel", "arbitrary"),
        name="hgrn_scan",
    )(*args)


GD_NQ = GD_HEADS * GD_DK
GD_QKV = 2 * GD_NQ + GD_HEADS * GD_DV
HALO = 8


def _gdprep_kernel(x_ref, prev_ref, next_ref, w_ref, o_ref, *, tiles_per_seq):
    R = x_ref.shape[0]
    t = pl.program_id(0) % tiles_per_seq
    x = x_ref[...]
    prev_row = jnp.where(t == 0, 0.0, prev_ref[HALO - 1:HALO, :])
    next_row = jnp.where(t == tiles_per_seq - 1, 0.0, next_ref[0:1, :])
    row = lax.broadcasted_iota(jnp.int32, x.shape, 0)
    xm1 = jnp.where(row == 0, prev_row, pltpu.roll(x, 1, 0))
    xp1 = jnp.where(row == R - 1, next_row, pltpu.roll(x, R - 1, 0))
    y = _silu(w_ref[0:1, :] * xm1 + w_ref[1:2, :] * x + w_ref[2:3, :] * xp1)
    for j in range(2 * GD_HEADS):
        cs = slice(j * GD_DK, (j + 1) * GD_DK)
        seg = y[:, cs]
        seg = seg * lax.rsqrt(jnp.sum(seg * seg, axis=-1, keepdims=True) + EPS)
        if j < GD_HEADS:
            seg = seg * (GD_DK ** -0.5)
        o_ref[:, cs] = seg
    o_ref[:, 2 * GD_NQ:] = y[:, 2 * GD_NQ:]


def _gdprep(z_gd, conv_w, L, rows, row0, T):
    tps = L // rows
    hb = rows // HALO
    nhalo = z_gd.shape[0] // HALO
    i0 = row0 // rows
    return pl.pallas_call(
        functools.partial(_gdprep_kernel, tiles_per_seq=tps),
        grid=(T // rows,),
        in_specs=[pl.BlockSpec((rows, GD_QKV), lambda i: (i0 + i, 0)),
                  pl.BlockSpec((HALO, GD_QKV), lambda i: (jnp.maximum((i0 + i) * hb - 1, 0), 0)),
                  pl.BlockSpec((HALO, GD_QKV), lambda i: (jnp.minimum((i0 + i + 1) * hb, nhalo - 1), 0)),
                  pl.BlockSpec((3, GD_QKV), lambda i: (0, 0))],
        out_specs=pl.BlockSpec((rows, GD_QKV), lambda i: (i, 0)),
        out_shape=jax.ShapeDtypeStruct((T, GD_QKV), F32),
        compiler_params=_cparams("parallel"),
        name="gdn_prep",
    )(z_gd, z_gd, z_gd, conv_w)


def _gdn_kernel(xf_ref, abf_ref, xb_ref, abb_ref, par_ref, tri_ref, sl_ref, incl_ref, strict_ref, msk_ref, *rest,
                has_state):
    if has_state:
        s0_ref, of_ref, ob_ref, sfin_ref, st_ref = rest
    else:
        of_ref, ob_ref, sfin_ref, st_ref = rest
    C = SCAN_C
    c = pl.program_id(1)
    last_c = pl.num_programs(1) - 1
    nlev = msk_ref.shape[1] - 1

    @pl.when(c == 0)
    def _():
        for d in range(2):
            for h in range(GD_HEADS):
                if has_state:
                    st_ref[d, h] = s0_ref[0, d, h].T
                else:
                    st_ref[d, h] = jnp.zeros((GD_DV, GD_DK), F32)

    x_refs, ab_refs, o_refs = (xf_ref, xb_ref), (abf_ref, abb_ref), (of_ref, ob_ref)
    sub = xf_ref.shape[0] // C
    rows = lambda j: slice(j * C, (j + 1) * C)
    chains = [(d, h) for d in range(2) for h in range(GD_HEADS)]
    parts = [(d, j, h) for d in range(2) for j in range(sub) for h in range(GD_HEADS)]
    q_of = lambda d, j, h: x_refs[d][rows(j), h * GD_DK:(h + 1) * GD_DK]
    k_of = lambda d, j, h: x_refs[d][rows(j), GD_NQ + h * GD_DK:GD_NQ + (h + 1) * GD_DK]
    v_of = lambda d, j, h: x_refs[d][rows(j), 2 * GD_NQ + h * GD_DV:2 * GD_NQ + (h + 1) * GD_DV]
    g_all, beta_all = [], []
    for d in range(2):
        ab = ab_refs[d][...]
        g_all.append(-jnp.exp(par_ref[0:1, :]) * _softplus(ab + par_ref[1:2, :]))
        beta_all.append(jax.nn.sigmoid(ab))

    def beta_of(d, j, h):
        lane = 2 * GD_HEADS + d * GD_HEADS + h
        return jnp.broadcast_to(beta_all[d][rows(j), lane:lane + 1], (C, LANES))

    decay, e_cum, e_rest, kb, a = {}, {}, {}, {}, {}
    for d, j, h in parts:
        lane = d * GD_HEADS + h
        g_b = jnp.broadcast_to(g_all[d][rows(j), lane:lane + 1], (C, 2 * LANES))
        sums = _dot01(tri_ref[d], g_b * sl_ref[d])
        decay[d, j, h] = jnp.exp(sums[:C, :C])
        e_cum[d, j, h] = jnp.exp(sums[:C, LANES:])
        e_rest[d, j, h] = jnp.exp(sums[C:, LANES:])
    for d, j, h in parts:
        k = k_of(d, j, h)
        kb[d, j, h] = k * beta_of(d, j, h)
        a[d, j, h] = strict_ref[d] * decay[d, j, h] * _bdot_g(kb[d, j, h], k, NT)
    inv_m = {pt: -(msk_ref[pt[0], nlev - 1] * a[pt]) for pt in parts}
    for lev in range(nlev - 2, -1, -1):
        a_w = {pt: msk_ref[pt[0], lev] * a[pt] for pt in parts}
        p = {pt: a_w[pt] + _bdot(inv_m[pt], a_w[pt]) for pt in parts}
        inv_m = {pt: inv_m[pt] - p[pt] - _bdot(p[pt], inv_m[pt]) for pt in parts}
    sol, att = {}, {}
    for d, j, h in parts:
        rhs = jnp.concatenate([v_of(d, j, h) * beta_of(d, j, h), kb[d, j, h] * e_cum[d, j, h]], axis=1)
        sol[d, j, h] = rhs + _dot3(inv_m[d, j, h], rhs)
        att[d, j, h] = incl_ref[d] * decay[d, j, h] * _bdot_g(q_of(d, j, h), k_of(d, j, h), NT)
    for t in range(sub):
        jd = (t, sub - 1 - t)
        u = {}
        for d, h in chains:
            pt = (d, jd[d], h)
            u[d, h] = sol[pt][:, :GD_DV] - _bdot_g(sol[pt][:, GD_DV:], st_ref[d, h], NT)
        for d, h in chains:
            pt = (d, jd[d], h)
            o_refs[d][rows(jd[d]), h * GD_DV:(h + 1) * GD_DV] = (
                _bdot_g(q_of(*pt) * e_cum[pt], st_ref[d, h], NT) + _bdot(att[pt], u[d, h]))
        for d, h in chains:
            pt = (d, jd[d], h)
            e_last = e_cum[pt][C - 1:C] if d == 0 else e_cum[pt][0:1]
            st_ref[d, h] = st_ref[d, h] * e_last + _bdot_g(u[d, h], k_of(*pt) * e_rest[pt], TN)

    @pl.when(c == last_c)
    def _():
        for d in range(2):
            for h in range(GD_HEADS):
                sfin_ref[0, d, h] = st_ref[d, h].T


def _gdn(qkv, z_gd, par, s0, B, L, row0):
    n = L // SCAN_ROWS
    c0 = row0 // SCAN_ROWS
    tri, sl, incl, strict = (jnp.asarray(a) for a in _delta_consts())
    tri = jnp.tile(tri, (1, 1, 3)).astype(BF16)
    masks = jnp.asarray(_gla_consts()[1])
    ab_col = (GD_QKV + GD_HEADS * GD_DV) // LANES
    xblk = (SCAN_ROWS, GD_QKV)
    ablk = (SCAN_ROWS, LANES)
    oblk = (SCAN_ROWS, GD_HEADS * GD_DV)
    st_blk = (1, 2, GD_HEADS, GD_DK, GD_DV)
    fwd = lambda b, c: b * n + c
    bwd = lambda b, c: b * n + n - 1 - c
    const3 = lambda a: pl.BlockSpec(a.shape, lambda b, c: (0, 0, 0))
    in_specs = [pl.BlockSpec(xblk, lambda b, c: (fwd(b, c), 0)),
                pl.BlockSpec(ablk, lambda b, c: (c0 + fwd(b, c), ab_col)),
                pl.BlockSpec(xblk, lambda b, c: (bwd(b, c), 0)),
                pl.BlockSpec(ablk, lambda b, c: (c0 + bwd(b, c), ab_col)),
                pl.BlockSpec(par.shape, lambda b, c: (0, 0)),
                const3(tri), const3(sl), const3(incl), const3(strict),
                pl.BlockSpec(masks.shape, lambda b, c: (0, 0, 0, 0))]
    args = [qkv, z_gd, qkv, z_gd, par, tri, sl, incl, strict, masks]
    if s0 is not None:
        in_specs.append(pl.BlockSpec(st_blk, lambda b, c: (b, 0, 0, 0, 0)))
        args.append(s0)
    return pl.pallas_call(
        functools.partial(_gdn_kernel, has_state=s0 is not None),
        grid=(B, n),
        in_specs=in_specs,
        out_specs=[pl.BlockSpec(oblk, lambda b, c: (fwd(b, c), 0)),
                   pl.BlockSpec(oblk, lambda b, c: (bwd(b, c), 0)),
                   pl.BlockSpec(st_blk, lambda b, c: (b, 0, 0, 0, 0))],
        out_shape=[jax.ShapeDtypeStruct((B * L, GD_HEADS * GD_DV), F32),
                   jax.ShapeDtypeStruct((B * L, GD_HEADS * GD_DV), F32),
                   jax.ShapeDtypeStruct((B,) + st_blk[1:], F32)],
        scratch_shapes=[pltpu.VMEM((2, GD_HEADS, GD_DV, GD_DK), F32)],
        compiler_params=_cparams("parallel", "arbitrary"),
        name="gdn_scan",
    )(*args)


NA_W = NA_HEADS * NA_DH


def _softmax_pv(s, v):
    m = jnp.max(s, axis=-1, keepdims=True)
    e = jnp.exp(s - m)
    den = jnp.sum(e, axis=-1, keepdims=True)
    return jnp.dot(e.astype(BF16), v, preferred_element_type=F32) / den


def _ctx_attn_kernel(q_ref, k_ref, v_ref, o_ref):
    for h in range(NA_HEADS):
        cs = slice(h * NA_DH, (h + 1) * NA_DH)
        q = (q_ref[:, cs] * (NA_DH ** -0.5)).astype(BF16)
        s = lax.dot_general(q, k_ref[:, cs].astype(BF16), NT, preferred_element_type=F32)
        o_ref[:, cs] = _softmax_pv(s, v_ref[:, cs].astype(BF16)).astype(o_ref.dtype)


def _ctx_attn(z_na, B, L, row0):
    blk = (L, NA_W)
    b0 = row0 // L
    return pl.pallas_call(
        _ctx_attn_kernel,
        grid=(B,),
        in_specs=[pl.BlockSpec(blk, lambda b: (b0 + b, 0)),
                  pl.BlockSpec(blk, lambda b: (b0 + b, 1)),
                  pl.BlockSpec(blk, lambda b: (b0 + b, 2))],
        out_specs=pl.BlockSpec(blk, lambda b: (b, 0)),
        out_shape=jax.ShapeDtypeStruct((B * L, NA_W), BF16),
        compiler_params=_cparams("parallel"),
        name="ctx_attn",
    )(z_na, z_na, z_na)


def _na_kernel(q_ref, *rest, n_kblk, nkeys_nb):
    k_refs = rest[:n_kblk]
    v_refs = rest[n_kblk:2 * n_kblk]
    kc_ref, vc_ref, bias_ref, o_ref, kbuf, vbuf = rest[2 * n_kblk:]
    qb = q_ref.shape[0]
    for i in range(n_kblk):
        kbuf[i * qb:(i + 1) * qb, :] = k_refs[i][...].astype(BF16)
        vbuf[i * qb:(i + 1) * qb, :] = v_refs[i][...].astype(BF16)
    kbuf[nkeys_nb:, :] = kc_ref[0].astype(BF16)
    vbuf[nkeys_nb:, :] = vc_ref[0].astype(BF16)
    for h in range(NA_HEADS):
        cs = slice(h * NA_DH, (h + 1) * NA_DH)
        q = (q_ref[:, cs] * (NA_DH ** -0.5)).astype(BF16)
        s = lax.dot_general(q, kbuf[:, cs], NT, preferred_element_type=F32)
        s_nb = s[:, :nkeys_nb] + bias_ref[0, h]
        s_cx = s[:, nkeys_nb:]
        m = jnp.maximum(jnp.max(s_nb, axis=-1, keepdims=True), jnp.max(s_cx, axis=-1, keepdims=True))
        e_nb = jnp.exp(s_nb - m)
        e_cx = jnp.exp(s_cx - m)
        den = jnp.sum(e_nb, axis=-1, keepdims=True) + jnp.sum(e_cx, axis=-1, keepdims=True)
        pv = (jnp.dot(e_nb.astype(BF16), vbuf[:nkeys_nb, cs], preferred_element_type=F32)
              + jnp.dot(e_cx.astype(BF16), vbuf[nkeys_nb:, cs], preferred_element_type=F32))
        o_ref[:, cs] = (pv / den).astype(o_ref.dtype)


def _na_bias(rpb, rows):
    qr, kr, col = np.arange(NA_QROWS), np.arange(NA_KROWS), np.arange(GRID_W)
    nblk = rows // NA_QROWS
    ndr, ndc = 2 * NA_KH - 1, 2 * NA_KW - 1
    sel_r, row_ok = [], []
    for m in (0, 1, nblk - 1):
        r = (NA_QROWS * m + qr)[:, None]
        start = np.clip(NA_QROWS * m - NA_KH // 2, 0, rows - NA_KROWS)
        kra = (start + kr)[None, :]
        r0 = np.clip(r - NA_KH // 2, 0, rows - NA_KH)
        row_ok.append((kra >= r0) & (kra < r0 + NA_KH))
        dr = np.clip(kra - r + NA_KH - 1, 0, ndr - 1)
        sel_r.append(dr[..., None] == np.arange(ndr))
    sel_r = np.stack(sel_r).astype(np.float32)
    row_ok = np.stack(row_ok)
    col_start = np.clip(col - NA_KW // 2, 0, GRID_W - NA_KW)[:, None]
    col_ok = (col[None, :] >= col_start) & (col[None, :] < col_start + NA_KW)
    dc = np.clip(col[None, :] - col[:, None], -(NA_KW - 1), NA_KW - 1) + NA_KW - 1
    sel_c = (dc[..., None] == np.arange(ndc)).astype(np.float32)
    bias = jnp.einsum('hab,vqka,xyb->vhqxky', rpb.astype(F32), sel_r, sel_c, precision=HI)
    ok = row_ok[:, None, :, None, :, None] & col_ok[None, None, None, :, None, :]
    bias = jnp.where(ok, bias, -jnp.inf)
    return bias.reshape(3, NA_HEADS, NA_QROWS * GRID_W, NA_KROWS * GRID_W)


def _na_attn(z_na, k_ctx, v_ctx, bias, B, S, row0):
    rows = S // GRID_W
    qb = NA_QROWS * GRID_W
    m0 = row0 // qb
    nblk = rows // NA_QROWS
    n_kblk = NA_KROWS // NA_QROWS
    lc = k_ctx.shape[1]
    nkeys_nb = NA_KROWS * GRID_W
    kstart = lambda m: jnp.clip(m - 1, 0, nblk - n_kblk)
    variant = lambda m: jnp.where(m == 0, 0, jnp.where(m == nblk - 1, 2, 1))
    kv_specs = lambda col: [pl.BlockSpec((qb, NA_W), functools.partial(
        lambda b, m, i, col: (m0 + b * nblk + kstart(m) + i, col), i=i, col=col)) for i in range(n_kblk)]
    return pl.pallas_call(
        functools.partial(_na_kernel, n_kblk=n_kblk, nkeys_nb=nkeys_nb),
        grid=(B, nblk),
        in_specs=([pl.BlockSpec((qb, NA_W), lambda b, m: (m0 + b * nblk + m, 0))] + kv_specs(1) + kv_specs(2)
                  + [pl.BlockSpec((1, lc, NA_W), lambda b, m: (b, 0, 0)),
                     pl.BlockSpec((1, lc, NA_W), lambda b, m: (b, 0, 0)),
                     pl.BlockSpec((1,) + bias.shape[1:], lambda b, m: (variant(m), 0, 0, 0))]),
        out_specs=pl.BlockSpec((qb, NA_W), lambda b, m: (b * nblk + m, 0)),
        out_shape=jax.ShapeDtypeStruct((B * S, NA_W), BF16),
        scratch_shapes=[pltpu.VMEM((nkeys_nb + lc, NA_W), BF16), pltpu.VMEM((nkeys_nb + lc, NA_W), BF16)],
        compiler_params=_cparams("parallel", "arbitrary"),
        name="na_attn",
    )(z_na, *([z_na] * (2 * n_kblk)), k_ctx, v_ctx, bias)


def _head_rms(o, g_row):
    parts = []
    for h in range(o.shape[1] // LANES):
        parts.append(_rms(o[:, h * LANES:(h + 1) * LANES]) * g_row)
    return jnp.concatenate(parts, axis=1)


def _merge_kernel(x_ref, oa_ref, hf_ref, hb_ref, hg_ref, gf_ref, gb_ref, gg_ref, od_ref, mg_ref,
                  hn_ref, gn_ref, wb_ref, wo_ref, g1_ref, *rest):
    o_ref = rest[-1]
    o_b = _head_rms(hf_ref[...] + hb_ref[...], hn_ref[...]) * _silu(hg_ref[...])
    o_c = _head_rms(gf_ref[...] + gb_ref[...], gn_ref[...]) * _silu(gg_ref[...])
    branches = (oa_ref[...], o_b.astype(BF16), o_c.astype(BF16), od_ref[...])
    merged = None
    for n_, o_n in enumerate(branches):
        gate = jax.nn.sigmoid(mg_ref[:, n_ * D_MODEL:(n_ + 1) * D_MODEL])
        term = gate * jnp.dot(o_n, wb_ref[n_], preferred_element_type=F32)
        merged = term if merged is None else merged + term
    mix = jnp.dot(merged.astype(BF16), wo_ref[...], preferred_element_type=F32)
    o_ref[...] = x_ref[...] + g1_ref[0] * mix


def _merge(x, o_a, o_hf, o_hb, z_hg, o_gf, o_gb, z_gd, o_d, z_mg, hg_onorm, gd_onorm, wb, wo, mods, mod_row, tm,
           row0, T, x_acc):
    i0 = row0 // tm
    glob = lambda w, col=0: pl.BlockSpec((tm, w), lambda i: (i0 + i, col))
    loc = lambda w: pl.BlockSpec((tm, w), lambda i: (i, 0))
    full = lambda a: pl.BlockSpec(a.shape, lambda i: (0,) * a.ndim)
    hn = hg_onorm.reshape(1, HG_DV)
    gn = gd_onorm.reshape(1, GD_DV)
    in_specs = [glob(D_MODEL), loc(BRANCH_W), loc(BRANCH_W), loc(BRANCH_W), glob(BRANCH_W, 4),
                loc(BRANCH_W), loc(BRANCH_W), glob(BRANCH_W, GD_QKV // BRANCH_W), loc(BRANCH_W),
                glob(N_BRANCH * D_MODEL), full(hn), full(gn), full(wb), full(wo),
                pl.BlockSpec((1, 1, D_MODEL), lambda i: (mod_row(row0 + i * tm) * 6 + 2, 0, 0))]
    args = [x, o_a, o_hf, o_hb, z_hg, o_gf, o_gb, z_gd, o_d, z_mg, hn, gn, wb, wo, mods]
    aliases = {}
    if x_acc is not None:
        in_specs.append(pl.BlockSpec(memory_space=pl.ANY))
        aliases = {len(args): 0}
        args.append(x_acc)
    return pl.pallas_call(
        _merge_kernel,
        grid=(T // tm,),
        in_specs=in_specs,
        out_specs=glob(D_MODEL),
        out_shape=jax.ShapeDtypeStruct(x.shape, F32),
        input_output_aliases=aliases,
        compiler_params=_cparams("parallel"),
        name="merge",
    )(*args)


SC_CORES = 2
SC_SUBCORES = 16
SC_WIN = 32


def _sc_gather(table, idx):
    V, D = table.shape
    N = idx.shape[0]
    nw = SC_CORES * SC_SUBCORES
    per_w = N // nw
    n_win = per_w // SC_WIN
    assert per_w * nw == N and n_win * SC_WIN == per_w
    mesh = plsc.VectorSubcoreMesh(core_axis_name="c", subcore_axis_name="s")

    @functools.partial(
        pl.kernel, mesh=mesh,
        out_type=jax.ShapeDtypeStruct((N, D), table.dtype),
        scratch_types=[pltpu.VMEM((n_win, SC_WIN), jnp.int32),
                       pltpu.VMEM((SC_WIN, D), table.dtype),
                       pltpu.SemaphoreType.DMA],
    )
    def gather_rows(table_hbm, idx_hbm, out_hbm, idx_v, rows_v, sem):
        wid = lax.axis_index("s") * SC_CORES + lax.axis_index("c")
        pltpu.sync_copy(idx_hbm.at[wid], idx_v)

        @pl.loop(0, n_win)
        def _(w):
            pltpu.async_copy(table_hbm.at[idx_v.at[w]], rows_v, sem).wait()
            pltpu.sync_copy(rows_v, out_hbm.at[pl.ds(wid * per_w + w * SC_WIN, SC_WIN)])

    return gather_rows(table, idx.reshape(nw, n_win, SC_WIN))


def _router_kernel(x_ref, g_ref, sc_ref, sh_ref, wr_ref, br_ref, h_ref, e_ref, w_ref):
    h = (_rms(x_ref[...]) * g_ref[...]) * (1.0 + sc_ref[0]) + sh_ref[0]
    h_ref[...] = h.astype(h_ref.dtype)
    logits = _hdot(h, wr_ref[...]) + br_ref[...]
    lane = lax.broadcasted_iota(jnp.int32, logits.shape, 1)
    e_out = jnp.zeros(logits.shape, jnp.int32)
    v_out = jnp.zeros(logits.shape, F32)
    top0 = None
    for k in range(TOP_K):
        m = jnp.max(logits, axis=-1, keepdims=True)
        idx = jnp.min(jnp.where(logits == m, lane, LANES), axis=-1, keepdims=True)
        if k == 0:
            top0 = m
        e_out = jnp.where(lane == k, idx, e_out)
        v_out = jnp.where(lane == k, jnp.exp(m - top0), v_out)
        logits = jnp.where(lane == idx, -jnp.inf, logits)
    e_ref[...] = e_out
    w_ref[...] = v_out / jnp.sum(v_out, axis=-1, keepdims=True)


def _router(x, g, mods, mod_row, w_router, b_router, tm):
    T = x.shape[0]
    wr = jnp.zeros((D_MODEL, LANES), F32).at[:, :N_EXP].set(w_router)
    br = jnp.full((1, LANES), -jnp.inf, F32).at[0, :N_EXP].set(b_router)
    row = lambda w: pl.BlockSpec((tm, w), lambda i: (i, 0))
    return pl.pallas_call(
        _router_kernel,
        grid=(T // tm,),
        in_specs=[row(D_MODEL),
                  pl.BlockSpec((1, D_MODEL), lambda i: (0, 0)),
                  pl.BlockSpec((1, 1, D_MODEL), lambda i: (mod_row(i * tm) * 6 + 4, 0, 0)),
                  pl.BlockSpec((1, 1, D_MODEL), lambda i: (mod_row(i * tm) * 6 + 3, 0, 0)),
                  pl.BlockSpec((D_MODEL, LANES), lambda i: (0, 0)),
                  pl.BlockSpec((1, LANES), lambda i: (0, 0))],
        out_specs=[row(D_MODEL), row(LANES), row(LANES)],
        out_shape=[jax.ShapeDtypeStruct((T, D_MODEL), F32),
                   jax.ShapeDtypeStruct((T, LANES), jnp.int32),
                   jax.ShapeDtypeStruct((T, LANES), F32)],
        compiler_params=_cparams("parallel"),
        name="router",
    )(x, g.reshape(1, D_MODEL), mods, mods, wr, br)


def _expert_kernel(blk_e_ref, x_ref, wgu_ref, bgu_ref, wdn_ref, bdn_ref, o_ref, wgu_bf, wdn_bf):
    i = pl.program_id(0)
    new_expert = jnp.logical_or(i == 0, blk_e_ref[i] != blk_e_ref[jnp.maximum(i - 1, 0)])

    @pl.when(new_expert)
    def _():
        wgu_bf[...] = wgu_ref[0].astype(BF16)
        wdn_bf[...] = wdn_ref[0].astype(BF16)

    gu = jnp.dot(x_ref[...].astype(BF16), wgu_bf[...], preferred_element_type=F32) + bgu_ref[0]
    a = jnp.minimum(gu[:, :D_FF], SWIGLU_LIMIT)
    lin = jnp.clip(gu[:, D_FF:], -SWIGLU_LIMIT, SWIGLU_LIMIT)
    y = a * jax.nn.sigmoid(SWIGLU_ALPHA * a) * (lin + 1.0)
    o_ref[...] = jnp.dot(y.astype(BF16), wdn_bf[...], preferred_element_type=F32) + bdn_ref[0]


def _experts(xb, blk_e, w_gu, b_gu, w_dn, b_dn, layer):
    n_pad = xb.shape[0]
    n_blocks = n_pad // MOE_BLOCK
    e0 = layer * N_EXP
    w_gu = w_gu.reshape(DEPTH * N_EXP, D_MODEL, 2 * D_FF)
    w_dn = w_dn.reshape(DEPTH * N_EXP, D_FF, D_MODEL)
    grid_spec = pltpu.PrefetchScalarGridSpec(
        num_scalar_prefetch=1,
        grid=(n_blocks,),
        in_specs=[pl.BlockSpec((MOE_BLOCK, D_MODEL), lambda i, e: (i, 0)),
                  pl.BlockSpec((1, D_MODEL, 2 * D_FF), lambda i, e: (e0 + e[i], 0, 0)),
                  pl.BlockSpec((1, 1, 2 * D_FF), lambda i, e: (e0 + e[i], 0, 0)),
                  pl.BlockSpec((1, D_FF, D_MODEL), lambda i, e: (e0 + e[i], 0, 0)),
                  pl.BlockSpec((1, 1, D_MODEL), lambda i, e: (e0 + e[i], 0, 0))],
        out_specs=pl.BlockSpec((MOE_BLOCK, D_MODEL), lambda i, e: (i, 0)),
        scratch_shapes=[pltpu.VMEM((D_MODEL, 2 * D_FF), BF16), pltpu.VMEM((D_FF, D_MODEL), BF16)],
    )
    return pl.pallas_call(
        _expert_kernel,
        grid_spec=grid_spec,
        out_shape=jax.ShapeDtypeStruct((n_pad, D_MODEL), F32),
        compiler_params=_cparams("arbitrary"),
        name="experts",
    )(blk_e, xb, w_gu, b_gu.reshape(DEPTH * N_EXP, 1, 2 * D_FF), w_dn, b_dn.reshape(DEPTH * N_EXP, 1, D_MODEL))


def _combine_kernel(x_ref, y_ref, w_ref, g2_ref, nf_ref, *rest, final_norm):
    o_ref = rest[-1]
    acc = None
    for k in range(TOP_K):
        term = y_ref[k] * w_ref[:, k:k + 1]
        acc = term if acc is None else acc + term
    x = x_ref[...] + g2_ref[0] * acc
    if final_norm:
        x = _rms(x) * nf_ref[...]
    o_ref[...] = x


def _combine(x, yg, wts, mods, mod_row, norm_f, final_norm, tm, row0, x_acc):
    T = yg.shape[1]
    i0 = row0 // tm
    glob = lambda w: pl.BlockSpec((tm, w), lambda i: (i0 + i, 0))
    in_specs = [glob(D_MODEL), pl.BlockSpec((TOP_K, tm, D_MODEL), lambda i: (0, i, 0)), glob(LANES),
                pl.BlockSpec((1, 1, D_MODEL), lambda i: (mod_row(row0 + i * tm) * 6 + 5, 0, 0)),
                pl.BlockSpec((1, D_MODEL), lambda i: (0, 0))]
    args = [x, yg, wts, mods, norm_f.reshape(1, D_MODEL)]
    aliases = {}
    if x_acc is not None:
        in_specs.append(pl.BlockSpec(memory_space=pl.ANY))
        aliases = {len(args): 0}
        args.append(x_acc)
    return pl.pallas_call(
        functools.partial(_combine_kernel, final_norm=final_norm),
        grid=(T // tm,),
        in_specs=in_specs,
        out_specs=glob(D_MODEL),
        out_shape=jax.ShapeDtypeStruct(x.shape, F32),
        input_output_aliases=aliases,
        compiler_params=_cparams("parallel"),
        name="combine",
    )(*args)


def _route(top_e, T):
    n_assign = T * TOP_K
    n_blocks = n_assign // MOE_BLOCK + N_EXP
    e_flat = top_e.reshape(n_assign)
    onehot = e_flat[:, None] == jnp.arange(N_EXP, dtype=jnp.int32)[None, :]
    counts = jnp.sum(onehot, axis=0, dtype=jnp.int32)
    start = jnp.cumsum(counts) - counts
    padded = (counts + MOE_BLOCK - 1) // MOE_BLOCK * MOE_BLOCK
    pad_end = jnp.cumsum(padded)
    pad_start = pad_end - padded
    iota = jnp.arange(n_assign, dtype=jnp.int32)
    _, order = lax.sort((e_flat, iota), num_keys=1, is_stable=True)
    _, rank = lax.sort((order, iota), num_keys=1)
    pos = rank + jnp.sum(jnp.where(onehot, (pad_start - start)[None, :], 0), axis=1)
    blk_first = jnp.arange(n_blocks, dtype=jnp.int32) * MOE_BLOCK
    blk_e = jnp.minimum(jnp.sum(pad_end[None, :] <= blk_first[:, None], axis=1), N_EXP - 1).astype(jnp.int32)
    r = blk_first[:, None] - pad_start[blk_e][:, None] + jnp.arange(MOE_BLOCK, dtype=jnp.int32)[None, :]
    valid = r < counts[blk_e][:, None]
    src = jnp.clip(start[blk_e][:, None] + r, 0, n_assign - 1)
    filler = (blk_first[:, None] + jnp.arange(MOE_BLOCK, dtype=jnp.int32)[None, :]) % T
    tok = jnp.where(valid, order[src] // TOP_K, filler).reshape(n_blocks * MOE_BLOCK).astype(jnp.int32)
    return tok, pos, blk_e


def _prep_layer(l, w_in, sgu_w, w_branch, w_out, w_gu, w_dn, gd_A_log, gd_dt_bias, lb):
    offs = np.cumsum([0, BRANCH_W, BRANCH_W, 512, 512, 512, 512, 512, GD_QKV, 8, 8, 512, 3 * NA_W, N_BRANCH * D_MODEL])
    w = w_in[l]
    seg = lambda i, j: w[:, offs[i]:offs[j]]
    w_gd = jnp.concatenate([seg(7, 8), seg(10, 11), seg(8, 10),
                            jnp.zeros((D_MODEL, LANES - 4 * GD_HEADS), F32)], axis=1)
    par = jnp.zeros((2, LANES), F32)
    par = par.at[0, :2 * GD_HEADS].set(gd_A_log[l].reshape(-1)).at[1, :2 * GD_HEADS].set(gd_dt_bias[l].reshape(-1))
    lb_l = lb[:, l]
    return {
        'w_sgu': seg(0, 2).astype(BF16), 'w_hg': seg(2, 7).astype(BF16), 'w_gd': w_gd.astype(BF16),
        'w_na': seg(11, 12).astype(BF16), 'w_mg': seg(12, 13).astype(BF16),
        'sgu_w': sgu_w[l], 'wb': w_branch[l].astype(BF16), 'wo': w_out[l].astype(BF16),
        'w_gu': w_gu, 'w_dn': w_dn, 'layer': l, 'gd_par': par,
        'lbp': jnp.concatenate([jnp.log(lb_l), jnp.log1p(-lb_l), 1.0 - lb_l], axis=0),
    }


def _layer(x, groups, mods, mod_row, lw, p, norm_f, final_norm):
    T = x.shape[0]
    h = _normmod(x, p['norm1'], mods, mod_row, 1024, part_shift=0, part_scale=1)
    z_sgu = _matmul(h, lw['w_sgu'], 1024, 1024)
    z_hg = _matmul(h, lw['w_hg'], 1024, 1280)
    z_gd = _matmul(h, lw['w_gd'], 512, 2176)
    z_na = _matmul(h, lw['w_na'], 1024, 768)
    z_mg = _matmul(h, lw['w_mg'], 1024, 1024)

    x_mix, states = None, []
    for row0, B, L, ctx in groups:
        Tg = B * L
        o_a = _sgu(z_sgu, p['sgu_norm'], lw['sgu_w'], p['sgu_b'], 256, row0, Tg)
        s_hg0 = None if ctx is None else ctx[2]
        s_gd0 = None if ctx is None else ctx[3]
        o_hf, o_hb, s_hg = _hgrn(z_hg, lw['lbp'], s_hg0, B, L, row0)
        qkv = _gdprep(z_gd, p['gd_conv'], L, 256, row0, Tg)
        o_gf, o_gb, s_gd = _gdn(qkv, z_gd, lw['gd_par'], s_gd0, B, L, row0)
        if ctx is None:
            o_d = _ctx_attn(z_na, B, L, row0)
        else:
            o_d = _na_attn(z_na, ctx[0], ctx[1], _na_bias(p['na_rpb'], L // GRID_W), B, L, row0)
        x_mix = _merge(x, o_a, o_hf, o_hb, z_hg, o_gf, o_gb, z_gd, o_d, z_mg, p['hg_onorm'], p['gd_onorm'],
                       lw['wb'], lw['wo'], mods, mod_row, 256, row0, Tg, x_mix)
        states.append((s_hg, s_gd))
    x = x_mix

    h2, top_e, wts = _router(x, p['norm2'], mods, mod_row, p['w_router'], p['b_router'], 512)
    th = T // MOE_SPLIT
    x_out = None
    for j in range(MOE_SPLIT):
        r0 = j * th
        tok, pos, blk_e = _route(top_e[r0:r0 + th, :TOP_K], th)
        xb = _sc_gather(h2, tok + r0)
        yb = _experts(xb, blk_e, lw['w_gu'], p['b_gu'], lw['w_dn'], p['b_dn'], lw['layer'])
        yg = _sc_gather(yb, pos.reshape(th, TOP_K).T.reshape(-1)).reshape(TOP_K, th, D_MODEL)
        x_out = _combine(x, yg, wts, mods, mod_row, norm_f, final_norm, 256, r0, x_out)
    return x_out, z_na, states


def kernel(x_prompt, x_sample, c, cache_na_k, cache_na_v, state_hgrn, state_gdn, c_ctx, w_ada, b_ada, norm1, norm2, norm_f, w_in, sgu_norm, sgu_w, sgu_b, hg_lb, hg_onorm, gd_conv, gd_A_log, gd_dt_bias, gd_onorm, na_rpb, w_branch, w_out, w_router, b_router, w_gu, b_gu, w_dn, b_dn):
    Bp, Lp, D = x_prompt.shape
    Bs, Ls, _ = x_sample.shape
    ctx_row = Bs
    cvecs = jnp.zeros((MOD_ROWS, D), F32).at[:Bs].set(c).at[ctx_row].set(c_ctx)
    mods = _modulation(cvecs, w_ada, b_ada)

    cs = jnp.cumsum(jax.nn.softmax(hg_lb.astype(F32), axis=1), axis=1)
    lb = cs - cs[:, :1]

    Tp, Ts = Bp * Lp, Bs * Ls
    x = jnp.concatenate([x_prompt.reshape(Tp, D), x_sample.reshape(Ts, D)], axis=0)
    mod_row = lambda r: jnp.where(r < Tp, ctx_row, (r - Tp) // Ls)
    ks_, vs_, hs_, gs_ = [], [], [], []
    for l in range(DEPTH):
        lw = _prep_layer(l, w_in, sgu_w, w_branch, w_out, w_gu, w_dn, gd_A_log, gd_dt_bias, lb)
        p = {'norm1': norm1[l], 'norm2': norm2[l], 'sgu_norm': sgu_norm[l], 'sgu_b': sgu_b[l],
             'gd_conv': gd_conv[l], 'hg_onorm': hg_onorm[l], 'gd_onorm': gd_onorm[l], 'na_rpb': na_rpb[l],
             'w_router': w_router[l], 'b_router': b_router[l], 'b_gu': b_gu, 'b_dn': b_dn}
        ctx = (cache_na_k[:, l].reshape(Bs, -1, NA_W), cache_na_v[:, l].reshape(Bs, -1, NA_W),
               state_hgrn[:, l], state_gdn[:, l])
        groups = [(0, Bp, Lp, None), (Tp, Bs, Ls, ctx)]
        x, z_na, states = _layer(x, groups, mods[l], mod_row, lw, p, norm_f, l == DEPTH - 1)
        ks_.append(z_na[:Tp, NA_W:2 * NA_W].reshape(Bp, Lp, NA_HEADS, NA_DH))
        vs_.append(z_na[:Tp, 2 * NA_W:].reshape(Bp, Lp, NA_HEADS, NA_DH))
        hs_.append(states[0][0])
        gs_.append(states[0][1])

    return (x[:Tp].reshape(Bp, Lp, D), x[Tp:].reshape(Bs, Ls, D),
            jnp.stack(ks_, axis=1), jnp.stack(vs_, axis=1), jnp.stack(hs_, axis=1), jnp.stack(gs_, axis=1))
```

```python
import functools
import math

import numpy as np
import jax
import jax.numpy as jnp
from jax import lax
from jax.experimental import pallas as pl
from jax.experimental.pallas import tpu as pltpu
from jax.experimental.pallas import tpu_sc as plsc

D_MODEL = 1024
DEPTH = 2
GRID_W = 64
BRANCH_W = 512
N_BRANCH = 4
SGU_CHUNK = 128
SGU_GROUPS = 4
HG_HEADS = 4
HG_DK = 128
HG_DV = 128
GD_HEADS = 4
GD_DK = 128
GD_DV = 128
NA_HEADS = 8
NA_DH = 64
NA_KH = 8
NA_KW = 16
N_EXP = 32
TOP_K = 4
D_FF = 1024
SWIGLU_LIMIT = 7.0
SWIGLU_ALPHA = 1.702
EPS = 1e-6

F32 = jnp.float32
BF16 = jnp.bfloat16
HI = lax.Precision.HIGHEST

LANES = 128
MOD_ROWS = 16
SCAN_C = 64
SCAN_ROWS = 2 * SCAN_C
MOE_BLOCK = 256
MOE_SPLIT = 2
NA_QROWS = 4
NA_KROWS = NA_QROWS + NA_KH
VMEM_LIMIT = 48 * 1024 * 1024

NT = (((1,), (1,)), ((), ()))
TN = (((0,), (0,)), ((), ()))


def _cparams(*sem):
    return pltpu.CompilerParams(dimension_semantics=sem, vmem_limit_bytes=VMEM_LIMIT)


def _bdot(a, b):
    return jnp.dot(a.astype(BF16), b.astype(BF16), preferred_element_type=F32)


def _bdot_g(a, b, dims):
    return lax.dot_general(a.astype(BF16), b.astype(BF16), dims, preferred_element_type=F32)


def _hdot(a, b):
    return jnp.dot(a, b, precision=HI, preferred_element_type=F32)


def _dot01(m3, x):
    hi = x.astype(BF16)
    r1 = x - hi.astype(F32)
    mid = r1.astype(BF16)
    lo = (r1 - mid.astype(F32)).astype(BF16)
    return jnp.dot(m3, jnp.concatenate([hi, mid, lo], axis=0), preferred_element_type=F32)


def _dot3(a, b):
    ah = a.astype(BF16)
    al = (a - ah.astype(F32)).astype(BF16)
    bh = b.astype(BF16)
    bl = (b - bh.astype(F32)).astype(BF16)
    return jnp.dot(jnp.concatenate([al, ah, ah], axis=1), jnp.concatenate([bh, bl, bh], axis=0),
                   preferred_element_type=F32)


HI_HALF = 0xFFFF0000


def _pack_bf16_pairs(x):
    w = x.shape[1] // 2
    b = lax.bitcast_convert_type(x.astype(BF16).astype(F32), jnp.uint32)
    return lax.bitcast_convert_type((b[:, :w] >> 16) | (b[:, w:] & jnp.uint32(HI_HALF)), jnp.int32)


def _unpack_bf16_pairs(p):
    p = lax.bitcast_convert_type(p, jnp.uint32)
    lo = lax.bitcast_convert_type(p << 16, F32)
    hi = lax.bitcast_convert_type(p & jnp.uint32(HI_HALF), F32)
    return jnp.concatenate([lo, hi], axis=1)


def _silu(x):
    return x * jax.nn.sigmoid(x)


def _log_sigmoid(x):
    return jnp.minimum(x, 0.0) - jnp.log1p(jnp.exp(-jnp.abs(x)))


def _logaddexp(a, b):
    return jnp.maximum(a, b) + jnp.log1p(jnp.exp(-jnp.abs(a - b)))


def _softplus(x):
    return jnp.maximum(x, 0.0) + jnp.log1p(jnp.exp(-jnp.abs(x)))


def _ada_kernel(c_ref, w_ref, b_ref, o_ref):
    o_ref[0] = _hdot(_silu(c_ref[...]), w_ref[0]) + b_ref[0]


def _modulation(cvecs, w_ada, b_ada):
    tn = 1536
    out = pl.pallas_call(
        _ada_kernel,
        grid=(DEPTH, 6 * D_MODEL // tn),
        in_specs=[pl.BlockSpec((MOD_ROWS, D_MODEL), lambda l, j: (0, 0)),
                  pl.BlockSpec((1, D_MODEL, tn), lambda l, j: (l, 0, j)),
                  pl.BlockSpec((1, 1, tn), lambda l, j: (l, 0, j))],
        out_specs=pl.BlockSpec((1, MOD_ROWS, tn), lambda l, j: (l, 0, j)),
        out_shape=jax.ShapeDtypeStruct((DEPTH, MOD_ROWS, 6 * D_MODEL), F32),
        compiler_params=_cparams("arbitrary", "arbitrary"),
        name="ada_modulation",
    )(cvecs, w_ada, b_ada.reshape(DEPTH, 1, 6 * D_MODEL))
    return out.reshape(DEPTH, MOD_ROWS * 6, 1, D_MODEL)


def _rms(x):
    return x * lax.rsqrt(jnp.mean(x * x, axis=-1, keepdims=True) + EPS)


def _normmod_kernel(x_ref, g_ref, sc_ref, sh_ref, o_ref):
    h = (_rms(x_ref[...]) * g_ref[...]) * (1.0 + sc_ref[0]) + sh_ref[0]
    o_ref[...] = h.astype(o_ref.dtype)


def _normmod(x, g, mods, mod_row, tm, part_shift, part_scale):
    T = x.shape[0]
    return pl.pallas_call(
        _normmod_kernel,
        grid=(T // tm,),
        in_specs=[pl.BlockSpec((tm, D_MODEL), lambda i: (i, 0)),
                  pl.BlockSpec((1, D_MODEL), lambda i: (0, 0)),
                  pl.BlockSpec((1, 1, D_MODEL), lambda i: (mod_row(i * tm) * 6 + part_scale, 0, 0)),
                  pl.BlockSpec((1, 1, D_MODEL), lambda i: (mod_row(i * tm) * 6 + part_shift, 0, 0))],
        out_specs=pl.BlockSpec((tm, D_MODEL), lambda i: (i, 0)),
        out_shape=jax.ShapeDtypeStruct((T, D_MODEL), BF16),
        compiler_params=_cparams("parallel"),
        name="normmod",
    )(x, g.reshape(1, D_MODEL), mods, mods)


def _mm_kernel(a_ref, w_ref, o_ref):
    o_ref[...] = jnp.dot(a_ref[...], w_ref[...], preferred_element_type=F32).astype(o_ref.dtype)


def _matmul(a, w, tm, tn, out_dtype=F32):
    T, K = a.shape
    N = w.shape[1]
    return pl.pallas_call(
        _mm_kernel,
        grid=(N // tn, T // tm),
        in_specs=[pl.BlockSpec((tm, K), lambda j, i: (i, 0)),
                  pl.BlockSpec((K, tn), lambda j, i: (0, j))],
        out_specs=pl.BlockSpec((tm, tn), lambda j, i: (i, j)),
        out_shape=jax.ShapeDtypeStruct((T, N), out_dtype),
        compiler_params=_cparams("parallel", "parallel"),
        name="in_proj",
    )(a, w)


def _sgu_kernel(u_ref, v_ref, gn_ref, ws_ref, bs_ref, o_ref):
    rows = u_ref.shape[0]
    gw = BRANCH_W // SGU_GROUPS
    u = jax.nn.gelu(u_ref[...])
    v = (_rms(jax.nn.gelu(v_ref[...])) * gn_ref[...]).astype(BF16)
    for n in range(rows // SGU_CHUNK):
        r = slice(n * SGU_CHUNK, (n + 1) * SGU_CHUNK)
        for g in range(SGU_GROUPS):
            cs = slice(g * gw, (g + 1) * gw)
            s = jnp.dot(ws_ref[g], v[r, cs], preferred_element_type=F32) + bs_ref[:, cs]
            o_ref[r, cs] = (u[r, cs] * s).astype(o_ref.dtype)


def _sgu(z_sgu, g_norm, w_s, b_s, rows, row0, T):
    gw = BRANCH_W // SGU_GROUPS
    b_exp = jnp.repeat(b_s.T, gw, axis=1)
    i0 = row0 // rows
    return pl.pallas_call(
        _sgu_kernel,
        grid=(T // rows,),
        in_specs=[pl.BlockSpec((rows, BRANCH_W), lambda i: (i0 + i, 0)),
                  pl.BlockSpec((rows, BRANCH_W), lambda i: (i0 + i, 1)),
                  pl.BlockSpec((1, BRANCH_W), lambda i: (0, 0)),
                  pl.BlockSpec((SGU_GROUPS, SGU_CHUNK, SGU_CHUNK), lambda i: (0, 0, 0)),
                  pl.BlockSpec((SGU_CHUNK, BRANCH_W), lambda i: (0, 0))],
        out_specs=pl.BlockSpec((rows, BRANCH_W), lambda i: (i, 0)),
        out_shape=jax.ShapeDtypeStruct((T, BRANCH_W), BF16),
        compiler_params=_cparams("parallel"),
        name="sgu",
    )(z_sgu, z_sgu, g_norm.reshape(1, BRANCH_W), w_s.astype(BF16), b_exp)


def _order(reverse):
    p = np.arange(SCAN_C)
    return SCAN_C - 1 - p if reverse else p


def _gla_consts():
    C = SCAN_C
    nlev = int(math.log2(C))
    mats, masks = [], []
    for reverse in (False, True):
        p = _order(reverse)
        pt, pr = p[:, None], p[None, :]
        m_d, k_d = [], []
        for lev in range(nlev):
            w = C >> (lev + 1)
            parent = p // (2 * w)
            later = (p % (2 * w)) >= w
            anchor = (parent * 2 * w + w - 1)[:, None]
            m = np.where(later[:, None], (pr > anchor) & (pr <= pt), (pr > pt) & (pr <= anchor))
            m_d.append(m)
            k_d.append((parent[:, None] == parent[None, :]) & later[:, None] & ~later[None, :])
        m_d.append(pr <= pt)
        m_d.append(pr > pt)
        k_d.append(np.eye(C, dtype=bool))
        mats.append(np.concatenate(m_d, axis=0))
        masks.append(np.stack(k_d))
    return (np.stack(mats).astype(np.float32), np.stack(masks).astype(np.float32))


def _delta_consts():
    C = SCAN_C
    tri, sl, incl, strict = [], [], [], []
    for reverse in (False, True):
        p = _order(reverse)
        pt, pr = p[:, None], p[None, :]
        tri.append(np.concatenate([pr <= pt, pr > pt], axis=0))
        sl.append(np.concatenate([pt > pr, np.zeros((C, LANES - C), bool), np.ones((C, LANES), bool)], axis=1))
        incl.append(pr <= pt)
        strict.append(pr < pt)
    f = lambda a: np.stack(a).astype(np.float32)
    return f(tri), f(sl), f(incl), f(strict)


def _hgrn_kernel(qf_ref, ff_ref, vf_ref, qb_ref, fb_ref, vb_ref, lb_ref, mat_ref, msk_ref, *rest, has_state):
    if has_state:
        s0_ref, of_ref, ob_ref, sfin_ref, st_ref = rest
    else:
        of_ref, ob_ref, sfin_ref, st_ref = rest
    C = SCAN_C
    nlev = msk_ref.shape[1] - 1
    c = pl.program_id(1)
    last_c = pl.num_programs(1) - 1

    @pl.when(c == 0)
    def _():
        for d in range(2):
            for h in range(HG_HEADS):
                if has_state:
                    st_ref[d, h] = s0_ref[0, d, h].T
                else:
                    st_ref[d, h] = jnp.zeros((HG_DV, HG_DK), F32)

    q_refs, f_refs, v_refs, o_refs = (qf_ref, qb_ref), (ff_ref, fb_ref), (vf_ref, vb_ref), (of_ref, ob_ref)
    sub = qf_ref.shape[0] // C
    rows = lambda j: slice(j * C, (j + 1) * C)
    col = lambda h: slice(h * HG_DK, (h + 1) * HG_DK)
    parts = [(d, j) for d in range(2) for j in range(sub)]
    q, k, fac, att = {}, {}, {}, {}
    for d, j in parts:
        zf = f_refs[d][rows(j), :]
        logf = _logaddexp(lb_ref[d:d + 1, :], lb_ref[2 + d:3 + d, :] + _log_sigmoid(zf))
        k[d, j] = lb_ref[4 + d:5 + d, :] * jax.nn.sigmoid(-zf)
        q[d, j] = _silu(q_refs[d][rows(j), :]) * (HG_DK ** -0.5)
        fac[d, j] = jnp.exp(_dot01(mat_ref[d], logf))
    for d, j in parts:
        for h in range(HG_HEADS):
            qh, kh = q[d, j][:, col(h)], k[d, j][:, col(h)]
            acc = msk_ref[d, nlev] * _bdot_g(qh, kh, NT)
            for i in range(nlev):
                fi = fac[d, j][i * C:(i + 1) * C, col(h)]
                acc = acc + msk_ref[d, i] * _bdot_g(qh * fi, kh * fi, NT)
            att[d, j, h] = acc
    chains = [(d, h) for d in range(2) for h in range(HG_HEADS)]
    for t in range(sub):
        jd = (t, sub - 1 - t)
        for d, h in chains:
            j = jd[d]
            eb = fac[d, j][nlev * C:(nlev + 1) * C, col(h)]
            o_refs[d][rows(j), col(h)] = (_bdot(att[d, j, h], v_refs[d][rows(j), col(h)])
                                          + _bdot_g(q[d, j][:, col(h)] * eb, st_ref[d, h], NT))
        for d, h in chains:
            j = jd[d]
            eb = fac[d, j][nlev * C:(nlev + 1) * C, col(h)]
            er = fac[d, j][(nlev + 1) * C:, col(h)]
            e_last = eb[C - 1:C] if d == 0 else eb[0:1]
            st_ref[d, h] = st_ref[d, h] * e_last + _bdot_g(v_refs[d][rows(j), col(h)], k[d, j][:, col(h)] * er, TN)

    @pl.when(c == last_c)
    def _():
        for d in range(2):
            for h in range(HG_HEADS):
                sfin_ref[0, d, h] = st_ref[d, h].T


def _hgrn(z_hg, lbp, s0, B, L, row0):
    n = L // SCAN_ROWS
    c0 = row0 // SCAN_ROWS
    mats, masks = _gla_consts()
    mats = np.tile(mats, (1, 1, 3))
    blk = (SCAN_ROWS, HG_HEADS * HG_DK)
    fwd = lambda col: pl.BlockSpec(blk, lambda b, c: (c0 + b * n + c, col))
    bwd = lambda col: pl.BlockSpec(blk, lambda b, c: (c0 + b * n + n - 1 - c, col))
    st_blk = (1, 2, HG_HEADS, HG_DK, HG_DV)
    in_specs = [fwd(0), fwd(1), fwd(3), bwd(0), bwd(2), bwd(3),
                pl.BlockSpec(lbp.shape, lambda b, c: (0, 0)),
                pl.BlockSpec(mats.shape, lambda b, c: (0, 0, 0)),
                pl.BlockSpec(masks.shape, lambda b, c: (0, 0, 0, 0))]
    args = [z_hg] * 6 + [lbp, jnp.asarray(mats, BF16), jnp.asarray(masks)]
    if s0 is not None:
        in_specs.append(pl.BlockSpec(st_blk, lambda b, c: (b, 0, 0, 0, 0)))
        args.append(s0)
    return pl.pallas_call(
        functools.partial(_hgrn_kernel, has_state=s0 is not None),
        grid=(B, n),
        in_specs=in_specs,
        out_specs=[pl.BlockSpec(blk, lambda b, c: (b * n + c, 0)),
                   pl.BlockSpec(blk, lambda b, c: (b * n + n - 1 - c, 0)),
                   pl.BlockSpec(st_blk, lambda b, c: (b, 0, 0, 0, 0))],
        out_shape=[jax.ShapeDtypeStruct((B * L, HG_HEADS * HG_DV), F32),
                   jax.ShapeDtypeStruct((B * L, HG_HEADS * HG_DV), F32),
                   jax.ShapeDtypeStruct((B,) + st_blk[1:], F32)],
        scratch_shapes=[pltpu.VMEM((2, HG_HEADS, HG_DV, HG_DK), F32)],
        compiler_params=_cparams("parallel", "arbitrary"),
        name="hgrn_scan",
    )(*args)


GD_NQ = GD_HEADS * GD_DK
GD_QKV = 2 * GD_NQ + GD_HEADS * GD_DV
HALO = 8


def _gdprep_kernel(x_ref, prev_ref, next_ref, w_ref, o_ref, *, tiles_per_seq):
    R = x_ref.shape[0]
    t = pl.program_id(0) % tiles_per_seq
    x = x_ref[...]
    prev_row = jnp.where(t == 0, 0.0, prev_ref[HALO - 1:HALO, :])
    next_row = jnp.where(t == tiles_per_seq - 1, 0.0, next_ref[0:1, :])
    row = lax.broadcasted_iota(jnp.int32, x.shape, 0)
    xm1 = jnp.where(row == 0, prev_row, pltpu.roll(x, 1, 0))
    xp1 = jnp.where(row == R - 1, next_row, pltpu.roll(x, R - 1, 0))
    y = _silu(w_ref[0:1, :] * xm1 + w_ref[1:2, :] * x + w_ref[2:3, :] * xp1)
    for j in range(2 * GD_HEADS):
        cs = slice(j * GD_DK, (j + 1) * GD_DK)
        seg = y[:, cs]
        seg = seg * lax.rsqrt(jnp.sum(seg * seg, axis=-1, keepdims=True) + EPS)
        if j < GD_HEADS:
            seg = seg * (GD_DK ** -0.5)
        o_ref[:, cs] = seg
    o_ref[:, 2 * GD_NQ:] = y[:, 2 * GD_NQ:]


def _gdprep(z_gd, conv_w, L, rows, row0, T):
    tps = L // rows
    hb = rows // HALO
    nhalo = z_gd.shape[0] // HALO
    i0 = row0 // rows
    return pl.pallas_call(
        functools.partial(_gdprep_kernel, tiles_per_seq=tps),
        grid=(T // rows,),
        in_specs=[pl.BlockSpec((rows, GD_QKV), lambda i: (i0 + i, 0)),
                  pl.BlockSpec((HALO, GD_QKV), lambda i: (jnp.maximum((i0 + i) * hb - 1, 0), 0)),
                  pl.BlockSpec((HALO, GD_QKV), lambda i: (jnp.minimum((i0 + i + 1) * hb, nhalo - 1), 0)),
                  pl.BlockSpec((3, GD_QKV), lambda i: (0, 0))],
        out_specs=pl.BlockSpec((rows, GD_QKV), lambda i: (i, 0)),
        out_shape=jax.ShapeDtypeStruct((T, GD_QKV), F32),
        compiler_params=_cparams("parallel"),
        name="gdn_prep",
    )(z_gd, z_gd, z_gd, conv_w)


def _gdn_kernel(xf_ref, abf_ref, xb_ref, abb_ref, par_ref, tri_ref, sl_ref, incl_ref, strict_ref, msk_ref, *rest,
                has_state):
    if has_state:
        s0_ref, of_ref, ob_ref, sfin_ref, st_ref = rest
    else:
        of_ref, ob_ref, sfin_ref, st_ref = rest
    C = SCAN_C
    c = pl.program_id(1)
    last_c = pl.num_programs(1) - 1
    nlev = msk_ref.shape[1] - 1

    @pl.when(c == 0)
    def _():
        for d in range(2):
            for h in range(GD_HEADS):
                if has_state:
                    st_ref[d, h] = s0_ref[0, d, h].T
                else:
                    st_ref[d, h] = jnp.zeros((GD_DV, GD_DK), F32)

    x_refs, ab_refs, o_refs = (xf_ref, xb_ref), (abf_ref, abb_ref), (of_ref, ob_ref)
    sub = xf_ref.shape[0] // C
    rows = lambda j: slice(j * C, (j + 1) * C)
    chains = [(d, h) for d in range(2) for h in range(GD_HEADS)]
    parts = [(d, j, h) for d in range(2) for j in range(sub) for h in range(GD_HEADS)]
    q_of = lambda d, j, h: x_refs[d][rows(j), h * GD_DK:(h + 1) * GD_DK]
    k_of = lambda d, j, h: x_refs[d][rows(j), GD_NQ + h * GD_DK:GD_NQ + (h + 1) * GD_DK]
    v_of = lambda d, j, h: x_refs[d][rows(j), 2 * GD_NQ + h * GD_DV:2 * GD_NQ + (h + 1) * GD_DV]
    g_all, beta_all = [], []
    for d in range(2):
        ab = ab_refs[d][...]
        g_all.append(-jnp.exp(par_ref[0:1, :]) * _softplus(ab + par_ref[1:2, :]))
        beta_all.append(jax.nn.sigmoid(ab))

    def beta_of(d, j, h):
        lane = 2 * GD_HEADS + d * GD_HEADS + h
        return jnp.broadcast_to(beta_all[d][rows(j), lane:lane + 1], (C, LANES))

    decay, e_cum, e_rest, kb, a = {}, {}, {}, {}, {}
    for d, j, h in parts:
        lane = d * GD_HEADS + h
        g_b = jnp.broadcast_to(g_all[d][rows(j), lane:lane + 1], (C, 2 * LANES))
        sums = _dot01(tri_ref[d], g_b * sl_ref[d])
        decay[d, j, h] = jnp.exp(sums[:C, :C])
        e_cum[d, j, h] = jnp.exp(sums[:C, LANES:])
        e_rest[d, j, h] = jnp.exp(sums[C:, LANES:])
    for d, j, h in parts:
        k = k_of(d, j, h)
        kb[d, j, h] = k * beta_of(d, j, h)
        a[d, j, h] = strict_ref[d] * decay[d, j, h] * _bdot_g(kb[d, j, h], k, NT)
    inv_m = {pt: -(msk_ref[pt[0], nlev - 1] * a[pt]) for pt in parts}
    for lev in range(nlev - 2, -1, -1):
        a_w = {pt: msk_ref[pt[0], lev] * a[pt] for pt in parts}
        p = {pt: a_w[pt] + _bdot(inv_m[pt], a_w[pt]) for pt in parts}
        inv_m = {pt: inv_m[pt] - p[pt] - _bdot(p[pt], inv_m[pt]) for pt in parts}
    sol, att = {}, {}
    for d, j, h in parts:
        rhs = jnp.concatenate([v_of(d, j, h) * beta_of(d, j, h), kb[d, j, h] * e_cum[d, j, h]], axis=1)
        sol[d, j, h] = rhs + _dot3(inv_m[d, j, h], rhs)
        att[d, j, h] = incl_ref[d] * decay[d, j, h] * _bdot_g(q_of(d, j, h), k_of(d, j, h), NT)
    for t in range(sub):
        jd = (t, sub - 1 - t)
        u = {}
        for d, h in chains:
            pt = (d, jd[d], h)
            u[d, h] = sol[pt][:, :GD_DV] - _bdot_g(sol[pt][:, GD_DV:], st_ref[d, h], NT)
        for d, h in chains:
            pt = (d, jd[d], h)
            o_refs[d][rows(jd[d]), h * GD_DV:(h + 1) * GD_DV] = (
                _bdot_g(q_of(*pt) * e_cum[pt], st_ref[d, h], NT) + _bdot(att[pt], u[d, h]))
        for d, h in chains:
            pt = (d, jd[d], h)
            e_last = e_cum[pt][C - 1:C] if d == 0 else e_cum[pt][0:1]
            st_ref[d, h] = st_ref[d, h] * e_last + _bdot_g(u[d, h], k_of(*pt) * e_rest[pt], TN)

    @pl.when(c == last_c)
    def _():
        for d in range(2):
            for h in range(GD_HEADS):
                sfin_ref[0, d, h] = st_ref[d, h].T


def _gdn(qkv, z_gd, par, s0, B, L, row0):
    n = L // SCAN_ROWS
    c0 = row0 // SCAN_ROWS
    tri, sl, incl, strict = (jnp.asarray(a) for a in _delta_consts())
    tri = jnp.tile(tri, (1, 1, 3)).astype(BF16)
    masks = jnp.asarray(_gla_consts()[1])
    ab_col = (GD_QKV + GD_HEADS * GD_DV) // LANES
    xblk = (SCAN_ROWS, GD_QKV)
    ablk = (SCAN_ROWS, LANES)
    oblk = (SCAN_ROWS, GD_HEADS * GD_DV)
    st_blk = (1, 2, GD_HEADS, GD_DK, GD_DV)
    fwd = lambda b, c: b * n + c
    bwd = lambda b, c: b * n + n - 1 - c
    const3 = lambda a: pl.BlockSpec(a.shape, lambda b, c: (0, 0, 0))
    in_specs = [pl.BlockSpec(xblk, lambda b, c: (fwd(b, c), 0)),
                pl.BlockSpec(ablk, lambda b, c: (c0 + fwd(b, c), ab_col)),
                pl.BlockSpec(xblk, lambda b, c: (bwd(b, c), 0)),
                pl.BlockSpec(ablk, lambda b, c: (c0 + bwd(b, c), ab_col)),
                pl.BlockSpec(par.shape, lambda b, c: (0, 0)),
                const3(tri), const3(sl), const3(incl), const3(strict),
                pl.BlockSpec(masks.shape, lambda b, c: (0, 0, 0, 0))]
    args = [qkv, z_gd, qkv, z_gd, par, tri, sl, incl, strict, masks]
    if s0 is not None:
        in_specs.append(pl.BlockSpec(st_blk, lambda b, c: (b, 0, 0, 0, 0)))
        args.append(s0)
    return pl.pallas_call(
        functools.partial(_gdn_kernel, has_state=s0 is not None),
        grid=(B, n),
        in_specs=in_specs,
        out_specs=[pl.BlockSpec(oblk, lambda b, c: (fwd(b, c), 0)),
                   pl.BlockSpec(oblk, lambda b, c: (bwd(b, c), 0)),
                   pl.BlockSpec(st_blk, lambda b, c: (b, 0, 0, 0, 0))],
        out_shape=[jax.ShapeDtypeStruct((B * L, GD_HEADS * GD_DV), F32),
                   jax.ShapeDtypeStruct((B * L, GD_HEADS * GD_DV), F32),
                   jax.ShapeDtypeStruct((B,) + st_blk[1:], F32)],
        scratch_shapes=[pltpu.VMEM((2, GD_HEADS, GD_DV, GD_DK), F32)],
        compiler_params=_cparams("parallel", "arbitrary"),
        name="gdn_scan",
    )(*args)


NA_W = NA_HEADS * NA_DH


def _softmax_pv(s, v):
    m = jnp.max(s, axis=-1, keepdims=True)
    e = jnp.exp(s - m)
    den = jnp.sum(e, axis=-1, keepdims=True)
    return jnp.dot(e.astype(BF16), v, preferred_element_type=F32) / den


def _ctx_attn_kernel(q_ref, k_ref, v_ref, o_ref):
    for h in range(NA_HEADS):
        cs = slice(h * NA_DH, (h + 1) * NA_DH)
        q = (q_ref[:, cs] * (NA_DH ** -0.5)).astype(BF16)
        s = lax.dot_general(q, k_ref[:, cs].astype(BF16), NT, preferred_element_type=F32)
        o_ref[:, cs] = _softmax_pv(s, v_ref[:, cs].astype(BF16)).astype(o_ref.dtype)


def _ctx_attn(z_na, B, L, row0):
    blk = (L, NA_W)
    b0 = row0 // L
    return pl.pallas_call(
        _ctx_attn_kernel,
        grid=(B,),
        in_specs=[pl.BlockSpec(blk, lambda b: (b0 + b, 0)),
                  pl.BlockSpec(blk, lambda b: (b0 + b, 1)),
                  pl.BlockSpec(blk, lambda b: (b0 + b, 2))],
        out_specs=pl.BlockSpec(blk, lambda b: (b, 0)),
        out_shape=jax.ShapeDtypeStruct((B * L, NA_W), BF16),
        compiler_params=_cparams("parallel"),
        name="ctx_attn",
    )(z_na, z_na, z_na)


def _na_kernel(q_ref, *rest, n_kblk, nkeys_nb):
    k_refs = rest[:n_kblk]
    v_refs = rest[n_kblk:2 * n_kblk]
    kc_ref, vc_ref, bias_ref, o_ref, kbuf, vbuf = rest[2 * n_kblk:]
    qb = q_ref.shape[0]
    for i in range(n_kblk):
        kbuf[i * qb:(i + 1) * qb, :] = k_refs[i][...].astype(BF16)
        vbuf[i * qb:(i + 1) * qb, :] = v_refs[i][...].astype(BF16)
    kbuf[nkeys_nb:, :] = kc_ref[0].astype(BF16)
    vbuf[nkeys_nb:, :] = vc_ref[0].astype(BF16)
    for h in range(NA_HEADS):
        cs = slice(h * NA_DH, (h + 1) * NA_DH)
        q = (q_ref[:, cs] * (NA_DH ** -0.5)).astype(BF16)
        s = lax.dot_general(q, kbuf[:, cs], NT, preferred_element_type=F32)
        s_nb = s[:, :nkeys_nb] + bias_ref[0, h]
        s_cx = s[:, nkeys_nb:]
        m = jnp.maximum(jnp.max(s_nb, axis=-1, keepdims=True), jnp.max(s_cx, axis=-1, keepdims=True))
        e_nb = jnp.exp(s_nb - m)
        e_cx = jnp.exp(s_cx - m)
        den = jnp.sum(e_nb, axis=-1, keepdims=True) + jnp.sum(e_cx, axis=-1, keepdims=True)
        pv = (jnp.dot(e_nb.astype(BF16), vbuf[:nkeys_nb, cs], preferred_element_type=F32)
              + jnp.dot(e_cx.astype(BF16), vbuf[nkeys_nb:, cs], preferred_element_type=F32))
        o_ref[:, cs] = (pv / den).astype(o_ref.dtype)


def _na_bias(rpb, rows):
    qr, kr, col = np.arange(NA_QROWS), np.arange(NA_KROWS), np.arange(GRID_W)
    nblk = rows // NA_QROWS
    ndr, ndc = 2 * NA_KH - 1, 2 * NA_KW - 1
    sel_r, row_ok = [], []
    for m in (0, 1, nblk - 1):
        r = (NA_QROWS * m + qr)[:, None]
        start = np.clip(NA_QROWS * m - NA_KH // 2, 0, rows - NA_KROWS)
        kra = (start + kr)[None, :]
        r0 = np.clip(r - NA_KH // 2, 0, rows - NA_KH)
        row_ok.append((kra >= r0) & (kra < r0 + NA_KH))
        dr = np.clip(kra - r + NA_KH - 1, 0, ndr - 1)
        sel_r.append(dr[..., None] == np.arange(ndr))
    sel_r = np.stack(sel_r).astype(np.float32)
    row_ok = np.stack(row_ok)
    col_start = np.clip(col - NA_KW // 2, 0, GRID_W - NA_KW)[:, None]
    col_ok = (col[None, :] >= col_start) & (col[None, :] < col_start + NA_KW)
    dc = np.clip(col[None, :] - col[:, None], -(NA_KW - 1), NA_KW - 1) + NA_KW - 1
    sel_c = (dc[..., None] == np.arange(ndc)).astype(np.float32)
    bias = jnp.einsum('hab,vqka,xyb->vhqxky', rpb.astype(F32), sel_r, sel_c, precision=HI)
    ok = row_ok[:, None, :, None, :, None] & col_ok[None, None, None, :, None, :]
    bias = jnp.where(ok, bias, -jnp.inf)
    return bias.reshape(3, NA_HEADS, NA_QROWS * GRID_W, NA_KROWS * GRID_W)


def _na_attn(z_na, k_ctx, v_ctx, bias, B, S, row0):
    rows = S // GRID_W
    qb = NA_QROWS * GRID_W
    m0 = row0 // qb
    nblk = rows // NA_QROWS
    n_kblk = NA_KROWS // NA_QROWS
    lc = k_ctx.shape[1]
    nkeys_nb = NA_KROWS * GRID_W
    kstart = lambda m: jnp.clip(m - 1, 0, nblk - n_kblk)
    variant = lambda m: jnp.where(m == 0, 0, jnp.where(m == nblk - 1, 2, 1))
    kv_specs = lambda col: [pl.BlockSpec((qb, NA_W), functools.partial(
        lambda b, m, i, col: (m0 + b * nblk + kstart(m) + i, col), i=i, col=col)) for i in range(n_kblk)]
    return pl.pallas_call(
        functools.partial(_na_kernel, n_kblk=n_kblk, nkeys_nb=nkeys_nb),
        grid=(B, nblk),
        in_specs=([pl.BlockSpec((qb, NA_W), lambda b, m: (m0 + b * nblk + m, 0))] + kv_specs(1) + kv_specs(2)
                  + [pl.BlockSpec((1, lc, NA_W), lambda b, m: (b, 0, 0)),
                     pl.BlockSpec((1, lc, NA_W), lambda b, m: (b, 0, 0)),
                     pl.BlockSpec((1,) + bias.shape[1:], lambda b, m: (variant(m), 0, 0, 0))]),
        out_specs=pl.BlockSpec((qb, NA_W), lambda b, m: (b * nblk + m, 0)),
        out_shape=jax.ShapeDtypeStruct((B * S, NA_W), BF16),
        scratch_shapes=[pltpu.VMEM((nkeys_nb + lc, NA_W), BF16), pltpu.VMEM((nkeys_nb + lc, NA_W), BF16)],
        compiler_params=_cparams("parallel", "arbitrary"),
        name="na_attn",
    )(z_na, *([z_na] * (2 * n_kblk)), k_ctx, v_ctx, bias)


def _head_rms(o, g_row):
    parts = []
    for h in range(o.shape[1] // LANES):
        parts.append(_rms(o[:, h * LANES:(h + 1) * LANES]) * g_row)
    return jnp.concatenate(parts, axis=1)


def _merge_kernel(x_ref, oa_ref, hf_ref, hb_ref, hg_ref, gf_ref, gb_ref, gg_ref, od_ref, mg_ref,
                  hn_ref, gn_ref, wb_ref, wo_ref, g1_ref, *rest):
    o_ref = rest[-1]
    o_b = _head_rms(hf_ref[...] + hb_ref[...], hn_ref[...]) * _silu(hg_ref[...])
    o_c = _head_rms(gf_ref[...] + gb_ref[...], gn_ref[...]) * _silu(gg_ref[...])
    branches = (oa_ref[...], o_b.astype(BF16), o_c.astype(BF16), od_ref[...])
    merged = None
    for n_, o_n in enumerate(branches):
        gate = jax.nn.sigmoid(mg_ref[:, n_ * D_MODEL:(n_ + 1) * D_MODEL].astype(F32))
        term = gate * jnp.dot(o_n, wb_ref[n_], preferred_element_type=F32)
        merged = term if merged is None else merged + term
    mix = jnp.dot(merged.astype(BF16), wo_ref[...], preferred_element_type=F32)
    o_ref[...] = x_ref[...] + g1_ref[0] * mix


def _merge(x, o_a, o_hf, o_hb, z_hg, o_gf, o_gb, z_gd, o_d, z_mg, hg_onorm, gd_onorm, wb, wo, mods, mod_row, tm,
           row0, T, x_acc):
    i0 = row0 // tm
    glob = lambda w, col=0: pl.BlockSpec((tm, w), lambda i: (i0 + i, col))
    loc = lambda w: pl.BlockSpec((tm, w), lambda i: (i, 0))
    full = lambda a: pl.BlockSpec(a.shape, lambda i: (0,) * a.ndim)
    hn = hg_onorm.reshape(1, HG_DV)
    gn = gd_onorm.reshape(1, GD_DV)
    in_specs = [glob(D_MODEL), loc(BRANCH_W), loc(BRANCH_W), loc(BRANCH_W), glob(BRANCH_W, 4),
                loc(BRANCH_W), loc(BRANCH_W), glob(BRANCH_W, GD_QKV // BRANCH_W), loc(BRANCH_W),
                glob(N_BRANCH * D_MODEL), full(hn), full(gn), full(wb), full(wo),
                pl.BlockSpec((1, 1, D_MODEL), lambda i: (mod_row(row0 + i * tm) * 6 + 2, 0, 0))]
    args = [x, o_a, o_hf, o_hb, z_hg, o_gf, o_gb, z_gd, o_d, z_mg, hn, gn, wb, wo, mods]
    aliases = {}
    if x_acc is not None:
        in_specs.append(pl.BlockSpec(memory_space=pl.ANY))
        aliases = {len(args): 0}
        args.append(x_acc)
    return pl.pallas_call(
        _merge_kernel,
        grid=(T // tm,),
        in_specs=in_specs,
        out_specs=glob(D_MODEL),
        out_shape=jax.ShapeDtypeStruct(x.shape, F32),
        input_output_aliases=aliases,
        compiler_params=_cparams("parallel"),
        name="merge",
    )(*args)


SC_CORES = 2
SC_SUBCORES = 16
SC_WIN = 64


def _sc_gather(table, idx):
    V, D = table.shape
    N = idx.shape[0]
    nw = SC_CORES * SC_SUBCORES
    per_w = N // nw
    n_win = per_w // SC_WIN
    assert per_w * nw == N and n_win * SC_WIN == per_w
    mesh = plsc.VectorSubcoreMesh(core_axis_name="c", subcore_axis_name="s")

    @functools.partial(
        pl.kernel, mesh=mesh,
        out_type=jax.ShapeDtypeStruct((N, D), table.dtype),
        scratch_types=[pltpu.VMEM((n_win, SC_WIN), jnp.int32),
                       pltpu.VMEM((SC_WIN, D), table.dtype),
                       pltpu.SemaphoreType.DMA],
    )
    def gather_rows(table_hbm, idx_hbm, out_hbm, idx_v, rows_v, sem):
        wid = lax.axis_index("s") * SC_CORES + lax.axis_index("c")
        pltpu.sync_copy(idx_hbm.at[wid], idx_v)

        @pl.loop(0, n_win)
        def _(w):
            pltpu.async_copy(table_hbm.at[idx_v.at[w]], rows_v, sem).wait()
            pltpu.sync_copy(rows_v, out_hbm.at[pl.ds(wid * per_w + w * SC_WIN, SC_WIN)])

    return gather_rows(table, idx.reshape(nw, n_win, SC_WIN))


def _router_kernel(x_ref, g_ref, sc_ref, sh_ref, wr_ref, br_ref, h_ref, e_ref, w_ref):
    h = (_rms(x_ref[...]) * g_ref[...]) * (1.0 + sc_ref[0]) + sh_ref[0]
    h_ref[...] = _pack_bf16_pairs(h)
    logits = _hdot(h, wr_ref[...]) + br_ref[...]
    lane = lax.broadcasted_iota(jnp.int32, logits.shape, 1)
    e_out = jnp.zeros(logits.shape, jnp.int32)
    v_out = jnp.zeros(logits.shape, F32)
    top0 = None
    for k in range(TOP_K):
        m = jnp.max(logits, axis=-1, keepdims=True)
        idx = jnp.min(jnp.where(logits == m, lane, LANES), axis=-1, keepdims=True)
        if k == 0:
            top0 = m
        e_out = jnp.where(lane == k, idx, e_out)
        v_out = jnp.where(lane == k, jnp.exp(m - top0), v_out)
        logits = jnp.where(lane == idx, -jnp.inf, logits)
    e_ref[...] = e_out
    w_ref[...] = v_out / jnp.sum(v_out, axis=-1, keepdims=True)


def _router(x, g, mods, mod_row, w_router, b_router, tm):
    T = x.shape[0]
    wr = jnp.zeros((D_MODEL, LANES), F32).at[:, :N_EXP].set(w_router)
    br = jnp.full((1, LANES), -jnp.inf, F32).at[0, :N_EXP].set(b_router)
    row = lambda w: pl.BlockSpec((tm, w), lambda i: (i, 0))
    return pl.pallas_call(
        _router_kernel,
        grid=(T // tm,),
        in_specs=[row(D_MODEL),
                  pl.BlockSpec((1, D_MODEL), lambda i: (0, 0)),
                  pl.BlockSpec((1, 1, D_MODEL), lambda i: (mod_row(i * tm) * 6 + 4, 0, 0)),
                  pl.BlockSpec((1, 1, D_MODEL), lambda i: (mod_row(i * tm) * 6 + 3, 0, 0)),
                  pl.BlockSpec((D_MODEL, LANES), lambda i: (0, 0)),
                  pl.BlockSpec((1, LANES), lambda i: (0, 0))],
        out_specs=[row(D_MODEL // 2), row(LANES), row(LANES)],
        out_shape=[jax.ShapeDtypeStruct((T, D_MODEL // 2), jnp.int32),
                   jax.ShapeDtypeStruct((T, LANES), jnp.int32),
                   jax.ShapeDtypeStruct((T, LANES), F32)],
        compiler_params=_cparams("parallel"),
        name="router",
    )(x, g.reshape(1, D_MODEL), mods, mods, wr, br)


def _expert_kernel(blk_e_ref, x_ref, wgu_ref, bgu_ref, wdn_ref, bdn_ref, o_ref, wgu_bf, wdn_bf):
    i = pl.program_id(0)
    new_expert = jnp.logical_or(i == 0, blk_e_ref[i] != blk_e_ref[jnp.maximum(i - 1, 0)])

    @pl.when(new_expert)
    def _():
        wgu_bf[...] = wgu_ref[0].astype(BF16)
        wdn_bf[...] = wdn_ref[0].astype(BF16)

    x = _unpack_bf16_pairs(x_ref[...]).astype(BF16)
    gu = jnp.dot(x, wgu_bf[...], preferred_element_type=F32) + bgu_ref[0]
    a = jnp.minimum(gu[:, :D_FF], SWIGLU_LIMIT)
    lin = jnp.clip(gu[:, D_FF:], -SWIGLU_LIMIT, SWIGLU_LIMIT)
    y = a * jax.nn.sigmoid(SWIGLU_ALPHA * a) * (lin + 1.0)
    o_ref[...] = _pack_bf16_pairs(jnp.dot(y.astype(BF16), wdn_bf[...], preferred_element_type=F32) + bdn_ref[0])


def _experts(xb, blk_e, w_gu, b_gu, w_dn, b_dn, layer):
    n_pad = xb.shape[0]
    n_blocks = n_pad // MOE_BLOCK
    e0 = layer * N_EXP
    w_gu = w_gu.reshape(DEPTH * N_EXP, D_MODEL, 2 * D_FF)
    w_dn = w_dn.reshape(DEPTH * N_EXP, D_FF, D_MODEL)
    grid_spec = pltpu.PrefetchScalarGridSpec(
        num_scalar_prefetch=1,
        grid=(n_blocks,),
        in_specs=[pl.BlockSpec((MOE_BLOCK, D_MODEL // 2), lambda i, e: (i, 0)),
                  pl.BlockSpec((1, D_MODEL, 2 * D_FF), lambda i, e: (e0 + e[i], 0, 0)),
                  pl.BlockSpec((1, 1, 2 * D_FF), lambda i, e: (e0 + e[i], 0, 0)),
                  pl.BlockSpec((1, D_FF, D_MODEL), lambda i, e: (e0 + e[i], 0, 0)),
                  pl.BlockSpec((1, 1, D_MODEL), lambda i, e: (e0 + e[i], 0, 0))],
        out_specs=pl.BlockSpec((MOE_BLOCK, D_MODEL // 2), lambda i, e: (i, 0)),
        scratch_shapes=[pltpu.VMEM((D_MODEL, 2 * D_FF), BF16), pltpu.VMEM((D_FF, D_MODEL), BF16)],
    )
    return pl.pallas_call(
        _expert_kernel,
        grid_spec=grid_spec,
        out_shape=jax.ShapeDtypeStruct((n_pad, D_MODEL // 2), jnp.int32),
        compiler_params=_cparams("arbitrary"),
        name="experts",
    )(blk_e, xb, w_gu, b_gu.reshape(DEPTH * N_EXP, 1, 2 * D_FF), w_dn, b_dn.reshape(DEPTH * N_EXP, 1, D_MODEL))


def _combine_kernel(x_ref, y_ref, w_ref, g2_ref, nf_ref, *rest, final_norm):
    o_ref = rest[-1]
    acc = None
    for k in range(TOP_K):
        term = _unpack_bf16_pairs(y_ref[k]) * w_ref[:, k:k + 1]
        acc = term if acc is None else acc + term
    x = x_ref[...] + g2_ref[0] * acc
    if final_norm:
        x = _rms(x) * nf_ref[...]
    o_ref[...] = x


def _combine(x, yg, wts, mods, mod_row, norm_f, final_norm, tm, row0, x_acc):
    T = yg.shape[1]
    i0 = row0 // tm
    glob = lambda w: pl.BlockSpec((tm, w), lambda i: (i0 + i, 0))
    in_specs = [glob(D_MODEL), pl.BlockSpec((TOP_K, tm, D_MODEL // 2), lambda i: (0, i, 0)), glob(LANES),
                pl.BlockSpec((1, 1, D_MODEL), lambda i: (mod_row(row0 + i * tm) * 6 + 5, 0, 0)),
                pl.BlockSpec((1, D_MODEL), lambda i: (0, 0))]
    args = [x, yg, wts, mods, norm_f.reshape(1, D_MODEL)]
    aliases = {}
    if x_acc is not None:
        in_specs.append(pl.BlockSpec(memory_space=pl.ANY))
        aliases = {len(args): 0}
        args.append(x_acc)
    return pl.pallas_call(
        functools.partial(_combine_kernel, final_norm=final_norm),
        grid=(T // tm,),
        in_specs=in_specs,
        out_specs=glob(D_MODEL),
        out_shape=jax.ShapeDtypeStruct(x.shape, F32),
        input_output_aliases=aliases,
        compiler_params=_cparams("parallel"),
        name="combine",
    )(*args)


def _route(top_e, T):
    n_assign = T * TOP_K
    n_blocks = n_assign // MOE_BLOCK + N_EXP
    e_flat = top_e.reshape(n_assign)
    onehot = e_flat[:, None] == jnp.arange(N_EXP, dtype=jnp.int32)[None, :]
    counts = jnp.sum(onehot, axis=0, dtype=jnp.int32)
    start = jnp.cumsum(counts) - counts
    padded = (counts + MOE_BLOCK - 1) // MOE_BLOCK * MOE_BLOCK
    pad_end = jnp.cumsum(padded)
    pad_start = pad_end - padded
    iota = jnp.arange(n_assign, dtype=jnp.int32)
    _, order = lax.sort((e_flat, iota), num_keys=1, is_stable=True)
    _, rank = lax.sort((order, iota), num_keys=1)
    pos = rank + jnp.sum(jnp.where(onehot, (pad_start - start)[None, :], 0), axis=1)
    blk_first = jnp.arange(n_blocks, dtype=jnp.int32) * MOE_BLOCK
    blk_e = jnp.minimum(jnp.sum(pad_end[None, :] <= blk_first[:, None], axis=1), N_EXP - 1).astype(jnp.int32)
    r = blk_first[:, None] - pad_start[blk_e][:, None] + jnp.arange(MOE_BLOCK, dtype=jnp.int32)[None, :]
    valid = r < counts[blk_e][:, None]
    src = jnp.clip(start[blk_e][:, None] + r, 0, n_assign - 1)
    filler = (blk_first[:, None] + jnp.arange(MOE_BLOCK, dtype=jnp.int32)[None, :]) % T
    tok = jnp.where(valid, order[src] // TOP_K, filler).reshape(n_blocks * MOE_BLOCK).astype(jnp.int32)
    return tok, pos, blk_e


def _prep_layer(l, w_in, sgu_w, w_branch, w_out, w_gu, w_dn, gd_A_log, gd_dt_bias, lb):
    offs = np.cumsum([0, BRANCH_W, BRANCH_W, 512, 512, 512, 512, 512, GD_QKV, 8, 8, 512, 3 * NA_W, N_BRANCH * D_MODEL])
    w = w_in[l]
    seg = lambda i, j: w[:, offs[i]:offs[j]]
    w_gd = jnp.concatenate([seg(7, 8), seg(10, 11), seg(8, 10),
                            jnp.zeros((D_MODEL, LANES - 4 * GD_HEADS), F32)], axis=1)
    par = jnp.zeros((2, LANES), F32)
    par = par.at[0, :2 * GD_HEADS].set(gd_A_log[l].reshape(-1)).at[1, :2 * GD_HEADS].set(gd_dt_bias[l].reshape(-1))
    lb_l = lb[:, l]
    return {
        'w_sgu': seg(0, 2).astype(BF16), 'w_hg': seg(2, 7).astype(BF16), 'w_gd': w_gd.astype(BF16),
        'w_na': seg(11, 12).astype(BF16), 'w_mg': seg(12, 13).astype(BF16),
        'sgu_w': sgu_w[l], 'wb': w_branch[l].astype(BF16), 'wo': w_out[l].astype(BF16),
        'w_gu': w_gu, 'w_dn': w_dn, 'layer': l, 'gd_par': par,
        'lbp': jnp.concatenate([jnp.log(lb_l), jnp.log1p(-lb_l), 1.0 - lb_l], axis=0),
    }


def _layer(x, groups, mods, mod_row, lw, p, norm_f, final_norm):
    T = x.shape[0]
    h = _normmod(x, p['norm1'], mods, mod_row, 1024, part_shift=0, part_scale=1)
    z_sgu = _matmul(h, lw['w_sgu'], 1024, 1024)
    z_hg = _matmul(h, lw['w_hg'], 1024, 1280)
    z_gd = _matmul(h, lw['w_gd'], 512, 2176)
    z_na = _matmul(h, lw['w_na'], 1024, 768)
    z_mg = _matmul(h, lw['w_mg'], 1024, 1024, out_dtype=BF16)

    x_mix, states = None, []
    for row0, B, L, ctx in groups:
        Tg = B * L
        o_a = _sgu(z_sgu, p['sgu_norm'], lw['sgu_w'], p['sgu_b'], 256, row0, Tg)
        s_hg0 = None if ctx is None else ctx[2]
        s_gd0 = None if ctx is None else ctx[3]
        o_hf, o_hb, s_hg = _hgrn(z_hg, lw['lbp'], s_hg0, B, L, row0)
        qkv = _gdprep(z_gd, p['gd_conv'], L, 256, row0, Tg)
        o_gf, o_gb, s_gd = _gdn(qkv, z_gd, lw['gd_par'], s_gd0, B, L, row0)
        if ctx is None:
            o_d = _ctx_attn(z_na, B, L, row0)
        else:
            o_d = _na_attn(z_na, ctx[0], ctx[1], _na_bias(p['na_rpb'], L // GRID_W), B, L, row0)
        x_mix = _merge(x, o_a, o_hf, o_hb, z_hg, o_gf, o_gb, z_gd, o_d, z_mg, p['hg_onorm'], p['gd_onorm'],
                       lw['wb'], lw['wo'], mods, mod_row, 256, row0, Tg, x_mix)
        states.append((s_hg, s_gd))
    x = x_mix

    h2, top_e, wts = _router(x, p['norm2'], mods, mod_row, p['w_router'], p['b_router'], 512)
    th = T // MOE_SPLIT
    x_out = None
    for j in range(MOE_SPLIT):
        r0 = j * th
        tok, pos, blk_e = _route(top_e[r0:r0 + th, :TOP_K], th)
        xb = _sc_gather(h2, tok + r0)
        yb = _experts(xb, blk_e, lw['w_gu'], p['b_gu'], lw['w_dn'], p['b_dn'], lw['layer'])
        yg = _sc_gather(yb, pos.reshape(th, TOP_K).T.reshape(-1)).reshape(TOP_K, th, D_MODEL // 2)
        x_out = _combine(x, yg, wts, mods, mod_row, norm_f, final_norm, 256, r0, x_out)
    return x_out, z_na, states


def kernel(x_prompt, x_sample, c, cache_na_k, cache_na_v, state_hgrn, state_gdn, c_ctx, w_ada, b_ada, norm1, norm2, norm_f, w_in, sgu_norm, sgu_w, sgu_b, hg_lb, hg_onorm, gd_conv, gd_A_log, gd_dt_bias, gd_onorm, na_rpb, w_branch, w_out, w_router, b_router, w_gu, b_gu, w_dn, b_dn):
    Bp, Lp, D = x_prompt.shape
    Bs, Ls, _ = x_sample.shape
    ctx_row = Bs
    cvecs = jnp.zeros((MOD_ROWS, D), F32).at[:Bs].set(c).at[ctx_row].set(c_ctx)
    mods = _modulation(cvecs, w_ada, b_ada)

    cs = jnp.cumsum(jax.nn.softmax(hg_lb.astype(F32), axis=1), axis=1)
    lb = cs - cs[:, :1]

    Tp, Ts = Bp * Lp, Bs * Ls
    x = jnp.concatenate([x_prompt.reshape(Tp, D), x_sample.reshape(Ts, D)], axis=0)
    mod_row = lambda r: jnp.where(r < Tp, ctx_row, (r - Tp) // Ls)
    ks_, vs_, hs_, gs_ = [], [], [], []
    for l in range(DEPTH):
        lw = _prep_layer(l, w_in, sgu_w, w_branch, w_out, w_gu, w_dn, gd_A_log, gd_dt_bias, lb)
        p = {'norm1': norm1[l], 'norm2': norm2[l], 'sgu_norm': sgu_norm[l], 'sgu_b': sgu_b[l],
             'gd_conv': gd_conv[l], 'hg_onorm': hg_onorm[l], 'gd_onorm': gd_onorm[l], 'na_rpb': na_rpb[l],
             'w_router': w_router[l], 'b_router': b_router[l], 'b_gu': b_gu, 'b_dn': b_dn}
        ctx = (cache_na_k[:, l].reshape(Bs, -1, NA_W), cache_na_v[:, l].reshape(Bs, -1, NA_W),
               state_hgrn[:, l], state_gdn[:, l])
        groups = [(0, Bp, Lp, None), (Tp, Bs, Ls, ctx)]
        x, z_na, states = _layer(x, groups, mods[l], mod_row, lw, p, norm_f, l == DEPTH - 1)
        ks_.append(z_na[:Tp, NA_W:2 * NA_W].reshape(Bp, Lp, NA_HEADS, NA_DH))
        vs_.append(z_na[:Tp, 2 * NA_W:].reshape(Bp, Lp, NA_HEADS, NA_DH))
        hs_.append(states[0][0])
        gs_.append(states[0][1])

    return (x[:Tp].reshape(Bp, Lp, D), x[Tp:].reshape(Bs, Ls, D),
            jnp.stack(ks_, axis=1), jnp.stack(vs_, axis=1), jnp.stack(hs_, axis=1), jnp.stack(gs_, axis=1))
```

```python
import functools
import math

import numpy as np
import jax
import jax.numpy as jnp
from jax import lax
from jax.experimental import pallas as pl
from jax.experimental.pallas import tpu as pltpu
from jax.experimental.pallas import tpu_sc as plsc

D_MODEL = 1024
DEPTH = 2
GRID_W = 64
BRANCH_W = 512
N_BRANCH = 4
SGU_CHUNK = 128
SGU_GROUPS = 4
HG_HEADS = 4
HG_DK = 128
HG_DV = 128
GD_HEADS = 4
GD_DK = 128
GD_DV = 128
NA_HEADS = 8
NA_DH = 64
NA_KH = 8
NA_KW = 16
N_EXP = 32
TOP_K = 4
D_FF = 1024
SWIGLU_LIMIT = 7.0
SWIGLU_ALPHA = 1.702
EPS = 1e-6

F32 = jnp.float32
BF16 = jnp.bfloat16
HI = lax.Precision.HIGHEST

LANES = 128
MOD_ROWS = 16
SCAN_C = 64
SCAN_ROWS = 4 * SCAN_C
MOE_BLOCK = 512
MOE_SPLIT = 2
NA_QROWS = 4
NA_KROWS = NA_QROWS + NA_KH
VMEM_LIMIT = 48 * 1024 * 1024

NT = (((1,), (1,)), ((), ()))
TN = (((0,), (0,)), ((), ()))


def _cparams(*sem):
    return pltpu.CompilerParams(dimension_semantics=sem, vmem_limit_bytes=VMEM_LIMIT)


def _bdot(a, b):
    return jnp.dot(a.astype(BF16), b.astype(BF16), preferred_element_type=F32)


def _bdot_g(a, b, dims):
    return lax.dot_general(a.astype(BF16), b.astype(BF16), dims, preferred_element_type=F32)


def _hdot(a, b):
    return jnp.dot(a, b, precision=HI, preferred_element_type=F32)


def _dot01(m3, x):
    hi = x.astype(BF16)
    r1 = x - hi.astype(F32)
    mid = r1.astype(BF16)
    lo = (r1 - mid.astype(F32)).astype(BF16)
    return jnp.dot(m3, jnp.concatenate([hi, mid, lo], axis=0), preferred_element_type=F32)


def _dot3(a, b):
    ah = a.astype(BF16)
    al = (a - ah.astype(F32)).astype(BF16)
    bh = b.astype(BF16)
    bl = (b - bh.astype(F32)).astype(BF16)
    return jnp.dot(jnp.concatenate([al, ah, ah], axis=1), jnp.concatenate([bh, bl, bh], axis=0),
                   preferred_element_type=F32)


HI_HALF = 0xFFFF0000


def _pack_bf16_pairs(x):
    w = x.shape[1] // 2
    b = lax.bitcast_convert_type(x.astype(BF16).astype(F32), jnp.uint32)
    return lax.bitcast_convert_type((b[:, :w] >> 16) | (b[:, w:] & jnp.uint32(HI_HALF)), jnp.int32)


def _unpack_bf16_pairs(p):
    p = lax.bitcast_convert_type(p, jnp.uint32)
    lo = lax.bitcast_convert_type(p << 16, F32)
    hi = lax.bitcast_convert_type(p & jnp.uint32(HI_HALF), F32)
    return jnp.concatenate([lo, hi], axis=1)


def _silu(x):
    return x * jax.nn.sigmoid(x)


def _log_sigmoid(x):
    return jnp.minimum(x, 0.0) - jnp.log1p(jnp.exp(-jnp.abs(x)))


def _logaddexp(a, b):
    return jnp.maximum(a, b) + jnp.log1p(jnp.exp(-jnp.abs(a - b)))


def _softplus(x):
    return jnp.maximum(x, 0.0) + jnp.log1p(jnp.exp(-jnp.abs(x)))


def _ada_kernel(c_ref, w_ref, b_ref, o_ref):
    o_ref[0] = _hdot(_silu(c_ref[...]), w_ref[0]) + b_ref[0]


def _modulation(cvecs, w_ada, b_ada):
    tn = 1536
    out = pl.pallas_call(
        _ada_kernel,
        grid=(DEPTH, 6 * D_MODEL // tn),
        in_specs=[pl.BlockSpec((MOD_ROWS, D_MODEL), lambda l, j: (0, 0)),
                  pl.BlockSpec((1, D_MODEL, tn), lambda l, j: (l, 0, j)),
                  pl.BlockSpec((1, 1, tn), lambda l, j: (l, 0, j))],
        out_specs=pl.BlockSpec((1, MOD_ROWS, tn), lambda l, j: (l, 0, j)),
        out_shape=jax.ShapeDtypeStruct((DEPTH, MOD_ROWS, 6 * D_MODEL), F32),
        compiler_params=_cparams("arbitrary", "arbitrary"),
        name="ada_modulation",
    )(cvecs, w_ada, b_ada.reshape(DEPTH, 1, 6 * D_MODEL))
    return out.reshape(DEPTH, MOD_ROWS * 6, 1, D_MODEL)


def _rms(x):
    return x * lax.rsqrt(jnp.mean(x * x, axis=-1, keepdims=True) + EPS)


def _normmod_kernel(x_ref, g_ref, sc_ref, sh_ref, o_ref):
    h = (_rms(x_ref[...]) * g_ref[...]) * (1.0 + sc_ref[0]) + sh_ref[0]
    o_ref[...] = h.astype(o_ref.dtype)


def _normmod(x, g, mods, mod_row, tm, part_shift, part_scale):
    T = x.shape[0]
    return pl.pallas_call(
        _normmod_kernel,
        grid=(T // tm,),
        in_specs=[pl.BlockSpec((tm, D_MODEL), lambda i: (i, 0)),
                  pl.BlockSpec((1, D_MODEL), lambda i: (0, 0)),
                  pl.BlockSpec((1, 1, D_MODEL), lambda i: (mod_row(i * tm) * 6 + part_scale, 0, 0)),
                  pl.BlockSpec((1, 1, D_MODEL), lambda i: (mod_row(i * tm) * 6 + part_shift, 0, 0))],
        out_specs=pl.BlockSpec((tm, D_MODEL), lambda i: (i, 0)),
        out_shape=jax.ShapeDtypeStruct((T, D_MODEL), BF16),
        compiler_params=_cparams("parallel"),
        name="normmod",
    )(x, g.reshape(1, D_MODEL), mods, mods)


def _mm_kernel(a_ref, w_ref, o_ref):
    o_ref[...] = jnp.dot(a_ref[...], w_ref[...], preferred_element_type=F32).astype(o_ref.dtype)


def _matmul(a, w, tm, tn, out_dtype=F32):
    T, K = a.shape
    N = w.shape[1]
    return pl.pallas_call(
        _mm_kernel,
        grid=(N // tn, T // tm),
        in_specs=[pl.BlockSpec((tm, K), lambda j, i: (i, 0)),
                  pl.BlockSpec((K, tn), lambda j, i: (0, j))],
        out_specs=pl.BlockSpec((tm, tn), lambda j, i: (i, j)),
        out_shape=jax.ShapeDtypeStruct((T, N), out_dtype),
        compiler_params=_cparams("parallel", "parallel"),
        name="in_proj",
    )(a, w)


def _sgu_kernel(u_ref, v_ref, gn_ref, ws_ref, bs_ref, o_ref):
    rows = u_ref.shape[0]
    gw = BRANCH_W // SGU_GROUPS
    u = jax.nn.gelu(u_ref[...])
    v = (_rms(jax.nn.gelu(v_ref[...])) * gn_ref[...]).astype(BF16)
    for n in range(rows // SGU_CHUNK):
        r = slice(n * SGU_CHUNK, (n + 1) * SGU_CHUNK)
        for g in range(SGU_GROUPS):
            cs = slice(g * gw, (g + 1) * gw)
            s = jnp.dot(ws_ref[g], v[r, cs], preferred_element_type=F32) + bs_ref[:, cs]
            o_ref[r, cs] = (u[r, cs] * s).astype(o_ref.dtype)


def _sgu(z_sgu, g_norm, w_s, b_s, rows, row0, T):
    gw = BRANCH_W // SGU_GROUPS
    b_exp = jnp.repeat(b_s.T, gw, axis=1)
    i0 = row0 // rows
    return pl.pallas_call(
        _sgu_kernel,
        grid=(T // rows,),
        in_specs=[pl.BlockSpec((rows, BRANCH_W), lambda i: (i0 + i, 0)),
                  pl.BlockSpec((rows, BRANCH_W), lambda i: (i0 + i, 1)),
                  pl.BlockSpec((1, BRANCH_W), lambda i: (0, 0)),
                  pl.BlockSpec((SGU_GROUPS, SGU_CHUNK, SGU_CHUNK), lambda i: (0, 0, 0)),
                  pl.BlockSpec((SGU_CHUNK, BRANCH_W), lambda i: (0, 0))],
        out_specs=pl.BlockSpec((rows, BRANCH_W), lambda i: (i, 0)),
        out_shape=jax.ShapeDtypeStruct((T, BRANCH_W), BF16),
        compiler_params=_cparams("parallel"),
        name="sgu",
    )(z_sgu, z_sgu, g_norm.reshape(1, BRANCH_W), w_s.astype(BF16), b_exp)


def _order(reverse):
    p = np.arange(SCAN_C)
    return SCAN_C - 1 - p if reverse else p


def _gla_consts():
    C = SCAN_C
    nlev = int(math.log2(C))
    mats, masks = [], []
    for reverse in (False, True):
        p = _order(reverse)
        pt, pr = p[:, None], p[None, :]
        m_d, k_d = [], []
        for lev in range(nlev):
            w = C >> (lev + 1)
            parent = p // (2 * w)
            later = (p % (2 * w)) >= w
            anchor = (parent * 2 * w + w - 1)[:, None]
            m = np.where(later[:, None], (pr > anchor) & (pr <= pt), (pr > pt) & (pr <= anchor))
            m_d.append(m)
            k_d.append((parent[:, None] == parent[None, :]) & later[:, None] & ~later[None, :])
        m_d.append(pr <= pt)
        m_d.append(pr > pt)
        k_d.append(np.eye(C, dtype=bool))
        mats.append(np.concatenate(m_d, axis=0))
        masks.append(np.stack(k_d))
    return (np.stack(mats).astype(np.float32), np.stack(masks).astype(np.float32))


def _delta_consts():
    C = SCAN_C
    tri, sl, incl, strict = [], [], [], []
    for reverse in (False, True):
        p = _order(reverse)
        pt, pr = p[:, None], p[None, :]
        tri.append(np.concatenate([pr <= pt, pr > pt], axis=0))
        sl.append(np.concatenate([pt > pr, np.zeros((C, LANES - C), bool), np.ones((C, LANES), bool)], axis=1))
        incl.append(pr <= pt)
        strict.append(pr < pt)
    f = lambda a: np.stack(a).astype(np.float32)
    return f(tri), f(sl), f(incl), f(strict)


def _hgrn_kernel(qf_ref, ff_ref, vf_ref, qb_ref, fb_ref, vb_ref, lb_ref, mat_ref, msk_ref, *rest, has_state):
    if has_state:
        s0_ref, of_ref, ob_ref, sfin_ref, st_ref = rest
    else:
        of_ref, ob_ref, sfin_ref, st_ref = rest
    C = SCAN_C
    nlev = msk_ref.shape[1] - 1
    c = pl.program_id(1)
    last_c = pl.num_programs(1) - 1

    @pl.when(c == 0)
    def _():
        for d in range(2):
            for h in range(HG_HEADS):
                if has_state:
                    st_ref[d, h] = s0_ref[0, d, h].T
                else:
                    st_ref[d, h] = jnp.zeros((HG_DV, HG_DK), F32)

    q_refs, f_refs, v_refs, o_refs = (qf_ref, qb_ref), (ff_ref, fb_ref), (vf_ref, vb_ref), (of_ref, ob_ref)
    sub = qf_ref.shape[0] // C
    rows = lambda j: slice(j * C, (j + 1) * C)
    col = lambda h: slice(h * HG_DK, (h + 1) * HG_DK)
    parts = [(d, j) for d in range(2) for j in range(sub)]
    q, k, fac, att = {}, {}, {}, {}
    for d, j in parts:
        zf = f_refs[d][rows(j), :]
        logf = _logaddexp(lb_ref[d:d + 1, :], lb_ref[2 + d:3 + d, :] + _log_sigmoid(zf))
        k[d, j] = lb_ref[4 + d:5 + d, :] * jax.nn.sigmoid(-zf)
        q[d, j] = _silu(q_refs[d][rows(j), :]) * (HG_DK ** -0.5)
        fac[d, j] = jnp.exp(_dot01(mat_ref[d], logf))
    for d, j in parts:
        for h in range(HG_HEADS):
            qh, kh = q[d, j][:, col(h)], k[d, j][:, col(h)]
            acc = msk_ref[d, nlev] * _bdot_g(qh, kh, NT)
            for i in range(nlev):
                fi = fac[d, j][i * C:(i + 1) * C, col(h)]
                acc = acc + msk_ref[d, i] * _bdot_g(qh * fi, kh * fi, NT)
            att[d, j, h] = acc
    chains = [(d, h) for d in range(2) for h in range(HG_HEADS)]
    for t in range(sub):
        jd = (t, sub - 1 - t)
        for d, h in chains:
            j = jd[d]
            eb = fac[d, j][nlev * C:(nlev + 1) * C, col(h)]
            o_refs[d][rows(j), col(h)] = (_bdot(att[d, j, h], v_refs[d][rows(j), col(h)])
                                          + _bdot_g(q[d, j][:, col(h)] * eb, st_ref[d, h], NT))
        for d, h in chains:
            j = jd[d]
            eb = fac[d, j][nlev * C:(nlev + 1) * C, col(h)]
            er = fac[d, j][(nlev + 1) * C:, col(h)]
            e_last = eb[C - 1:C] if d == 0 else eb[0:1]
            st_ref[d, h] = st_ref[d, h] * e_last + _bdot_g(v_refs[d][rows(j), col(h)], k[d, j][:, col(h)] * er, TN)

    @pl.when(c == last_c)
    def _():
        for d in range(2):
            for h in range(HG_HEADS):
                sfin_ref[0, d, h] = st_ref[d, h].T


def _hgrn(z_hg, lbp, s0, B, L, row0):
    n = L // SCAN_ROWS
    c0 = row0 // SCAN_ROWS
    mats, masks = _gla_consts()
    mats = np.tile(mats, (1, 1, 3))
    blk = (SCAN_ROWS, HG_HEADS * HG_DK)
    fwd = lambda col: pl.BlockSpec(blk, lambda b, c: (c0 + b * n + c, col))
    bwd = lambda col: pl.BlockSpec(blk, lambda b, c: (c0 + b * n + n - 1 - c, col))
    st_blk = (1, 2, HG_HEADS, HG_DK, HG_DV)
    in_specs = [fwd(0), fwd(1), fwd(3), bwd(0), bwd(2), bwd(3),
                pl.BlockSpec(lbp.shape, lambda b, c: (0, 0)),
                pl.BlockSpec(mats.shape, lambda b, c: (0, 0, 0)),
                pl.BlockSpec(masks.shape, lambda b, c: (0, 0, 0, 0))]
    args = [z_hg] * 6 + [lbp, jnp.asarray(mats, BF16), jnp.asarray(masks)]
    if s0 is not None:
        in_specs.append(pl.BlockSpec(st_blk, lambda b, c: (b, 0, 0, 0, 0)))
        args.append(s0)
    return pl.pallas_call(
        functools.partial(_hgrn_kernel, has_state=s0 is not None),
        grid=(B, n),
        in_specs=in_specs,
        out_specs=[pl.BlockSpec(blk, lambda b, c: (b * n + c, 0)),
                   pl.BlockSpec(blk, lambda b, c: (b * n + n - 1 - c, 0)),
                   pl.BlockSpec(st_blk, lambda b, c: (b, 0, 0, 0, 0))],
        out_shape=[jax.ShapeDtypeStruct((B * L, HG_HEADS * HG_DV), F32),
                   jax.ShapeDtypeStruct((B * L, HG_HEADS * HG_DV), F32),
                   jax.ShapeDtypeStruct((B,) + st_blk[1:], F32)],
        scratch_shapes=[pltpu.VMEM((2, HG_HEADS, HG_DV, HG_DK), F32)],
        compiler_params=_cparams("parallel", "arbitrary"),
        name="hgrn_scan",
    )(*args)


GD_NQ = GD_HEADS * GD_DK
GD_QKV = 2 * GD_NQ + GD_HEADS * GD_DV
HALO = 8


def _gdprep_kernel(x_ref, prev_ref, next_ref, w_ref, o_ref, *, tiles_per_seq):
    R = x_ref.shape[0]
    t = pl.program_id(0) % tiles_per_seq
    x = x_ref[...]
    prev_row = jnp.where(t == 0, 0.0, prev_ref[HALO - 1:HALO, :])
    next_row = jnp.where(t == tiles_per_seq - 1, 0.0, next_ref[0:1, :])
    row = lax.broadcasted_iota(jnp.int32, x.shape, 0)
    xm1 = jnp.where(row == 0, prev_row, pltpu.roll(x, 1, 0))
    xp1 = jnp.where(row == R - 1, next_row, pltpu.roll(x, R - 1, 0))
    y = _silu(w_ref[0:1, :] * xm1 + w_ref[1:2, :] * x + w_ref[2:3, :] * xp1)
    for j in range(2 * GD_HEADS):
        cs = slice(j * GD_DK, (j + 1) * GD_DK)
        seg = y[:, cs]
        seg = seg * lax.rsqrt(jnp.sum(seg * seg, axis=-1, keepdims=True) + EPS)
        if j < GD_HEADS:
            seg = seg * (GD_DK ** -0.5)
        o_ref[:, cs] = seg
    o_ref[:, 2 * GD_NQ:] = y[:, 2 * GD_NQ:]


def _gdprep(z_gd, conv_w, L, rows, row0, T):
    tps = L // rows
    hb = rows // HALO
    nhalo = z_gd.shape[0] // HALO
    i0 = row0 // rows
    return pl.pallas_call(
        functools.partial(_gdprep_kernel, tiles_per_seq=tps),
        grid=(T // rows,),
        in_specs=[pl.BlockSpec((rows, GD_QKV), lambda i: (i0 + i, 0)),
                  pl.BlockSpec((HALO, GD_QKV), lambda i: (jnp.maximum((i0 + i) * hb - 1, 0), 0)),
                  pl.BlockSpec((HALO, GD_QKV), lambda i: (jnp.minimum((i0 + i + 1) * hb, nhalo - 1), 0)),
                  pl.BlockSpec((3, GD_QKV), lambda i: (0, 0))],
        out_specs=pl.BlockSpec((rows, GD_QKV), lambda i: (i, 0)),
        out_shape=jax.ShapeDtypeStruct((T, GD_QKV), F32),
        compiler_params=_cparams("parallel"),
        name="gdn_prep",
    )(z_gd, z_gd, z_gd, conv_w)


def _gdn_kernel(xf_ref, abf_ref, xb_ref, abb_ref, par_ref, tri_ref, sl_ref, incl_ref, strict_ref, msk_ref, *rest,
                has_state):
    if has_state:
        s0_ref, of_ref, ob_ref, sfin_ref, st_ref = rest
    else:
        of_ref, ob_ref, sfin_ref, st_ref = rest
    C = SCAN_C
    c = pl.program_id(1)
    last_c = pl.num_programs(1) - 1
    nlev = msk_ref.shape[1] - 1

    @pl.when(c == 0)
    def _():
        for d in range(2):
            for h in range(GD_HEADS):
                if has_state:
                    st_ref[d, h] = s0_ref[0, d, h].T
                else:
                    st_ref[d, h] = jnp.zeros((GD_DV, GD_DK), F32)

    x_refs, ab_refs, o_refs = (xf_ref, xb_ref), (abf_ref, abb_ref), (of_ref, ob_ref)
    sub = xf_ref.shape[0] // C
    rows = lambda j: slice(j * C, (j + 1) * C)
    chains = [(d, h) for d in range(2) for h in range(GD_HEADS)]
    parts = [(d, j, h) for d in range(2) for j in range(sub) for h in range(GD_HEADS)]
    q_of = lambda d, j, h: x_refs[d][rows(j), h * GD_DK:(h + 1) * GD_DK]
    k_of = lambda d, j, h: x_refs[d][rows(j), GD_NQ + h * GD_DK:GD_NQ + (h + 1) * GD_DK]
    v_of = lambda d, j, h: x_refs[d][rows(j), 2 * GD_NQ + h * GD_DV:2 * GD_NQ + (h + 1) * GD_DV]
    g_all, beta_all = [], []
    for d in range(2):
        ab = ab_refs[d][...]
        g_all.append(-jnp.exp(par_ref[0:1, :]) * _softplus(ab + par_ref[1:2, :]))
        beta_all.append(jax.nn.sigmoid(ab))

    def beta_of(d, j, h):
        lane = 2 * GD_HEADS + d * GD_HEADS + h
        return jnp.broadcast_to(beta_all[d][rows(j), lane:lane + 1], (C, LANES))

    decay, e_cum, e_rest, kb, a = {}, {}, {}, {}, {}
    for d, j, h in parts:
        lane = d * GD_HEADS + h
        g_b = jnp.broadcast_to(g_all[d][rows(j), lane:lane + 1], (C, 2 * LANES))
        sums = _dot01(tri_ref[d], g_b * sl_ref[d])
        decay[d, j, h] = jnp.exp(sums[:C, :C])
        e_cum[d, j, h] = jnp.exp(sums[:C, LANES:])
        e_rest[d, j, h] = jnp.exp(sums[C:, LANES:])
    for d, j, h in parts:
        k = k_of(d, j, h)
        kb[d, j, h] = k * beta_of(d, j, h)
        a[d, j, h] = strict_ref[d] * decay[d, j, h] * _bdot_g(kb[d, j, h], k, NT)
    inv_m = {pt: -(msk_ref[pt[0], nlev - 1] * a[pt]) for pt in parts}
    for lev in range(nlev - 2, -1, -1):
        a_w = {pt: msk_ref[pt[0], lev] * a[pt] for pt in parts}
        p = {pt: a_w[pt] + _bdot(inv_m[pt], a_w[pt]) for pt in parts}
        inv_m = {pt: inv_m[pt] - p[pt] - _bdot(p[pt], inv_m[pt]) for pt in parts}
    sol, att = {}, {}
    for d, j, h in parts:
        rhs = jnp.concatenate([v_of(d, j, h) * beta_of(d, j, h), kb[d, j, h] * e_cum[d, j, h]], axis=1)
        sol[d, j, h] = rhs + _dot3(inv_m[d, j, h], rhs)
        att[d, j, h] = incl_ref[d] * decay[d, j, h] * _bdot_g(q_of(d, j, h), k_of(d, j, h), NT)
    for t in range(sub):
        jd = (t, sub - 1 - t)
        u = {}
        for d, h in chains:
            pt = (d, jd[d], h)
            u[d, h] = sol[pt][:, :GD_DV] - _bdot_g(sol[pt][:, GD_DV:], st_ref[d, h], NT)
        for d, h in chains:
            pt = (d, jd[d], h)
            o_refs[d][rows(jd[d]), h * GD_DV:(h + 1) * GD_DV] = (
                _bdot_g(q_of(*pt) * e_cum[pt], st_ref[d, h], NT) + _bdot(att[pt], u[d, h]))
        for d, h in chains:
            pt = (d, jd[d], h)
            e_last = e_cum[pt][C - 1:C] if d == 0 else e_cum[pt][0:1]
            st_ref[d, h] = st_ref[d, h] * e_last + _bdot_g(u[d, h], k_of(*pt) * e_rest[pt], TN)

    @pl.when(c == last_c)
    def _():
        for d in range(2):
            for h in range(GD_HEADS):
                sfin_ref[0, d, h] = st_ref[d, h].T


def _gdn(qkv, z_gd, par, s0, B, L, row0):
    n = L // SCAN_ROWS
    c0 = row0 // SCAN_ROWS
    tri, sl, incl, strict = (jnp.asarray(a) for a in _delta_consts())
    tri = jnp.tile(tri, (1, 1, 3)).astype(BF16)
    masks = jnp.asarray(_gla_consts()[1])
    ab_col = (GD_QKV + GD_HEADS * GD_DV) // LANES
    xblk = (SCAN_ROWS, GD_QKV)
    ablk = (SCAN_ROWS, LANES)
    oblk = (SCAN_ROWS, GD_HEADS * GD_DV)
    st_blk = (1, 2, GD_HEADS, GD_DK, GD_DV)
    fwd = lambda b, c: b * n + c
    bwd = lambda b, c: b * n + n - 1 - c
    const3 = lambda a: pl.BlockSpec(a.shape, lambda b, c: (0, 0, 0))
    in_specs = [pl.BlockSpec(xblk, lambda b, c: (fwd(b, c), 0)),
                pl.BlockSpec(ablk, lambda b, c: (c0 + fwd(b, c), ab_col)),
                pl.BlockSpec(xblk, lambda b, c: (bwd(b, c), 0)),
                pl.BlockSpec(ablk, lambda b, c: (c0 + bwd(b, c), ab_col)),
                pl.BlockSpec(par.shape, lambda b, c: (0, 0)),
                const3(tri), const3(sl), const3(incl), const3(strict),
                pl.BlockSpec(masks.shape, lambda b, c: (0, 0, 0, 0))]
    args = [qkv, z_gd, qkv, z_gd, par, tri, sl, incl, strict, masks]
    if s0 is not None:
        in_specs.append(pl.BlockSpec(st_blk, lambda b, c: (b, 0, 0, 0, 0)))
        args.append(s0)
    return pl.pallas_call(
        functools.partial(_gdn_kernel, has_state=s0 is not None),
        grid=(B, n),
        in_specs=in_specs,
        out_specs=[pl.BlockSpec(oblk, lambda b, c: (fwd(b, c), 0)),
                   pl.BlockSpec(oblk, lambda b, c: (bwd(b, c), 0)),
                   pl.BlockSpec(st_blk, lambda b, c: (b, 0, 0, 0, 0))],
        out_shape=[jax.ShapeDtypeStruct((B * L, GD_HEADS * GD_DV), F32),
                   jax.ShapeDtypeStruct((B * L, GD_HEADS * GD_DV), F32),
                   jax.ShapeDtypeStruct((B,) + st_blk[1:], F32)],
        scratch_shapes=[pltpu.VMEM((2, GD_HEADS, GD_DV, GD_DK), F32)],
        compiler_params=_cparams("parallel", "arbitrary"),
        name="gdn_scan",
    )(*args)


NA_W = NA_HEADS * NA_DH


def _softmax_pv(s, v):
    m = jnp.max(s, axis=-1, keepdims=True)
    e = jnp.exp(s - m)
    den = jnp.sum(e, axis=-1, keepdims=True)
    return jnp.dot(e.astype(BF16), v, preferred_element_type=F32) / den


def _ctx_attn_kernel(q_ref, k_ref, v_ref, o_ref):
    for h in range(NA_HEADS):
        cs = slice(h * NA_DH, (h + 1) * NA_DH)
        q = (q_ref[:, cs] * (NA_DH ** -0.5)).astype(BF16)
        s = lax.dot_general(q, k_ref[:, cs].astype(BF16), NT, preferred_element_type=F32)
        o_ref[:, cs] = _softmax_pv(s, v_ref[:, cs].astype(BF16)).astype(o_ref.dtype)


def _ctx_attn(z_na, B, L, row0):
    blk = (L, NA_W)
    b0 = row0 // L
    return pl.pallas_call(
        _ctx_attn_kernel,
        grid=(B,),
        in_specs=[pl.BlockSpec(blk, lambda b: (b0 + b, 0)),
                  pl.BlockSpec(blk, lambda b: (b0 + b, 1)),
                  pl.BlockSpec(blk, lambda b: (b0 + b, 2))],
        out_specs=pl.BlockSpec(blk, lambda b: (b, 0)),
        out_shape=jax.ShapeDtypeStruct((B * L, NA_W), BF16),
        compiler_params=_cparams("parallel"),
        name="ctx_attn",
    )(z_na, z_na, z_na)


def _na_kernel(q_ref, *rest, n_kblk, nkeys_nb):
    k_refs = rest[:n_kblk]
    v_refs = rest[n_kblk:2 * n_kblk]
    kc_ref, vc_ref, bias_ref, o_ref, kbuf, vbuf = rest[2 * n_kblk:]
    qb = q_ref.shape[0]
    for i in range(n_kblk):
        kbuf[i * qb:(i + 1) * qb, :] = k_refs[i][...].astype(BF16)
        vbuf[i * qb:(i + 1) * qb, :] = v_refs[i][...].astype(BF16)
    kbuf[nkeys_nb:, :] = kc_ref[0].astype(BF16)
    vbuf[nkeys_nb:, :] = vc_ref[0].astype(BF16)
    for h in range(NA_HEADS):
        cs = slice(h * NA_DH, (h + 1) * NA_DH)
        q = (q_ref[:, cs] * (NA_DH ** -0.5)).astype(BF16)
        s = lax.dot_general(q, kbuf[:, cs], NT, preferred_element_type=F32)
        s_nb = s[:, :nkeys_nb] + bias_ref[0, h]
        s_cx = s[:, nkeys_nb:]
        m = jnp.maximum(jnp.max(s_nb, axis=-1, keepdims=True), jnp.max(s_cx, axis=-1, keepdims=True))
        e_nb = jnp.exp(s_nb - m)
        e_cx = jnp.exp(s_cx - m)
        den = jnp.sum(e_nb, axis=-1, keepdims=True) + jnp.sum(e_cx, axis=-1, keepdims=True)
        pv = (jnp.dot(e_nb.astype(BF16), vbuf[:nkeys_nb, cs], preferred_element_type=F32)
              + jnp.dot(e_cx.astype(BF16), vbuf[nkeys_nb:, cs], preferred_element_type=F32))
        o_ref[:, cs] = (pv / den).astype(o_ref.dtype)


def _na_bias(rpb, rows):
    qr, kr, col = np.arange(NA_QROWS), np.arange(NA_KROWS), np.arange(GRID_W)
    nblk = rows // NA_QROWS
    ndr, ndc = 2 * NA_KH - 1, 2 * NA_KW - 1
    sel_r, row_ok = [], []
    for m in (0, 1, nblk - 1):
        r = (NA_QROWS * m + qr)[:, None]
        start = np.clip(NA_QROWS * m - NA_KH // 2, 0, rows - NA_KROWS)
        kra = (start + kr)[None, :]
        r0 = np.clip(r - NA_KH // 2, 0, rows - NA_KH)
        row_ok.append((kra >= r0) & (kra < r0 + NA_KH))
        dr = np.clip(kra - r + NA_KH - 1, 0, ndr - 1)
        sel_r.append(dr[..., None] == np.arange(ndr))
    sel_r = np.stack(sel_r).astype(np.float32)
    row_ok = np.stack(row_ok)
    col_start = np.clip(col - NA_KW // 2, 0, GRID_W - NA_KW)[:, None]
    col_ok = (col[None, :] >= col_start) & (col[None, :] < col_start + NA_KW)
    dc = np.clip(col[None, :] - col[:, None], -(NA_KW - 1), NA_KW - 1) + NA_KW - 1
    sel_c = (dc[..., None] == np.arange(ndc)).astype(np.float32)
    bias = jnp.einsum('hab,vqka,xyb->vhqxky', rpb.astype(F32), sel_r, sel_c, precision=HI)
    ok = row_ok[:, None, :, None, :, None] & col_ok[None, None, None, :, None, :]
    bias = jnp.where(ok, bias, -jnp.inf)
    return bias.reshape(3, NA_HEADS, NA_QROWS * GRID_W, NA_KROWS * GRID_W)


def _na_attn(z_na, k_ctx, v_ctx, bias, B, S, row0):
    rows = S // GRID_W
    qb = NA_QROWS * GRID_W
    m0 = row0 // qb
    nblk = rows // NA_QROWS
    n_kblk = NA_KROWS // NA_QROWS
    lc = k_ctx.shape[1]
    nkeys_nb = NA_KROWS * GRID_W
    kstart = lambda m: jnp.clip(m - 1, 0, nblk - n_kblk)
    variant = lambda m: jnp.where(m == 0, 0, jnp.where(m == nblk - 1, 2, 1))
    kv_specs = lambda col: [pl.BlockSpec((qb, NA_W), functools.partial(
        lambda b, m, i, col: (m0 + b * nblk + kstart(m) + i, col), i=i, col=col)) for i in range(n_kblk)]
    return pl.pallas_call(
        functools.partial(_na_kernel, n_kblk=n_kblk, nkeys_nb=nkeys_nb),
        grid=(B, nblk),
        in_specs=([pl.BlockSpec((qb, NA_W), lambda b, m: (m0 + b * nblk + m, 0))] + kv_specs(1) + kv_specs(2)
                  + [pl.BlockSpec((1, lc, NA_W), lambda b, m: (b, 0, 0)),
                     pl.BlockSpec((1, lc, NA_W), lambda b, m: (b, 0, 0)),
                     pl.BlockSpec((1,) + bias.shape[1:], lambda b, m: (variant(m), 0, 0, 0))]),
        out_specs=pl.BlockSpec((qb, NA_W), lambda b, m: (b * nblk + m, 0)),
        out_shape=jax.ShapeDtypeStruct((B * S, NA_W), BF16),
        scratch_shapes=[pltpu.VMEM((nkeys_nb + lc, NA_W), BF16), pltpu.VMEM((nkeys_nb + lc, NA_W), BF16)],
        compiler_params=_cparams("parallel", "arbitrary"),
        name="na_attn",
    )(z_na, *([z_na] * (2 * n_kblk)), k_ctx, v_ctx, bias)


def _head_rms(o, g_row):
    parts = []
    for h in range(o.shape[1] // LANES):
        parts.append(_rms(o[:, h * LANES:(h + 1) * LANES]) * g_row)
    return jnp.concatenate(parts, axis=1)


def _merge_kernel(x_ref, oa_ref, hf_ref, hb_ref, hg_ref, gf_ref, gb_ref, gg_ref, od_ref, mg_ref,
                  hn_ref, gn_ref, wb_ref, wo_ref, g1_ref, *rest):
    o_ref = rest[-1]
    o_b = _head_rms(hf_ref[...] + hb_ref[...], hn_ref[...]) * _silu(hg_ref[...])
    o_c = _head_rms(gf_ref[...] + gb_ref[...], gn_ref[...]) * _silu(gg_ref[...])
    branches = (oa_ref[...], o_b.astype(BF16), o_c.astype(BF16), od_ref[...])
    merged = None
    for n_, o_n in enumerate(branches):
        gate = jax.nn.sigmoid(mg_ref[:, n_ * D_MODEL:(n_ + 1) * D_MODEL].astype(F32))
        term = gate * jnp.dot(o_n, wb_ref[n_], preferred_element_type=F32)
        merged = term if merged is None else merged + term
    mix = jnp.dot(merged.astype(BF16), wo_ref[...], preferred_element_type=F32)
    o_ref[...] = x_ref[...] + g1_ref[0] * mix


def _merge(x, o_a, o_hf, o_hb, z_hg, o_gf, o_gb, z_gd, o_d, z_mg, hg_onorm, gd_onorm, wb, wo, mods, mod_row, tm,
           row0, T, x_acc):
    i0 = row0 // tm
    glob = lambda w, col=0: pl.BlockSpec((tm, w), lambda i: (i0 + i, col))
    loc = lambda w: pl.BlockSpec((tm, w), lambda i: (i, 0))
    full = lambda a: pl.BlockSpec(a.shape, lambda i: (0,) * a.ndim)
    hn = hg_onorm.reshape(1, HG_DV)
    gn = gd_onorm.reshape(1, GD_DV)
    in_specs = [glob(D_MODEL), loc(BRANCH_W), loc(BRANCH_W), loc(BRANCH_W), glob(BRANCH_W, 4),
                loc(BRANCH_W), loc(BRANCH_W), glob(BRANCH_W, GD_QKV // BRANCH_W), loc(BRANCH_W),
                glob(N_BRANCH * D_MODEL), full(hn), full(gn), full(wb), full(wo),
                pl.BlockSpec((1, 1, D_MODEL), lambda i: (mod_row(row0 + i * tm) * 6 + 2, 0, 0))]
    args = [x, o_a, o_hf, o_hb, z_hg, o_gf, o_gb, z_gd, o_d, z_mg, hn, gn, wb, wo, mods]
    aliases = {}
    if x_acc is not None:
        in_specs.append(pl.BlockSpec(memory_space=pl.ANY))
        aliases = {len(args): 0}
        args.append(x_acc)
    return pl.pallas_call(
        _merge_kernel,
        grid=(T // tm,),
        in_specs=in_specs,
        out_specs=glob(D_MODEL),
        out_shape=jax.ShapeDtypeStruct(x.shape, F32),
        input_output_aliases=aliases,
        compiler_params=_cparams("parallel"),
        name="merge",
    )(*args)


SC_CORES = 2
SC_SUBCORES = 16
SC_WIN = 64


def _sc_gather(table, idx):
    V, D = table.shape
    N = idx.shape[0]
    nw = SC_CORES * SC_SUBCORES
    per_w = N // nw
    n_win = per_w // SC_WIN
    assert per_w * nw == N and n_win * SC_WIN == per_w
    mesh = plsc.VectorSubcoreMesh(core_axis_name="c", subcore_axis_name="s")

    @functools.partial(
        pl.kernel, mesh=mesh,
        out_type=jax.ShapeDtypeStruct((N, D), table.dtype),
        scratch_types=[pltpu.VMEM((n_win, SC_WIN), jnp.int32),
                       pltpu.VMEM((SC_WIN, D), table.dtype),
                       pltpu.SemaphoreType.DMA],
    )
    def gather_rows(table_hbm, idx_hbm, out_hbm, idx_v, rows_v, sem):
        wid = lax.axis_index("s") * SC_CORES + lax.axis_index("c")
        pltpu.sync_copy(idx_hbm.at[wid], idx_v)

        @pl.loop(0, n_win)
        def _(w):
            pltpu.async_copy(table_hbm.at[idx_v.at[w]], rows_v, sem).wait()
            pltpu.sync_copy(rows_v, out_hbm.at[pl.ds(wid * per_w + w * SC_WIN, SC_WIN)])

    return gather_rows(table, idx.reshape(nw, n_win, SC_WIN))


def _router_kernel(x_ref, g_ref, sc_ref, sh_ref, wr_ref, br_ref, h_ref, e_ref, w_ref):
    h = (_rms(x_ref[...]) * g_ref[...]) * (1.0 + sc_ref[0]) + sh_ref[0]
    h_ref[...] = _pack_bf16_pairs(h)
    logits = _hdot(h, wr_ref[...]) + br_ref[...]
    lane = lax.broadcasted_iota(jnp.int32, logits.shape, 1)
    e_out = jnp.zeros(logits.shape, jnp.int32)
    v_out = jnp.zeros(logits.shape, F32)
    top0 = None
    for k in range(TOP_K):
        m = jnp.max(logits, axis=-1, keepdims=True)
        idx = jnp.min(jnp.where(logits == m, lane, LANES), axis=-1, keepdims=True)
        if k == 0:
            top0 = m
        e_out = jnp.where(lane == k, idx, e_out)
        v_out = jnp.where(lane == k, jnp.exp(m - top0), v_out)
        logits = jnp.where(lane == idx, -jnp.inf, logits)
    e_ref[...] = e_out
    w_ref[...] = v_out / jnp.sum(v_out, axis=-1, keepdims=True)


def _router(x, g, mods, mod_row, w_router, b_router, tm):
    T = x.shape[0]
    wr = jnp.zeros((D_MODEL, LANES), F32).at[:, :N_EXP].set(w_router)
    br = jnp.full((1, LANES), -jnp.inf, F32).at[0, :N_EXP].set(b_router)
    row = lambda w: pl.BlockSpec((tm, w), lambda i: (i, 0))
    return pl.pallas_call(
        _router_kernel,
        grid=(T // tm,),
        in_specs=[row(D_MODEL),
                  pl.BlockSpec((1, D_MODEL), lambda i: (0, 0)),
                  pl.BlockSpec((1, 1, D_MODEL), lambda i: (mod_row(i * tm) * 6 + 4, 0, 0)),
                  pl.BlockSpec((1, 1, D_MODEL), lambda i: (mod_row(i * tm) * 6 + 3, 0, 0)),
                  pl.BlockSpec((D_MODEL, LANES), lambda i: (0, 0)),
                  pl.BlockSpec((1, LANES), lambda i: (0, 0))],
        out_specs=[row(D_MODEL // 2), row(LANES), row(LANES)],
        out_shape=[jax.ShapeDtypeStruct((T, D_MODEL // 2), jnp.int32),
                   jax.ShapeDtypeStruct((T, LANES), jnp.int32),
                   jax.ShapeDtypeStruct((T, LANES), F32)],
        compiler_params=_cparams("parallel"),
        name="router",
    )(x, g.reshape(1, D_MODEL), mods, mods, wr, br)


def _expert_kernel(blk_e_ref, x_ref, wgu_ref, bgu_ref, wdn_ref, bdn_ref, o_ref, wgu_bf, wdn_bf):
    i = pl.program_id(0)
    new_expert = jnp.logical_or(i == 0, blk_e_ref[i] != blk_e_ref[jnp.maximum(i - 1, 0)])

    @pl.when(new_expert)
    def _():
        wgu_bf[...] = wgu_ref[0].astype(BF16)
        wdn_bf[...] = wdn_ref[0].astype(BF16)

    x = _unpack_bf16_pairs(x_ref[...]).astype(BF16)
    gu = jnp.dot(x, wgu_bf[...], preferred_element_type=F32) + bgu_ref[0]
    a = jnp.minimum(gu[:, :D_FF], SWIGLU_LIMIT)
    lin = jnp.clip(gu[:, D_FF:], -SWIGLU_LIMIT, SWIGLU_LIMIT)
    y = a * jax.nn.sigmoid(SWIGLU_ALPHA * a) * (lin + 1.0)
    o_ref[...] = _pack_bf16_pairs(jnp.dot(y.astype(BF16), wdn_bf[...], preferred_element_type=F32) + bdn_ref[0])


def _experts(xb, blk_e, w_gu, b_gu, w_dn, b_dn, layer):
    n_pad = xb.shape[0]
    n_blocks = n_pad // MOE_BLOCK
    e0 = layer * N_EXP
    w_gu = w_gu.reshape(DEPTH * N_EXP, D_MODEL, 2 * D_FF)
    w_dn = w_dn.reshape(DEPTH * N_EXP, D_FF, D_MODEL)
    grid_spec = pltpu.PrefetchScalarGridSpec(
        num_scalar_prefetch=1,
        grid=(n_blocks,),
        in_specs=[pl.BlockSpec((MOE_BLOCK, D_MODEL // 2), lambda i, e: (i, 0)),
                  pl.BlockSpec((1, D_MODEL, 2 * D_FF), lambda i, e: (e0 + e[i], 0, 0)),
                  pl.BlockSpec((1, 1, 2 * D_FF), lambda i, e: (e0 + e[i], 0, 0)),
                  pl.BlockSpec((1, D_FF, D_MODEL), lambda i, e: (e0 + e[i], 0, 0)),
                  pl.BlockSpec((1, 1, D_MODEL), lambda i, e: (e0 + e[i], 0, 0))],
        out_specs=pl.BlockSpec((MOE_BLOCK, D_MODEL // 2), lambda i, e: (i, 0)),
        scratch_shapes=[pltpu.VMEM((D_MODEL, 2 * D_FF), BF16), pltpu.VMEM((D_FF, D_MODEL), BF16)],
    )
    return pl.pallas_call(
        _expert_kernel,
        grid_spec=grid_spec,
        out_shape=jax.ShapeDtypeStruct((n_pad, D_MODEL // 2), jnp.int32),
        compiler_params=_cparams("arbitrary"),
        name="experts",
    )(blk_e, xb, w_gu, b_gu.reshape(DEPTH * N_EXP, 1, 2 * D_FF), w_dn, b_dn.reshape(DEPTH * N_EXP, 1, D_MODEL))


def _combine_kernel(x_ref, y_ref, w_ref, g2_ref, nf_ref, *rest, final_norm):
    o_ref = rest[-1]
    acc = None
    for k in range(TOP_K):
        term = _unpack_bf16_pairs(y_ref[k]) * w_ref[:, k:k + 1]
        acc = term if acc is None else acc + term
    x = x_ref[...] + g2_ref[0] * acc
    if final_norm:
        x = _rms(x) * nf_ref[...]
    o_ref[...] = x


def _combine(x, yg, wts, mods, mod_row, norm_f, final_norm, tm, row0, x_acc):
    T = yg.shape[1]
    i0 = row0 // tm
    glob = lambda w: pl.BlockSpec((tm, w), lambda i: (i0 + i, 0))
    in_specs = [glob(D_MODEL), pl.BlockSpec((TOP_K, tm, D_MODEL // 2), lambda i: (0, i, 0)), glob(LANES),
                pl.BlockSpec((1, 1, D_MODEL), lambda i: (mod_row(row0 + i * tm) * 6 + 5, 0, 0)),
                pl.BlockSpec((1, D_MODEL), lambda i: (0, 0))]
    args = [x, yg, wts, mods, norm_f.reshape(1, D_MODEL)]
    aliases = {}
    if x_acc is not None:
        in_specs.append(pl.BlockSpec(memory_space=pl.ANY))
        aliases = {len(args): 0}
        args.append(x_acc)
    return pl.pallas_call(
        functools.partial(_combine_kernel, final_norm=final_norm),
        grid=(T // tm,),
        in_specs=in_specs,
        out_specs=glob(D_MODEL),
        out_shape=jax.ShapeDtypeStruct(x.shape, F32),
        input_output_aliases=aliases,
        compiler_params=_cparams("parallel"),
        name="combine",
    )(*args)


def _route(top_e, T):
    n_assign = T * TOP_K
    n_blocks = n_assign // MOE_BLOCK + N_EXP
    e_flat = top_e.reshape(n_assign)
    onehot = e_flat[:, None] == jnp.arange(N_EXP, dtype=jnp.int32)[None, :]
    counts = jnp.sum(onehot, axis=0, dtype=jnp.int32)
    start = jnp.cumsum(counts) - counts
    padded = (counts + MOE_BLOCK - 1) // MOE_BLOCK * MOE_BLOCK
    pad_end = jnp.cumsum(padded)
    pad_start = pad_end - padded
    iota = jnp.arange(n_assign, dtype=jnp.int32)
    _, order = lax.sort((e_flat, iota), num_keys=1, is_stable=True)
    _, rank = lax.sort((order, iota), num_keys=1)
    pos = rank + jnp.sum(jnp.where(onehot, (pad_start - start)[None, :], 0), axis=1)
    blk_first = jnp.arange(n_blocks, dtype=jnp.int32) * MOE_BLOCK
    blk_e = jnp.minimum(jnp.sum(pad_end[None, :] <= blk_first[:, None], axis=1), N_EXP - 1).astype(jnp.int32)
    r = blk_first[:, None] - pad_start[blk_e][:, None] + jnp.arange(MOE_BLOCK, dtype=jnp.int32)[None, :]
    valid = r < counts[blk_e][:, None]
    src = jnp.clip(start[blk_e][:, None] + r, 0, n_assign - 1)
    filler = (blk_first[:, None] + jnp.arange(MOE_BLOCK, dtype=jnp.int32)[None, :]) % T
    tok = jnp.where(valid, order[src] // TOP_K, filler).reshape(n_blocks * MOE_BLOCK).astype(jnp.int32)
    return tok, pos, blk_e


def _prep_layer(l, w_in, sgu_w, w_branch, w_out, w_gu, w_dn, gd_A_log, gd_dt_bias, lb):
    offs = np.cumsum([0, BRANCH_W, BRANCH_W, 512, 512, 512, 512, 512, GD_QKV, 8, 8, 512, 3 * NA_W, N_BRANCH * D_MODEL])
    w = w_in[l]
    seg = lambda i, j: w[:, offs[i]:offs[j]]
    w_gd = jnp.concatenate([seg(7, 8), seg(10, 11), seg(8, 10),
                            jnp.zeros((D_MODEL, LANES - 4 * GD_HEADS), F32)], axis=1)
    par = jnp.zeros((2, LANES), F32)
    par = par.at[0, :2 * GD_HEADS].set(gd_A_log[l].reshape(-1)).at[1, :2 * GD_HEADS].set(gd_dt_bias[l].reshape(-1))
    lb_l = lb[:, l]
    return {
        'w_sgu': seg(0, 2).astype(BF16), 'w_hg': seg(2, 7).astype(BF16), 'w_gd': w_gd.astype(BF16),
        'w_na': seg(11, 12).astype(BF16), 'w_mg': seg(12, 13).astype(BF16),
        'sgu_w': sgu_w[l], 'wb': w_branch[l].astype(BF16), 'wo': w_out[l].astype(BF16),
        'w_gu': w_gu, 'w_dn': w_dn, 'layer': l, 'gd_par': par,
        'lbp': jnp.concatenate([jnp.log(lb_l), jnp.log1p(-lb_l), 1.0 - lb_l], axis=0),
    }


def _layer(x, groups, mods, mod_row, lw, p, norm_f, final_norm):
    T = x.shape[0]
    h = _normmod(x, p['norm1'], mods, mod_row, 1024, part_shift=0, part_scale=1)
    z_sgu = _matmul(h, lw['w_sgu'], 1024, 1024)
    z_hg = _matmul(h, lw['w_hg'], 1024, 1280)
    z_gd = _matmul(h, lw['w_gd'], 512, 2176)
    z_na = _matmul(h, lw['w_na'], 1024, 768)
    z_mg = _matmul(h, lw['w_mg'], 1024, 1024, out_dtype=BF16)

    x_mix, states = None, []
    for row0, B, L, ctx in groups:
        Tg = B * L
        o_a = _sgu(z_sgu, p['sgu_norm'], lw['sgu_w'], p['sgu_b'], 256, row0, Tg)
        s_hg0 = None if ctx is None else ctx[2]
        s_gd0 = None if ctx is None else ctx[3]
        o_hf, o_hb, s_hg = _hgrn(z_hg, lw['lbp'], s_hg0, B, L, row0)
        qkv = _gdprep(z_gd, p['gd_conv'], L, 256, row0, Tg)
        o_gf, o_gb, s_gd = _gdn(qkv, z_gd, lw['gd_par'], s_gd0, B, L, row0)
        if ctx is None:
            o_d = _ctx_attn(z_na, B, L, row0)
        else:
            o_d = _na_attn(z_na, ctx[0], ctx[1], _na_bias(p['na_rpb'], L // GRID_W), B, L, row0)
        x_mix = _merge(x, o_a, o_hf, o_hb, z_hg, o_gf, o_gb, z_gd, o_d, z_mg, p['hg_onorm'], p['gd_onorm'],
                       lw['wb'], lw['wo'], mods, mod_row, 256, row0, Tg, x_mix)
        states.append((s_hg, s_gd))
    x = x_mix

    h2, top_e, wts = _router(x, p['norm2'], mods, mod_row, p['w_router'], p['b_router'], 512)
    th = T // MOE_SPLIT
    x_out = None
    for j in range(MOE_SPLIT):
        r0 = j * th
        tok, pos, blk_e = _route(top_e[r0:r0 + th, :TOP_K], th)
        xb = _sc_gather(h2, tok + r0)
        yb = _experts(xb, blk_e, lw['w_gu'], p['b_gu'], lw['w_dn'], p['b_dn'], lw['layer'])
        yg = _sc_gather(yb, pos.reshape(th, TOP_K).T.reshape(-1)).reshape(TOP_K, th, D_MODEL // 2)
        x_out = _combine(x, yg, wts, mods, mod_row, norm_f, final_norm, 256, r0, x_out)
    return x_out, z_na, states


def kernel(x_prompt, x_sample, c, cache_na_k, cache_na_v, state_hgrn, state_gdn, c_ctx, w_ada, b_ada, norm1, norm2, norm_f, w_in, sgu_norm, sgu_w, sgu_b, hg_lb, hg_onorm, gd_conv, gd_A_log, gd_dt_bias, gd_onorm, na_rpb, w_branch, w_out, w_router, b_router, w_gu, b_gu, w_dn, b_dn):
    Bp, Lp, D = x_prompt.shape
    Bs, Ls, _ = x_sample.shape
    ctx_row = Bs
    cvecs = jnp.zeros((MOD_ROWS, D), F32).at[:Bs].set(c).at[ctx_row].set(c_ctx)
    mods = _modulation(cvecs, w_ada, b_ada)

    cs = jnp.cumsum(jax.nn.softmax(hg_lb.astype(F32), axis=1), axis=1)
    lb = cs - cs[:, :1]

    Tp, Ts = Bp * Lp, Bs * Ls
    x = jnp.concatenate([x_prompt.reshape(Tp, D), x_sample.reshape(Ts, D)], axis=0)
    mod_row = lambda r: jnp.where(r < Tp, ctx_row, (r - Tp) // Ls)
    ks_, vs_, hs_, gs_ = [], [], [], []
    for l in range(DEPTH):
        lw = _prep_layer(l, w_in, sgu_w, w_branch, w_out, w_gu, w_dn, gd_A_log, gd_dt_bias, lb)
        p = {'norm1': norm1[l], 'norm2': norm2[l], 'sgu_norm': sgu_norm[l], 'sgu_b': sgu_b[l],
             'gd_conv': gd_conv[l], 'hg_onorm': hg_onorm[l], 'gd_onorm': gd_onorm[l], 'na_rpb': na_rpb[l],
             'w_router': w_router[l], 'b_router': b_router[l], 'b_gu': b_gu, 'b_dn': b_dn}
        ctx = (cache_na_k[:, l].reshape(Bs, -1, NA_W), cache_na_v[:, l].reshape(Bs, -1, NA_W),
               state_hgrn[:, l], state_gdn[:, l])
        groups = [(0, Bp, Lp, None), (Tp, Bs, Ls, ctx)]
        x, z_na, states = _layer(x, groups, mods[l], mod_row, lw, p, norm_f, l == DEPTH - 1)
        ks_.append(z_na[:Tp, NA_W:2 * NA_W].reshape(Bp, Lp, NA_HEADS, NA_DH))
        vs_.append(z_na[:Tp, 2 * NA_W:].reshape(Bp, Lp, NA_HEADS, NA_DH))
        hs_.append(states[0][0])
        gs_.append(states[0][1])

    return (x[:Tp].reshape(Bp, Lp, D), x[Tp:].reshape(Bs, Ls, D),
            jnp.stack(ks_, axis=1), jnp.stack(vs_, axis=1), jnp.stack(hs_, axis=1), jnp.stack(gs_, axis=1))
```

```python
import functools
import math

import numpy as np
import jax
import jax.numpy as jnp
from jax import lax
from jax.experimental import pallas as pl
from jax.experimental.pallas import tpu as pltpu
from jax.experimental.pallas import tpu_sc as plsc

D_MODEL = 1024
DEPTH = 2
GRID_W = 64
BRANCH_W = 512
N_BRANCH = 4
SGU_CHUNK = 128
SGU_GROUPS = 4
HG_HEADS = 4
HG_DK = 128
HG_DV = 128
GD_HEADS = 4
GD_DK = 128
GD_DV = 128
NA_HEADS = 8
NA_DH = 64
NA_KH = 8
NA_KW = 16
N_EXP = 32
TOP_K = 4
D_FF = 1024
SWIGLU_LIMIT = 7.0
SWIGLU_ALPHA = 1.702
EPS = 1e-6

F32 = jnp.float32
BF16 = jnp.bfloat16
HI = lax.Precision.HIGHEST

LANES = 128
MOD_ROWS = 16
SCAN_C = 64
SCAN_ROWS = 4 * SCAN_C
MOE_BLOCK = 512
MOE_SPLIT = 2
NA_QROWS = 4
NA_KROWS = NA_QROWS + NA_KH
VMEM_LIMIT = 48 * 1024 * 1024

NT = (((1,), (1,)), ((), ()))
TN = (((0,), (0,)), ((), ()))


def _cparams(*sem):
    return pltpu.CompilerParams(dimension_semantics=sem, vmem_limit_bytes=VMEM_LIMIT)


def _bdot(a, b):
    return jnp.dot(a.astype(BF16), b.astype(BF16), preferred_element_type=F32)


def _bdot_g(a, b, dims):
    return lax.dot_general(a.astype(BF16), b.astype(BF16), dims, preferred_element_type=F32)


def _hdot(a, b):
    return jnp.dot(a, b, precision=HI, preferred_element_type=F32)


def _dot01(m3, x):
    hi = x.astype(BF16)
    r1 = x - hi.astype(F32)
    mid = r1.astype(BF16)
    lo = (r1 - mid.astype(F32)).astype(BF16)
    return jnp.dot(m3, jnp.concatenate([hi, mid, lo], axis=0), preferred_element_type=F32)


def _dot3(a, b):
    ah = a.astype(BF16)
    al = (a - ah.astype(F32)).astype(BF16)
    bh = b.astype(BF16)
    bl = (b - bh.astype(F32)).astype(BF16)
    return jnp.dot(jnp.concatenate([al, ah, ah], axis=1), jnp.concatenate([bh, bl, bh], axis=0),
                   preferred_element_type=F32)


LOG2E = 1.4426950408889634
HI_HALF = 0xFFFF0000


def _pack_bf16_pairs(x):
    w = x.shape[1] // 2
    b = lax.bitcast_convert_type(x.astype(BF16).astype(F32), jnp.uint32)
    return lax.bitcast_convert_type((b[:, :w] >> 16) | (b[:, w:] & jnp.uint32(HI_HALF)), jnp.int32)


def _unpack_bf16_pairs(p):
    p = lax.bitcast_convert_type(p, jnp.uint32)
    lo = lax.bitcast_convert_type(p << 16, F32)
    hi = lax.bitcast_convert_type(p & jnp.uint32(HI_HALF), F32)
    return jnp.concatenate([lo, hi], axis=1)


def _silu(x):
    return x * jax.nn.sigmoid(x)


def _logaddexp(a, b):
    return jnp.maximum(a, b) + jnp.log(1.0 + jnp.exp(-jnp.abs(a - b)))


def _softplus(x):
    return jnp.maximum(x, 0.0) + jnp.log(1.0 + jnp.exp(-jnp.abs(x)))


def _ada_kernel(c_ref, w_ref, b_ref, o_ref):
    o_ref[0] = _hdot(_silu(c_ref[...]), w_ref[0]) + b_ref[0]


def _modulation(cvecs, w_ada, b_ada):
    tn = 1536
    out = pl.pallas_call(
        _ada_kernel,
        grid=(DEPTH, 6 * D_MODEL // tn),
        in_specs=[pl.BlockSpec((MOD_ROWS, D_MODEL), lambda l, j: (0, 0)),
                  pl.BlockSpec((1, D_MODEL, tn), lambda l, j: (l, 0, j)),
                  pl.BlockSpec((1, 1, tn), lambda l, j: (l, 0, j))],
        out_specs=pl.BlockSpec((1, MOD_ROWS, tn), lambda l, j: (l, 0, j)),
        out_shape=jax.ShapeDtypeStruct((DEPTH, MOD_ROWS, 6 * D_MODEL), F32),
        compiler_params=_cparams("arbitrary", "arbitrary"),
        name="ada_modulation",
    )(cvecs, w_ada, b_ada.reshape(DEPTH, 1, 6 * D_MODEL))
    return out.reshape(DEPTH, MOD_ROWS * 6, 1, D_MODEL)


def _rms(x):
    return x * lax.rsqrt(jnp.mean(x * x, axis=-1, keepdims=True) + EPS)


def _normmod_kernel(x_ref, g_ref, sc_ref, sh_ref, o_ref):
    h = (_rms(x_ref[...]) * g_ref[...]) * (1.0 + sc_ref[0]) + sh_ref[0]
    o_ref[...] = h.astype(o_ref.dtype)


def _normmod(x, g, mods, mod_row, tm, part_shift, part_scale):
    T = x.shape[0]
    return pl.pallas_call(
        _normmod_kernel,
        grid=(T // tm,),
        in_specs=[pl.BlockSpec((tm, D_MODEL), lambda i: (i, 0)),
                  pl.BlockSpec((1, D_MODEL), lambda i: (0, 0)),
                  pl.BlockSpec((1, 1, D_MODEL), lambda i: (mod_row(i * tm) * 6 + part_scale, 0, 0)),
                  pl.BlockSpec((1, 1, D_MODEL), lambda i: (mod_row(i * tm) * 6 + part_shift, 0, 0))],
        out_specs=pl.BlockSpec((tm, D_MODEL), lambda i: (i, 0)),
        out_shape=jax.ShapeDtypeStruct((T, D_MODEL), BF16),
        compiler_params=_cparams("parallel"),
        name="normmod",
    )(x, g.reshape(1, D_MODEL), mods, mods)


def _mm_kernel(a_ref, w_ref, o_ref):
    o_ref[...] = jnp.dot(a_ref[...], w_ref[...], preferred_element_type=F32).astype(o_ref.dtype)


def _matmul(a, w, tm, tn, out_dtype=F32):
    T, K = a.shape
    N = w.shape[1]
    return pl.pallas_call(
        _mm_kernel,
        grid=(N // tn, T // tm),
        in_specs=[pl.BlockSpec((tm, K), lambda j, i: (i, 0)),
                  pl.BlockSpec((K, tn), lambda j, i: (0, j))],
        out_specs=pl.BlockSpec((tm, tn), lambda j, i: (i, j)),
        out_shape=jax.ShapeDtypeStruct((T, N), out_dtype),
        compiler_params=_cparams("parallel", "parallel"),
        name="in_proj",
    )(a, w)


def _sgu_kernel(u_ref, v_ref, gn_ref, ws_ref, bs_ref, o_ref):
    rows = u_ref.shape[0]
    gw = BRANCH_W // SGU_GROUPS
    u = jax.nn.gelu(u_ref[...])
    v = (_rms(jax.nn.gelu(v_ref[...])) * gn_ref[...]).astype(BF16)
    for n in range(rows // SGU_CHUNK):
        r = slice(n * SGU_CHUNK, (n + 1) * SGU_CHUNK)
        for g in range(SGU_GROUPS):
            cs = slice(g * gw, (g + 1) * gw)
            s = jnp.dot(ws_ref[g], v[r, cs], preferred_element_type=F32) + bs_ref[:, cs]
            o_ref[r, cs] = (u[r, cs] * s).astype(o_ref.dtype)


def _sgu(z_sgu, g_norm, w_s, b_s, rows, row0, T):
    gw = BRANCH_W // SGU_GROUPS
    b_exp = jnp.repeat(b_s.T, gw, axis=1)
    i0 = row0 // rows
    return pl.pallas_call(
        _sgu_kernel,
        grid=(T // rows,),
        in_specs=[pl.BlockSpec((rows, BRANCH_W), lambda i: (i0 + i, 0)),
                  pl.BlockSpec((rows, BRANCH_W), lambda i: (i0 + i, 1)),
                  pl.BlockSpec((1, BRANCH_W), lambda i: (0, 0)),
                  pl.BlockSpec((SGU_GROUPS, SGU_CHUNK, SGU_CHUNK), lambda i: (0, 0, 0)),
                  pl.BlockSpec((SGU_CHUNK, BRANCH_W), lambda i: (0, 0))],
        out_specs=pl.BlockSpec((rows, BRANCH_W), lambda i: (i, 0)),
        out_shape=jax.ShapeDtypeStruct((T, BRANCH_W), BF16),
        compiler_params=_cparams("parallel"),
        name="sgu",
    )(z_sgu, z_sgu, g_norm.reshape(1, BRANCH_W), w_s.astype(BF16), b_exp)


def _order(reverse):
    p = np.arange(SCAN_C)
    return SCAN_C - 1 - p if reverse else p


def _gla_consts():
    C = SCAN_C
    nlev = int(math.log2(C))
    mats, masks = [], []
    for reverse in (False, True):
        p = _order(reverse)
        pt, pr = p[:, None], p[None, :]
        m_d, k_d = [], []
        for lev in range(nlev):
            w = C >> (lev + 1)
            parent = p // (2 * w)
            later = (p % (2 * w)) >= w
            anchor = (parent * 2 * w + w - 1)[:, None]
            m = np.where(later[:, None], (pr > anchor) & (pr <= pt), (pr > pt) & (pr <= anchor))
            m_d.append(m)
            k_d.append((parent[:, None] == parent[None, :]) & later[:, None] & ~later[None, :])
        m_d.append(pr <= pt)
        m_d.append(pr > pt)
        k_d.append(np.eye(C, dtype=bool))
        mats.append(np.concatenate(m_d, axis=0))
        masks.append(np.stack(k_d))
    return (np.stack(mats).astype(np.float32), np.stack(masks).astype(np.float32))


def _delta_consts():
    C = SCAN_C
    tri, sl, incl, strict = [], [], [], []
    for reverse in (False, True):
        p = _order(reverse)
        pt, pr = p[:, None], p[None, :]
        tri.append(np.concatenate([pr <= pt, pr > pt], axis=0))
        sl.append(np.concatenate([pt > pr, np.zeros((C, LANES - C), bool), np.ones((C, LANES), bool)], axis=1))
        incl.append(pr <= pt)
        strict.append(pr < pt)
    f = lambda a: np.stack(a).astype(np.float32)
    return f(tri), f(sl), f(incl), f(strict)


def _hgrn_kernel(qf_ref, ff_ref, vf_ref, qb_ref, fb_ref, vb_ref, lb_ref, mat_ref, msk_ref, lat_ref, *rest, has_state):
    if has_state:
        s0_ref, of_ref, ob_ref, sfin_ref, st_ref = rest
    else:
        of_ref, ob_ref, sfin_ref, st_ref = rest
    C = SCAN_C
    nlev = msk_ref.shape[1] - 1
    c = pl.program_id(1)
    last_c = pl.num_programs(1) - 1

    @pl.when(c == 0)
    def _():
        for d in range(2):
            for h in range(HG_HEADS):
                if has_state:
                    st_ref[d, h] = s0_ref[0, d, h].T
                else:
                    st_ref[d, h] = jnp.zeros((HG_DV, HG_DK), F32)

    q_refs, f_refs, v_refs, o_refs = (qf_ref, qb_ref), (ff_ref, fb_ref), (vf_ref, vb_ref), (of_ref, ob_ref)
    sub = qf_ref.shape[0] // C
    rows = lambda j: slice(j * C, (j + 1) * C)
    col = lambda h: slice(h * HG_DK, (h + 1) * HG_DK)
    parts = [(d, j) for d in range(2) for j in range(sub)]
    q, k, fac, att = {}, {}, {}, {}
    for d, j in parts:
        zf = f_refs[d][rows(j), :]
        t = jnp.exp(-jnp.abs(zf))
        log_sig = jnp.minimum(zf, 0.0) - jnp.log(1.0 + t)
        logf = _logaddexp(lb_ref[d:d + 1, :], lb_ref[2 + d:3 + d, :] + log_sig)
        k[d, j] = lb_ref[4 + d:5 + d, :] * (jnp.where(zf >= 0.0, t, 1.0) / (1.0 + t))
        q[d, j] = _silu(q_refs[d][rows(j), :]) * (HG_DK ** -0.5)
        fac[d, j] = jnp.exp2(_dot01(mat_ref[d], logf * LOG2E))
    later = {(d, i): lat_ref[d, i] > 0.5 for d in range(2) for i in range(nlev)}
    for d, j in parts:
        for h in range(HG_HEADS):
            qh, kh = q[d, j][:, col(h)], k[d, j][:, col(h)]
            acc = msk_ref[d, nlev] * _bdot_g(qh, kh, NT)
            for i in range(nlev):
                z = (jnp.where(later[d, i], qh, kh) * fac[d, j][i * C:(i + 1) * C, col(h)]).astype(BF16)
                acc = acc + msk_ref[d, i] * lax.dot_general(z, z, NT, preferred_element_type=F32)
            att[d, j, h] = acc
    chains = [(d, h) for d in range(2) for h in range(HG_HEADS)]
    for t in range(sub):
        jd = (t, sub - 1 - t)
        for d, h in chains:
            j = jd[d]
            eb = fac[d, j][nlev * C:(nlev + 1) * C, col(h)]
            o_refs[d][rows(j), col(h)] = (_bdot(att[d, j, h], v_refs[d][rows(j), col(h)])
                                          + _bdot_g(q[d, j][:, col(h)] * eb, st_ref[d, h], NT))
        for d, h in chains:
            j = jd[d]
            eb = fac[d, j][nlev * C:(nlev + 1) * C, col(h)]
            er = fac[d, j][(nlev + 1) * C:, col(h)]
            e_last = eb[C - 1:C] if d == 0 else eb[0:1]
            st_ref[d, h] = st_ref[d, h] * e_last + _bdot_g(v_refs[d][rows(j), col(h)], k[d, j][:, col(h)] * er, TN)

    @pl.when(c == last_c)
    def _():
        for d in range(2):
            for h in range(HG_HEADS):
                sfin_ref[0, d, h] = st_ref[d, h].T


def _hgrn(z_hg, lbp, s0, B, L, row0):
    n = L // SCAN_ROWS
    c0 = row0 // SCAN_ROWS
    mats, masks = _gla_consts()
    later = np.broadcast_to(masks[:, :-1].any(axis=3)[..., None], masks[:, :-1].shape[:3] + (HG_DK,)).astype(np.float32)
    mats = np.tile(mats, (1, 1, 3))
    blk = (SCAN_ROWS, HG_HEADS * HG_DK)
    fwd = lambda col: pl.BlockSpec(blk, lambda b, c: (c0 + b * n + c, col))
    bwd = lambda col: pl.BlockSpec(blk, lambda b, c: (c0 + b * n + n - 1 - c, col))
    st_blk = (1, 2, HG_HEADS, HG_DK, HG_DV)
    in_specs = [fwd(0), fwd(1), fwd(3), bwd(0), bwd(2), bwd(3),
                pl.BlockSpec(lbp.shape, lambda b, c: (0, 0)),
                pl.BlockSpec(mats.shape, lambda b, c: (0, 0, 0)),
                pl.BlockSpec(masks.shape, lambda b, c: (0, 0, 0, 0)),
                pl.BlockSpec(later.shape, lambda b, c: (0, 0, 0, 0))]
    args = [z_hg] * 6 + [lbp, jnp.asarray(mats, BF16), jnp.asarray(masks), jnp.asarray(later)]
    if s0 is not None:
        in_specs.append(pl.BlockSpec(st_blk, lambda b, c: (b, 0, 0, 0, 0)))
        args.append(s0)
    return pl.pallas_call(
        functools.partial(_hgrn_kernel, has_state=s0 is not None),
        grid=(B, n),
        in_specs=in_specs,
        out_specs=[pl.BlockSpec(blk, lambda b, c: (b * n + c, 0)),
                   pl.BlockSpec(blk, lambda b, c: (b * n + n - 1 - c, 0)),
                   pl.BlockSpec(st_blk, lambda b, c: (b, 0, 0, 0, 0))],
        out_shape=[jax.ShapeDtypeStruct((B * L, HG_HEADS * HG_DV), F32),
                   jax.ShapeDtypeStruct((B * L, HG_HEADS * HG_DV), F32),
                   jax.ShapeDtypeStruct((B,) + st_blk[1:], F32)],
        scratch_shapes=[pltpu.VMEM((2, HG_HEADS, HG_DV, HG_DK), F32)],
        compiler_params=_cparams("parallel", "arbitrary"),
        name="hgrn_scan",
    )(*args)


GD_NQ = GD_HEADS * GD_DK
GD_QKV = 2 * GD_NQ + GD_HEADS * GD_DV
HALO = 8


def _gdprep_kernel(x_ref, prev_ref, next_ref, w_ref, o_ref, *, tiles_per_seq):
    R = x_ref.shape[0]
    t = pl.program_id(0) % tiles_per_seq
    x = x_ref[...]
    prev_row = jnp.where(t == 0, 0.0, prev_ref[HALO - 1:HALO, :])
    next_row = jnp.where(t == tiles_per_seq - 1, 0.0, next_ref[0:1, :])
    row = lax.broadcasted_iota(jnp.int32, x.shape, 0)
    xm1 = jnp.where(row == 0, prev_row, pltpu.roll(x, 1, 0))
    xp1 = jnp.where(row == R - 1, next_row, pltpu.roll(x, R - 1, 0))
    y = _silu(w_ref[0:1, :] * xm1 + w_ref[1:2, :] * x + w_ref[2:3, :] * xp1)
    for j in range(2 * GD_HEADS):
        cs = slice(j * GD_DK, (j + 1) * GD_DK)
        seg = y[:, cs]
        seg = seg * lax.rsqrt(jnp.sum(seg * seg, axis=-1, keepdims=True) + EPS)
        if j < GD_HEADS:
            seg = seg * (GD_DK ** -0.5)
        o_ref[:, cs] = seg
    o_ref[:, 2 * GD_NQ:] = y[:, 2 * GD_NQ:]


def _gdprep(z_gd, conv_w, L, rows, row0, T):
    tps = L // rows
    hb = rows // HALO
    nhalo = z_gd.shape[0] // HALO
    i0 = row0 // rows
    return pl.pallas_call(
        functools.partial(_gdprep_kernel, tiles_per_seq=tps),
        grid=(T // rows,),
        in_specs=[pl.BlockSpec((rows, GD_QKV), lambda i: (i0 + i, 0)),
                  pl.BlockSpec((HALO, GD_QKV), lambda i: (jnp.maximum((i0 + i) * hb - 1, 0), 0)),
                  pl.BlockSpec((HALO, GD_QKV), lambda i: (jnp.minimum((i0 + i + 1) * hb, nhalo - 1), 0)),
                  pl.BlockSpec((3, GD_QKV), lambda i: (0, 0))],
        out_specs=pl.BlockSpec((rows, GD_QKV), lambda i: (i, 0)),
        out_shape=jax.ShapeDtypeStruct((T, GD_QKV), F32),
        compiler_params=_cparams("parallel"),
        name="gdn_prep",
    )(z_gd, z_gd, z_gd, conv_w)


def _gdn_kernel(xf_ref, abf_ref, xb_ref, abb_ref, par_ref, tri_ref, sl_ref, incl_ref, strict_ref, msk_ref, *rest,
                has_state):
    if has_state:
        s0_ref, of_ref, ob_ref, sfin_ref, st_ref = rest
    else:
        of_ref, ob_ref, sfin_ref, st_ref = rest
    C = SCAN_C
    c = pl.program_id(1)
    last_c = pl.num_programs(1) - 1
    nlev = msk_ref.shape[1] - 1

    @pl.when(c == 0)
    def _():
        for d in range(2):
            for h in range(GD_HEADS):
                if has_state:
                    st_ref[d, h] = s0_ref[0, d, h].T
                else:
                    st_ref[d, h] = jnp.zeros((GD_DV, GD_DK), F32)

    x_refs, ab_refs, o_refs = (xf_ref, xb_ref), (abf_ref, abb_ref), (of_ref, ob_ref)
    sub = xf_ref.shape[0] // C
    rows = lambda j: slice(j * C, (j + 1) * C)
    chains = [(d, h) for d in range(2) for h in range(GD_HEADS)]
    parts = [(d, j, h) for d in range(2) for j in range(sub) for h in range(GD_HEADS)]
    q_of = lambda d, j, h: x_refs[d][rows(j), h * GD_DK:(h + 1) * GD_DK]
    k_of = lambda d, j, h: x_refs[d][rows(j), GD_NQ + h * GD_DK:GD_NQ + (h + 1) * GD_DK]
    v_of = lambda d, j, h: x_refs[d][rows(j), 2 * GD_NQ + h * GD_DV:2 * GD_NQ + (h + 1) * GD_DV]
    g_all, beta_all = [], []
    for d in range(2):
        ab = ab_refs[d][...]
        g_all.append(-jnp.exp(par_ref[0:1, :]) * _softplus(ab + par_ref[1:2, :]))
        beta_all.append(jax.nn.sigmoid(ab))

    def beta_of(d, j, h):
        lane = 2 * GD_HEADS + d * GD_HEADS + h
        return jnp.broadcast_to(beta_all[d][rows(j), lane:lane + 1], (C, LANES))

    decay, e_cum, e_rest, kb, a = {}, {}, {}, {}, {}
    for d, j, h in parts:
        lane = d * GD_HEADS + h
        g_b = jnp.broadcast_to(g_all[d][rows(j), lane:lane + 1], (C, 2 * LANES))
        sums = _dot01(tri_ref[d], g_b * sl_ref[d])
        decay[d, j, h] = jnp.exp(sums[:C, :C])
        e_cum[d, j, h] = jnp.exp(sums[:C, LANES:])
        e_rest[d, j, h] = jnp.exp(sums[C:, LANES:])
    for d, j, h in parts:
        k = k_of(d, j, h)
        kb[d, j, h] = k * beta_of(d, j, h)
        a[d, j, h] = strict_ref[d] * decay[d, j, h] * _bdot_g(kb[d, j, h], k, NT)
    inv_m = {pt: -(msk_ref[pt[0], nlev - 1] * a[pt]) for pt in parts}
    for lev in range(nlev - 2, -1, -1):
        a_w = {pt: msk_ref[pt[0], lev] * a[pt] for pt in parts}
        p = {pt: a_w[pt] + _bdot(inv_m[pt], a_w[pt]) for pt in parts}
        inv_m = {pt: inv_m[pt] - p[pt] - _bdot(p[pt], inv_m[pt]) for pt in parts}
    sol, att = {}, {}
    for d, j, h in parts:
        rhs = jnp.concatenate([v_of(d, j, h) * beta_of(d, j, h), kb[d, j, h] * e_cum[d, j, h]], axis=1)
        sol[d, j, h] = rhs + _dot3(inv_m[d, j, h], rhs)
        att[d, j, h] = incl_ref[d] * decay[d, j, h] * _bdot_g(q_of(d, j, h), k_of(d, j, h), NT)
    for t in range(sub):
        jd = (t, sub - 1 - t)
        u = {}
        for d, h in chains:
            pt = (d, jd[d], h)
            u[d, h] = sol[pt][:, :GD_DV] - _bdot_g(sol[pt][:, GD_DV:], st_ref[d, h], NT)
        for d, h in chains:
            pt = (d, jd[d], h)
            o_refs[d][rows(jd[d]), h * GD_DV:(h + 1) * GD_DV] = (
                _bdot_g(q_of(*pt) * e_cum[pt], st_ref[d, h], NT) + _bdot(att[pt], u[d, h]))
        for d, h in chains:
            pt = (d, jd[d], h)
            e_last = e_cum[pt][C - 1:C] if d == 0 else e_cum[pt][0:1]
            st_ref[d, h] = st_ref[d, h] * e_last + _bdot_g(u[d, h], k_of(*pt) * e_rest[pt], TN)

    @pl.when(c == last_c)
    def _():
        for d in range(2):
            for h in range(GD_HEADS):
                sfin_ref[0, d, h] = st_ref[d, h].T


def _gdn(qkv, z_gd, par, s0, B, L, row0):
    n = L // SCAN_ROWS
    c0 = row0 // SCAN_ROWS
    tri, sl, incl, strict = (jnp.asarray(a) for a in _delta_consts())
    tri = jnp.tile(tri, (1, 1, 3)).astype(BF16)
    masks = jnp.asarray(_gla_consts()[1])
    ab_col = (GD_QKV + GD_HEADS * GD_DV) // LANES
    xblk = (SCAN_ROWS, GD_QKV)
    ablk = (SCAN_ROWS, LANES)
    oblk = (SCAN_ROWS, GD_HEADS * GD_DV)
    st_blk = (1, 2, GD_HEADS, GD_DK, GD_DV)
    fwd = lambda b, c: b * n + c
    bwd = lambda b, c: b * n + n - 1 - c
    const3 = lambda a: pl.BlockSpec(a.shape, lambda b, c: (0, 0, 0))
    in_specs = [pl.BlockSpec(xblk, lambda b, c: (fwd(b, c), 0)),
                pl.BlockSpec(ablk, lambda b, c: (c0 + fwd(b, c), ab_col)),
                pl.BlockSpec(xblk, lambda b, c: (bwd(b, c), 0)),
                pl.BlockSpec(ablk, lambda b, c: (c0 + bwd(b, c), ab_col)),
                pl.BlockSpec(par.shape, lambda b, c: (0, 0)),
                const3(tri), const3(sl), const3(incl), const3(strict),
                pl.BlockSpec(masks.shape, lambda b, c: (0, 0, 0, 0))]
    args = [qkv, z_gd, qkv, z_gd, par, tri, sl, incl, strict, masks]
    if s0 is not None:
        in_specs.append(pl.BlockSpec(st_blk, lambda b, c: (b, 0, 0, 0, 0)))
        args.append(s0)
    return pl.pallas_call(
        functools.partial(_gdn_kernel, has_state=s0 is not None),
        grid=(B, n),
        in_specs=in_specs,
        out_specs=[pl.BlockSpec(oblk, lambda b, c: (fwd(b, c), 0)),
                   pl.BlockSpec(oblk, lambda b, c: (bwd(b, c), 0)),
                   pl.BlockSpec(st_blk, lambda b, c: (b, 0, 0, 0, 0))],
        out_shape=[jax.ShapeDtypeStruct((B * L, GD_HEADS * GD_DV), F32),
                   jax.ShapeDtypeStruct((B * L, GD_HEADS * GD_DV), F32),
                   jax.ShapeDtypeStruct((B,) + st_blk[1:], F32)],
        scratch_shapes=[pltpu.VMEM((2, GD_HEADS, GD_DV, GD_DK), F32)],
        compiler_params=_cparams("parallel", "arbitrary"),
        name="gdn_scan",
    )(*args)


NA_W = NA_HEADS * NA_DH


def _softmax_pv(s, v):
    m = jnp.max(s, axis=-1, keepdims=True)
    e = jnp.exp(s - m)
    den = jnp.sum(e, axis=-1, keepdims=True)
    return jnp.dot(e.astype(BF16), v, preferred_element_type=F32) / den


def _ctx_attn_kernel(q_ref, k_ref, v_ref, o_ref):
    for h in range(NA_HEADS):
        cs = slice(h * NA_DH, (h + 1) * NA_DH)
        q = (q_ref[:, cs] * (NA_DH ** -0.5)).astype(BF16)
        s = lax.dot_general(q, k_ref[:, cs].astype(BF16), NT, preferred_element_type=F32)
        o_ref[:, cs] = _softmax_pv(s, v_ref[:, cs].astype(BF16)).astype(o_ref.dtype)


def _ctx_attn(z_na, B, L, row0):
    blk = (L, NA_W)
    b0 = row0 // L
    return pl.pallas_call(
        _ctx_attn_kernel,
        grid=(B,),
        in_specs=[pl.BlockSpec(blk, lambda b: (b0 + b, 0)),
                  pl.BlockSpec(blk, lambda b: (b0 + b, 1)),
                  pl.BlockSpec(blk, lambda b: (b0 + b, 2))],
        out_specs=pl.BlockSpec(blk, lambda b: (b, 0)),
        out_shape=jax.ShapeDtypeStruct((B * L, NA_W), BF16),
        compiler_params=_cparams("parallel"),
        name="ctx_attn",
    )(z_na, z_na, z_na)


def _na_kernel(q_ref, *rest, n_kblk, nkeys_nb):
    k_refs = rest[:n_kblk]
    v_refs = rest[n_kblk:2 * n_kblk]
    kc_ref, vc_ref, bias_ref, o_ref, kbuf, vbuf = rest[2 * n_kblk:]
    qb = q_ref.shape[0]
    for i in range(n_kblk):
        kbuf[i * qb:(i + 1) * qb, :] = k_refs[i][...].astype(BF16)
        vbuf[i * qb:(i + 1) * qb, :] = v_refs[i][...].astype(BF16)
    kbuf[nkeys_nb:, :] = kc_ref[0].astype(BF16)
    vbuf[nkeys_nb:, :] = vc_ref[0].astype(BF16)
    for h in range(NA_HEADS):
        cs = slice(h * NA_DH, (h + 1) * NA_DH)
        q = (q_ref[:, cs] * (NA_DH ** -0.5)).astype(BF16)
        s = lax.dot_general(q, kbuf[:, cs], NT, preferred_element_type=F32)
        s_nb = s[:, :nkeys_nb] + bias_ref[0, h]
        s_cx = s[:, nkeys_nb:]
        m = jnp.maximum(jnp.max(s_nb, axis=-1, keepdims=True), jnp.max(s_cx, axis=-1, keepdims=True))
        e_nb = jnp.exp(s_nb - m)
        e_cx = jnp.exp(s_cx - m)
        den = jnp.sum(e_nb, axis=-1, keepdims=True) + jnp.sum(e_cx, axis=-1, keepdims=True)
        pv = (jnp.dot(e_nb.astype(BF16), vbuf[:nkeys_nb, cs], preferred_element_type=F32)
              + jnp.dot(e_cx.astype(BF16), vbuf[nkeys_nb:, cs], preferred_element_type=F32))
        o_ref[:, cs] = (pv / den).astype(o_ref.dtype)


def _na_bias(rpb, rows):
    qr, kr, col = np.arange(NA_QROWS), np.arange(NA_KROWS), np.arange(GRID_W)
    nblk = rows // NA_QROWS
    ndr, ndc = 2 * NA_KH - 1, 2 * NA_KW - 1
    sel_r, row_ok = [], []
    for m in (0, 1, nblk - 1):
        r = (NA_QROWS * m + qr)[:, None]
        start = np.clip(NA_QROWS * m - NA_KH // 2, 0, rows - NA_KROWS)
        kra = (start + kr)[None, :]
        r0 = np.clip(r - NA_KH // 2, 0, rows - NA_KH)
        row_ok.append((kra >= r0) & (kra < r0 + NA_KH))
        dr = np.clip(kra - r + NA_KH - 1, 0, ndr - 1)
        sel_r.append(dr[..., None] == np.arange(ndr))
    sel_r = np.stack(sel_r).astype(np.float32)
    row_ok = np.stack(row_ok)
    col_start = np.clip(col - NA_KW // 2, 0, GRID_W - NA_KW)[:, None]
    col_ok = (col[None, :] >= col_start) & (col[None, :] < col_start + NA_KW)
    dc = np.clip(col[None, :] - col[:, None], -(NA_KW - 1), NA_KW - 1) + NA_KW - 1
    sel_c = (dc[..., None] == np.arange(ndc)).astype(np.float32)
    bias = jnp.einsum('hab,vqka,xyb->vhqxky', rpb.astype(F32), sel_r, sel_c, precision=HI)
    ok = row_ok[:, None, :, None, :, None] & col_ok[None, None, None, :, None, :]
    bias = jnp.where(ok, bias, -jnp.inf)
    return bias.reshape(3, NA_HEADS, NA_QROWS * GRID_W, NA_KROWS * GRID_W)


def _na_attn(z_na, k_ctx, v_ctx, bias, B, S, row0):
    rows = S // GRID_W
    qb = NA_QROWS * GRID_W
    m0 = row0 // qb
    nblk = rows // NA_QROWS
    n_kblk = NA_KROWS // NA_QROWS
    lc = k_ctx.shape[1]
    nkeys_nb = NA_KROWS * GRID_W
    kstart = lambda m: jnp.clip(m - 1, 0, nblk - n_kblk)
    variant = lambda m: jnp.where(m == 0, 0, jnp.where(m == nblk - 1, 2, 1))
    kv_specs = lambda col: [pl.BlockSpec((qb, NA_W), functools.partial(
        lambda b, m, i, col: (m0 + b * nblk + kstart(m) + i, col), i=i, col=col)) for i in range(n_kblk)]
    return pl.pallas_call(
        functools.partial(_na_kernel, n_kblk=n_kblk, nkeys_nb=nkeys_nb),
        grid=(B, nblk),
        in_specs=([pl.BlockSpec((qb, NA_W), lambda b, m: (m0 + b * nblk + m, 0))] + kv_specs(1) + kv_specs(2)
                  + [pl.BlockSpec((1, lc, NA_W), lambda b, m: (b, 0, 0)),
                     pl.BlockSpec((1, lc, NA_W), lambda b, m: (b, 0, 0)),
                     pl.BlockSpec((1,) + bias.shape[1:], lambda b, m: (variant(m), 0, 0, 0))]),
        out_specs=pl.BlockSpec((qb, NA_W), lambda b, m: (b * nblk + m, 0)),
        out_shape=jax.ShapeDtypeStruct((B * S, NA_W), BF16),
        scratch_shapes=[pltpu.VMEM((nkeys_nb + lc, NA_W), BF16), pltpu.VMEM((nkeys_nb + lc, NA_W), BF16)],
        compiler_params=_cparams("parallel", "arbitrary"),
        name="na_attn",
    )(z_na, *([z_na] * (2 * n_kblk)), k_ctx, v_ctx, bias)


def _head_rms(o, g_row):
    parts = []
    for h in range(o.shape[1] // LANES):
        parts.append(_rms(o[:, h * LANES:(h + 1) * LANES]) * g_row)
    return jnp.concatenate(parts, axis=1)


def _merge_kernel(x_ref, oa_ref, hf_ref, hb_ref, hg_ref, gf_ref, gb_ref, gg_ref, od_ref, mg_ref,
                  hn_ref, gn_ref, wb_ref, wo_ref, g1_ref, *rest):
    o_ref = rest[-1]
    o_b = _head_rms(hf_ref[...] + hb_ref[...], hn_ref[...]) * _silu(hg_ref[...])
    o_c = _head_rms(gf_ref[...] + gb_ref[...], gn_ref[...]) * _silu(gg_ref[...])
    branches = (oa_ref[...], o_b.astype(BF16), o_c.astype(BF16), od_ref[...])
    merged = None
    for n_, o_n in enumerate(branches):
        gate = jax.nn.sigmoid(mg_ref[:, n_ * D_MODEL:(n_ + 1) * D_MODEL].astype(F32))
        term = gate * jnp.dot(o_n, wb_ref[n_], preferred_element_type=F32)
        merged = term if merged is None else merged + term
    mix = jnp.dot(merged.astype(BF16), wo_ref[...], preferred_element_type=F32)
    o_ref[...] = x_ref[...] + g1_ref[0] * mix


def _merge(x, o_a, o_hf, o_hb, z_hg, o_gf, o_gb, z_gd, o_d, z_mg, hg_onorm, gd_onorm, wb, wo, mods, mod_row, tm,
           row0, T, x_acc):
    i0 = row0 // tm
    glob = lambda w, col=0: pl.BlockSpec((tm, w), lambda i: (i0 + i, col))
    loc = lambda w: pl.BlockSpec((tm, w), lambda i: (i, 0))
    full = lambda a: pl.BlockSpec(a.shape, lambda i: (0,) * a.ndim)
    hn = hg_onorm.reshape(1, HG_DV)
    gn = gd_onorm.reshape(1, GD_DV)
    in_specs = [glob(D_MODEL), loc(BRANCH_W), loc(BRANCH_W), loc(BRANCH_W), glob(BRANCH_W, 4),
                loc(BRANCH_W), loc(BRANCH_W), glob(BRANCH_W, GD_QKV // BRANCH_W), loc(BRANCH_W),
                glob(N_BRANCH * D_MODEL), full(hn), full(gn), full(wb), full(wo),
                pl.BlockSpec((1, 1, D_MODEL), lambda i: (mod_row(row0 + i * tm) * 6 + 2, 0, 0))]
    args = [x, o_a, o_hf, o_hb, z_hg, o_gf, o_gb, z_gd, o_d, z_mg, hn, gn, wb, wo, mods]
    aliases = {}
    if x_acc is not None:
        in_specs.append(pl.BlockSpec(memory_space=pl.ANY))
        aliases = {len(args): 0}
        args.append(x_acc)
    return pl.pallas_call(
        _merge_kernel,
        grid=(T // tm,),
        in_specs=in_specs,
        out_specs=glob(D_MODEL),
        out_shape=jax.ShapeDtypeStruct(x.shape, F32),
        input_output_aliases=aliases,
        compiler_params=_cparams("parallel"),
        name="merge",
    )(*args)


SC_CORES = 2
SC_SUBCORES = 16
SC_WIN = 64


def _sc_gather(table, idx):
    V, D = table.shape
    N = idx.shape[0]
    nw = SC_CORES * SC_SUBCORES
    per_w = N // nw
    n_win = per_w // SC_WIN
    assert per_w * nw == N and n_win * SC_WIN == per_w
    mesh = plsc.VectorSubcoreMesh(core_axis_name="c", subcore_axis_name="s")

    @functools.partial(
        pl.kernel, mesh=mesh,
        out_type=jax.ShapeDtypeStruct((N, D), table.dtype),
        scratch_types=[pltpu.VMEM((n_win, SC_WIN), jnp.int32),
                       pltpu.VMEM((SC_WIN, D), table.dtype),
                       pltpu.SemaphoreType.DMA],
    )
    def gather_rows(table_hbm, idx_hbm, out_hbm, idx_v, rows_v, sem):
        wid = lax.axis_index("s") * SC_CORES + lax.axis_index("c")
        pltpu.sync_copy(idx_hbm.at[wid], idx_v)

        @pl.loop(0, n_win)
        def _(w):
            pltpu.async_copy(table_hbm.at[idx_v.at[w]], rows_v, sem).wait()
            pltpu.sync_copy(rows_v, out_hbm.at[pl.ds(wid * per_w + w * SC_WIN, SC_WIN)])

    return gather_rows(table, idx.reshape(nw, n_win, SC_WIN))


def _router_kernel(x_ref, g_ref, sc_ref, sh_ref, wr_ref, br_ref, tri_ref, h_ref, e_ref, w_ref, r_ref, cnt_ref, run_ref,
                   *, tiles_per_range):
    @pl.when(pl.program_id(0) % tiles_per_range == 0)
    def _():
        run_ref[...] = jnp.zeros(run_ref.shape, F32)

    h = (_rms(x_ref[...]) * g_ref[...]) * (1.0 + sc_ref[0]) + sh_ref[0]
    h_ref[...] = _pack_bf16_pairs(h)
    logits = _hdot(h, wr_ref[...]) + br_ref[...]
    lane = lax.broadcasted_iota(jnp.int32, logits.shape, 1)
    e_out = jnp.zeros(logits.shape, jnp.int32)
    v_out = jnp.zeros(logits.shape, F32)
    onehot = jnp.zeros(logits.shape, F32)
    top0, picks = None, []
    for k in range(TOP_K):
        m = jnp.max(logits, axis=-1, keepdims=True)
        idx = jnp.min(jnp.where(logits == m, lane, LANES), axis=-1, keepdims=True)
        if k == 0:
            top0 = m
        picks.append(lane == idx)
        e_out = jnp.where(lane == k, idx, e_out)
        v_out = jnp.where(lane == k, jnp.exp(m - top0), v_out)
        onehot = jnp.where(picks[k], 1.0, onehot)
        logits = jnp.where(picks[k], -jnp.inf, logits)
    e_ref[...] = e_out
    w_ref[...] = v_out / jnp.sum(v_out, axis=-1, keepdims=True)
    before = run_ref[...] + jnp.dot(tri_ref[...], onehot.astype(BF16), preferred_element_type=F32)
    r_out = jnp.zeros(logits.shape, jnp.int32)
    for k in range(TOP_K):
        rank = jnp.sum(jnp.where(picks[k], before, 0.0), axis=-1, keepdims=True)
        r_out = jnp.where(lane == k, rank.astype(jnp.int32), r_out)
    r_ref[...] = r_out
    run_ref[...] = run_ref[...] + jnp.sum(onehot, axis=0, keepdims=True)
    cnt_ref[0] = run_ref[...]


def _router(x, g, mods, mod_row, w_router, b_router, tm, n_ranges):
    T = x.shape[0]
    wr = jnp.zeros((D_MODEL, LANES), F32).at[:, :N_EXP].set(w_router)
    br = jnp.full((1, LANES), -jnp.inf, F32).at[0, :N_EXP].set(b_router)
    tri = jnp.asarray(np.tril(np.ones((tm, tm), np.float32), -1), BF16)
    tiles_per_range = T // n_ranges // tm
    row = lambda w: pl.BlockSpec((tm, w), lambda i: (i, 0))
    return pl.pallas_call(
        functools.partial(_router_kernel, tiles_per_range=tiles_per_range),
        grid=(T // tm,),
        in_specs=[row(D_MODEL),
                  pl.BlockSpec((1, D_MODEL), lambda i: (0, 0)),
                  pl.BlockSpec((1, 1, D_MODEL), lambda i: (mod_row(i * tm) * 6 + 4, 0, 0)),
                  pl.BlockSpec((1, 1, D_MODEL), lambda i: (mod_row(i * tm) * 6 + 3, 0, 0)),
                  pl.BlockSpec((D_MODEL, LANES), lambda i: (0, 0)),
                  pl.BlockSpec((1, LANES), lambda i: (0, 0)),
                  pl.BlockSpec((tm, tm), lambda i: (0, 0))],
        out_specs=[row(D_MODEL // 2), row(LANES), row(LANES), row(LANES),
                   pl.BlockSpec((1, 1, LANES), lambda i: (i // tiles_per_range, 0, 0))],
        out_shape=[jax.ShapeDtypeStruct((T, D_MODEL // 2), jnp.int32),
                   jax.ShapeDtypeStruct((T, LANES), jnp.int32),
                   jax.ShapeDtypeStruct((T, LANES), F32),
                   jax.ShapeDtypeStruct((T, LANES), jnp.int32),
                   jax.ShapeDtypeStruct((n_ranges, 1, LANES), F32)],
        scratch_shapes=[pltpu.VMEM((1, LANES), F32)],
        compiler_params=_cparams("arbitrary"),
        name="router",
    )(x, g.reshape(1, D_MODEL), mods, mods, wr, br, tri)


def _expert_kernel(blk_e_ref, x_ref, wgu_ref, bgu_ref, wdn_ref, bdn_ref, o_ref, wgu_bf, wdn_bf):
    i = pl.program_id(0)
    new_expert = jnp.logical_or(i == 0, blk_e_ref[i] != blk_e_ref[jnp.maximum(i - 1, 0)])

    @pl.when(new_expert)
    def _():
        wgu_bf[...] = wgu_ref[0].astype(BF16)
        wdn_bf[...] = wdn_ref[0].astype(BF16)

    x = _unpack_bf16_pairs(x_ref[...]).astype(BF16)
    gu = jnp.dot(x, wgu_bf[...], preferred_element_type=F32) + bgu_ref[0]
    a = jnp.minimum(gu[:, :D_FF], SWIGLU_LIMIT)
    lin = jnp.clip(gu[:, D_FF:], -SWIGLU_LIMIT, SWIGLU_LIMIT)
    y = a * jax.nn.sigmoid(SWIGLU_ALPHA * a) * (lin + 1.0)
    o_ref[...] = _pack_bf16_pairs(jnp.dot(y.astype(BF16), wdn_bf[...], preferred_element_type=F32) + bdn_ref[0])


def _experts(xb, blk_e, w_gu, b_gu, w_dn, b_dn, layer):
    n_pad = xb.shape[0]
    n_blocks = n_pad // MOE_BLOCK
    e0 = layer * N_EXP
    w_gu = w_gu.reshape(DEPTH * N_EXP, D_MODEL, 2 * D_FF)
    w_dn = w_dn.reshape(DEPTH * N_EXP, D_FF, D_MODEL)
    grid_spec = pltpu.PrefetchScalarGridSpec(
        num_scalar_prefetch=1,
        grid=(n_blocks,),
        in_specs=[pl.BlockSpec((MOE_BLOCK, D_MODEL // 2), lambda i, e: (i, 0)),
                  pl.BlockSpec((1, D_MODEL, 2 * D_FF), lambda i, e: (e0 + e[i], 0, 0)),
                  pl.BlockSpec((1, 1, 2 * D_FF), lambda i, e: (e0 + e[i], 0, 0)),
                  pl.BlockSpec((1, D_FF, D_MODEL), lambda i, e: (e0 + e[i], 0, 0)),
                  pl.BlockSpec((1, 1, D_MODEL), lambda i, e: (e0 + e[i], 0, 0))],
        out_specs=pl.BlockSpec((MOE_BLOCK, D_MODEL // 2), lambda i, e: (i, 0)),
        scratch_shapes=[pltpu.VMEM((D_MODEL, 2 * D_FF), BF16), pltpu.VMEM((D_FF, D_MODEL), BF16)],
    )
    return pl.pallas_call(
        _expert_kernel,
        grid_spec=grid_spec,
        out_shape=jax.ShapeDtypeStruct((n_pad, D_MODEL // 2), jnp.int32),
        compiler_params=_cparams("arbitrary"),
        name="experts",
    )(blk_e, xb, w_gu, b_gu.reshape(DEPTH * N_EXP, 1, 2 * D_FF), w_dn, b_dn.reshape(DEPTH * N_EXP, 1, D_MODEL))


def _combine_kernel(x_ref, y_ref, w_ref, g2_ref, nf_ref, *rest, final_norm):
    o_ref = rest[-1]
    acc = None
    for k in range(TOP_K):
        term = _unpack_bf16_pairs(y_ref[k]) * w_ref[:, k:k + 1]
        acc = term if acc is None else acc + term
    x = x_ref[...] + g2_ref[0] * acc
    if final_norm:
        x = _rms(x) * nf_ref[...]
    o_ref[...] = x


def _combine(x, yg, wts, mods, mod_row, norm_f, final_norm, tm, row0, x_acc):
    T = yg.shape[1]
    i0 = row0 // tm
    glob = lambda w: pl.BlockSpec((tm, w), lambda i: (i0 + i, 0))
    in_specs = [glob(D_MODEL), pl.BlockSpec((TOP_K, tm, D_MODEL // 2), lambda i: (0, i, 0)), glob(LANES),
                pl.BlockSpec((1, 1, D_MODEL), lambda i: (mod_row(row0 + i * tm) * 6 + 5, 0, 0)),
                pl.BlockSpec((1, D_MODEL), lambda i: (0, 0))]
    args = [x, yg, wts, mods, norm_f.reshape(1, D_MODEL)]
    aliases = {}
    if x_acc is not None:
        in_specs.append(pl.BlockSpec(memory_space=pl.ANY))
        aliases = {len(args): 0}
        args.append(x_acc)
    return pl.pallas_call(
        functools.partial(_combine_kernel, final_norm=final_norm),
        grid=(T // tm,),
        in_specs=in_specs,
        out_specs=glob(D_MODEL),
        out_shape=jax.ShapeDtypeStruct(x.shape, F32),
        input_output_aliases=aliases,
        compiler_params=_cparams("parallel"),
        name="combine",
    )(*args)


def _route(top_e, rank, counts, T):
    n_assign = T * TOP_K
    n_blocks = n_assign // MOE_BLOCK + N_EXP
    e_flat = top_e.reshape(n_assign)
    onehot = e_flat[:, None] == jnp.arange(N_EXP, dtype=jnp.int32)[None, :]
    start = jnp.cumsum(counts) - counts
    padded = (counts + MOE_BLOCK - 1) // MOE_BLOCK * MOE_BLOCK
    pad_end = jnp.cumsum(padded)
    pad_start = pad_end - padded
    iota = jnp.arange(n_assign, dtype=jnp.int32)
    _, order = lax.sort((e_flat, iota), num_keys=1, is_stable=True)
    pos = rank.reshape(n_assign) + jnp.sum(jnp.where(onehot, pad_start[None, :], 0), axis=1)
    blk_first = jnp.arange(n_blocks, dtype=jnp.int32) * MOE_BLOCK
    blk_e = jnp.minimum(jnp.sum(pad_end[None, :] <= blk_first[:, None], axis=1), N_EXP - 1).astype(jnp.int32)
    r = blk_first[:, None] - pad_start[blk_e][:, None] + jnp.arange(MOE_BLOCK, dtype=jnp.int32)[None, :]
    valid = r < counts[blk_e][:, None]
    src = jnp.clip(start[blk_e][:, None] + r, 0, n_assign - 1)
    filler = (blk_first[:, None] + jnp.arange(MOE_BLOCK, dtype=jnp.int32)[None, :]) % T
    tok = jnp.where(valid, order[src] // TOP_K, filler).reshape(n_blocks * MOE_BLOCK).astype(jnp.int32)
    return tok, pos, blk_e


def _prep_layer(l, w_in, sgu_w, w_branch, w_out, w_gu, w_dn, gd_A_log, gd_dt_bias, lb):
    offs = np.cumsum([0, BRANCH_W, BRANCH_W, 512, 512, 512, 512, 512, GD_QKV, 8, 8, 512, 3 * NA_W, N_BRANCH * D_MODEL])
    w = w_in[l]
    seg = lambda i, j: w[:, offs[i]:offs[j]]
    w_gd = jnp.concatenate([seg(7, 8), seg(10, 11), seg(8, 10),
                            jnp.zeros((D_MODEL, LANES - 4 * GD_HEADS), F32)], axis=1)
    par = jnp.zeros((2, LANES), F32)
    par = par.at[0, :2 * GD_HEADS].set(gd_A_log[l].reshape(-1)).at[1, :2 * GD_HEADS].set(gd_dt_bias[l].reshape(-1))
    lb_l = lb[:, l]
    return {
        'w_sgu': seg(0, 2).astype(BF16), 'w_hg': seg(2, 7).astype(BF16), 'w_gd': w_gd.astype(BF16),
        'w_na': seg(11, 12).astype(BF16), 'w_mg': seg(12, 13).astype(BF16),
        'sgu_w': sgu_w[l], 'wb': w_branch[l].astype(BF16), 'wo': w_out[l].astype(BF16),
        'w_gu': w_gu, 'w_dn': w_dn, 'layer': l, 'gd_par': par,
        'lbp': jnp.concatenate([jnp.log(lb_l), jnp.log1p(-lb_l), 1.0 - lb_l], axis=0),
    }


def _layer(x, groups, mods, mod_row, lw, p, norm_f, final_norm):
    T = x.shape[0]
    h = _normmod(x, p['norm1'], mods, mod_row, 1024, part_shift=0, part_scale=1)
    z_sgu = _matmul(h, lw['w_sgu'], 1024, 1024)
    z_hg = _matmul(h, lw['w_hg'], 1024, 1280)
    z_gd = _matmul(h, lw['w_gd'], 512, 2176)
    z_na = _matmul(h, lw['w_na'], 1024, 768)
    z_mg = _matmul(h, lw['w_mg'], 1024, 1024, out_dtype=BF16)

    x_mix, states = None, []
    for row0, B, L, ctx in groups:
        Tg = B * L
        o_a = _sgu(z_sgu, p['sgu_norm'], lw['sgu_w'], p['sgu_b'], 256, row0, Tg)
        s_hg0 = None if ctx is None else ctx[2]
        s_gd0 = None if ctx is None else ctx[3]
        o_hf, o_hb, s_hg = _hgrn(z_hg, lw['lbp'], s_hg0, B, L, row0)
        qkv = _gdprep(z_gd, p['gd_conv'], L, 256, row0, Tg)
        o_gf, o_gb, s_gd = _gdn(qkv, z_gd, lw['gd_par'], s_gd0, B, L, row0)
        if ctx is None:
            o_d = _ctx_attn(z_na, B, L, row0)
        else:
            o_d = _na_attn(z_na, ctx[0], ctx[1], _na_bias(p['na_rpb'], L // GRID_W), B, L, row0)
        x_mix = _merge(x, o_a, o_hf, o_hb, z_hg, o_gf, o_gb, z_gd, o_d, z_mg, p['hg_onorm'], p['gd_onorm'],
                       lw['wb'], lw['wo'], mods, mod_row, 256, row0, Tg, x_mix)
        states.append((s_hg, s_gd))
    x = x_mix

    h2, top_e, wts, rank, counts = _router(x, p['norm2'], mods, mod_row, p['w_router'], p['b_router'], 512, MOE_SPLIT)
    counts = counts[:, 0, :N_EXP].astype(jnp.int32)
    th = T // MOE_SPLIT
    x_out = None
    for j in range(MOE_SPLIT):
        r0 = j * th
        tok, pos, blk_e = _route(top_e[r0:r0 + th, :TOP_K], rank[r0:r0 + th, :TOP_K], counts[j], th)
        xb = _sc_gather(h2, tok + r0)
        yb = _experts(xb, blk_e, lw['w_gu'], p['b_gu'], lw['w_dn'], p['b_dn'], lw['layer'])
        yg = _sc_gather(yb, pos.reshape(th, TOP_K).T.reshape(-1)).reshape(TOP_K, th, D_MODEL // 2)
        x_out = _combine(x, yg, wts, mods, mod_row, norm_f, final_norm, 256, r0, x_out)
    return x_out, z_na, states


def kernel(x_prompt, x_sample, c, cache_na_k, cache_na_v, state_hgrn, state_gdn, c_ctx, w_ada, b_ada, norm1, norm2, norm_f, w_in, sgu_norm, sgu_w, sgu_b, hg_lb, hg_onorm, gd_conv, gd_A_log, gd_dt_bias, gd_onorm, na_rpb, w_branch, w_out, w_router, b_router, w_gu, b_gu, w_dn, b_dn):
    Bp, Lp, D = x_prompt.shape
    Bs, Ls, _ = x_sample.shape
    ctx_row = Bs
    cvecs = jnp.zeros((MOD_ROWS, D), F32).at[:Bs].set(c).at[ctx_row].set(c_ctx)
    mods = _modulation(cvecs, w_ada, b_ada)

    cs = jnp.cumsum(jax.nn.softmax(hg_lb.astype(F32), axis=1), axis=1)
    lb = cs - cs[:, :1]

    Tp, Ts = Bp * Lp, Bs * Ls
    x = jnp.concatenate([x_prompt.reshape(Tp, D), x_sample.reshape(Ts, D)], axis=0)
    mod_row = lambda r: jnp.where(r < Tp, ctx_row, (r - Tp) // Ls)
    ks_, vs_, hs_, gs_ = [], [], [], []
    for l in range(DEPTH):
        lw = _prep_layer(l, w_in, sgu_w, w_branch, w_out, w_gu, w_dn, gd_A_log, gd_dt_bias, lb)
        p = {'norm1': norm1[l], 'norm2': norm2[l], 'sgu_norm': sgu_norm[l], 'sgu_b': sgu_b[l],
             'gd_conv': gd_conv[l], 'hg_onorm': hg_onorm[l], 'gd_onorm': gd_onorm[l], 'na_rpb': na_rpb[l],
             'w_router': w_router[l], 'b_router': b_router[l], 'b_gu': b_gu, 'b_dn': b_dn}
        ctx = (cache_na_k[:, l].reshape(Bs, -1, NA_W), cache_na_v[:, l].reshape(Bs, -1, NA_W),
               state_hgrn[:, l], state_gdn[:, l])
        groups = [(0, Bp, Lp, None), (Tp, Bs, Ls, ctx)]
        x, z_na, states = _layer(x, groups, mods[l], mod_row, lw, p, norm_f, l == DEPTH - 1)
        ks_.append(z_na[:Tp, NA_W:2 * NA_W].reshape(Bp, Lp, NA_HEADS, NA_DH))
        vs_.append(z_na[:Tp, 2 * NA_W:].reshape(Bp, Lp, NA_HEADS, NA_DH))
        hs_.append(states[0][0])
        gs_.append(states[0][1])

    return (x[:Tp].reshape(Bp, Lp, D), x[Tp:].reshape(Bs, Ls, D),
            jnp.stack(ks_, axis=1), jnp.stack(vs_, axis=1), jnp.stack(hs_, axis=1), jnp.stack(gs_, axis=1))
```

```python
import functools
import math

import numpy as np
import jax
import jax.numpy as jnp
from jax import lax
from jax.experimental import pallas as pl
from jax.experimental.pallas import tpu as pltpu
from jax.experimental.pallas import tpu_sc as plsc

D_MODEL = 1024
DEPTH = 2
GRID_W = 64
BRANCH_W = 512
N_BRANCH = 4
SGU_CHUNK = 128
SGU_GROUPS = 4
HG_HEADS = 4
HG_DK = 128
HG_DV = 128
GD_HEADS = 4
GD_DK = 128
GD_DV = 128
NA_HEADS = 8
NA_DH = 64
NA_KH = 8
NA_KW = 16
N_EXP = 32
TOP_K = 4
D_FF = 1024
SWIGLU_LIMIT = 7.0
SWIGLU_ALPHA = 1.702
EPS = 1e-6

F32 = jnp.float32
BF16 = jnp.bfloat16
HI = lax.Precision.HIGHEST

LANES = 128
MOD_ROWS = 16
SCAN_C = 64
SCAN_ROWS = 4 * SCAN_C
MOE_BLOCK = 512
MOE_SPLIT = 2
NA_QROWS = 4
NA_KROWS = NA_QROWS + NA_KH
VMEM_LIMIT = 48 * 1024 * 1024

NT = (((1,), (1,)), ((), ()))
TN = (((0,), (0,)), ((), ()))


def _cparams(*sem):
    return pltpu.CompilerParams(dimension_semantics=sem, vmem_limit_bytes=VMEM_LIMIT)


def _bdot(a, b):
    return jnp.dot(a.astype(BF16), b.astype(BF16), preferred_element_type=F32)


def _bdot_g(a, b, dims):
    return lax.dot_general(a.astype(BF16), b.astype(BF16), dims, preferred_element_type=F32)


def _hdot(a, b):
    return jnp.dot(a, b, precision=HI, preferred_element_type=F32)


def _dot01(m3, x):
    hi = x.astype(BF16)
    r1 = x - hi.astype(F32)
    mid = r1.astype(BF16)
    lo = (r1 - mid.astype(F32)).astype(BF16)
    return jnp.dot(m3, jnp.concatenate([hi, mid, lo], axis=0), preferred_element_type=F32)


def _dot01_sel(m3, g, sel):
    hi = g.astype(BF16)
    r1 = g - hi.astype(F32)
    mid = r1.astype(BF16)
    lo = (r1 - mid.astype(F32)).astype(BF16)
    terms = [jnp.concatenate([t * sel, t], axis=1) for t in (hi, mid, lo)]
    return jnp.dot(m3, jnp.concatenate(terms, axis=0), preferred_element_type=F32)


def _dot3(a, b):
    ah = a.astype(BF16)
    al = (a - ah.astype(F32)).astype(BF16)
    bh = b.astype(BF16)
    bl = (b - bh.astype(F32)).astype(BF16)
    return jnp.dot(jnp.concatenate([al, ah, ah], axis=1), jnp.concatenate([bh, bl, bh], axis=0),
                   preferred_element_type=F32)


LOG2E = 1.4426950408889634
HI_HALF = 0xFFFF0000


def _pack_bf16_pairs(x):
    w = x.shape[1] // 2
    b = lax.bitcast_convert_type(x.astype(BF16).astype(F32), jnp.uint32)
    return lax.bitcast_convert_type((b[:, :w] >> 16) | (b[:, w:] & jnp.uint32(HI_HALF)), jnp.int32)


def _unpack_bf16_pairs(p):
    p = lax.bitcast_convert_type(p, jnp.uint32)
    lo = lax.bitcast_convert_type(p << 16, F32)
    hi = lax.bitcast_convert_type(p & jnp.uint32(HI_HALF), F32)
    return jnp.concatenate([lo, hi], axis=1)


def _silu(x):
    return x * jax.nn.sigmoid(x)


def _logaddexp(a, b):
    return jnp.maximum(a, b) + jnp.log(1.0 + jnp.exp(-jnp.abs(a - b)))


def _softplus(x):
    return jnp.maximum(x, 0.0) + jnp.log(1.0 + jnp.exp(-jnp.abs(x)))


def _ada_kernel(c_ref, w_ref, b_ref, o_ref):
    o_ref[0] = _hdot(_silu(c_ref[...]), w_ref[0]) + b_ref[0]


def _modulation(cvecs, w_ada, b_ada):
    tn = 1536
    out = pl.pallas_call(
        _ada_kernel,
        grid=(DEPTH, 6 * D_MODEL // tn),
        in_specs=[pl.BlockSpec((MOD_ROWS, D_MODEL), lambda l, j: (0, 0)),
                  pl.BlockSpec((1, D_MODEL, tn), lambda l, j: (l, 0, j)),
                  pl.BlockSpec((1, 1, tn), lambda l, j: (l, 0, j))],
        out_specs=pl.BlockSpec((1, MOD_ROWS, tn), lambda l, j: (l, 0, j)),
        out_shape=jax.ShapeDtypeStruct((DEPTH, MOD_ROWS, 6 * D_MODEL), F32),
        compiler_params=_cparams("arbitrary", "arbitrary"),
        name="ada_modulation",
    )(cvecs, w_ada, b_ada.reshape(DEPTH, 1, 6 * D_MODEL))
    return out.reshape(DEPTH, MOD_ROWS * 6, 1, D_MODEL)


def _rms(x):
    return x * lax.rsqrt(jnp.mean(x * x, axis=-1, keepdims=True) + EPS)


def _normmod_kernel(x_ref, g_ref, sc_ref, sh_ref, o_ref):
    h = (_rms(x_ref[...]) * g_ref[...]) * (1.0 + sc_ref[0]) + sh_ref[0]
    o_ref[...] = h.astype(o_ref.dtype)


def _normmod(x, g, mods, mod_row, tm, part_shift, part_scale):
    T = x.shape[0]
    return pl.pallas_call(
        _normmod_kernel,
        grid=(T // tm,),
        in_specs=[pl.BlockSpec((tm, D_MODEL), lambda i: (i, 0)),
                  pl.BlockSpec((1, D_MODEL), lambda i: (0, 0)),
                  pl.BlockSpec((1, 1, D_MODEL), lambda i: (mod_row(i * tm) * 6 + part_scale, 0, 0)),
                  pl.BlockSpec((1, 1, D_MODEL), lambda i: (mod_row(i * tm) * 6 + part_shift, 0, 0))],
        out_specs=pl.BlockSpec((tm, D_MODEL), lambda i: (i, 0)),
        out_shape=jax.ShapeDtypeStruct((T, D_MODEL), BF16),
        compiler_params=_cparams("parallel"),
        name="normmod",
    )(x, g.reshape(1, D_MODEL), mods, mods)


def _mm_kernel(a_ref, w_ref, o_ref):
    o_ref[...] = jnp.dot(a_ref[...], w_ref[...], preferred_element_type=F32).astype(o_ref.dtype)


def _matmul(a, w, tm, tn, out_dtype=F32):
    T, K = a.shape
    N = w.shape[1]
    return pl.pallas_call(
        _mm_kernel,
        grid=(N // tn, T // tm),
        in_specs=[pl.BlockSpec((tm, K), lambda j, i: (i, 0)),
                  pl.BlockSpec((K, tn), lambda j, i: (0, j))],
        out_specs=pl.BlockSpec((tm, tn), lambda j, i: (i, j)),
        out_shape=jax.ShapeDtypeStruct((T, N), out_dtype),
        compiler_params=_cparams("parallel", "parallel"),
        name="in_proj",
    )(a, w)


def _sgu_kernel(u_ref, v_ref, gn_ref, ws_ref, bs_ref, o_ref):
    rows = u_ref.shape[0]
    gw = BRANCH_W // SGU_GROUPS
    u = jax.nn.gelu(u_ref[...])
    v = (_rms(jax.nn.gelu(v_ref[...])) * gn_ref[...]).astype(BF16)
    for n in range(rows // SGU_CHUNK):
        r = slice(n * SGU_CHUNK, (n + 1) * SGU_CHUNK)
        for g in range(SGU_GROUPS):
            cs = slice(g * gw, (g + 1) * gw)
            s = jnp.dot(ws_ref[g], v[r, cs], preferred_element_type=F32) + bs_ref[:, cs]
            o_ref[r, cs] = (u[r, cs] * s).astype(o_ref.dtype)


def _sgu(z_sgu, g_norm, w_s, b_s, rows, row0, T):
    gw = BRANCH_W // SGU_GROUPS
    b_exp = jnp.repeat(b_s.T, gw, axis=1)
    i0 = row0 // rows
    return pl.pallas_call(
        _sgu_kernel,
        grid=(T // rows,),
        in_specs=[pl.BlockSpec((rows, BRANCH_W), lambda i: (i0 + i, 0)),
                  pl.BlockSpec((rows, BRANCH_W), lambda i: (i0 + i, 1)),
                  pl.BlockSpec((1, BRANCH_W), lambda i: (0, 0)),
                  pl.BlockSpec((SGU_GROUPS, SGU_CHUNK, SGU_CHUNK), lambda i: (0, 0, 0)),
                  pl.BlockSpec((SGU_CHUNK, BRANCH_W), lambda i: (0, 0))],
        out_specs=pl.BlockSpec((rows, BRANCH_W), lambda i: (i, 0)),
        out_shape=jax.ShapeDtypeStruct((T, BRANCH_W), BF16),
        compiler_params=_cparams("parallel"),
        name="sgu",
    )(z_sgu, z_sgu, g_norm.reshape(1, BRANCH_W), w_s.astype(BF16), b_exp)


def _order(reverse):
    p = np.arange(SCAN_C)
    return SCAN_C - 1 - p if reverse else p


def _gla_consts():
    C = SCAN_C
    nlev = int(math.log2(C))
    mats, masks = [], []
    for reverse in (False, True):
        p = _order(reverse)
        pt, pr = p[:, None], p[None, :]
        m_d, k_d = [], []
        for lev in range(nlev):
            w = C >> (lev + 1)
            parent = p // (2 * w)
            later = (p % (2 * w)) >= w
            anchor = (parent * 2 * w + w - 1)[:, None]
            m = np.where(later[:, None], (pr > anchor) & (pr <= pt), (pr > pt) & (pr <= anchor))
            m_d.append(m)
            k_d.append((parent[:, None] == parent[None, :]) & later[:, None] & ~later[None, :])
        m_d.append(pr <= pt)
        m_d.append(pr > pt)
        k_d.append(np.eye(C, dtype=bool))
        mats.append(np.concatenate(m_d, axis=0))
        masks.append(np.stack(k_d))
    return (np.stack(mats).astype(np.float32), np.stack(masks).astype(np.float32))


def _delta_consts():
    C = SCAN_C
    tri, sl, incl, strict = [], [], [], []
    for reverse in (False, True):
        p = _order(reverse)
        pt, pr = p[:, None], p[None, :]
        tri.append(np.concatenate([pr <= pt, pr > pt], axis=0))
        sl.append(np.concatenate([pt > pr, np.zeros((C, LANES - C), bool)], axis=1))
        incl.append(pr <= pt)
        strict.append(pr < pt)
    f = lambda a: np.stack(a).astype(np.float32)
    return f(tri), f(sl), f(incl), f(strict)


def _hgrn_kernel(qf_ref, ff_ref, vf_ref, qb_ref, fb_ref, vb_ref, lb_ref, mat_ref, msk_ref, lat_ref, *rest, has_state):
    if has_state:
        s0_ref, of_ref, ob_ref, sfin_ref, st_ref = rest
    else:
        of_ref, ob_ref, sfin_ref, st_ref = rest
    C = SCAN_C
    nlev = msk_ref.shape[1] - 1
    c = pl.program_id(1)
    last_c = pl.num_programs(1) - 1

    @pl.when(c == 0)
    def _():
        for d in range(2):
            for h in range(HG_HEADS):
                if has_state:
                    st_ref[d, h] = s0_ref[0, d, h].T
                else:
                    st_ref[d, h] = jnp.zeros((HG_DV, HG_DK), F32)

    q_refs, f_refs, v_refs, o_refs = (qf_ref, qb_ref), (ff_ref, fb_ref), (vf_ref, vb_ref), (of_ref, ob_ref)
    sub = qf_ref.shape[0] // C
    rows = lambda j: slice(j * C, (j + 1) * C)
    col = lambda h: slice(h * HG_DK, (h + 1) * HG_DK)
    parts = [(d, j) for d in range(2) for j in range(sub)]
    q, k, fac, att = {}, {}, {}, {}
    for d, j in parts:
        zf = f_refs[d][rows(j), :]
        t = jnp.exp(-jnp.abs(zf))
        log_sig = jnp.minimum(zf, 0.0) - jnp.log(1.0 + t)
        logf = _logaddexp(lb_ref[d:d + 1, :], lb_ref[2 + d:3 + d, :] + log_sig)
        k[d, j] = lb_ref[4 + d:5 + d, :] * (jnp.where(zf >= 0.0, t, 1.0) / (1.0 + t))
        q[d, j] = _silu(q_refs[d][rows(j), :]) * (HG_DK ** -0.5)
        fac[d, j] = jnp.exp2(_dot01(mat_ref[d], logf * LOG2E))
    later = {(d, i): lat_ref[d, i] > 0.5 for d in range(2) for i in range(nlev)}
    for d, j in parts:
        for h in range(HG_HEADS):
            qh, kh = q[d, j][:, col(h)], k[d, j][:, col(h)]
            acc = msk_ref[d, nlev] * _bdot_g(qh, kh, NT)
            for i in range(nlev):
                z = (jnp.where(later[d, i], qh, kh) * fac[d, j][i * C:(i + 1) * C, col(h)]).astype(BF16)
                acc = acc + msk_ref[d, i] * lax.dot_general(z, z, NT, preferred_element_type=F32)
            att[d, j, h] = acc
    chains = [(d, h) for d in range(2) for h in range(HG_HEADS)]
    for t in range(sub):
        jd = (t, sub - 1 - t)
        for d, h in chains:
            j = jd[d]
            eb = fac[d, j][nlev * C:(nlev + 1) * C, col(h)]
            o_refs[d][rows(j), col(h)] = (_bdot(att[d, j, h], v_refs[d][rows(j), col(h)])
                                          + _bdot_g(q[d, j][:, col(h)] * eb, st_ref[d, h], NT))
        for d, h in chains:
            j = jd[d]
            eb = fac[d, j][nlev * C:(nlev + 1) * C, col(h)]
            er = fac[d, j][(nlev + 1) * C:, col(h)]
            e_last = eb[C - 1:C] if d == 0 else eb[0:1]
            st_ref[d, h] = st_ref[d, h] * e_last + _bdot_g(v_refs[d][rows(j), col(h)], k[d, j][:, col(h)] * er, TN)

    @pl.when(c == last_c)
    def _():
        for d in range(2):
            for h in range(HG_HEADS):
                sfin_ref[0, d, h] = st_ref[d, h].T


def _hgrn(z_hg, lbp, s0, B, L, row0):
    n = L // SCAN_ROWS
    c0 = row0 // SCAN_ROWS
    mats, masks = _gla_consts()
    later = np.broadcast_to(masks[:, :-1].any(axis=3)[..., None], masks[:, :-1].shape[:3] + (HG_DK,)).astype(np.float32)
    mats = np.tile(mats, (1, 1, 3))
    blk = (SCAN_ROWS, HG_HEADS * HG_DK)
    fwd = lambda col: pl.BlockSpec(blk, lambda b, c: (c0 + b * n + c, col))
    bwd = lambda col: pl.BlockSpec(blk, lambda b, c: (c0 + b * n + n - 1 - c, col))
    st_blk = (1, 2, HG_HEADS, HG_DK, HG_DV)
    in_specs = [fwd(0), fwd(1), fwd(3), bwd(0), bwd(2), bwd(3),
                pl.BlockSpec(lbp.shape, lambda b, c: (0, 0)),
                pl.BlockSpec(mats.shape, lambda b, c: (0, 0, 0)),
                pl.BlockSpec(masks.shape, lambda b, c: (0, 0, 0, 0)),
                pl.BlockSpec(later.shape, lambda b, c: (0, 0, 0, 0))]
    args = [z_hg] * 6 + [lbp, jnp.asarray(mats, BF16), jnp.asarray(masks), jnp.asarray(later)]
    if s0 is not None:
        in_specs.append(pl.BlockSpec(st_blk, lambda b, c: (b, 0, 0, 0, 0)))
        args.append(s0)
    return pl.pallas_call(
        functools.partial(_hgrn_kernel, has_state=s0 is not None),
        grid=(B, n),
        in_specs=in_specs,
        out_specs=[pl.BlockSpec(blk, lambda b, c: (b * n + c, 0)),
                   pl.BlockSpec(blk, lambda b, c: (b * n + n - 1 - c, 0)),
                   pl.BlockSpec(st_blk, lambda b, c: (b, 0, 0, 0, 0))],
        out_shape=[jax.ShapeDtypeStruct((B * L, HG_HEADS * HG_DV), F32),
                   jax.ShapeDtypeStruct((B * L, HG_HEADS * HG_DV), F32),
                   jax.ShapeDtypeStruct((B,) + st_blk[1:], F32)],
        scratch_shapes=[pltpu.VMEM((2, HG_HEADS, HG_DV, HG_DK), F32)],
        compiler_params=_cparams("parallel", "arbitrary"),
        name="hgrn_scan",
    )(*args)


GD_NQ = GD_HEADS * GD_DK
GD_QKV = 2 * GD_NQ + GD_HEADS * GD_DV
HALO = 8


def _gdprep_kernel(x_ref, prev_ref, next_ref, w_ref, o_ref, *, tiles_per_seq):
    R = x_ref.shape[0]
    t = pl.program_id(0) % tiles_per_seq
    x = x_ref[...]
    prev_row = jnp.where(t == 0, 0.0, prev_ref[HALO - 1:HALO, :])
    next_row = jnp.where(t == tiles_per_seq - 1, 0.0, next_ref[0:1, :])
    row = lax.broadcasted_iota(jnp.int32, x.shape, 0)
    xm1 = jnp.where(row == 0, prev_row, pltpu.roll(x, 1, 0))
    xp1 = jnp.where(row == R - 1, next_row, pltpu.roll(x, R - 1, 0))
    y = _silu(w_ref[0:1, :] * xm1 + w_ref[1:2, :] * x + w_ref[2:3, :] * xp1)
    for j in range(2 * GD_HEADS):
        cs = slice(j * GD_DK, (j + 1) * GD_DK)
        seg = y[:, cs]
        seg = seg * lax.rsqrt(jnp.sum(seg * seg, axis=-1, keepdims=True) + EPS)
        if j < GD_HEADS:
            seg = seg * (GD_DK ** -0.5)
        o_ref[:, cs] = seg
    o_ref[:, 2 * GD_NQ:] = y[:, 2 * GD_NQ:]


def _gdprep(z_gd, conv_w, L, rows, row0, T):
    tps = L // rows
    hb = rows // HALO
    nhalo = z_gd.shape[0] // HALO
    i0 = row0 // rows
    return pl.pallas_call(
        functools.partial(_gdprep_kernel, tiles_per_seq=tps),
        grid=(T // rows,),
        in_specs=[pl.BlockSpec((rows, GD_QKV), lambda i: (i0 + i, 0)),
                  pl.BlockSpec((HALO, GD_QKV), lambda i: (jnp.maximum((i0 + i) * hb - 1, 0), 0)),
                  pl.BlockSpec((HALO, GD_QKV), lambda i: (jnp.minimum((i0 + i + 1) * hb, nhalo - 1), 0)),
                  pl.BlockSpec((3, GD_QKV), lambda i: (0, 0))],
        out_specs=pl.BlockSpec((rows, GD_QKV), lambda i: (i, 0)),
        out_shape=jax.ShapeDtypeStruct((T, GD_QKV), F32),
        compiler_params=_cparams("parallel"),
        name="gdn_prep",
    )(z_gd, z_gd, z_gd, conv_w)


def _gdn_kernel(xf_ref, abf_ref, xb_ref, abb_ref, par_ref, tri_ref, sl_ref, incl_ref, strict_ref, msk_ref, *rest,
                has_state):
    if has_state:
        s0_ref, of_ref, ob_ref, sfin_ref, st_ref = rest
    else:
        of_ref, ob_ref, sfin_ref, st_ref = rest
    C = SCAN_C
    c = pl.program_id(1)
    last_c = pl.num_programs(1) - 1
    nlev = msk_ref.shape[1] - 1

    @pl.when(c == 0)
    def _():
        for d in range(2):
            for h in range(GD_HEADS):
                if has_state:
                    st_ref[d, h] = s0_ref[0, d, h].T
                else:
                    st_ref[d, h] = jnp.zeros((GD_DV, GD_DK), F32)

    x_refs, ab_refs, o_refs = (xf_ref, xb_ref), (abf_ref, abb_ref), (of_ref, ob_ref)
    sub = xf_ref.shape[0] // C
    rows = lambda j: slice(j * C, (j + 1) * C)
    chains = [(d, h) for d in range(2) for h in range(GD_HEADS)]
    parts = [(d, j, h) for d in range(2) for j in range(sub) for h in range(GD_HEADS)]
    q_of = lambda d, j, h: x_refs[d][rows(j), h * GD_DK:(h + 1) * GD_DK]
    k_of = lambda d, j, h: x_refs[d][rows(j), GD_NQ + h * GD_DK:GD_NQ + (h + 1) * GD_DK]
    v_of = lambda d, j, h: x_refs[d][rows(j), 2 * GD_NQ + h * GD_DV:2 * GD_NQ + (h + 1) * GD_DV]
    g_all, beta_all = [], []
    for d in range(2):
        ab = ab_refs[d][...]
        g_all.append(-jnp.exp(par_ref[0:1, :]) * _softplus(ab + par_ref[1:2, :]))
        beta_all.append(jax.nn.sigmoid(ab))

    def beta_of(d, j, h):
        lane = 2 * GD_HEADS + d * GD_HEADS + h
        return jnp.broadcast_to(beta_all[d][rows(j), lane:lane + 1], (C, LANES))

    decay, e_cum, e_rest, kb, a = {}, {}, {}, {}, {}
    for d, j, h in parts:
        lane = d * GD_HEADS + h
        g_b = jnp.broadcast_to(g_all[d][rows(j), lane:lane + 1], (C, LANES))
        sums = _dot01_sel(tri_ref[d], g_b, sl_ref[d])
        decay[d, j, h] = jnp.exp(sums[:C, :C])
        e_cum[d, j, h] = jnp.exp(sums[:C, LANES:])
        e_rest[d, j, h] = jnp.exp(sums[C:, LANES:])
    for d, j, h in parts:
        k = k_of(d, j, h)
        kb[d, j, h] = k * beta_of(d, j, h)
        a[d, j, h] = strict_ref[d] * decay[d, j, h] * _bdot_g(kb[d, j, h], k, NT)
    inv_m = {pt: -(msk_ref[pt[0], nlev - 1] * a[pt]) for pt in parts}
    for lev in range(nlev - 2, -1, -1):
        a_w = {pt: msk_ref[pt[0], lev] * a[pt] for pt in parts}
        p = {pt: a_w[pt] + _bdot(inv_m[pt], a_w[pt]) for pt in parts}
        inv_m = {pt: inv_m[pt] - p[pt] - _bdot(p[pt], inv_m[pt]) for pt in parts}
    sol, att = {}, {}
    for d, j, h in parts:
        rhs = jnp.concatenate([v_of(d, j, h) * beta_of(d, j, h), kb[d, j, h] * e_cum[d, j, h]], axis=1)
        sol[d, j, h] = rhs + _dot3(inv_m[d, j, h], rhs)
        att[d, j, h] = incl_ref[d] * decay[d, j, h] * _bdot_g(q_of(d, j, h), k_of(d, j, h), NT)
    for t in range(sub):
        jd = (t, sub - 1 - t)
        u = {}
        for d, h in chains:
            pt = (d, jd[d], h)
            u[d, h] = sol[pt][:, :GD_DV] - _bdot_g(sol[pt][:, GD_DV:], st_ref[d, h], NT)
        for d, h in chains:
            pt = (d, jd[d], h)
            o_refs[d][rows(jd[d]), h * GD_DV:(h + 1) * GD_DV] = (
                _bdot_g(q_of(*pt) * e_cum[pt], st_ref[d, h], NT) + _bdot(att[pt], u[d, h]))
        for d, h in chains:
            pt = (d, jd[d], h)
            e_last = e_cum[pt][C - 1:C] if d == 0 else e_cum[pt][0:1]
            st_ref[d, h] = st_ref[d, h] * e_last + _bdot_g(u[d, h], k_of(*pt) * e_rest[pt], TN)

    @pl.when(c == last_c)
    def _():
        for d in range(2):
            for h in range(GD_HEADS):
                sfin_ref[0, d, h] = st_ref[d, h].T


def _gdn(qkv, z_gd, par, s0, B, L, row0):
    n = L // SCAN_ROWS
    c0 = row0 // SCAN_ROWS
    tri, sl, incl, strict = (jnp.asarray(a) for a in _delta_consts())
    tri = jnp.tile(tri, (1, 1, 3)).astype(BF16)
    sl = sl.astype(BF16)
    masks = jnp.asarray(_gla_consts()[1])
    ab_col = (GD_QKV + GD_HEADS * GD_DV) // LANES
    xblk = (SCAN_ROWS, GD_QKV)
    ablk = (SCAN_ROWS, LANES)
    oblk = (SCAN_ROWS, GD_HEADS * GD_DV)
    st_blk = (1, 2, GD_HEADS, GD_DK, GD_DV)
    fwd = lambda b, c: b * n + c
    bwd = lambda b, c: b * n + n - 1 - c
    const3 = lambda a: pl.BlockSpec(a.shape, lambda b, c: (0, 0, 0))
    in_specs = [pl.BlockSpec(xblk, lambda b, c: (fwd(b, c), 0)),
                pl.BlockSpec(ablk, lambda b, c: (c0 + fwd(b, c), ab_col)),
                pl.BlockSpec(xblk, lambda b, c: (bwd(b, c), 0)),
                pl.BlockSpec(ablk, lambda b, c: (c0 + bwd(b, c), ab_col)),
                pl.BlockSpec(par.shape, lambda b, c: (0, 0)),
                const3(tri), const3(sl), const3(incl), const3(strict),
                pl.BlockSpec(masks.shape, lambda b, c: (0, 0, 0, 0))]
    args = [qkv, z_gd, qkv, z_gd, par, tri, sl, incl, strict, masks]
    if s0 is not None:
        in_specs.append(pl.BlockSpec(st_blk, lambda b, c: (b, 0, 0, 0, 0)))
        args.append(s0)
    return pl.pallas_call(
        functools.partial(_gdn_kernel, has_state=s0 is not None),
        grid=(B, n),
        in_specs=in_specs,
        out_specs=[pl.BlockSpec(oblk, lambda b, c: (fwd(b, c), 0)),
                   pl.BlockSpec(oblk, lambda b, c: (bwd(b, c), 0)),
                   pl.BlockSpec(st_blk, lambda b, c: (b, 0, 0, 0, 0))],
        out_shape=[jax.ShapeDtypeStruct((B * L, GD_HEADS * GD_DV), F32),
                   jax.ShapeDtypeStruct((B * L, GD_HEADS * GD_DV), F32),
                   jax.ShapeDtypeStruct((B,) + st_blk[1:], F32)],
        scratch_shapes=[pltpu.VMEM((2, GD_HEADS, GD_DV, GD_DK), F32)],
        compiler_params=_cparams("parallel", "arbitrary"),
        name="gdn_scan",
    )(*args)


NA_W = NA_HEADS * NA_DH


def _softmax_pv(s, v):
    m = jnp.max(s, axis=-1, keepdims=True)
    e = jnp.exp(s - m)
    den = jnp.sum(e, axis=-1, keepdims=True)
    return jnp.dot(e.astype(BF16), v, preferred_element_type=F32) / den


def _ctx_attn_kernel(q_ref, k_ref, v_ref, o_ref):
    for h in range(NA_HEADS):
        cs = slice(h * NA_DH, (h + 1) * NA_DH)
        q = (q_ref[:, cs] * (NA_DH ** -0.5)).astype(BF16)
        s = lax.dot_general(q, k_ref[:, cs].astype(BF16), NT, preferred_element_type=F32)
        o_ref[:, cs] = _softmax_pv(s, v_ref[:, cs].astype(BF16)).astype(o_ref.dtype)


def _ctx_attn(z_na, B, L, row0):
    blk = (L, NA_W)
    b0 = row0 // L
    return pl.pallas_call(
        _ctx_attn_kernel,
        grid=(B,),
        in_specs=[pl.BlockSpec(blk, lambda b: (b0 + b, 0)),
                  pl.BlockSpec(blk, lambda b: (b0 + b, 1)),
                  pl.BlockSpec(blk, lambda b: (b0 + b, 2))],
        out_specs=pl.BlockSpec(blk, lambda b: (b, 0)),
        out_shape=jax.ShapeDtypeStruct((B * L, NA_W), BF16),
        compiler_params=_cparams("parallel"),
        name="ctx_attn",
    )(z_na, z_na, z_na)


def _na_kernel(q_ref, *rest, n_kblk, nkeys_nb):
    k_refs = rest[:n_kblk]
    v_refs = rest[n_kblk:2 * n_kblk]
    kc_ref, vc_ref, bias_ref, o_ref, kbuf, vbuf = rest[2 * n_kblk:]
    qb = q_ref.shape[0]
    for i in range(n_kblk):
        kbuf[i * qb:(i + 1) * qb, :] = k_refs[i][...].astype(BF16)
        vbuf[i * qb:(i + 1) * qb, :] = v_refs[i][...].astype(BF16)
    kbuf[nkeys_nb:, :] = kc_ref[0].astype(BF16)
    vbuf[nkeys_nb:, :] = vc_ref[0].astype(BF16)
    for h in range(NA_HEADS):
        cs = slice(h * NA_DH, (h + 1) * NA_DH)
        q = (q_ref[:, cs] * (NA_DH ** -0.5)).astype(BF16)
        s = lax.dot_general(q, kbuf[:, cs], NT, preferred_element_type=F32)
        s_nb = s[:, :nkeys_nb] + bias_ref[0, h]
        s_cx = s[:, nkeys_nb:]
        m = jnp.maximum(jnp.max(s_nb, axis=-1, keepdims=True), jnp.max(s_cx, axis=-1, keepdims=True))
        e_nb = jnp.exp(s_nb - m)
        e_cx = jnp.exp(s_cx - m)
        den = jnp.sum(e_nb, axis=-1, keepdims=True) + jnp.sum(e_cx, axis=-1, keepdims=True)
        pv = (jnp.dot(e_nb.astype(BF16), vbuf[:nkeys_nb, cs], preferred_element_type=F32)
              + jnp.dot(e_cx.astype(BF16), vbuf[nkeys_nb:, cs], preferred_element_type=F32))
        o_ref[:, cs] = (pv / den).astype(o_ref.dtype)


def _na_bias(rpb, rows):
    qr, kr, col = np.arange(NA_QROWS), np.arange(NA_KROWS), np.arange(GRID_W)
    nblk = rows // NA_QROWS
    ndr, ndc = 2 * NA_KH - 1, 2 * NA_KW - 1
    sel_r, row_ok = [], []
    for m in (0, 1, nblk - 1):
        r = (NA_QROWS * m + qr)[:, None]
        start = np.clip(NA_QROWS * m - NA_KH // 2, 0, rows - NA_KROWS)
        kra = (start + kr)[None, :]
        r0 = np.clip(r - NA_KH // 2, 0, rows - NA_KH)
        row_ok.append((kra >= r0) & (kra < r0 + NA_KH))
        dr = np.clip(kra - r + NA_KH - 1, 0, ndr - 1)
        sel_r.append(dr[..., None] == np.arange(ndr))
    sel_r = np.stack(sel_r).astype(np.float32)
    row_ok = np.stack(row_ok)
    col_start = np.clip(col - NA_KW // 2, 0, GRID_W - NA_KW)[:, None]
    col_ok = (col[None, :] >= col_start) & (col[None, :] < col_start + NA_KW)
    dc = np.clip(col[None, :] - col[:, None], -(NA_KW - 1), NA_KW - 1) + NA_KW - 1
    sel_c = (dc[..., None] == np.arange(ndc)).astype(np.float32)
    bias = jnp.einsum('hab,vqka,xyb->vhqxky', rpb.astype(F32), sel_r, sel_c, precision=HI)
    ok = row_ok[:, None, :, None, :, None] & col_ok[None, None, None, :, None, :]
    bias = jnp.where(ok, bias, -jnp.inf)
    return bias.reshape(3, NA_HEADS, NA_QROWS * GRID_W, NA_KROWS * GRID_W)


def _na_attn(z_na, k_ctx, v_ctx, bias, B, S, row0):
    rows = S // GRID_W
    qb = NA_QROWS * GRID_W
    m0 = row0 // qb
    nblk = rows // NA_QROWS
    n_kblk = NA_KROWS // NA_QROWS
    lc = k_ctx.shape[1]
    nkeys_nb = NA_KROWS * GRID_W
    kstart = lambda m: jnp.clip(m - 1, 0, nblk - n_kblk)
    variant = lambda m: jnp.where(m == 0, 0, jnp.where(m == nblk - 1, 2, 1))
    kv_specs = lambda col: [pl.BlockSpec((qb, NA_W), functools.partial(
        lambda b, m, i, col: (m0 + b * nblk + kstart(m) + i, col), i=i, col=col)) for i in range(n_kblk)]
    return pl.pallas_call(
        functools.partial(_na_kernel, n_kblk=n_kblk, nkeys_nb=nkeys_nb),
        grid=(B, nblk),
        in_specs=([pl.BlockSpec((qb, NA_W), lambda b, m: (m0 + b * nblk + m, 0))] + kv_specs(1) + kv_specs(2)
                  + [pl.BlockSpec((1, lc, NA_W), lambda b, m: (b, 0, 0)),
                     pl.BlockSpec((1, lc, NA_W), lambda b, m: (b, 0, 0)),
                     pl.BlockSpec((1,) + bias.shape[1:], lambda b, m: (variant(m), 0, 0, 0))]),
        out_specs=pl.BlockSpec((qb, NA_W), lambda b, m: (b * nblk + m, 0)),
        out_shape=jax.ShapeDtypeStruct((B * S, NA_W), BF16),
        scratch_shapes=[pltpu.VMEM((nkeys_nb + lc, NA_W), BF16), pltpu.VMEM((nkeys_nb + lc, NA_W), BF16)],
        compiler_params=_cparams("parallel", "arbitrary"),
        name="na_attn",
    )(z_na, *([z_na] * (2 * n_kblk)), k_ctx, v_ctx, bias)


def _head_rms(o, g_row):
    parts = []
    for h in range(o.shape[1] // LANES):
        parts.append(_rms(o[:, h * LANES:(h + 1) * LANES]) * g_row)
    return jnp.concatenate(parts, axis=1)


def _merge_kernel(x_ref, oa_ref, hf_ref, hb_ref, hg_ref, gf_ref, gb_ref, gg_ref, od_ref, mg_ref,
                  hn_ref, gn_ref, wb_ref, wo_ref, g1_ref, *rest):
    o_ref = rest[-1]
    o_b = _head_rms(hf_ref[...] + hb_ref[...], hn_ref[...]) * _silu(hg_ref[...])
    o_c = _head_rms(gf_ref[...] + gb_ref[...], gn_ref[...]) * _silu(gg_ref[...])
    branches = (oa_ref[...], o_b.astype(BF16), o_c.astype(BF16), od_ref[...])
    merged = None
    for n_, o_n in enumerate(branches):
        gate = jax.nn.sigmoid(mg_ref[:, n_ * D_MODEL:(n_ + 1) * D_MODEL].astype(F32))
        term = gate * jnp.dot(o_n, wb_ref[n_], preferred_element_type=F32)
        merged = term if merged is None else merged + term
    mix = jnp.dot(merged.astype(BF16), wo_ref[...], preferred_element_type=F32)
    o_ref[...] = x_ref[...] + g1_ref[0] * mix


def _merge(x, o_a, o_hf, o_hb, z_hg, o_gf, o_gb, z_gd, o_d, z_mg, hg_onorm, gd_onorm, wb, wo, mods, mod_row, tm,
           row0, T, x_acc):
    i0 = row0 // tm
    glob = lambda w, col=0: pl.BlockSpec((tm, w), lambda i: (i0 + i, col))
    loc = lambda w: pl.BlockSpec((tm, w), lambda i: (i, 0))
    full = lambda a: pl.BlockSpec(a.shape, lambda i: (0,) * a.ndim)
    hn = hg_onorm.reshape(1, HG_DV)
    gn = gd_onorm.reshape(1, GD_DV)
    in_specs = [glob(D_MODEL), loc(BRANCH_W), loc(BRANCH_W), loc(BRANCH_W), glob(BRANCH_W, 4),
                loc(BRANCH_W), loc(BRANCH_W), glob(BRANCH_W, GD_QKV // BRANCH_W), loc(BRANCH_W),
                glob(N_BRANCH * D_MODEL), full(hn), full(gn), full(wb), full(wo),
                pl.BlockSpec((1, 1, D_MODEL), lambda i: (mod_row(row0 + i * tm) * 6 + 2, 0, 0))]
    args = [x, o_a, o_hf, o_hb, z_hg, o_gf, o_gb, z_gd, o_d, z_mg, hn, gn, wb, wo, mods]
    aliases = {}
    if x_acc is not None:
        in_specs.append(pl.BlockSpec(memory_space=pl.ANY))
        aliases = {len(args): 0}
        args.append(x_acc)
    return pl.pallas_call(
        _merge_kernel,
        grid=(T // tm,),
        in_specs=in_specs,
        out_specs=glob(D_MODEL),
        out_shape=jax.ShapeDtypeStruct(x.shape, F32),
        input_output_aliases=aliases,
        compiler_params=_cparams("parallel"),
        name="merge",
    )(*args)


SC_CORES = 2
SC_SUBCORES = 16
SC_WIN = 64


def _sc_gather(table, idx):
    V, D = table.shape
    N = idx.shape[0]
    nw = SC_CORES * SC_SUBCORES
    per_w = N // nw
    n_win = per_w // SC_WIN
    assert per_w * nw == N and n_win * SC_WIN == per_w
    mesh = plsc.VectorSubcoreMesh(core_axis_name="c", subcore_axis_name="s")

    @functools.partial(
        pl.kernel, mesh=mesh,
        out_type=jax.ShapeDtypeStruct((N, D), table.dtype),
        scratch_types=[pltpu.VMEM((n_win, SC_WIN), jnp.int32),
                       pltpu.VMEM((SC_WIN, D), table.dtype),
                       pltpu.SemaphoreType.DMA],
    )
    def gather_rows(table_hbm, idx_hbm, out_hbm, idx_v, rows_v, sem):
        wid = lax.axis_index("s") * SC_CORES + lax.axis_index("c")
        pltpu.sync_copy(idx_hbm.at[wid], idx_v)

        @pl.loop(0, n_win)
        def _(w):
            pltpu.async_copy(table_hbm.at[idx_v.at[w]], rows_v, sem).wait()
            pltpu.sync_copy(rows_v, out_hbm.at[pl.ds(wid * per_w + w * SC_WIN, SC_WIN)])

    return gather_rows(table, idx.reshape(nw, n_win, SC_WIN))


def _router_kernel(x_ref, g_ref, sc_ref, sh_ref, wr_ref, br_ref, tri_ref, h_ref, e_ref, w_ref, r_ref, cnt_ref, run_ref,
                   *, tiles_per_range):
    @pl.when(pl.program_id(0) % tiles_per_range == 0)
    def _():
        run_ref[...] = jnp.zeros(run_ref.shape, F32)

    h = (_rms(x_ref[...]) * g_ref[...]) * (1.0 + sc_ref[0]) + sh_ref[0]
    h_ref[...] = _pack_bf16_pairs(h)
    logits = _hdot(h, wr_ref[...]) + br_ref[...]
    lane = lax.broadcasted_iota(jnp.int32, logits.shape, 1)
    e_out = jnp.zeros(logits.shape, jnp.int32)
    v_out = jnp.zeros(logits.shape, F32)
    onehot = jnp.zeros(logits.shape, F32)
    top0, picks = None, []
    for k in range(TOP_K):
        m = jnp.max(logits, axis=-1, keepdims=True)
        idx = jnp.min(jnp.where(logits == m, lane, LANES), axis=-1, keepdims=True)
        if k == 0:
            top0 = m
        picks.append(lane == idx)
        e_out = jnp.where(lane == k, idx, e_out)
        v_out = jnp.where(lane == k, jnp.exp(m - top0), v_out)
        onehot = jnp.where(picks[k], 1.0, onehot)
        logits = jnp.where(picks[k], -jnp.inf, logits)
    e_ref[...] = e_out
    w_ref[...] = v_out / jnp.sum(v_out, axis=-1, keepdims=True)
    before = run_ref[...] + jnp.dot(tri_ref[...], onehot.astype(BF16), preferred_element_type=F32)
    r_out = jnp.zeros(logits.shape, jnp.int32)
    for k in range(TOP_K):
        rank = jnp.sum(jnp.where(picks[k], before, 0.0), axis=-1, keepdims=True)
        r_out = jnp.where(lane == k, rank.astype(jnp.int32), r_out)
    r_ref[...] = r_out
    run_ref[...] = run_ref[...] + jnp.sum(onehot, axis=0, keepdims=True)
    cnt_ref[0] = run_ref[...]


def _router(x, g, mods, mod_row, w_router, b_router, tm, n_ranges):
    T = x.shape[0]
    wr = jnp.zeros((D_MODEL, LANES), F32).at[:, :N_EXP].set(w_router)
    br = jnp.full((1, LANES), -jnp.inf, F32).at[0, :N_EXP].set(b_router)
    tri = jnp.asarray(np.tril(np.ones((tm, tm), np.float32), -1), BF16)
    tiles_per_range = T // n_ranges // tm
    row = lambda w: pl.BlockSpec((tm, w), lambda i: (i, 0))
    return pl.pallas_call(
        functools.partial(_router_kernel, tiles_per_range=tiles_per_range),
        grid=(T // tm,),
        in_specs=[row(D_MODEL),
                  pl.BlockSpec((1, D_MODEL), lambda i: (0, 0)),
                  pl.BlockSpec((1, 1, D_MODEL), lambda i: (mod_row(i * tm) * 6 + 4, 0, 0)),
                  pl.BlockSpec((1, 1, D_MODEL), lambda i: (mod_row(i * tm) * 6 + 3, 0, 0)),
                  pl.BlockSpec((D_MODEL, LANES), lambda i: (0, 0)),
                  pl.BlockSpec((1, LANES), lambda i: (0, 0)),
                  pl.BlockSpec((tm, tm), lambda i: (0, 0))],
        out_specs=[row(D_MODEL // 2), row(LANES), row(LANES), row(LANES),
                   pl.BlockSpec((1, 1, LANES), lambda i: (i // tiles_per_range, 0, 0))],
        out_shape=[jax.ShapeDtypeStruct((T, D_MODEL // 2), jnp.int32),
                   jax.ShapeDtypeStruct((T, LANES), jnp.int32),
                   jax.ShapeDtypeStruct((T, LANES), F32),
                   jax.ShapeDtypeStruct((T, LANES), jnp.int32),
                   jax.ShapeDtypeStruct((n_ranges, 1, LANES), F32)],
        scratch_shapes=[pltpu.VMEM((1, LANES), F32)],
        compiler_params=_cparams("arbitrary"),
        name="router",
    )(x, g.reshape(1, D_MODEL), mods, mods, wr, br, tri)


def _expert_kernel(blk_e_ref, x_ref, wgu_ref, bgu_ref, wdn_ref, bdn_ref, o_ref, wgu_bf, wdn_bf):
    i = pl.program_id(0)
    n_used = blk_e_ref[pl.num_programs(0)]
    new_expert = jnp.logical_or(i == 0, blk_e_ref[i] != blk_e_ref[jnp.maximum(i - 1, 0)])

    @pl.when(jnp.logical_and(new_expert, i < n_used))
    def _():
        wgu_bf[...] = wgu_ref[0].astype(BF16)
        wdn_bf[...] = wdn_ref[0].astype(BF16)

    @pl.when(i < n_used)
    def _():
        x = _unpack_bf16_pairs(x_ref[...]).astype(BF16)
        gu = jnp.dot(x, wgu_bf[...], preferred_element_type=F32) + bgu_ref[0]
        a = jnp.minimum(gu[:, :D_FF], SWIGLU_LIMIT)
        lin = jnp.clip(gu[:, D_FF:], -SWIGLU_LIMIT, SWIGLU_LIMIT)
        y = a * jax.nn.sigmoid(SWIGLU_ALPHA * a) * (lin + 1.0)
        o_ref[...] = _pack_bf16_pairs(jnp.dot(y.astype(BF16), wdn_bf[...], preferred_element_type=F32) + bdn_ref[0])

    @pl.when(i >= n_used)
    def _():
        o_ref[...] = jnp.zeros(o_ref.shape, o_ref.dtype)


def _experts(xb, blk_e, w_gu, b_gu, w_dn, b_dn, layer):
    n_pad = xb.shape[0]
    n_blocks = n_pad // MOE_BLOCK
    e0 = layer * N_EXP
    w_gu = w_gu.reshape(DEPTH * N_EXP, D_MODEL, 2 * D_FF)
    w_dn = w_dn.reshape(DEPTH * N_EXP, D_FF, D_MODEL)
    grid_spec = pltpu.PrefetchScalarGridSpec(
        num_scalar_prefetch=1,
        grid=(n_blocks,),
        in_specs=[pl.BlockSpec((MOE_BLOCK, D_MODEL // 2), lambda i, e: (i, 0)),
                  pl.BlockSpec((1, D_MODEL, 2 * D_FF), lambda i, e: (e0 + e[i], 0, 0)),
                  pl.BlockSpec((1, 1, 2 * D_FF), lambda i, e: (e0 + e[i], 0, 0)),
                  pl.BlockSpec((1, D_FF, D_MODEL), lambda i, e: (e0 + e[i], 0, 0)),
                  pl.BlockSpec((1, 1, D_MODEL), lambda i, e: (e0 + e[i], 0, 0))],
        out_specs=pl.BlockSpec((MOE_BLOCK, D_MODEL // 2), lambda i, e: (i, 0)),
        scratch_shapes=[pltpu.VMEM((D_MODEL, 2 * D_FF), BF16), pltpu.VMEM((D_FF, D_MODEL), BF16)],
    )
    return pl.pallas_call(
        _expert_kernel,
        grid_spec=grid_spec,
        out_shape=jax.ShapeDtypeStruct((n_pad, D_MODEL // 2), jnp.int32),
        compiler_params=_cparams("arbitrary"),
        name="experts",
    )(blk_e, xb, w_gu, b_gu.reshape(DEPTH * N_EXP, 1, 2 * D_FF), w_dn, b_dn.reshape(DEPTH * N_EXP, 1, D_MODEL))


def _combine_kernel(x_ref, y_ref, w_ref, g2_ref, nf_ref, *rest, final_norm):
    o_ref = rest[-1]
    acc = None
    for k in range(TOP_K):
        term = _unpack_bf16_pairs(y_ref[k]) * w_ref[:, k:k + 1]
        acc = term if acc is None else acc + term
    x = x_ref[...] + g2_ref[0] * acc
    if final_norm:
        x = _rms(x) * nf_ref[...]
    o_ref[...] = x


def _combine(x, yg, wts, mods, mod_row, norm_f, final_norm, tm, row0, x_acc):
    T = yg.shape[1]
    i0 = row0 // tm
    glob = lambda w: pl.BlockSpec((tm, w), lambda i: (i0 + i, 0))
    in_specs = [glob(D_MODEL), pl.BlockSpec((TOP_K, tm, D_MODEL // 2), lambda i: (0, i, 0)), glob(LANES),
                pl.BlockSpec((1, 1, D_MODEL), lambda i: (mod_row(row0 + i * tm) * 6 + 5, 0, 0)),
                pl.BlockSpec((1, D_MODEL), lambda i: (0, 0))]
    args = [x, yg, wts, mods, norm_f.reshape(1, D_MODEL)]
    aliases = {}
    if x_acc is not None:
        in_specs.append(pl.BlockSpec(memory_space=pl.ANY))
        aliases = {len(args): 0}
        args.append(x_acc)
    return pl.pallas_call(
        functools.partial(_combine_kernel, final_norm=final_norm),
        grid=(T // tm,),
        in_specs=in_specs,
        out_specs=glob(D_MODEL),
        out_shape=jax.ShapeDtypeStruct(x.shape, F32),
        input_output_aliases=aliases,
        compiler_params=_cparams("parallel"),
        name="combine",
    )(*args)


def _route(top_e, rank, counts, T):
    n_assign = T * TOP_K
    n_blocks = n_assign // MOE_BLOCK + N_EXP
    e_flat = top_e.reshape(n_assign)
    onehot = e_flat[:, None] == jnp.arange(N_EXP, dtype=jnp.int32)[None, :]
    start = jnp.cumsum(counts) - counts
    padded = (counts + MOE_BLOCK - 1) // MOE_BLOCK * MOE_BLOCK
    pad_end = jnp.cumsum(padded)
    pad_start = pad_end - padded
    iota = jnp.arange(n_assign, dtype=jnp.int32)
    _, order = lax.sort((e_flat, iota), num_keys=1, is_stable=True)
    pos = rank.reshape(n_assign) + jnp.sum(jnp.where(onehot, pad_start[None, :], 0), axis=1)
    blk_first = jnp.arange(n_blocks, dtype=jnp.int32) * MOE_BLOCK
    blk_e = jnp.minimum(jnp.sum(pad_end[None, :] <= blk_first[:, None], axis=1), N_EXP - 1).astype(jnp.int32)
    n_used = (pad_end[-1] // MOE_BLOCK).astype(jnp.int32)
    r = blk_first[:, None] - pad_start[blk_e][:, None] + jnp.arange(MOE_BLOCK, dtype=jnp.int32)[None, :]
    valid = r < counts[blk_e][:, None]
    src = jnp.clip(start[blk_e][:, None] + r, 0, n_assign - 1)
    filler = (blk_first[:, None] + jnp.arange(MOE_BLOCK, dtype=jnp.int32)[None, :]) % T
    tok = jnp.where(valid, order[src] // TOP_K, filler).reshape(n_blocks * MOE_BLOCK).astype(jnp.int32)
    return tok, pos, jnp.concatenate([blk_e, n_used[None]])


def _prep_layer(l, w_in, sgu_w, w_branch, w_out, w_gu, w_dn, gd_A_log, gd_dt_bias, lb):
    offs = np.cumsum([0, BRANCH_W, BRANCH_W, 512, 512, 512, 512, 512, GD_QKV, 8, 8, 512, 3 * NA_W, N_BRANCH * D_MODEL])
    w = w_in[l]
    seg = lambda i, j: w[:, offs[i]:offs[j]]
    w_gd = jnp.concatenate([seg(7, 8), seg(10, 11), seg(8, 10),
                            jnp.zeros((D_MODEL, LANES - 4 * GD_HEADS), F32)], axis=1)
    par = jnp.zeros((2, LANES), F32)
    par = par.at[0, :2 * GD_HEADS].set(gd_A_log[l].reshape(-1)).at[1, :2 * GD_HEADS].set(gd_dt_bias[l].reshape(-1))
    lb_l = lb[:, l]
    return {
        'w_sgu': seg(0, 2).astype(BF16), 'w_hg': seg(2, 7).astype(BF16), 'w_gd': w_gd.astype(BF16),
        'w_na': seg(11, 12).astype(BF16), 'w_mg': seg(12, 13).astype(BF16),
        'sgu_w': sgu_w[l], 'wb': w_branch[l].astype(BF16), 'wo': w_out[l].astype(BF16),
        'w_gu': w_gu, 'w_dn': w_dn, 'layer': l, 'gd_par': par,
        'lbp': jnp.concatenate([jnp.log(lb_l), jnp.log1p(-lb_l), 1.0 - lb_l], axis=0),
    }


def _layer(x, groups, mods, mod_row, lw, p, norm_f, final_norm):
    T = x.shape[0]
    h = _normmod(x, p['norm1'], mods, mod_row, 1024, part_shift=0, part_scale=1)
    z_sgu = _matmul(h, lw['w_sgu'], 1024, 1024)
    z_hg = _matmul(h, lw['w_hg'], 1024, 1280)
    z_gd = _matmul(h, lw['w_gd'], 512, 2176)
    z_na = _matmul(h, lw['w_na'], 1024, 768)
    z_mg = _matmul(h, lw['w_mg'], 1024, 1024, out_dtype=BF16)

    x_mix, states = None, []
    for row0, B, L, ctx in groups:
        Tg = B * L
        o_a = _sgu(z_sgu, p['sgu_norm'], lw['sgu_w'], p['sgu_b'], 256, row0, Tg)
        s_hg0 = None if ctx is None else ctx[2]
        s_gd0 = None if ctx is None else ctx[3]
        o_hf, o_hb, s_hg = _hgrn(z_hg, lw['lbp'], s_hg0, B, L, row0)
        qkv = _gdprep(z_gd, p['gd_conv'], L, 256, row0, Tg)
        o_gf, o_gb, s_gd = _gdn(qkv, z_gd, lw['gd_par'], s_gd0, B, L, row0)
        if ctx is None:
            o_d = _ctx_attn(z_na, B, L, row0)
        else:
            o_d = _na_attn(z_na, ctx[0], ctx[1], _na_bias(p['na_rpb'], L // GRID_W), B, L, row0)
        x_mix = _merge(x, o_a, o_hf, o_hb, z_hg, o_gf, o_gb, z_gd, o_d, z_mg, p['hg_onorm'], p['gd_onorm'],
                       lw['wb'], lw['wo'], mods, mod_row, 256, row0, Tg, x_mix)
        states.append((s_hg, s_gd))
    x = x_mix

    h2, top_e, wts, rank, counts = _router(x, p['norm2'], mods, mod_row, p['w_router'], p['b_router'], 512, MOE_SPLIT)
    counts = counts[:, 0, :N_EXP].astype(jnp.int32)
    th = T // MOE_SPLIT
    x_out = None
    for j in range(MOE_SPLIT):
        r0 = j * th
        tok, pos, blk_e = _route(top_e[r0:r0 + th, :TOP_K], rank[r0:r0 + th, :TOP_K], counts[j], th)
        xb = _sc_gather(h2, tok + r0)
        yb = _experts(xb, blk_e, lw['w_gu'], p['b_gu'], lw['w_dn'], p['b_dn'], lw['layer'])
        yg = _sc_gather(yb, pos.reshape(th, TOP_K).T.reshape(-1)).reshape(TOP_K, th, D_MODEL // 2)
        x_out = _combine(x, yg, wts, mods, mod_row, norm_f, final_norm, 512, r0, x_out)
    return x_out, z_na, states


def kernel(x_prompt, x_sample, c, cache_na_k, cache_na_v, state_hgrn, state_gdn, c_ctx, w_ada, b_ada, norm1, norm2, norm_f, w_in, sgu_norm, sgu_w, sgu_b, hg_lb, hg_onorm, gd_conv, gd_A_log, gd_dt_bias, gd_onorm, na_rpb, w_branch, w_out, w_router, b_router, w_gu, b_gu, w_dn, b_dn):
    Bp, Lp, D = x_prompt.shape
    Bs, Ls, _ = x_sample.shape
    ctx_row = Bs
    cvecs = jnp.zeros((MOD_ROWS, D), F32).at[:Bs].set(c).at[ctx_row].set(c_ctx)
    mods = _modulation(cvecs, w_ada, b_ada)

    cs = jnp.cumsum(jax.nn.softmax(hg_lb.astype(F32), axis=1), axis=1)
    lb = cs - cs[:, :1]

    Tp, Ts = Bp * Lp, Bs * Ls
    x = jnp.concatenate([x_prompt.reshape(Tp, D), x_sample.reshape(Ts, D)], axis=0)
    mod_row = lambda r: jnp.where(r < Tp, ctx_row, (r - Tp) // Ls)
    ks_, vs_, hs_, gs_ = [], [], [], []
    for l in range(DEPTH):
        lw = _prep_layer(l, w_in, sgu_w, w_branch, w_out, w_gu, w_dn, gd_A_log, gd_dt_bias, lb)
        p = {'norm1': norm1[l], 'norm2': norm2[l], 'sgu_norm': sgu_norm[l], 'sgu_b': sgu_b[l],
             'gd_conv': gd_conv[l], 'hg_onorm': hg_onorm[l], 'gd_onorm': gd_onorm[l], 'na_rpb': na_rpb[l],
             'w_router': w_router[l], 'b_router': b_router[l], 'b_gu': b_gu, 'b_dn': b_dn}
        ctx = (cache_na_k[:, l].reshape(Bs, -1, NA_W), cache_na_v[:, l].reshape(Bs, -1, NA_W),
               state_hgrn[:, l], state_gdn[:, l])
        groups = [(0, Bp, Lp, None), (Tp, Bs, Ls, ctx)]
        x, z_na, states = _layer(x, groups, mods[l], mod_row, lw, p, norm_f, l == DEPTH - 1)
        ks_.append(z_na[:Tp, NA_W:2 * NA_W].reshape(Bp, Lp, NA_HEADS, NA_DH))
        vs_.append(z_na[:Tp, 2 * NA_W:].reshape(Bp, Lp, NA_HEADS, NA_DH))
        hs_.append(states[0][0])
        gs_.append(states[0][1])

    return (x[:Tp].reshape(Bp, Lp, D), x[Tp:].reshape(Bs, Ls, D),
            jnp.stack(ks_, axis=1), jnp.stack(vs_, axis=1), jnp.stack(hs_, axis=1), jnp.stack(gs_, axis=1))
```

```python
import functools
import math

import numpy as np
import jax
import jax.numpy as jnp
from jax import lax
from jax.experimental import pallas as pl
from jax.experimental.pallas import tpu as pltpu
from jax.experimental.pallas import tpu_sc as plsc

D_MODEL = 1024
DEPTH = 2
GRID_W = 64
BRANCH_W = 512
N_BRANCH = 4
SGU_CHUNK = 128
SGU_GROUPS = 4
HG_HEADS = 4
HG_DK = 128
HG_DV = 128
GD_HEADS = 4
GD_DK = 128
GD_DV = 128
NA_HEADS = 8
NA_DH = 64
NA_KH = 8
NA_KW = 16
N_EXP = 32
TOP_K = 4
D_FF = 1024
SWIGLU_LIMIT = 7.0
SWIGLU_ALPHA = 1.702
EPS = 1e-6

F32 = jnp.float32
BF16 = jnp.bfloat16
HI = lax.Precision.HIGHEST

LANES = 128
MOD_ROWS = 16
SCAN_C = 64
SCAN_ROWS = 4 * SCAN_C
MOE_BLOCK = 512
MOE_SPLIT = 2
NA_QROWS = 4
NA_KROWS = NA_QROWS + NA_KH
VMEM_LIMIT = 48 * 1024 * 1024

NT = (((1,), (1,)), ((), ()))
TN = (((0,), (0,)), ((), ()))


def _cparams(*sem):
    return pltpu.CompilerParams(dimension_semantics=sem, vmem_limit_bytes=VMEM_LIMIT)


def _bdot(a, b):
    return jnp.dot(a.astype(BF16), b.astype(BF16), preferred_element_type=F32)


def _bdot_g(a, b, dims):
    return lax.dot_general(a.astype(BF16), b.astype(BF16), dims, preferred_element_type=F32)


def _hdot(a, b):
    return jnp.dot(a, b, precision=HI, preferred_element_type=F32)


def _dot01(m3, x):
    hi = x.astype(BF16)
    r1 = x - hi.astype(F32)
    mid = r1.astype(BF16)
    lo = (r1 - mid.astype(F32)).astype(BF16)
    return jnp.dot(m3, jnp.concatenate([hi, mid, lo], axis=0), preferred_element_type=F32)


def _dot01_sel(m3, g, sel):
    hi = g.astype(BF16)
    r1 = g - hi.astype(F32)
    mid = r1.astype(BF16)
    lo = (r1 - mid.astype(F32)).astype(BF16)
    terms = [jnp.concatenate([t * sel, t], axis=1) for t in (hi, mid, lo)]
    return jnp.dot(m3, jnp.concatenate(terms, axis=0), preferred_element_type=F32)


def _dot3(a, b):
    ah = a.astype(BF16)
    al = (a - ah.astype(F32)).astype(BF16)
    bh = b.astype(BF16)
    bl = (b - bh.astype(F32)).astype(BF16)
    return jnp.dot(jnp.concatenate([al, ah, ah], axis=1), jnp.concatenate([bh, bl, bh], axis=0),
                   preferred_element_type=F32)


LOG2E = 1.4426950408889634
HI_HALF = 0xFFFF0000


def _pack_bf16_pairs(x):
    w = x.shape[1] // 2
    b = lax.bitcast_convert_type(x.astype(BF16).astype(F32), jnp.uint32)
    return lax.bitcast_convert_type((b[:, :w] >> 16) | (b[:, w:] & jnp.uint32(HI_HALF)), jnp.int32)


def _unpack_bf16_pairs(p):
    p = lax.bitcast_convert_type(p, jnp.uint32)
    lo = lax.bitcast_convert_type(p << 16, F32)
    hi = lax.bitcast_convert_type(p & jnp.uint32(HI_HALF), F32)
    return jnp.concatenate([lo, hi], axis=1)


def _silu(x):
    return x * jax.nn.sigmoid(x)


def _logaddexp(a, b):
    return jnp.maximum(a, b) + jnp.log(1.0 + jnp.exp(-jnp.abs(a - b)))


def _softplus(x):
    return jnp.maximum(x, 0.0) + jnp.log(1.0 + jnp.exp(-jnp.abs(x)))


def _ada_kernel(c_ref, w_ref, b_ref, o_ref):
    o_ref[0] = _hdot(_silu(c_ref[...]), w_ref[0]) + b_ref[0]


def _modulation(cvecs, w_ada, b_ada):
    tn = 1536
    out = pl.pallas_call(
        _ada_kernel,
        grid=(DEPTH, 6 * D_MODEL // tn),
        in_specs=[pl.BlockSpec((MOD_ROWS, D_MODEL), lambda l, j: (0, 0)),
                  pl.BlockSpec((1, D_MODEL, tn), lambda l, j: (l, 0, j)),
                  pl.BlockSpec((1, 1, tn), lambda l, j: (l, 0, j))],
        out_specs=pl.BlockSpec((1, MOD_ROWS, tn), lambda l, j: (l, 0, j)),
        out_shape=jax.ShapeDtypeStruct((DEPTH, MOD_ROWS, 6 * D_MODEL), F32),
        compiler_params=_cparams("arbitrary", "arbitrary"),
        name="ada_modulation",
    )(cvecs, w_ada, b_ada.reshape(DEPTH, 1, 6 * D_MODEL))
    return out.reshape(DEPTH, MOD_ROWS * 6, 1, D_MODEL)


def _rms(x):
    return x * lax.rsqrt(jnp.mean(x * x, axis=-1, keepdims=True) + EPS)


def _normmod_kernel(x_ref, g_ref, sc_ref, sh_ref, o_ref):
    h = (_rms(x_ref[...]) * g_ref[...]) * (1.0 + sc_ref[0]) + sh_ref[0]
    o_ref[...] = h.astype(o_ref.dtype)


def _normmod(x, g, mods, mod_row, tm, part_shift, part_scale):
    T = x.shape[0]
    return pl.pallas_call(
        _normmod_kernel,
        grid=(T // tm,),
        in_specs=[pl.BlockSpec((tm, D_MODEL), lambda i: (i, 0)),
                  pl.BlockSpec((1, D_MODEL), lambda i: (0, 0)),
                  pl.BlockSpec((1, 1, D_MODEL), lambda i: (mod_row(i * tm) * 6 + part_scale, 0, 0)),
                  pl.BlockSpec((1, 1, D_MODEL), lambda i: (mod_row(i * tm) * 6 + part_shift, 0, 0))],
        out_specs=pl.BlockSpec((tm, D_MODEL), lambda i: (i, 0)),
        out_shape=jax.ShapeDtypeStruct((T, D_MODEL), BF16),
        compiler_params=_cparams("parallel"),
        name="normmod",
    )(x, g.reshape(1, D_MODEL), mods, mods)


def _mm_kernel(a_ref, w_ref, o_ref):
    o_ref[...] = jnp.dot(a_ref[...], w_ref[...], preferred_element_type=F32).astype(o_ref.dtype)


def _matmul(a, w, tm, tn, out_dtype=F32):
    T, K = a.shape
    N = w.shape[1]
    return pl.pallas_call(
        _mm_kernel,
        grid=(N // tn, T // tm),
        in_specs=[pl.BlockSpec((tm, K), lambda j, i: (i, 0)),
                  pl.BlockSpec((K, tn), lambda j, i: (0, j))],
        out_specs=pl.BlockSpec((tm, tn), lambda j, i: (i, j)),
        out_shape=jax.ShapeDtypeStruct((T, N), out_dtype),
        compiler_params=_cparams("parallel", "parallel"),
        name="in_proj",
    )(a, w)


def _sgu_kernel(u_ref, v_ref, gn_ref, ws_ref, bs_ref, o_ref):
    rows = u_ref.shape[0]
    gw = BRANCH_W // SGU_GROUPS
    u = jax.nn.gelu(u_ref[...])
    v = (_rms(jax.nn.gelu(v_ref[...])) * gn_ref[...]).astype(BF16)
    for n in range(rows // SGU_CHUNK):
        r = slice(n * SGU_CHUNK, (n + 1) * SGU_CHUNK)
        for g in range(SGU_GROUPS):
            cs = slice(g * gw, (g + 1) * gw)
            s = jnp.dot(ws_ref[g], v[r, cs], preferred_element_type=F32) + bs_ref[:, cs]
            o_ref[r, cs] = (u[r, cs] * s).astype(o_ref.dtype)


def _sgu(z_sgu, g_norm, w_s, b_s, rows, row0, T):
    gw = BRANCH_W // SGU_GROUPS
    b_exp = jnp.repeat(b_s.T, gw, axis=1)
    i0 = row0 // rows
    return pl.pallas_call(
        _sgu_kernel,
        grid=(T // rows,),
        in_specs=[pl.BlockSpec((rows, BRANCH_W), lambda i: (i0 + i, 0)),
                  pl.BlockSpec((rows, BRANCH_W), lambda i: (i0 + i, 1)),
                  pl.BlockSpec((1, BRANCH_W), lambda i: (0, 0)),
                  pl.BlockSpec((SGU_GROUPS, SGU_CHUNK, SGU_CHUNK), lambda i: (0, 0, 0)),
                  pl.BlockSpec((SGU_CHUNK, BRANCH_W), lambda i: (0, 0))],
        out_specs=pl.BlockSpec((rows, BRANCH_W), lambda i: (i, 0)),
        out_shape=jax.ShapeDtypeStruct((T, BRANCH_W), BF16),
        compiler_params=_cparams("parallel"),
        name="sgu",
    )(z_sgu, z_sgu, g_norm.reshape(1, BRANCH_W), w_s.astype(BF16), b_exp)


def _order(reverse):
    p = np.arange(SCAN_C)
    return SCAN_C - 1 - p if reverse else p


def _gla_consts():
    C = SCAN_C
    nlev = int(math.log2(C))
    mats, masks = [], []
    for reverse in (False, True):
        p = _order(reverse)
        pt, pr = p[:, None], p[None, :]
        m_d, k_d = [], []
        for lev in range(nlev):
            w = C >> (lev + 1)
            parent = p // (2 * w)
            later = (p % (2 * w)) >= w
            anchor = (parent * 2 * w + w - 1)[:, None]
            m = np.where(later[:, None], (pr > anchor) & (pr <= pt), (pr > pt) & (pr <= anchor))
            m_d.append(m)
            k_d.append((parent[:, None] == parent[None, :]) & later[:, None] & ~later[None, :])
        m_d.append(pr <= pt)
        m_d.append(pr > pt)
        k_d.append(np.eye(C, dtype=bool))
        mats.append(np.concatenate(m_d, axis=0))
        masks.append(np.stack(k_d))
    return (np.stack(mats).astype(np.float32), np.stack(masks).astype(np.float32))


def _delta_consts():
    C = SCAN_C
    tri, sl, incl, strict = [], [], [], []
    for reverse in (False, True):
        p = _order(reverse)
        pt, pr = p[:, None], p[None, :]
        tri.append(np.concatenate([pr <= pt, pr > pt], axis=0))
        sl.append(np.concatenate([pt > pr, np.zeros((C, LANES - C), bool)], axis=1))
        incl.append(pr <= pt)
        strict.append(pr < pt)
    f = lambda a: np.stack(a).astype(np.float32)
    return f(tri), f(sl), f(incl), f(strict)


def _hgrn_kernel(qf_ref, ff_ref, vf_ref, qb_ref, fb_ref, vb_ref, lb_ref, mat_ref, msk_ref, lat_ref, *rest, has_state):
    if has_state:
        s0_ref, of_ref, ob_ref, sfin_ref, st_ref = rest
    else:
        of_ref, ob_ref, sfin_ref, st_ref = rest
    C = SCAN_C
    nlev = msk_ref.shape[1] - 1
    c = pl.program_id(1)
    last_c = pl.num_programs(1) - 1

    @pl.when(c == 0)
    def _():
        for d in range(2):
            for h in range(HG_HEADS):
                if has_state:
                    st_ref[d, h] = s0_ref[0, d, h].T
                else:
                    st_ref[d, h] = jnp.zeros((HG_DV, HG_DK), F32)

    q_refs, f_refs, v_refs, o_refs = (qf_ref, qb_ref), (ff_ref, fb_ref), (vf_ref, vb_ref), (of_ref, ob_ref)
    sub = qf_ref.shape[0] // C
    rows = lambda j: slice(j * C, (j + 1) * C)
    col = lambda h: slice(h * HG_DK, (h + 1) * HG_DK)
    parts = [(d, j) for d in range(2) for j in range(sub)]
    q, k, fac, att = {}, {}, {}, {}
    for d, j in parts:
        zf = f_refs[d][rows(j), :]
        t = jnp.exp(-jnp.abs(zf))
        log_sig = jnp.minimum(zf, 0.0) - jnp.log(1.0 + t)
        logf = _logaddexp(lb_ref[d:d + 1, :], lb_ref[2 + d:3 + d, :] + log_sig)
        k[d, j] = lb_ref[4 + d:5 + d, :] * (jnp.where(zf >= 0.0, t, 1.0) / (1.0 + t))
        q[d, j] = _silu(q_refs[d][rows(j), :]) * (HG_DK ** -0.5)
        fac[d, j] = jnp.exp2(_dot01(mat_ref[d], logf * LOG2E))
    later = {(d, i): lat_ref[d, i] > 0.5 for d in range(2) for i in range(nlev)}
    for d, j in parts:
        for h in range(HG_HEADS):
            qh, kh = q[d, j][:, col(h)], k[d, j][:, col(h)]
            acc = msk_ref[d, nlev] * _bdot_g(qh, kh, NT)
            for i in range(nlev):
                z = (jnp.where(later[d, i], qh, kh) * fac[d, j][i * C:(i + 1) * C, col(h)]).astype(BF16)
                acc = acc + msk_ref[d, i] * lax.dot_general(z, z, NT, preferred_element_type=F32)
            att[d, j, h] = acc
    chains = [(d, h) for d in range(2) for h in range(HG_HEADS)]
    for t in range(sub):
        jd = (t, sub - 1 - t)
        for d, h in chains:
            j = jd[d]
            eb = fac[d, j][nlev * C:(nlev + 1) * C, col(h)]
            o_refs[d][rows(j), col(h)] = (_bdot(att[d, j, h], v_refs[d][rows(j), col(h)])
                                          + _bdot_g(q[d, j][:, col(h)] * eb, st_ref[d, h], NT)).astype(BF16)
        for d, h in chains:
            j = jd[d]
            eb = fac[d, j][nlev * C:(nlev + 1) * C, col(h)]
            er = fac[d, j][(nlev + 1) * C:, col(h)]
            e_last = eb[C - 1:C] if d == 0 else eb[0:1]
            st_ref[d, h] = st_ref[d, h] * e_last + _bdot_g(v_refs[d][rows(j), col(h)], k[d, j][:, col(h)] * er, TN)

    @pl.when(c == last_c)
    def _():
        for d in range(2):
            for h in range(HG_HEADS):
                sfin_ref[0, d, h] = st_ref[d, h].T


def _hgrn(z_hg, lbp, s0, B, L, row0):
    n = L // SCAN_ROWS
    c0 = row0 // SCAN_ROWS
    mats, masks = _gla_consts()
    later = np.broadcast_to(masks[:, :-1].any(axis=3)[..., None], masks[:, :-1].shape[:3] + (HG_DK,)).astype(np.float32)
    mats = np.tile(mats, (1, 1, 3))
    blk = (SCAN_ROWS, HG_HEADS * HG_DK)
    fwd = lambda col: pl.BlockSpec(blk, lambda b, c: (c0 + b * n + c, col))
    bwd = lambda col: pl.BlockSpec(blk, lambda b, c: (c0 + b * n + n - 1 - c, col))
    st_blk = (1, 2, HG_HEADS, HG_DK, HG_DV)
    in_specs = [fwd(0), fwd(1), fwd(3), bwd(0), bwd(2), bwd(3),
                pl.BlockSpec(lbp.shape, lambda b, c: (0, 0)),
                pl.BlockSpec(mats.shape, lambda b, c: (0, 0, 0)),
                pl.BlockSpec(masks.shape, lambda b, c: (0, 0, 0, 0)),
                pl.BlockSpec(later.shape, lambda b, c: (0, 0, 0, 0))]
    args = [z_hg] * 6 + [lbp, jnp.asarray(mats, BF16), jnp.asarray(masks), jnp.asarray(later)]
    if s0 is not None:
        in_specs.append(pl.BlockSpec(st_blk, lambda b, c: (b, 0, 0, 0, 0)))
        args.append(s0)
    return pl.pallas_call(
        functools.partial(_hgrn_kernel, has_state=s0 is not None),
        grid=(B, n),
        in_specs=in_specs,
        out_specs=[pl.BlockSpec(blk, lambda b, c: (b * n + c, 0)),
                   pl.BlockSpec(blk, lambda b, c: (b * n + n - 1 - c, 0)),
                   pl.BlockSpec(st_blk, lambda b, c: (b, 0, 0, 0, 0))],
        out_shape=[jax.ShapeDtypeStruct((B * L, HG_HEADS * HG_DV), BF16),
                   jax.ShapeDtypeStruct((B * L, HG_HEADS * HG_DV), BF16),
                   jax.ShapeDtypeStruct((B,) + st_blk[1:], F32)],
        scratch_shapes=[pltpu.VMEM((2, HG_HEADS, HG_DV, HG_DK), F32)],
        compiler_params=_cparams("parallel", "arbitrary"),
        name="hgrn_scan",
    )(*args)


GD_NQ = GD_HEADS * GD_DK
GD_QKV = 2 * GD_NQ + GD_HEADS * GD_DV
HALO = 8


def _gdprep_kernel(x_ref, prev_ref, next_ref, w_ref, o_ref, *, tiles_per_seq):
    R = x_ref.shape[0]
    t = pl.program_id(0) % tiles_per_seq
    x = x_ref[...]
    prev_row = jnp.where(t == 0, 0.0, prev_ref[HALO - 1:HALO, :])
    next_row = jnp.where(t == tiles_per_seq - 1, 0.0, next_ref[0:1, :])
    row = lax.broadcasted_iota(jnp.int32, x.shape, 0)
    xm1 = jnp.where(row == 0, prev_row, pltpu.roll(x, 1, 0))
    xp1 = jnp.where(row == R - 1, next_row, pltpu.roll(x, R - 1, 0))
    y = _silu(w_ref[0:1, :] * xm1 + w_ref[1:2, :] * x + w_ref[2:3, :] * xp1)
    for j in range(2 * GD_HEADS):
        cs = slice(j * GD_DK, (j + 1) * GD_DK)
        seg = y[:, cs]
        seg = seg * lax.rsqrt(jnp.sum(seg * seg, axis=-1, keepdims=True) + EPS)
        if j < GD_HEADS:
            seg = seg * (GD_DK ** -0.5)
        o_ref[:, cs] = seg
    o_ref[:, 2 * GD_NQ:] = y[:, 2 * GD_NQ:]


def _gdprep(z_gd, conv_w, L, rows, row0, T):
    tps = L // rows
    hb = rows // HALO
    nhalo = z_gd.shape[0] // HALO
    i0 = row0 // rows
    return pl.pallas_call(
        functools.partial(_gdprep_kernel, tiles_per_seq=tps),
        grid=(T // rows,),
        in_specs=[pl.BlockSpec((rows, GD_QKV), lambda i: (i0 + i, 0)),
                  pl.BlockSpec((HALO, GD_QKV), lambda i: (jnp.maximum((i0 + i) * hb - 1, 0), 0)),
                  pl.BlockSpec((HALO, GD_QKV), lambda i: (jnp.minimum((i0 + i + 1) * hb, nhalo - 1), 0)),
                  pl.BlockSpec((3, GD_QKV), lambda i: (0, 0))],
        out_specs=pl.BlockSpec((rows, GD_QKV), lambda i: (i, 0)),
        out_shape=jax.ShapeDtypeStruct((T, GD_QKV), F32),
        compiler_params=_cparams("parallel"),
        name="gdn_prep",
    )(z_gd, z_gd, z_gd, conv_w)


def _gdn_kernel(xf_ref, abf_ref, xb_ref, abb_ref, par_ref, tri_ref, sl_ref, incl_ref, strict_ref, msk_ref, *rest,
                has_state):
    if has_state:
        s0_ref, of_ref, ob_ref, sfin_ref, st_ref = rest
    else:
        of_ref, ob_ref, sfin_ref, st_ref = rest
    C = SCAN_C
    c = pl.program_id(1)
    last_c = pl.num_programs(1) - 1
    nlev = msk_ref.shape[1] - 1

    @pl.when(c == 0)
    def _():
        for d in range(2):
            for h in range(GD_HEADS):
                if has_state:
                    st_ref[d, h] = s0_ref[0, d, h].T
                else:
                    st_ref[d, h] = jnp.zeros((GD_DV, GD_DK), F32)

    x_refs, ab_refs, o_refs = (xf_ref, xb_ref), (abf_ref, abb_ref), (of_ref, ob_ref)
    sub = xf_ref.shape[0] // C
    rows = lambda j: slice(j * C, (j + 1) * C)
    chains = [(d, h) for d in range(2) for h in range(GD_HEADS)]
    parts = [(d, j, h) for d in range(2) for j in range(sub) for h in range(GD_HEADS)]
    q_of = lambda d, j, h: x_refs[d][rows(j), h * GD_DK:(h + 1) * GD_DK]
    k_of = lambda d, j, h: x_refs[d][rows(j), GD_NQ + h * GD_DK:GD_NQ + (h + 1) * GD_DK]
    v_of = lambda d, j, h: x_refs[d][rows(j), 2 * GD_NQ + h * GD_DV:2 * GD_NQ + (h + 1) * GD_DV]
    g_all, beta_all = [], []
    for d in range(2):
        ab = ab_refs[d][...]
        g_all.append(-jnp.exp(par_ref[0:1, :]) * _softplus(ab + par_ref[1:2, :]))
        beta_all.append(jax.nn.sigmoid(ab))

    def beta_of(d, j, h):
        lane = 2 * GD_HEADS + d * GD_HEADS + h
        return jnp.broadcast_to(beta_all[d][rows(j), lane:lane + 1], (C, LANES))

    decay, e_cum, e_rest, kb, a = {}, {}, {}, {}, {}
    for d, j, h in parts:
        lane = d * GD_HEADS + h
        g_b = jnp.broadcast_to(g_all[d][rows(j), lane:lane + 1], (C, LANES))
        sums = _dot01_sel(tri_ref[d], g_b, sl_ref[d])
        decay[d, j, h] = jnp.exp(sums[:C, :C])
        e_cum[d, j, h] = jnp.exp(sums[:C, LANES:])
        e_rest[d, j, h] = jnp.exp(sums[C:, LANES:])
    for d, j, h in parts:
        k = k_of(d, j, h)
        kb[d, j, h] = k * beta_of(d, j, h)
        a[d, j, h] = strict_ref[d] * decay[d, j, h] * _bdot_g(kb[d, j, h], k, NT)
    inv_m = {pt: -(msk_ref[pt[0], nlev - 1] * a[pt]) for pt in parts}
    for lev in range(nlev - 2, -1, -1):
        a_w = {pt: msk_ref[pt[0], lev] * a[pt] for pt in parts}
        p = {pt: a_w[pt] + _bdot(inv_m[pt], a_w[pt]) for pt in parts}
        inv_m = {pt: inv_m[pt] - p[pt] - _bdot(p[pt], inv_m[pt]) for pt in parts}
    sol, att = {}, {}
    for d, j, h in parts:
        rhs = jnp.concatenate([v_of(d, j, h) * beta_of(d, j, h), kb[d, j, h] * e_cum[d, j, h]], axis=1)
        sol[d, j, h] = rhs + _dot3(inv_m[d, j, h], rhs)
        att[d, j, h] = incl_ref[d] * decay[d, j, h] * _bdot_g(q_of(d, j, h), k_of(d, j, h), NT)
    for t in range(sub):
        jd = (t, sub - 1 - t)
        u = {}
        for d, h in chains:
            pt = (d, jd[d], h)
            u[d, h] = sol[pt][:, :GD_DV] - _bdot_g(sol[pt][:, GD_DV:], st_ref[d, h], NT)
        for d, h in chains:
            pt = (d, jd[d], h)
            o_refs[d][rows(jd[d]), h * GD_DV:(h + 1) * GD_DV] = (
                _bdot_g(q_of(*pt) * e_cum[pt], st_ref[d, h], NT) + _bdot(att[pt], u[d, h])).astype(BF16)
        for d, h in chains:
            pt = (d, jd[d], h)
            e_last = e_cum[pt][C - 1:C] if d == 0 else e_cum[pt][0:1]
            st_ref[d, h] = st_ref[d, h] * e_last + _bdot_g(u[d, h], k_of(*pt) * e_rest[pt], TN)

    @pl.when(c == last_c)
    def _():
        for d in range(2):
            for h in range(GD_HEADS):
                sfin_ref[0, d, h] = st_ref[d, h].T


def _gdn(qkv, z_gd, par, s0, B, L, row0):
    n = L // SCAN_ROWS
    c0 = row0 // SCAN_ROWS
    tri, sl, incl, strict = (jnp.asarray(a) for a in _delta_consts())
    tri = jnp.tile(tri, (1, 1, 3)).astype(BF16)
    sl = sl.astype(BF16)
    masks = jnp.asarray(_gla_consts()[1])
    ab_col = (GD_QKV + GD_HEADS * GD_DV) // LANES
    xblk = (SCAN_ROWS, GD_QKV)
    ablk = (SCAN_ROWS, LANES)
    oblk = (SCAN_ROWS, GD_HEADS * GD_DV)
    st_blk = (1, 2, GD_HEADS, GD_DK, GD_DV)
    fwd = lambda b, c: b * n + c
    bwd = lambda b, c: b * n + n - 1 - c
    const3 = lambda a: pl.BlockSpec(a.shape, lambda b, c: (0, 0, 0))
    in_specs = [pl.BlockSpec(xblk, lambda b, c: (fwd(b, c), 0)),
                pl.BlockSpec(ablk, lambda b, c: (c0 + fwd(b, c), ab_col)),
                pl.BlockSpec(xblk, lambda b, c: (bwd(b, c), 0)),
                pl.BlockSpec(ablk, lambda b, c: (c0 + bwd(b, c), ab_col)),
                pl.BlockSpec(par.shape, lambda b, c: (0, 0)),
                const3(tri), const3(sl), const3(incl), const3(strict),
                pl.BlockSpec(masks.shape, lambda b, c: (0, 0, 0, 0))]
    args = [qkv, z_gd, qkv, z_gd, par, tri, sl, incl, strict, masks]
    if s0 is not None:
        in_specs.append(pl.BlockSpec(st_blk, lambda b, c: (b, 0, 0, 0, 0)))
        args.append(s0)
    return pl.pallas_call(
        functools.partial(_gdn_kernel, has_state=s0 is not None),
        grid=(B, n),
        in_specs=in_specs,
        out_specs=[pl.BlockSpec(oblk, lambda b, c: (fwd(b, c), 0)),
                   pl.BlockSpec(oblk, lambda b, c: (bwd(b, c), 0)),
                   pl.BlockSpec(st_blk, lambda b, c: (b, 0, 0, 0, 0))],
        out_shape=[jax.ShapeDtypeStruct((B * L, GD_HEADS * GD_DV), BF16),
                   jax.ShapeDtypeStruct((B * L, GD_HEADS * GD_DV), BF16),
                   jax.ShapeDtypeStruct((B,) + st_blk[1:], F32)],
        scratch_shapes=[pltpu.VMEM((2, GD_HEADS, GD_DV, GD_DK), F32)],
        compiler_params=_cparams("parallel", "arbitrary"),
        name="gdn_scan",
    )(*args)


NA_W = NA_HEADS * NA_DH


def _softmax_pv(s, v):
    m = jnp.max(s, axis=-1, keepdims=True)
    e = jnp.exp(s - m)
    den = jnp.sum(e, axis=-1, keepdims=True)
    return jnp.dot(e.astype(BF16), v, preferred_element_type=F32) / den


def _ctx_attn_kernel(q_ref, k_ref, v_ref, o_ref):
    for h in range(NA_HEADS):
        cs = slice(h * NA_DH, (h + 1) * NA_DH)
        q = (q_ref[:, cs] * (NA_DH ** -0.5)).astype(BF16)
        s = lax.dot_general(q, k_ref[:, cs].astype(BF16), NT, preferred_element_type=F32)
        o_ref[:, cs] = _softmax_pv(s, v_ref[:, cs].astype(BF16)).astype(o_ref.dtype)


def _ctx_attn(z_na, B, L, row0):
    blk = (L, NA_W)
    b0 = row0 // L
    return pl.pallas_call(
        _ctx_attn_kernel,
        grid=(B,),
        in_specs=[pl.BlockSpec(blk, lambda b: (b0 + b, 0)),
                  pl.BlockSpec(blk, lambda b: (b0 + b, 1)),
                  pl.BlockSpec(blk, lambda b: (b0 + b, 2))],
        out_specs=pl.BlockSpec(blk, lambda b: (b, 0)),
        out_shape=jax.ShapeDtypeStruct((B * L, NA_W), BF16),
        compiler_params=_cparams("parallel"),
        name="ctx_attn",
    )(z_na, z_na, z_na)


def _na_kernel(q_ref, *rest, n_kblk, nkeys_nb):
    k_refs = rest[:n_kblk]
    v_refs = rest[n_kblk:2 * n_kblk]
    kc_ref, vc_ref, bias_ref, o_ref, kbuf, vbuf = rest[2 * n_kblk:]
    qb = q_ref.shape[0]
    for i in range(n_kblk):
        kbuf[i * qb:(i + 1) * qb, :] = k_refs[i][...].astype(BF16)
        vbuf[i * qb:(i + 1) * qb, :] = v_refs[i][...].astype(BF16)
    kbuf[nkeys_nb:, :] = kc_ref[0].astype(BF16)
    vbuf[nkeys_nb:, :] = vc_ref[0].astype(BF16)
    for h in range(NA_HEADS):
        cs = slice(h * NA_DH, (h + 1) * NA_DH)
        q = (q_ref[:, cs] * (NA_DH ** -0.5)).astype(BF16)
        s = lax.dot_general(q, kbuf[:, cs], NT, preferred_element_type=F32)
        s_nb = s[:, :nkeys_nb] + bias_ref[0, h]
        s_cx = s[:, nkeys_nb:]
        m = jnp.maximum(jnp.max(s_nb, axis=-1, keepdims=True), jnp.max(s_cx, axis=-1, keepdims=True))
        e_nb = jnp.exp(s_nb - m)
        e_cx = jnp.exp(s_cx - m)
        den = jnp.sum(e_nb, axis=-1, keepdims=True) + jnp.sum(e_cx, axis=-1, keepdims=True)
        pv = (jnp.dot(e_nb.astype(BF16), vbuf[:nkeys_nb, cs], preferred_element_type=F32)
              + jnp.dot(e_cx.astype(BF16), vbuf[nkeys_nb:, cs], preferred_element_type=F32))
        o_ref[:, cs] = (pv / den).astype(o_ref.dtype)


def _na_bias(rpb, rows):
    qr, kr, col = np.arange(NA_QROWS), np.arange(NA_KROWS), np.arange(GRID_W)
    nblk = rows // NA_QROWS
    ndr, ndc = 2 * NA_KH - 1, 2 * NA_KW - 1
    sel_r, row_ok = [], []
    for m in (0, 1, nblk - 1):
        r = (NA_QROWS * m + qr)[:, None]
        start = np.clip(NA_QROWS * m - NA_KH // 2, 0, rows - NA_KROWS)
        kra = (start + kr)[None, :]
        r0 = np.clip(r - NA_KH // 2, 0, rows - NA_KH)
        row_ok.append((kra >= r0) & (kra < r0 + NA_KH))
        dr = np.clip(kra - r + NA_KH - 1, 0, ndr - 1)
        sel_r.append(dr[..., None] == np.arange(ndr))
    sel_r = np.stack(sel_r).astype(np.float32)
    row_ok = np.stack(row_ok)
    col_start = np.clip(col - NA_KW // 2, 0, GRID_W - NA_KW)[:, None]
    col_ok = (col[None, :] >= col_start) & (col[None, :] < col_start + NA_KW)
    dc = np.clip(col[None, :] - col[:, None], -(NA_KW - 1), NA_KW - 1) + NA_KW - 1
    sel_c = (dc[..., None] == np.arange(ndc)).astype(np.float32)
    bias = jnp.einsum('hab,vqka,xyb->vhqxky', rpb.astype(F32), sel_r, sel_c, precision=HI)
    ok = row_ok[:, None, :, None, :, None] & col_ok[None, None, None, :, None, :]
    bias = jnp.where(ok, bias, -jnp.inf)
    return bias.reshape(3, NA_HEADS, NA_QROWS * GRID_W, NA_KROWS * GRID_W)


def _na_attn(z_na, k_ctx, v_ctx, bias, B, S, row0):
    rows = S // GRID_W
    qb = NA_QROWS * GRID_W
    m0 = row0 // qb
    nblk = rows // NA_QROWS
    n_kblk = NA_KROWS // NA_QROWS
    lc = k_ctx.shape[1]
    nkeys_nb = NA_KROWS * GRID_W
    kstart = lambda m: jnp.clip(m - 1, 0, nblk - n_kblk)
    variant = lambda m: jnp.where(m == 0, 0, jnp.where(m == nblk - 1, 2, 1))
    kv_specs = lambda col: [pl.BlockSpec((qb, NA_W), functools.partial(
        lambda b, m, i, col: (m0 + b * nblk + kstart(m) + i, col), i=i, col=col)) for i in range(n_kblk)]
    return pl.pallas_call(
        functools.partial(_na_kernel, n_kblk=n_kblk, nkeys_nb=nkeys_nb),
        grid=(B, nblk),
        in_specs=([pl.BlockSpec((qb, NA_W), lambda b, m: (m0 + b * nblk + m, 0))] + kv_specs(1) + kv_specs(2)
                  + [pl.BlockSpec((1, lc, NA_W), lambda b, m: (b, 0, 0)),
                     pl.BlockSpec((1, lc, NA_W), lambda b, m: (b, 0, 0)),
                     pl.BlockSpec((1,) + bias.shape[1:], lambda b, m: (variant(m), 0, 0, 0))]),
        out_specs=pl.BlockSpec((qb, NA_W), lambda b, m: (b * nblk + m, 0)),
        out_shape=jax.ShapeDtypeStruct((B * S, NA_W), BF16),
        scratch_shapes=[pltpu.VMEM((nkeys_nb + lc, NA_W), BF16), pltpu.VMEM((nkeys_nb + lc, NA_W), BF16)],
        compiler_params=_cparams("parallel", "arbitrary"),
        name="na_attn",
    )(z_na, *([z_na] * (2 * n_kblk)), k_ctx, v_ctx, bias)


def _head_rms(o, g_row):
    parts = []
    for h in range(o.shape[1] // LANES):
        parts.append(_rms(o[:, h * LANES:(h + 1) * LANES]) * g_row)
    return jnp.concatenate(parts, axis=1)


def _merge_kernel(x_ref, oa_ref, hf_ref, hb_ref, hg_ref, gf_ref, gb_ref, gg_ref, od_ref, mg_ref,
                  hn_ref, gn_ref, wb_ref, wo_ref, g1_ref, *rest):
    o_ref = rest[-1]
    o_b = _head_rms(hf_ref[...].astype(F32) + hb_ref[...].astype(F32), hn_ref[...]) * _silu(hg_ref[...])
    o_c = _head_rms(gf_ref[...].astype(F32) + gb_ref[...].astype(F32), gn_ref[...]) * _silu(gg_ref[...])
    branches = (oa_ref[...], o_b.astype(BF16), o_c.astype(BF16), od_ref[...])
    merged = None
    for n_, o_n in enumerate(branches):
        gate = jax.nn.sigmoid(mg_ref[:, n_ * D_MODEL:(n_ + 1) * D_MODEL].astype(F32))
        term = gate * jnp.dot(o_n, wb_ref[n_], preferred_element_type=F32)
        merged = term if merged is None else merged + term
    mix = jnp.dot(merged.astype(BF16), wo_ref[...], preferred_element_type=F32)
    o_ref[...] = x_ref[...] + g1_ref[0] * mix


def _merge(x, o_a, o_hf, o_hb, z_hg, o_gf, o_gb, z_gd, o_d, z_mg, hg_onorm, gd_onorm, wb, wo, mods, mod_row, tm,
           row0, T, x_acc):
    i0 = row0 // tm
    glob = lambda w, col=0: pl.BlockSpec((tm, w), lambda i: (i0 + i, col))
    loc = lambda w: pl.BlockSpec((tm, w), lambda i: (i, 0))
    full = lambda a: pl.BlockSpec(a.shape, lambda i: (0,) * a.ndim)
    hn = hg_onorm.reshape(1, HG_DV)
    gn = gd_onorm.reshape(1, GD_DV)
    in_specs = [glob(D_MODEL), loc(BRANCH_W), loc(BRANCH_W), loc(BRANCH_W), glob(BRANCH_W, 4),
                loc(BRANCH_W), loc(BRANCH_W), glob(BRANCH_W, GD_QKV // BRANCH_W), loc(BRANCH_W),
                glob(N_BRANCH * D_MODEL), full(hn), full(gn), full(wb), full(wo),
                pl.BlockSpec((1, 1, D_MODEL), lambda i: (mod_row(row0 + i * tm) * 6 + 2, 0, 0))]
    args = [x, o_a, o_hf, o_hb, z_hg, o_gf, o_gb, z_gd, o_d, z_mg, hn, gn, wb, wo, mods]
    aliases = {}
    if x_acc is not None:
        in_specs.append(pl.BlockSpec(memory_space=pl.ANY))
        aliases = {len(args): 0}
        args.append(x_acc)
    return pl.pallas_call(
        _merge_kernel,
        grid=(T // tm,),
        in_specs=in_specs,
        out_specs=glob(D_MODEL),
        out_shape=jax.ShapeDtypeStruct(x.shape, F32),
        input_output_aliases=aliases,
        compiler_params=_cparams("parallel"),
        name="merge",
    )(*args)


SC_CORES = 2
SC_SUBCORES = 16
SC_WIN = 64


def _sc_gather(table, idx):
    V, D = table.shape
    N = idx.shape[0]
    nw = SC_CORES * SC_SUBCORES
    per_w = N // nw
    n_win = per_w // SC_WIN
    assert per_w * nw == N and n_win * SC_WIN == per_w
    mesh = plsc.VectorSubcoreMesh(core_axis_name="c", subcore_axis_name="s")

    @functools.partial(
        pl.kernel, mesh=mesh,
        out_type=jax.ShapeDtypeStruct((N, D), table.dtype),
        scratch_types=[pltpu.VMEM((n_win, SC_WIN), jnp.int32),
                       pltpu.VMEM((SC_WIN, D), table.dtype),
                       pltpu.SemaphoreType.DMA],
    )
    def gather_rows(table_hbm, idx_hbm, out_hbm, idx_v, rows_v, sem):
        wid = lax.axis_index("s") * SC_CORES + lax.axis_index("c")
        pltpu.sync_copy(idx_hbm.at[wid], idx_v)

        @pl.loop(0, n_win)
        def _(w):
            pltpu.async_copy(table_hbm.at[idx_v.at[w]], rows_v, sem).wait()
            pltpu.sync_copy(rows_v, out_hbm.at[pl.ds(wid * per_w + w * SC_WIN, SC_WIN)])

    return gather_rows(table, idx.reshape(nw, n_win, SC_WIN))


def _router_kernel(x_ref, g_ref, sc_ref, sh_ref, wr_ref, br_ref, tri_ref, h_ref, e_ref, w_ref, r_ref, cnt_ref, run_ref,
                   *, tiles_per_range):
    @pl.when(pl.program_id(0) % tiles_per_range == 0)
    def _():
        run_ref[...] = jnp.zeros(run_ref.shape, F32)

    h = (_rms(x_ref[...]) * g_ref[...]) * (1.0 + sc_ref[0]) + sh_ref[0]
    h_ref[...] = _pack_bf16_pairs(h)
    h_hi = h.astype(BF16)
    h_lo = (h - h_hi.astype(F32)).astype(BF16)
    w_hi = wr_ref[...].astype(BF16)
    w_lo = (wr_ref[...] - w_hi.astype(F32)).astype(BF16)
    dot = lambda a, b: jnp.dot(a, b, preferred_element_type=F32)
    logits = (dot(h_lo, w_hi) + dot(h_hi, w_lo)) + dot(h_hi, w_hi) + br_ref[...]
    lane = lax.broadcasted_iota(jnp.int32, logits.shape, 1)
    e_out = jnp.zeros(logits.shape, jnp.int32)
    v_out = jnp.zeros(logits.shape, F32)
    onehot = jnp.zeros(logits.shape, F32)
    top0, picks = None, []
    for k in range(TOP_K):
        m = jnp.max(logits, axis=-1, keepdims=True)
        idx = jnp.min(jnp.where(logits == m, lane, LANES), axis=-1, keepdims=True)
        if k == 0:
            top0 = m
        picks.append(lane == idx)
        e_out = jnp.where(lane == k, idx, e_out)
        v_out = jnp.where(lane == k, jnp.exp(m - top0), v_out)
        onehot = jnp.where(picks[k], 1.0, onehot)
        logits = jnp.where(picks[k], -jnp.inf, logits)
    e_ref[...] = e_out
    w_ref[...] = v_out / jnp.sum(v_out, axis=-1, keepdims=True)
    before = run_ref[...] + jnp.dot(tri_ref[...], onehot.astype(BF16), preferred_element_type=F32)
    r_out = jnp.zeros(logits.shape, jnp.int32)
    for k in range(TOP_K):
        rank = jnp.sum(jnp.where(picks[k], before, 0.0), axis=-1, keepdims=True)
        r_out = jnp.where(lane == k, rank.astype(jnp.int32), r_out)
    r_ref[...] = r_out
    run_ref[...] = run_ref[...] + jnp.sum(onehot, axis=0, keepdims=True)
    cnt_ref[0] = run_ref[...]


def _router(x, g, mods, mod_row, w_router, b_router, tm, n_ranges):
    T = x.shape[0]
    wr = jnp.zeros((D_MODEL, LANES), F32).at[:, :N_EXP].set(w_router)
    br = jnp.full((1, LANES), -jnp.inf, F32).at[0, :N_EXP].set(b_router)
    tri = jnp.asarray(np.tril(np.ones((tm, tm), np.float32), -1), BF16)
    tiles_per_range = T // n_ranges // tm
    row = lambda w: pl.BlockSpec((tm, w), lambda i: (i, 0))
    return pl.pallas_call(
        functools.partial(_router_kernel, tiles_per_range=tiles_per_range),
        grid=(T // tm,),
        in_specs=[row(D_MODEL),
                  pl.BlockSpec((1, D_MODEL), lambda i: (0, 0)),
                  pl.BlockSpec((1, 1, D_MODEL), lambda i: (mod_row(i * tm) * 6 + 4, 0, 0)),
                  pl.BlockSpec((1, 1, D_MODEL), lambda i: (mod_row(i * tm) * 6 + 3, 0, 0)),
                  pl.BlockSpec((D_MODEL, LANES), lambda i: (0, 0)),
                  pl.BlockSpec((1, LANES), lambda i: (0, 0)),
                  pl.BlockSpec((tm, tm), lambda i: (0, 0))],
        out_specs=[row(D_MODEL // 2), row(LANES), row(LANES), row(LANES),
                   pl.BlockSpec((1, 1, LANES), lambda i: (i // tiles_per_range, 0, 0))],
        out_shape=[jax.ShapeDtypeStruct((T, D_MODEL // 2), jnp.int32),
                   jax.ShapeDtypeStruct((T, LANES), jnp.int32),
                   jax.ShapeDtypeStruct((T, LANES), F32),
                   jax.ShapeDtypeStruct((T, LANES), jnp.int32),
                   jax.ShapeDtypeStruct((n_ranges, 1, LANES), F32)],
        scratch_shapes=[pltpu.VMEM((1, LANES), F32)],
        compiler_params=_cparams("arbitrary"),
        name="router",
    )(x, g.reshape(1, D_MODEL), mods, mods, wr, br, tri)


def _expert_kernel(blk_e_ref, x_ref, wgu_ref, bgu_ref, wdn_ref, bdn_ref, o_ref, wgu_bf, wdn_bf):
    i = pl.program_id(0)
    n_used = blk_e_ref[pl.num_programs(0)]
    new_expert = jnp.logical_or(i == 0, blk_e_ref[i] != blk_e_ref[jnp.maximum(i - 1, 0)])

    @pl.when(jnp.logical_and(new_expert, i < n_used))
    def _():
        wgu_bf[...] = wgu_ref[0].astype(BF16)
        wdn_bf[...] = wdn_ref[0].astype(BF16)

    @pl.when(i < n_used)
    def _():
        x = _unpack_bf16_pairs(x_ref[...]).astype(BF16)
        gu = jnp.dot(x, wgu_bf[...], preferred_element_type=F32) + bgu_ref[0]
        a = jnp.minimum(gu[:, :D_FF], SWIGLU_LIMIT)
        lin = jnp.clip(gu[:, D_FF:], -SWIGLU_LIMIT, SWIGLU_LIMIT)
        y = a * jax.nn.sigmoid(SWIGLU_ALPHA * a) * (lin + 1.0)
        o_ref[...] = _pack_bf16_pairs(jnp.dot(y.astype(BF16), wdn_bf[...], preferred_element_type=F32) + bdn_ref[0])

    @pl.when(i >= n_used)
    def _():
        o_ref[...] = jnp.zeros(o_ref.shape, o_ref.dtype)


def _experts(xb, blk_e, w_gu, b_gu, w_dn, b_dn, layer):
    n_pad = xb.shape[0]
    n_blocks = n_pad // MOE_BLOCK
    e0 = layer * N_EXP
    w_gu = w_gu.reshape(DEPTH * N_EXP, D_MODEL, 2 * D_FF)
    w_dn = w_dn.reshape(DEPTH * N_EXP, D_FF, D_MODEL)
    grid_spec = pltpu.PrefetchScalarGridSpec(
        num_scalar_prefetch=1,
        grid=(n_blocks,),
        in_specs=[pl.BlockSpec((MOE_BLOCK, D_MODEL // 2), lambda i, e: (i, 0)),
                  pl.BlockSpec((1, D_MODEL, 2 * D_FF), lambda i, e: (e0 + e[i], 0, 0)),
                  pl.BlockSpec((1, 1, 2 * D_FF), lambda i, e: (e0 + e[i], 0, 0)),
                  pl.BlockSpec((1, D_FF, D_MODEL), lambda i, e: (e0 + e[i], 0, 0)),
                  pl.BlockSpec((1, 1, D_MODEL), lambda i, e: (e0 + e[i], 0, 0))],
        out_specs=pl.BlockSpec((MOE_BLOCK, D_MODEL // 2), lambda i, e: (i, 0)),
        scratch_shapes=[pltpu.VMEM((D_MODEL, 2 * D_FF), BF16), pltpu.VMEM((D_FF, D_MODEL), BF16)],
    )
    return pl.pallas_call(
        _expert_kernel,
        grid_spec=grid_spec,
        out_shape=jax.ShapeDtypeStruct((n_pad, D_MODEL // 2), jnp.int32),
        compiler_params=_cparams("arbitrary"),
        name="experts",
    )(blk_e, xb, w_gu, b_gu.reshape(DEPTH * N_EXP, 1, 2 * D_FF), w_dn, b_dn.reshape(DEPTH * N_EXP, 1, D_MODEL))


def _combine_kernel(x_ref, y_ref, w_ref, g2_ref, nf_ref, *rest, final_norm):
    o_ref = rest[-1]
    acc = None
    for k in range(TOP_K):
        term = _unpack_bf16_pairs(y_ref[k]) * w_ref[:, k:k + 1]
        acc = term if acc is None else acc + term
    x = x_ref[...] + g2_ref[0] * acc
    if final_norm:
        x = _rms(x) * nf_ref[...]
    o_ref[...] = x


def _combine(x, yg, wts, mods, mod_row, norm_f, final_norm, tm, row0, x_acc):
    T = yg.shape[1]
    i0 = row0 // tm
    glob = lambda w: pl.BlockSpec((tm, w), lambda i: (i0 + i, 0))
    in_specs = [glob(D_MODEL), pl.BlockSpec((TOP_K, tm, D_MODEL // 2), lambda i: (0, i, 0)), glob(LANES),
                pl.BlockSpec((1, 1, D_MODEL), lambda i: (mod_row(row0 + i * tm) * 6 + 5, 0, 0)),
                pl.BlockSpec((1, D_MODEL), lambda i: (0, 0))]
    args = [x, yg, wts, mods, norm_f.reshape(1, D_MODEL)]
    aliases = {}
    if x_acc is not None:
        in_specs.append(pl.BlockSpec(memory_space=pl.ANY))
        aliases = {len(args): 0}
        args.append(x_acc)
    return pl.pallas_call(
        functools.partial(_combine_kernel, final_norm=final_norm),
        grid=(T // tm,),
        in_specs=in_specs,
        out_specs=glob(D_MODEL),
        out_shape=jax.ShapeDtypeStruct(x.shape, F32),
        input_output_aliases=aliases,
        compiler_params=_cparams("parallel"),
        name="combine",
    )(*args)


def _route(top_e, rank, counts, T):
    n_assign = T * TOP_K
    n_blocks = n_assign // MOE_BLOCK + N_EXP
    e_flat = top_e.reshape(n_assign)
    onehot = e_flat[:, None] == jnp.arange(N_EXP, dtype=jnp.int32)[None, :]
    start = jnp.cumsum(counts) - counts
    padded = (counts + MOE_BLOCK - 1) // MOE_BLOCK * MOE_BLOCK
    pad_end = jnp.cumsum(padded)
    pad_start = pad_end - padded
    iota = jnp.arange(n_assign, dtype=jnp.int32)
    _, order = lax.sort((e_flat, iota), num_keys=1, is_stable=True)
    pos = rank.reshape(n_assign) + jnp.sum(jnp.where(onehot, pad_start[None, :], 0), axis=1)
    blk_first = jnp.arange(n_blocks, dtype=jnp.int32) * MOE_BLOCK
    blk_e = jnp.minimum(jnp.sum(pad_end[None, :] <= blk_first[:, None], axis=1), N_EXP - 1).astype(jnp.int32)
    n_used = (pad_end[-1] // MOE_BLOCK).astype(jnp.int32)
    r = blk_first[:, None] - pad_start[blk_e][:, None] + jnp.arange(MOE_BLOCK, dtype=jnp.int32)[None, :]
    valid = r < counts[blk_e][:, None]
    src = jnp.clip(start[blk_e][:, None] + r, 0, n_assign - 1)
    filler = (blk_first[:, None] + jnp.arange(MOE_BLOCK, dtype=jnp.int32)[None, :]) % T
    tok = jnp.where(valid, order[src] // TOP_K, filler).reshape(n_blocks * MOE_BLOCK).astype(jnp.int32)
    return tok, pos, jnp.concatenate([blk_e, n_used[None]])


def _prep_layer(l, w_in, sgu_w, w_branch, w_out, w_gu, w_dn, gd_A_log, gd_dt_bias, lb):
    offs = np.cumsum([0, BRANCH_W, BRANCH_W, 512, 512, 512, 512, 512, GD_QKV, 8, 8, 512, 3 * NA_W, N_BRANCH * D_MODEL])
    w = w_in[l]
    seg = lambda i, j: w[:, offs[i]:offs[j]]
    w_gd = jnp.concatenate([seg(7, 8), seg(10, 11), seg(8, 10),
                            jnp.zeros((D_MODEL, LANES - 4 * GD_HEADS), F32)], axis=1)
    par = jnp.zeros((2, LANES), F32)
    par = par.at[0, :2 * GD_HEADS].set(gd_A_log[l].reshape(-1)).at[1, :2 * GD_HEADS].set(gd_dt_bias[l].reshape(-1))
    lb_l = lb[:, l]
    return {
        'w_sgu': seg(0, 2).astype(BF16), 'w_hg': seg(2, 7).astype(BF16), 'w_gd': w_gd.astype(BF16),
        'w_na': seg(11, 12).astype(BF16), 'w_mg': seg(12, 13).astype(BF16),
        'sgu_w': sgu_w[l], 'wb': w_branch[l].astype(BF16), 'wo': w_out[l].astype(BF16),
        'w_gu': w_gu, 'w_dn': w_dn, 'layer': l, 'gd_par': par,
        'lbp': jnp.concatenate([jnp.log(lb_l), jnp.log1p(-lb_l), 1.0 - lb_l], axis=0),
    }


def _layer(x, groups, mods, mod_row, lw, p, norm_f, final_norm):
    T = x.shape[0]
    h = _normmod(x, p['norm1'], mods, mod_row, 1024, part_shift=0, part_scale=1)
    z_sgu = _matmul(h, lw['w_sgu'], 1024, 1024)
    z_hg = _matmul(h, lw['w_hg'], 1024, 1280)
    z_gd = _matmul(h, lw['w_gd'], 512, 2176)
    z_na = _matmul(h, lw['w_na'], 1024, 768)
    z_mg = _matmul(h, lw['w_mg'], 1024, 1024, out_dtype=BF16)

    x_mix, states = None, []
    for row0, B, L, ctx in groups:
        Tg = B * L
        o_a = _sgu(z_sgu, p['sgu_norm'], lw['sgu_w'], p['sgu_b'], 256, row0, Tg)
        s_hg0 = None if ctx is None else ctx[2]
        s_gd0 = None if ctx is None else ctx[3]
        o_hf, o_hb, s_hg = _hgrn(z_hg, lw['lbp'], s_hg0, B, L, row0)
        qkv = _gdprep(z_gd, p['gd_conv'], L, 256, row0, Tg)
        o_gf, o_gb, s_gd = _gdn(qkv, z_gd, lw['gd_par'], s_gd0, B, L, row0)
        if ctx is None:
            o_d = _ctx_attn(z_na, B, L, row0)
        else:
            o_d = _na_attn(z_na, ctx[0], ctx[1], _na_bias(p['na_rpb'], L // GRID_W), B, L, row0)
        x_mix = _merge(x, o_a, o_hf, o_hb, z_hg, o_gf, o_gb, z_gd, o_d, z_mg, p['hg_onorm'], p['gd_onorm'],
                       lw['wb'], lw['wo'], mods, mod_row, 256, row0, Tg, x_mix)
        states.append((s_hg, s_gd))
    x = x_mix

    h2, top_e, wts, rank, counts = _router(x, p['norm2'], mods, mod_row, p['w_router'], p['b_router'], 512, MOE_SPLIT)
    counts = counts[:, 0, :N_EXP].astype(jnp.int32)
    th = T // MOE_SPLIT
    x_out = None
    for j in range(MOE_SPLIT):
        r0 = j * th
        tok, pos, blk_e = _route(top_e[r0:r0 + th, :TOP_K], rank[r0:r0 + th, :TOP_K], counts[j], th)
        xb = _sc_gather(h2, tok + r0)
        yb = _experts(xb, blk_e, lw['w_gu'], p['b_gu'], lw['w_dn'], p['b_dn'], lw['layer'])
        yg = _sc_gather(yb, pos.reshape(th, TOP_K).T.reshape(-1)).reshape(TOP_K, th, D_MODEL // 2)
        x_out = _combine(x, yg, wts, mods, mod_row, norm_f, final_norm, 512, r0, x_out)
    return x_out, z_na, states


def kernel(x_prompt, x_sample, c, cache_na_k, cache_na_v, state_hgrn, state_gdn, c_ctx, w_ada, b_ada, norm1, norm2, norm_f, w_in, sgu_norm, sgu_w, sgu_b, hg_lb, hg_onorm, gd_conv, gd_A_log, gd_dt_bias, gd_onorm, na_rpb, w_branch, w_out, w_router, b_router, w_gu, b_gu, w_dn, b_dn):
    Bp, Lp, D = x_prompt.shape
    Bs, Ls, _ = x_sample.shape
    ctx_row = Bs
    cvecs = jnp.zeros((MOD_ROWS, D), F32).at[:Bs].set(c).at[ctx_row].set(c_ctx)
    mods = _modulation(cvecs, w_ada, b_ada)

    cs = jnp.cumsum(jax.nn.softmax(hg_lb.astype(F32), axis=1), axis=1)
    lb = cs - cs[:, :1]

    Tp, Ts = Bp * Lp, Bs * Ls
    x = jnp.concatenate([x_prompt.reshape(Tp, D), x_sample.reshape(Ts, D)], axis=0)
    mod_row = lambda r: jnp.where(r < Tp, ctx_row, (r - Tp) // Ls)
    ks_, vs_, hs_, gs_ = [], [], [], []
    for l in range(DEPTH):
        lw = _prep_layer(l, w_in, sgu_w, w_branch, w_out, w_gu, w_dn, gd_A_log, gd_dt_bias, lb)
        p = {'norm1': norm1[l], 'norm2': norm2[l], 'sgu_norm': sgu_norm[l], 'sgu_b': sgu_b[l],
             'gd_conv': gd_conv[l], 'hg_onorm': hg_onorm[l], 'gd_onorm': gd_onorm[l], 'na_rpb': na_rpb[l],
             'w_router': w_router[l], 'b_router': b_router[l], 'b_gu': b_gu, 'b_dn': b_dn}
        ctx = (cache_na_k[:, l].reshape(Bs, -1, NA_W), cache_na_v[:, l].reshape(Bs, -1, NA_W),
               state_hgrn[:, l], state_gdn[:, l])
        groups = [(0, Bp, Lp, None), (Tp, Bs, Ls, ctx)]
        x, z_na, states = _layer(x, groups, mods[l], mod_row, lw, p, norm_f, l == DEPTH - 1)
        ks_.append(z_na[:Tp, NA_W:2 * NA_W].reshape(Bp, Lp, NA_HEADS, NA_DH))
        vs_.append(z_na[:Tp, 2 * NA_W:].reshape(Bp, Lp, NA_HEADS, NA_DH))
        hs_.append(states[0][0])
        gs_.append(states[0][1])

    return (x[:Tp].reshape(Bp, Lp, D), x[Tp:].reshape(Bs, Ls, D),
            jnp.stack(ks_, axis=1), jnp.stack(vs_, axis=1), jnp.stack(hs_, axis=1), jnp.stack(gs_, axis=1))
```

```python
import functools
import math

import numpy as np
import jax
import jax.numpy as jnp
from jax import lax
from jax.experimental import pallas as pl
from jax.experimental.pallas import tpu as pltpu
from jax.experimental.pallas import tpu_sc as plsc

D_MODEL = 1024
DEPTH = 2
GRID_W = 64
BRANCH_W = 512
N_BRANCH = 4
SGU_CHUNK = 128
SGU_GROUPS = 4
HG_HEADS = 4
HG_DK = 128
HG_DV = 128
GD_HEADS = 4
GD_DK = 128
GD_DV = 128
NA_HEADS = 8
NA_DH = 64
NA_KH = 8
NA_KW = 16
N_EXP = 32
TOP_K = 4
D_FF = 1024
SWIGLU_LIMIT = 7.0
SWIGLU_ALPHA = 1.702
EPS = 1e-6

F32 = jnp.float32
BF16 = jnp.bfloat16
HI = lax.Precision.HIGHEST

LANES = 128
MOD_ROWS = 16
SCAN_C = 64
SCAN_ROWS = 4 * SCAN_C
MOE_BLOCK = 512
MOE_SPLIT = 2
NA_QROWS = 4
NA_KROWS = NA_QROWS + NA_KH
VMEM_LIMIT = 48 * 1024 * 1024

NT = (((1,), (1,)), ((), ()))
TN = (((0,), (0,)), ((), ()))


def _cparams(*sem):
    return pltpu.CompilerParams(dimension_semantics=sem, vmem_limit_bytes=VMEM_LIMIT)


def _bdot(a, b):
    return jnp.dot(a.astype(BF16), b.astype(BF16), preferred_element_type=F32)


def _bdot_g(a, b, dims):
    return lax.dot_general(a.astype(BF16), b.astype(BF16), dims, preferred_element_type=F32)


def _hdot(a, b):
    return jnp.dot(a, b, precision=HI, preferred_element_type=F32)


def _dot01(m3, x):
    hi = x.astype(BF16)
    r1 = x - hi.astype(F32)
    mid = r1.astype(BF16)
    lo = (r1 - mid.astype(F32)).astype(BF16)
    return jnp.dot(m3, jnp.concatenate([hi, mid, lo], axis=0), preferred_element_type=F32)


def _dot01_sel(m3, g, sel):
    hi = g.astype(BF16)
    r1 = g - hi.astype(F32)
    mid = r1.astype(BF16)
    lo = (r1 - mid.astype(F32)).astype(BF16)
    terms = [jnp.concatenate([t * sel, t], axis=1) for t in (hi, mid, lo)]
    return jnp.dot(m3, jnp.concatenate(terms, axis=0), preferred_element_type=F32)


def _dot3(a, b):
    ah = a.astype(BF16)
    al = (a - ah.astype(F32)).astype(BF16)
    bh = b.astype(BF16)
    bl = (b - bh.astype(F32)).astype(BF16)
    return jnp.dot(jnp.concatenate([al, ah, ah], axis=1), jnp.concatenate([bh, bl, bh], axis=0),
                   preferred_element_type=F32)


LOG2E = 1.4426950408889634
HI_HALF = 0xFFFF0000


def _pack_bf16_pairs(x):
    w = x.shape[1] // 2
    b = lax.bitcast_convert_type(x.astype(BF16).astype(F32), jnp.uint32)
    return lax.bitcast_convert_type((b[:, :w] >> 16) | (b[:, w:] & jnp.uint32(HI_HALF)), jnp.int32)


def _unpack_bf16_pairs(p):
    p = lax.bitcast_convert_type(p, jnp.uint32)
    lo = lax.bitcast_convert_type(p << 16, F32)
    hi = lax.bitcast_convert_type(p & jnp.uint32(HI_HALF), F32)
    return jnp.concatenate([lo, hi], axis=1)


def _sigmoid(x):
    return 0.5 * jnp.tanh(0.5 * x) + 0.5


def _silu(x):
    return x * _sigmoid(x)


def _logaddexp(a, b):
    return jnp.maximum(a, b) + jnp.log(1.0 + jnp.exp(-jnp.abs(a - b)))


def _softplus(x):
    return jnp.maximum(x, 0.0) + jnp.log(1.0 + jnp.exp(-jnp.abs(x)))


def _ada_kernel(c_ref, w_ref, b_ref, o_ref):
    o_ref[0] = _hdot(_silu(c_ref[...]), w_ref[0]) + b_ref[0]


def _modulation(cvecs, w_ada, b_ada):
    tn = 1536
    out = pl.pallas_call(
        _ada_kernel,
        grid=(DEPTH, 6 * D_MODEL // tn),
        in_specs=[pl.BlockSpec((MOD_ROWS, D_MODEL), lambda l, j: (0, 0)),
                  pl.BlockSpec((1, D_MODEL, tn), lambda l, j: (l, 0, j)),
                  pl.BlockSpec((1, 1, tn), lambda l, j: (l, 0, j))],
        out_specs=pl.BlockSpec((1, MOD_ROWS, tn), lambda l, j: (l, 0, j)),
        out_shape=jax.ShapeDtypeStruct((DEPTH, MOD_ROWS, 6 * D_MODEL), F32),
        compiler_params=_cparams("arbitrary", "arbitrary"),
        name="ada_modulation",
    )(cvecs, w_ada, b_ada.reshape(DEPTH, 1, 6 * D_MODEL))
    return out.reshape(DEPTH, MOD_ROWS * 6, 1, D_MODEL)


def _rms(x):
    return x * lax.rsqrt(jnp.mean(x * x, axis=-1, keepdims=True) + EPS)


def _normmod_kernel(x_ref, g_ref, sc_ref, sh_ref, o_ref):
    h = (_rms(x_ref[...]) * g_ref[...]) * (1.0 + sc_ref[0]) + sh_ref[0]
    o_ref[...] = h.astype(o_ref.dtype)


def _normmod(x, g, mods, mod_row, tm, part_shift, part_scale):
    T = x.shape[0]
    return pl.pallas_call(
        _normmod_kernel,
        grid=(T // tm,),
        in_specs=[pl.BlockSpec((tm, D_MODEL), lambda i: (i, 0)),
                  pl.BlockSpec((1, D_MODEL), lambda i: (0, 0)),
                  pl.BlockSpec((1, 1, D_MODEL), lambda i: (mod_row(i * tm) * 6 + part_scale, 0, 0)),
                  pl.BlockSpec((1, 1, D_MODEL), lambda i: (mod_row(i * tm) * 6 + part_shift, 0, 0))],
        out_specs=pl.BlockSpec((tm, D_MODEL), lambda i: (i, 0)),
        out_shape=jax.ShapeDtypeStruct((T, D_MODEL), BF16),
        compiler_params=_cparams("parallel"),
        name="normmod",
    )(x, g.reshape(1, D_MODEL), mods, mods)


def _mm_kernel(a_ref, w_ref, o_ref):
    o_ref[...] = jnp.dot(a_ref[...], w_ref[...], preferred_element_type=F32).astype(o_ref.dtype)


def _matmul(a, w, tm, tn, out_dtype=F32):
    T, K = a.shape
    N = w.shape[1]
    return pl.pallas_call(
        _mm_kernel,
        grid=(N // tn, T // tm),
        in_specs=[pl.BlockSpec((tm, K), lambda j, i: (i, 0)),
                  pl.BlockSpec((K, tn), lambda j, i: (0, j))],
        out_specs=pl.BlockSpec((tm, tn), lambda j, i: (i, j)),
        out_shape=jax.ShapeDtypeStruct((T, N), out_dtype),
        compiler_params=_cparams("parallel", "parallel"),
        name="in_proj",
    )(a, w)


def _sgu_kernel(u_ref, v_ref, gn_ref, ws_ref, bs_ref, o_ref):
    rows = u_ref.shape[0]
    gw = BRANCH_W // SGU_GROUPS
    u = jax.nn.gelu(u_ref[...])
    v = (_rms(jax.nn.gelu(v_ref[...])) * gn_ref[...]).astype(BF16)
    for n in range(rows // SGU_CHUNK):
        r = slice(n * SGU_CHUNK, (n + 1) * SGU_CHUNK)
        for g in range(SGU_GROUPS):
            cs = slice(g * gw, (g + 1) * gw)
            s = jnp.dot(ws_ref[g], v[r, cs], preferred_element_type=F32) + bs_ref[:, cs]
            o_ref[r, cs] = (u[r, cs] * s).astype(o_ref.dtype)


def _sgu(z_sgu, g_norm, w_s, b_s, rows, row0, T):
    gw = BRANCH_W // SGU_GROUPS
    b_exp = jnp.repeat(b_s.T, gw, axis=1)
    i0 = row0 // rows
    return pl.pallas_call(
        _sgu_kernel,
        grid=(T // rows,),
        in_specs=[pl.BlockSpec((rows, BRANCH_W), lambda i: (i0 + i, 0)),
                  pl.BlockSpec((rows, BRANCH_W), lambda i: (i0 + i, 1)),
                  pl.BlockSpec((1, BRANCH_W), lambda i: (0, 0)),
                  pl.BlockSpec((SGU_GROUPS, SGU_CHUNK, SGU_CHUNK), lambda i: (0, 0, 0)),
                  pl.BlockSpec((SGU_CHUNK, BRANCH_W), lambda i: (0, 0))],
        out_specs=pl.BlockSpec((rows, BRANCH_W), lambda i: (i, 0)),
        out_shape=jax.ShapeDtypeStruct((T, BRANCH_W), BF16),
        compiler_params=_cparams("parallel"),
        name="sgu",
    )(z_sgu, z_sgu, g_norm.reshape(1, BRANCH_W), w_s.astype(BF16), b_exp)


def _order(reverse):
    p = np.arange(SCAN_C)
    return SCAN_C - 1 - p if reverse else p


def _gla_consts():
    C = SCAN_C
    nlev = int(math.log2(C))
    mats, masks = [], []
    for reverse in (False, True):
        p = _order(reverse)
        pt, pr = p[:, None], p[None, :]
        m_d, k_d = [], []
        for lev in range(nlev):
            w = C >> (lev + 1)
            parent = p // (2 * w)
            later = (p % (2 * w)) >= w
            anchor = (parent * 2 * w + w - 1)[:, None]
            m = np.where(later[:, None], (pr > anchor) & (pr <= pt), (pr > pt) & (pr <= anchor))
            m_d.append(m)
            k_d.append((parent[:, None] == parent[None, :]) & later[:, None] & ~later[None, :])
        m_d.append(pr <= pt)
        m_d.append(pr > pt)
        k_d.append(np.eye(C, dtype=bool))
        mats.append(np.concatenate(m_d, axis=0))
        masks.append(np.stack(k_d))
    return (np.stack(mats).astype(np.float32), np.stack(masks).astype(np.float32))


def _delta_consts():
    C = SCAN_C
    tri, sl, incl, strict = [], [], [], []
    for reverse in (False, True):
        p = _order(reverse)
        pt, pr = p[:, None], p[None, :]
        tri.append(np.concatenate([pr <= pt, pr > pt], axis=0))
        sl.append(np.concatenate([pt > pr, np.zeros((C, LANES - C), bool)], axis=1))
        incl.append(pr <= pt)
        strict.append(pr < pt)
    f = lambda a: np.stack(a).astype(np.float32)
    return f(tri), f(sl), f(incl), f(strict)


def _hgrn_kernel(qf_ref, ff_ref, vf_ref, qb_ref, fb_ref, vb_ref, lb_ref, mat_ref, msk_ref, lat_ref, *rest, has_state):
    if has_state:
        s0_ref, of_ref, ob_ref, sfin_ref, st_ref = rest
    else:
        of_ref, ob_ref, sfin_ref, st_ref = rest
    C = SCAN_C
    nlev = msk_ref.shape[1] - 1
    c = pl.program_id(1)
    last_c = pl.num_programs(1) - 1

    @pl.when(c == 0)
    def _():
        for d in range(2):
            for h in range(HG_HEADS):
                if has_state:
                    st_ref[d, h] = s0_ref[0, d, h].T
                else:
                    st_ref[d, h] = jnp.zeros((HG_DV, HG_DK), F32)

    q_refs, f_refs, v_refs, o_refs = (qf_ref, qb_ref), (ff_ref, fb_ref), (vf_ref, vb_ref), (of_ref, ob_ref)
    sub = qf_ref.shape[0] // C
    rows = lambda j: slice(j * C, (j + 1) * C)
    col = lambda h: slice(h * HG_DK, (h + 1) * HG_DK)
    parts = [(d, j) for d in range(2) for j in range(sub)]
    q, k, fac, att = {}, {}, {}, {}
    for d, j in parts:
        zf = f_refs[d][rows(j), :]
        t = jnp.exp(-jnp.abs(zf))
        log_sig = jnp.minimum(zf, 0.0) - jnp.log(1.0 + t)
        logf = _logaddexp(lb_ref[d:d + 1, :], lb_ref[2 + d:3 + d, :] + log_sig)
        k[d, j] = lb_ref[4 + d:5 + d, :] * (jnp.where(zf >= 0.0, t, 1.0) / (1.0 + t))
        q[d, j] = _silu(q_refs[d][rows(j), :]) * (HG_DK ** -0.5)
        fac[d, j] = jnp.exp2(_dot01(mat_ref[d], logf * LOG2E))
    later = {(d, i): lat_ref[d, i] > 0.5 for d in range(2) for i in range(nlev)}
    for d, j in parts:
        for h in range(HG_HEADS):
            qh, kh = q[d, j][:, col(h)], k[d, j][:, col(h)]
            acc = msk_ref[d, nlev] * _bdot_g(qh, kh, NT)
            for i in range(nlev):
                z = (jnp.where(later[d, i], qh, kh) * fac[d, j][i * C:(i + 1) * C, col(h)]).astype(BF16)
                acc = acc + msk_ref[d, i] * lax.dot_general(z, z, NT, preferred_element_type=F32)
            att[d, j, h] = acc
    chains = [(d, h) for d in range(2) for h in range(HG_HEADS)]
    for t in range(sub):
        jd = (t, sub - 1 - t)
        for d, h in chains:
            j = jd[d]
            eb = fac[d, j][nlev * C:(nlev + 1) * C, col(h)]
            o_refs[d][rows(j), col(h)] = (_bdot(att[d, j, h], v_refs[d][rows(j), col(h)])
                                          + _bdot_g(q[d, j][:, col(h)] * eb, st_ref[d, h], NT)).astype(BF16)
        for d, h in chains:
            j = jd[d]
            eb = fac[d, j][nlev * C:(nlev + 1) * C, col(h)]
            er = fac[d, j][(nlev + 1) * C:, col(h)]
            e_last = eb[C - 1:C] if d == 0 else eb[0:1]
            st_ref[d, h] = st_ref[d, h] * e_last + _bdot_g(v_refs[d][rows(j), col(h)], k[d, j][:, col(h)] * er, TN)

    @pl.when(c == last_c)
    def _():
        for d in range(2):
            for h in range(HG_HEADS):
                sfin_ref[0, d, h] = st_ref[d, h].T


def _hgrn(z_hg, lbp, s0, B, L, row0):
    n = L // SCAN_ROWS
    c0 = row0 // SCAN_ROWS
    mats, masks = _gla_consts()
    later = np.broadcast_to(masks[:, :-1].any(axis=3)[..., None], masks[:, :-1].shape[:3] + (HG_DK,)).astype(np.float32)
    mats = np.tile(mats, (1, 1, 3))
    blk = (SCAN_ROWS, HG_HEADS * HG_DK)
    fwd = lambda col: pl.BlockSpec(blk, lambda b, c: (c0 + b * n + c, col))
    bwd = lambda col: pl.BlockSpec(blk, lambda b, c: (c0 + b * n + n - 1 - c, col))
    st_blk = (1, 2, HG_HEADS, HG_DK, HG_DV)
    in_specs = [fwd(0), fwd(1), fwd(3), bwd(0), bwd(2), bwd(3),
                pl.BlockSpec(lbp.shape, lambda b, c: (0, 0)),
                pl.BlockSpec(mats.shape, lambda b, c: (0, 0, 0)),
                pl.BlockSpec(masks.shape, lambda b, c: (0, 0, 0, 0)),
                pl.BlockSpec(later.shape, lambda b, c: (0, 0, 0, 0))]
    args = [z_hg] * 6 + [lbp, jnp.asarray(mats, BF16), jnp.asarray(masks), jnp.asarray(later)]
    if s0 is not None:
        in_specs.append(pl.BlockSpec(st_blk, lambda b, c: (b, 0, 0, 0, 0)))
        args.append(s0)
    return pl.pallas_call(
        functools.partial(_hgrn_kernel, has_state=s0 is not None),
        grid=(B, n),
        in_specs=in_specs,
        out_specs=[pl.BlockSpec(blk, lambda b, c: (b * n + c, 0)),
                   pl.BlockSpec(blk, lambda b, c: (b * n + n - 1 - c, 0)),
                   pl.BlockSpec(st_blk, lambda b, c: (b, 0, 0, 0, 0))],
        out_shape=[jax.ShapeDtypeStruct((B * L, HG_HEADS * HG_DV), BF16),
                   jax.ShapeDtypeStruct((B * L, HG_HEADS * HG_DV), BF16),
                   jax.ShapeDtypeStruct((B,) + st_blk[1:], F32)],
        scratch_shapes=[pltpu.VMEM((2, HG_HEADS, HG_DV, HG_DK), F32)],
        compiler_params=_cparams("parallel", "arbitrary"),
        name="hgrn_scan",
    )(*args)


GD_NQ = GD_HEADS * GD_DK
GD_QKV = 2 * GD_NQ + GD_HEADS * GD_DV
HALO = 8


def _gdprep_kernel(x_ref, prev_ref, next_ref, w_ref, o_ref, *, tiles_per_seq):
    R = x_ref.shape[0]
    t = pl.program_id(0) % tiles_per_seq
    x = x_ref[...]
    prev_row = jnp.where(t == 0, 0.0, prev_ref[HALO - 1:HALO, :])
    next_row = jnp.where(t == tiles_per_seq - 1, 0.0, next_ref[0:1, :])
    row = lax.broadcasted_iota(jnp.int32, x.shape, 0)
    xm1 = jnp.where(row == 0, prev_row, pltpu.roll(x, 1, 0))
    xp1 = jnp.where(row == R - 1, next_row, pltpu.roll(x, R - 1, 0))
    y = _silu(w_ref[0:1, :] * xm1 + w_ref[1:2, :] * x + w_ref[2:3, :] * xp1)
    for j in range(2 * GD_HEADS):
        cs = slice(j * GD_DK, (j + 1) * GD_DK)
        seg = y[:, cs]
        seg = seg * lax.rsqrt(jnp.sum(seg * seg, axis=-1, keepdims=True) + EPS)
        if j < GD_HEADS:
            seg = seg * (GD_DK ** -0.5)
        o_ref[:, cs] = seg
    o_ref[:, 2 * GD_NQ:] = y[:, 2 * GD_NQ:]


def _gdprep(z_gd, conv_w, L, rows, row0, T):
    tps = L // rows
    hb = rows // HALO
    nhalo = z_gd.shape[0] // HALO
    i0 = row0 // rows
    return pl.pallas_call(
        functools.partial(_gdprep_kernel, tiles_per_seq=tps),
        grid=(T // rows,),
        in_specs=[pl.BlockSpec((rows, GD_QKV), lambda i: (i0 + i, 0)),
                  pl.BlockSpec((HALO, GD_QKV), lambda i: (jnp.maximum((i0 + i) * hb - 1, 0), 0)),
                  pl.BlockSpec((HALO, GD_QKV), lambda i: (jnp.minimum((i0 + i + 1) * hb, nhalo - 1), 0)),
                  pl.BlockSpec((3, GD_QKV), lambda i: (0, 0))],
        out_specs=pl.BlockSpec((rows, GD_QKV), lambda i: (i, 0)),
        out_shape=jax.ShapeDtypeStruct((T, GD_QKV), F32),
        compiler_params=_cparams("parallel"),
        name="gdn_prep",
    )(z_gd, z_gd, z_gd, conv_w)


def _gdn_kernel(xf_ref, abf_ref, xb_ref, abb_ref, par_ref, tri_ref, sl_ref, incl_ref, strict_ref, msk_ref, *rest,
                has_state):
    if has_state:
        s0_ref, of_ref, ob_ref, sfin_ref, st_ref = rest
    else:
        of_ref, ob_ref, sfin_ref, st_ref = rest
    C = SCAN_C
    c = pl.program_id(1)
    last_c = pl.num_programs(1) - 1
    nlev = msk_ref.shape[1] - 1

    @pl.when(c == 0)
    def _():
        for d in range(2):
            for h in range(GD_HEADS):
                if has_state:
                    st_ref[d, h] = s0_ref[0, d, h].T
                else:
                    st_ref[d, h] = jnp.zeros((GD_DV, GD_DK), F32)

    x_refs, ab_refs, o_refs = (xf_ref, xb_ref), (abf_ref, abb_ref), (of_ref, ob_ref)
    sub = xf_ref.shape[0] // C
    rows = lambda j: slice(j * C, (j + 1) * C)
    chains = [(d, h) for d in range(2) for h in range(GD_HEADS)]
    parts = [(d, j, h) for d in range(2) for j in range(sub) for h in range(GD_HEADS)]
    q_of = lambda d, j, h: x_refs[d][rows(j), h * GD_DK:(h + 1) * GD_DK]
    k_of = lambda d, j, h: x_refs[d][rows(j), GD_NQ + h * GD_DK:GD_NQ + (h + 1) * GD_DK]
    v_of = lambda d, j, h: x_refs[d][rows(j), 2 * GD_NQ + h * GD_DV:2 * GD_NQ + (h + 1) * GD_DV]
    g_all, beta_all = [], []
    for d in range(2):
        ab = ab_refs[d][...]
        g_all.append(-jnp.exp(par_ref[0:1, :]) * _softplus(ab + par_ref[1:2, :]))
        beta_all.append(_sigmoid(ab))

    def beta_of(d, j, h):
        lane = 2 * GD_HEADS + d * GD_HEADS + h
        return jnp.broadcast_to(beta_all[d][rows(j), lane:lane + 1], (C, LANES))

    decay, e_cum, e_rest, kb, a = {}, {}, {}, {}, {}
    for d, j, h in parts:
        lane = d * GD_HEADS + h
        g_b = jnp.broadcast_to(g_all[d][rows(j), lane:lane + 1], (C, LANES))
        sums = _dot01_sel(tri_ref[d], g_b, sl_ref[d])
        decay[d, j, h] = jnp.exp(sums[:C, :C])
        e_cum[d, j, h] = jnp.exp(sums[:C, LANES:])
        e_rest[d, j, h] = jnp.exp(sums[C:, LANES:])
    for d, j, h in parts:
        k = k_of(d, j, h)
        kb[d, j, h] = k * beta_of(d, j, h)
        a[d, j, h] = strict_ref[d] * decay[d, j, h] * _bdot_g(kb[d, j, h], k, NT)
    inv_m = {pt: -(msk_ref[pt[0], nlev - 1] * a[pt]) for pt in parts}
    for lev in range(nlev - 2, -1, -1):
        a_w = {pt: msk_ref[pt[0], lev] * a[pt] for pt in parts}
        p = {pt: a_w[pt] + _bdot(inv_m[pt], a_w[pt]) for pt in parts}
        inv_m = {pt: inv_m[pt] - p[pt] - _bdot(p[pt], inv_m[pt]) for pt in parts}
    sol, att = {}, {}
    for d, j, h in parts:
        rhs = jnp.concatenate([v_of(d, j, h) * beta_of(d, j, h), kb[d, j, h] * e_cum[d, j, h]], axis=1)
        sol[d, j, h] = rhs + _dot3(inv_m[d, j, h], rhs)
        att[d, j, h] = incl_ref[d] * decay[d, j, h] * _bdot_g(q_of(d, j, h), k_of(d, j, h), NT)
    for t in range(sub):
        jd = (t, sub - 1 - t)
        u = {}
        for d, h in chains:
            pt = (d, jd[d], h)
            u[d, h] = sol[pt][:, :GD_DV] - _bdot_g(sol[pt][:, GD_DV:], st_ref[d, h], NT)
        for d, h in chains:
            pt = (d, jd[d], h)
            o_refs[d][rows(jd[d]), h * GD_DV:(h + 1) * GD_DV] = (
                _bdot_g(q_of(*pt) * e_cum[pt], st_ref[d, h], NT) + _bdot(att[pt], u[d, h])).astype(BF16)
        for d, h in chains:
            pt = (d, jd[d], h)
            e_last = e_cum[pt][C - 1:C] if d == 0 else e_cum[pt][0:1]
            st_ref[d, h] = st_ref[d, h] * e_last + _bdot_g(u[d, h], k_of(*pt) * e_rest[pt], TN)

    @pl.when(c == last_c)
    def _():
        for d in range(2):
            for h in range(GD_HEADS):
                sfin_ref[0, d, h] = st_ref[d, h].T


def _gdn(qkv, z_gd, par, s0, B, L, row0):
    n = L // SCAN_ROWS
    c0 = row0 // SCAN_ROWS
    tri, sl, incl, strict = (jnp.asarray(a) for a in _delta_consts())
    tri = jnp.tile(tri, (1, 1, 3)).astype(BF16)
    sl = sl.astype(BF16)
    masks = jnp.asarray(_gla_consts()[1])
    ab_col = (GD_QKV + GD_HEADS * GD_DV) // LANES
    xblk = (SCAN_ROWS, GD_QKV)
    ablk = (SCAN_ROWS, LANES)
    oblk = (SCAN_ROWS, GD_HEADS * GD_DV)
    st_blk = (1, 2, GD_HEADS, GD_DK, GD_DV)
    fwd = lambda b, c: b * n + c
    bwd = lambda b, c: b * n + n - 1 - c
    const3 = lambda a: pl.BlockSpec(a.shape, lambda b, c: (0, 0, 0))
    in_specs = [pl.BlockSpec(xblk, lambda b, c: (fwd(b, c), 0)),
                pl.BlockSpec(ablk, lambda b, c: (c0 + fwd(b, c), ab_col)),
                pl.BlockSpec(xblk, lambda b, c: (bwd(b, c), 0)),
                pl.BlockSpec(ablk, lambda b, c: (c0 + bwd(b, c), ab_col)),
                pl.BlockSpec(par.shape, lambda b, c: (0, 0)),
                const3(tri), const3(sl), const3(incl), const3(strict),
                pl.BlockSpec(masks.shape, lambda b, c: (0, 0, 0, 0))]
    args = [qkv, z_gd, qkv, z_gd, par, tri, sl, incl, strict, masks]
    if s0 is not None:
        in_specs.append(pl.BlockSpec(st_blk, lambda b, c: (b, 0, 0, 0, 0)))
        args.append(s0)
    return pl.pallas_call(
        functools.partial(_gdn_kernel, has_state=s0 is not None),
        grid=(B, n),
        in_specs=in_specs,
        out_specs=[pl.BlockSpec(oblk, lambda b, c: (fwd(b, c), 0)),
                   pl.BlockSpec(oblk, lambda b, c: (bwd(b, c), 0)),
                   pl.BlockSpec(st_blk, lambda b, c: (b, 0, 0, 0, 0))],
        out_shape=[jax.ShapeDtypeStruct((B * L, GD_HEADS * GD_DV), BF16),
                   jax.ShapeDtypeStruct((B * L, GD_HEADS * GD_DV), BF16),
                   jax.ShapeDtypeStruct((B,) + st_blk[1:], F32)],
        scratch_shapes=[pltpu.VMEM((2, GD_HEADS, GD_DV, GD_DK), F32)],
        compiler_params=_cparams("parallel", "arbitrary"),
        name="gdn_scan",
    )(*args)


NA_W = NA_HEADS * NA_DH


def _softmax_pv(s, v):
    m = jnp.max(s, axis=-1, keepdims=True)
    e = jnp.exp(s - m)
    den = jnp.sum(e, axis=-1, keepdims=True)
    return jnp.dot(e.astype(BF16), v, preferred_element_type=F32) / den


def _ctx_attn_kernel(q_ref, k_ref, v_ref, o_ref):
    for h in range(NA_HEADS):
        cs = slice(h * NA_DH, (h + 1) * NA_DH)
        q = (q_ref[:, cs] * (NA_DH ** -0.5)).astype(BF16)
        s = lax.dot_general(q, k_ref[:, cs].astype(BF16), NT, preferred_element_type=F32)
        o_ref[:, cs] = _softmax_pv(s, v_ref[:, cs].astype(BF16)).astype(o_ref.dtype)


def _ctx_attn(z_na, B, L, row0):
    blk = (L, NA_W)
    b0 = row0 // L
    return pl.pallas_call(
        _ctx_attn_kernel,
        grid=(B,),
        in_specs=[pl.BlockSpec(blk, lambda b: (b0 + b, 0)),
                  pl.BlockSpec(blk, lambda b: (b0 + b, 1)),
                  pl.BlockSpec(blk, lambda b: (b0 + b, 2))],
        out_specs=pl.BlockSpec(blk, lambda b: (b, 0)),
        out_shape=jax.ShapeDtypeStruct((B * L, NA_W), BF16),
        compiler_params=_cparams("parallel"),
        name="ctx_attn",
    )(z_na, z_na, z_na)


def _na_kernel(q_ref, *rest, n_kblk, nkeys_nb):
    k_refs = rest[:n_kblk]
    v_refs = rest[n_kblk:2 * n_kblk]
    kc_ref, vc_ref, bias_ref, o_ref, kbuf, vbuf = rest[2 * n_kblk:]
    qb = q_ref.shape[0]
    for i in range(n_kblk):
        kbuf[i * qb:(i + 1) * qb, :] = k_refs[i][...].astype(BF16)
        vbuf[i * qb:(i + 1) * qb, :] = v_refs[i][...].astype(BF16)
    kbuf[nkeys_nb:, :] = kc_ref[0].astype(BF16)
    vbuf[nkeys_nb:, :] = vc_ref[0].astype(BF16)
    for h in range(NA_HEADS):
        cs = slice(h * NA_DH, (h + 1) * NA_DH)
        q = (q_ref[:, cs] * (NA_DH ** -0.5)).astype(BF16)
        s = lax.dot_general(q, kbuf[:, cs], NT, preferred_element_type=F32)
        s_nb = s[:, :nkeys_nb] + bias_ref[0, h]
        s_cx = s[:, nkeys_nb:]
        m = jnp.maximum(jnp.max(s_nb, axis=-1, keepdims=True), jnp.max(s_cx, axis=-1, keepdims=True))
        e_nb = jnp.exp(s_nb - m)
        e_cx = jnp.exp(s_cx - m)
        den = jnp.sum(e_nb, axis=-1, keepdims=True) + jnp.sum(e_cx, axis=-1, keepdims=True)
        pv = (jnp.dot(e_nb.astype(BF16), vbuf[:nkeys_nb, cs], preferred_element_type=F32)
              + jnp.dot(e_cx.astype(BF16), vbuf[nkeys_nb:, cs], preferred_element_type=F32))
        o_ref[:, cs] = (pv / den).astype(o_ref.dtype)


def _na_bias(rpb, rows):
    qr, kr, col = np.arange(NA_QROWS), np.arange(NA_KROWS), np.arange(GRID_W)
    nblk = rows // NA_QROWS
    ndr, ndc = 2 * NA_KH - 1, 2 * NA_KW - 1
    sel_r, row_ok = [], []
    for m in (0, 1, nblk - 1):
        r = (NA_QROWS * m + qr)[:, None]
        start = np.clip(NA_QROWS * m - NA_KH // 2, 0, rows - NA_KROWS)
        kra = (start + kr)[None, :]
        r0 = np.clip(r - NA_KH // 2, 0, rows - NA_KH)
        row_ok.append((kra >= r0) & (kra < r0 + NA_KH))
        dr = np.clip(kra - r + NA_KH - 1, 0, ndr - 1)
        sel_r.append(dr[..., None] == np.arange(ndr))
    sel_r = np.stack(sel_r).astype(np.float32)
    row_ok = np.stack(row_ok)
    col_start = np.clip(col - NA_KW // 2, 0, GRID_W - NA_KW)[:, None]
    col_ok = (col[None, :] >= col_start) & (col[None, :] < col_start + NA_KW)
    dc = np.clip(col[None, :] - col[:, None], -(NA_KW - 1), NA_KW - 1) + NA_KW - 1
    sel_c = (dc[..., None] == np.arange(ndc)).astype(np.float32)
    bias = jnp.einsum('hab,vqka,xyb->vhqxky', rpb.astype(F32), sel_r, sel_c, precision=HI)
    ok = row_ok[:, None, :, None, :, None] & col_ok[None, None, None, :, None, :]
    bias = jnp.where(ok, bias, -jnp.inf)
    return bias.reshape(3, NA_HEADS, NA_QROWS * GRID_W, NA_KROWS * GRID_W)


def _na_attn(z_na, k_ctx, v_ctx, bias, B, S, row0):
    rows = S // GRID_W
    qb = NA_QROWS * GRID_W
    m0 = row0 // qb
    nblk = rows // NA_QROWS
    n_kblk = NA_KROWS // NA_QROWS
    lc = k_ctx.shape[1]
    nkeys_nb = NA_KROWS * GRID_W
    kstart = lambda m: jnp.clip(m - 1, 0, nblk - n_kblk)
    variant = lambda m: jnp.where(m == 0, 0, jnp.where(m == nblk - 1, 2, 1))
    kv_specs = lambda col: [pl.BlockSpec((qb, NA_W), functools.partial(
        lambda b, m, i, col: (m0 + b * nblk + kstart(m) + i, col), i=i, col=col)) for i in range(n_kblk)]
    return pl.pallas_call(
        functools.partial(_na_kernel, n_kblk=n_kblk, nkeys_nb=nkeys_nb),
        grid=(B, nblk),
        in_specs=([pl.BlockSpec((qb, NA_W), lambda b, m: (m0 + b * nblk + m, 0))] + kv_specs(1) + kv_specs(2)
                  + [pl.BlockSpec((1, lc, NA_W), lambda b, m: (b, 0, 0)),
                     pl.BlockSpec((1, lc, NA_W), lambda b, m: (b, 0, 0)),
                     pl.BlockSpec((1,) + bias.shape[1:], lambda b, m: (variant(m), 0, 0, 0))]),
        out_specs=pl.BlockSpec((qb, NA_W), lambda b, m: (b * nblk + m, 0)),
        out_shape=jax.ShapeDtypeStruct((B * S, NA_W), BF16),
        scratch_shapes=[pltpu.VMEM((nkeys_nb + lc, NA_W), BF16), pltpu.VMEM((nkeys_nb + lc, NA_W), BF16)],
        compiler_params=_cparams("parallel", "arbitrary"),
        name="na_attn",
    )(z_na, *([z_na] * (2 * n_kblk)), k_ctx, v_ctx, bias)


def _head_rms(o, g_row):
    parts = []
    for h in range(o.shape[1] // LANES):
        parts.append(_rms(o[:, h * LANES:(h + 1) * LANES]) * g_row)
    return jnp.concatenate(parts, axis=1)


def _merge_kernel(x_ref, oa_ref, hf_ref, hb_ref, hg_ref, gf_ref, gb_ref, gg_ref, od_ref, mg_ref,
                  hn_ref, gn_ref, wb_ref, wo_ref, g1_ref, *rest):
    o_ref = rest[-1]
    o_b = _head_rms(hf_ref[...].astype(F32) + hb_ref[...].astype(F32), hn_ref[...]) * _silu(hg_ref[...])
    o_c = _head_rms(gf_ref[...].astype(F32) + gb_ref[...].astype(F32), gn_ref[...]) * _silu(gg_ref[...])
    branches = (oa_ref[...], o_b.astype(BF16), o_c.astype(BF16), od_ref[...])
    merged = None
    for n_, o_n in enumerate(branches):
        gate = _sigmoid(mg_ref[:, n_ * D_MODEL:(n_ + 1) * D_MODEL].astype(F32))
        term = gate * jnp.dot(o_n, wb_ref[n_], preferred_element_type=F32)
        merged = term if merged is None else merged + term
    mix = jnp.dot(merged.astype(BF16), wo_ref[...], preferred_element_type=F32)
    o_ref[...] = x_ref[...] + g1_ref[0] * mix


def _merge(x, o_a, o_hf, o_hb, z_hg, o_gf, o_gb, z_gd, o_d, z_mg, hg_onorm, gd_onorm, wb, wo, mods, mod_row, tm,
           row0, T, x_acc):
    i0 = row0 // tm
    glob = lambda w, col=0: pl.BlockSpec((tm, w), lambda i: (i0 + i, col))
    loc = lambda w: pl.BlockSpec((tm, w), lambda i: (i, 0))
    full = lambda a: pl.BlockSpec(a.shape, lambda i: (0,) * a.ndim)
    hn = hg_onorm.reshape(1, HG_DV)
    gn = gd_onorm.reshape(1, GD_DV)
    in_specs = [glob(D_MODEL), loc(BRANCH_W), loc(BRANCH_W), loc(BRANCH_W), glob(BRANCH_W, 4),
                loc(BRANCH_W), loc(BRANCH_W), glob(BRANCH_W, GD_QKV // BRANCH_W), loc(BRANCH_W),
                glob(N_BRANCH * D_MODEL), full(hn), full(gn), full(wb), full(wo),
                pl.BlockSpec((1, 1, D_MODEL), lambda i: (mod_row(row0 + i * tm) * 6 + 2, 0, 0))]
    args = [x, o_a, o_hf, o_hb, z_hg, o_gf, o_gb, z_gd, o_d, z_mg, hn, gn, wb, wo, mods]
    aliases = {}
    if x_acc is not None:
        in_specs.append(pl.BlockSpec(memory_space=pl.ANY))
        aliases = {len(args): 0}
        args.append(x_acc)
    return pl.pallas_call(
        _merge_kernel,
        grid=(T // tm,),
        in_specs=in_specs,
        out_specs=glob(D_MODEL),
        out_shape=jax.ShapeDtypeStruct(x.shape, F32),
        input_output_aliases=aliases,
        compiler_params=_cparams("parallel"),
        name="merge",
    )(*args)


SC_CORES = 2
SC_SUBCORES = 16
SC_WIN = 64


def _sc_gather(table, idx):
    V, D = table.shape
    N = idx.shape[0]
    nw = SC_CORES * SC_SUBCORES
    per_w = N // nw
    n_win = per_w // SC_WIN
    assert per_w * nw == N and n_win * SC_WIN == per_w
    mesh = plsc.VectorSubcoreMesh(core_axis_name="c", subcore_axis_name="s")

    @functools.partial(
        pl.kernel, mesh=mesh,
        out_type=jax.ShapeDtypeStruct((N, D), table.dtype),
        scratch_types=[pltpu.VMEM((n_win, SC_WIN), jnp.int32),
                       pltpu.VMEM((SC_WIN, D), table.dtype),
                       pltpu.SemaphoreType.DMA],
    )
    def gather_rows(table_hbm, idx_hbm, out_hbm, idx_v, rows_v, sem):
        wid = lax.axis_index("s") * SC_CORES + lax.axis_index("c")
        pltpu.sync_copy(idx_hbm.at[wid], idx_v)

        @pl.loop(0, n_win)
        def _(w):
            pltpu.async_copy(table_hbm.at[idx_v.at[w]], rows_v, sem).wait()
            pltpu.sync_copy(rows_v, out_hbm.at[pl.ds(wid * per_w + w * SC_WIN, SC_WIN)])

    return gather_rows(table, idx.reshape(nw, n_win, SC_WIN))


def _router_kernel(x_ref, g_ref, sc_ref, sh_ref, wr_ref, br_ref, tri_ref, h_ref, e_ref, w_ref, r_ref, cnt_ref, run_ref,
                   *, tiles_per_range):
    @pl.when(pl.program_id(0) % tiles_per_range == 0)
    def _():
        run_ref[...] = jnp.zeros(run_ref.shape, F32)

    h = (_rms(x_ref[...]) * g_ref[...]) * (1.0 + sc_ref[0]) + sh_ref[0]
    h_ref[...] = _pack_bf16_pairs(h)
    h_hi = h.astype(BF16)
    h_lo = (h - h_hi.astype(F32)).astype(BF16)
    w_hi = wr_ref[...].astype(BF16)
    w_lo = (wr_ref[...] - w_hi.astype(F32)).astype(BF16)
    dot = lambda a, b: jnp.dot(a, b, preferred_element_type=F32)
    logits = (dot(h_lo, w_hi) + dot(h_hi, w_lo)) + dot(h_hi, w_hi) + br_ref[...]
    lane = lax.broadcasted_iota(jnp.int32, logits.shape, 1)
    e_out = jnp.zeros(logits.shape, jnp.int32)
    v_out = jnp.zeros(logits.shape, F32)
    onehot = jnp.zeros(logits.shape, F32)
    top0, picks = None, []
    for k in range(TOP_K):
        m = jnp.max(logits, axis=-1, keepdims=True)
        idx = jnp.min(jnp.where(logits == m, lane, LANES), axis=-1, keepdims=True)
        if k == 0:
            top0 = m
        picks.append(lane == idx)
        e_out = jnp.where(lane == k, idx, e_out)
        v_out = jnp.where(lane == k, jnp.exp(m - top0), v_out)
        onehot = jnp.where(picks[k], 1.0, onehot)
        logits = jnp.where(picks[k], -jnp.inf, logits)
    e_ref[...] = e_out
    w_ref[...] = v_out / jnp.sum(v_out, axis=-1, keepdims=True)
    before = run_ref[...] + jnp.dot(tri_ref[...], onehot.astype(BF16), preferred_element_type=F32)
    r_out = jnp.zeros(logits.shape, jnp.int32)
    for k in range(TOP_K):
        rank = jnp.sum(jnp.where(picks[k], before, 0.0), axis=-1, keepdims=True)
        r_out = jnp.where(lane == k, rank.astype(jnp.int32), r_out)
    r_ref[...] = r_out
    run_ref[...] = run_ref[...] + jnp.sum(onehot, axis=0, keepdims=True)
    cnt_ref[0] = run_ref[...]


def _router(x, g, mods, mod_row, w_router, b_router, tm, n_ranges):
    T = x.shape[0]
    wr = jnp.zeros((D_MODEL, LANES), F32).at[:, :N_EXP].set(w_router)
    br = jnp.full((1, LANES), -jnp.inf, F32).at[0, :N_EXP].set(b_router)
    tri = jnp.asarray(np.tril(np.ones((tm, tm), np.float32), -1), BF16)
    tiles_per_range = T // n_ranges // tm
    row = lambda w: pl.BlockSpec((tm, w), lambda i: (i, 0))
    return pl.pallas_call(
        functools.partial(_router_kernel, tiles_per_range=tiles_per_range),
        grid=(T // tm,),
        in_specs=[row(D_MODEL),
                  pl.BlockSpec((1, D_MODEL), lambda i: (0, 0)),
                  pl.BlockSpec((1, 1, D_MODEL), lambda i: (mod_row(i * tm) * 6 + 4, 0, 0)),
                  pl.BlockSpec((1, 1, D_MODEL), lambda i: (mod_row(i * tm) * 6 + 3, 0, 0)),
                  pl.BlockSpec((D_MODEL, LANES), lambda i: (0, 0)),
                  pl.BlockSpec((1, LANES), lambda i: (0, 0)),
                  pl.BlockSpec((tm, tm), lambda i: (0, 0))],
        out_specs=[row(D_MODEL // 2), row(LANES), row(LANES), row(LANES),
                   pl.BlockSpec((1, 1, LANES), lambda i: (i // tiles_per_range, 0, 0))],
        out_shape=[jax.ShapeDtypeStruct((T, D_MODEL // 2), jnp.int32),
                   jax.ShapeDtypeStruct((T, LANES), jnp.int32),
                   jax.ShapeDtypeStruct((T, LANES), F32),
                   jax.ShapeDtypeStruct((T, LANES), jnp.int32),
                   jax.ShapeDtypeStruct((n_ranges, 1, LANES), F32)],
        scratch_shapes=[pltpu.VMEM((1, LANES), F32)],
        compiler_params=_cparams("arbitrary"),
        name="router",
    )(x, g.reshape(1, D_MODEL), mods, mods, wr, br, tri)


def _expert_kernel(blk_e_ref, x_ref, wgu_ref, bgu_ref, wdn_ref, bdn_ref, o_ref, wgu_bf, wdn_bf):
    i = pl.program_id(0)
    n_used = blk_e_ref[pl.num_programs(0)]
    new_expert = jnp.logical_or(i == 0, blk_e_ref[i] != blk_e_ref[jnp.maximum(i - 1, 0)])

    @pl.when(jnp.logical_and(new_expert, i < n_used))
    def _():
        wgu_bf[...] = wgu_ref[0].astype(BF16)
        wdn_bf[...] = wdn_ref[0].astype(BF16)

    @pl.when(i < n_used)
    def _():
        x = _unpack_bf16_pairs(x_ref[...]).astype(BF16)
        gu = jnp.dot(x, wgu_bf[...], preferred_element_type=F32) + bgu_ref[0]
        a = jnp.minimum(gu[:, :D_FF], SWIGLU_LIMIT)
        lin = jnp.clip(gu[:, D_FF:], -SWIGLU_LIMIT, SWIGLU_LIMIT)
        y = a * _sigmoid(SWIGLU_ALPHA * a) * (lin + 1.0)
        o_ref[...] = _pack_bf16_pairs(jnp.dot(y.astype(BF16), wdn_bf[...], preferred_element_type=F32) + bdn_ref[0])

    @pl.when(i >= n_used)
    def _():
        o_ref[...] = jnp.zeros(o_ref.shape, o_ref.dtype)


def _experts(xb, blk_e, w_gu, b_gu, w_dn, b_dn, layer):
    n_pad = xb.shape[0]
    n_blocks = n_pad // MOE_BLOCK
    e0 = layer * N_EXP
    w_gu = w_gu.reshape(DEPTH * N_EXP, D_MODEL, 2 * D_FF)
    w_dn = w_dn.reshape(DEPTH * N_EXP, D_FF, D_MODEL)
    grid_spec = pltpu.PrefetchScalarGridSpec(
        num_scalar_prefetch=1,
        grid=(n_blocks,),
        in_specs=[pl.BlockSpec((MOE_BLOCK, D_MODEL // 2), lambda i, e: (i, 0)),
                  pl.BlockSpec((1, D_MODEL, 2 * D_FF), lambda i, e: (e0 + e[i], 0, 0)),
                  pl.BlockSpec((1, 1, 2 * D_FF), lambda i, e: (e0 + e[i], 0, 0)),
                  pl.BlockSpec((1, D_FF, D_MODEL), lambda i, e: (e0 + e[i], 0, 0)),
                  pl.BlockSpec((1, 1, D_MODEL), lambda i, e: (e0 + e[i], 0, 0))],
        out_specs=pl.BlockSpec((MOE_BLOCK, D_MODEL // 2), lambda i, e: (i, 0)),
        scratch_shapes=[pltpu.VMEM((D_MODEL, 2 * D_FF), BF16), pltpu.VMEM((D_FF, D_MODEL), BF16)],
    )
    return pl.pallas_call(
        _expert_kernel,
        grid_spec=grid_spec,
        out_shape=jax.ShapeDtypeStruct((n_pad, D_MODEL // 2), jnp.int32),
        compiler_params=_cparams("arbitrary"),
        name="experts",
    )(blk_e, xb, w_gu, b_gu.reshape(DEPTH * N_EXP, 1, 2 * D_FF), w_dn, b_dn.reshape(DEPTH * N_EXP, 1, D_MODEL))


def _combine_kernel(x_ref, y_ref, w_ref, g2_ref, nf_ref, *rest, final_norm):
    o_ref = rest[-1]
    acc = None
    for k in range(TOP_K):
        term = _unpack_bf16_pairs(y_ref[k]) * w_ref[:, k:k + 1]
        acc = term if acc is None else acc + term
    x = x_ref[...] + g2_ref[0] * acc
    if final_norm:
        x = _rms(x) * nf_ref[...]
    o_ref[...] = x


def _combine(x, yg, wts, mods, mod_row, norm_f, final_norm, tm, row0, x_acc):
    T = yg.shape[1]
    i0 = row0 // tm
    glob = lambda w: pl.BlockSpec((tm, w), lambda i: (i0 + i, 0))
    in_specs = [glob(D_MODEL), pl.BlockSpec((TOP_K, tm, D_MODEL // 2), lambda i: (0, i, 0)), glob(LANES),
                pl.BlockSpec((1, 1, D_MODEL), lambda i: (mod_row(row0 + i * tm) * 6 + 5, 0, 0)),
                pl.BlockSpec((1, D_MODEL), lambda i: (0, 0))]
    args = [x, yg, wts, mods, norm_f.reshape(1, D_MODEL)]
    aliases = {}
    if x_acc is not None:
        in_specs.append(pl.BlockSpec(memory_space=pl.ANY))
        aliases = {len(args): 0}
        args.append(x_acc)
    return pl.pallas_call(
        functools.partial(_combine_kernel, final_norm=final_norm),
        grid=(T // tm,),
        in_specs=in_specs,
        out_specs=glob(D_MODEL),
        out_shape=jax.ShapeDtypeStruct(x.shape, F32),
        input_output_aliases=aliases,
        compiler_params=_cparams("parallel"),
        name="combine",
    )(*args)


def _route(top_e, rank, counts, T):
    n_assign = T * TOP_K
    n_blocks = n_assign // MOE_BLOCK + N_EXP
    e_flat = top_e.reshape(n_assign)
    onehot = e_flat[:, None] == jnp.arange(N_EXP, dtype=jnp.int32)[None, :]
    start = jnp.cumsum(counts) - counts
    padded = (counts + MOE_BLOCK - 1) // MOE_BLOCK * MOE_BLOCK
    pad_end = jnp.cumsum(padded)
    pad_start = pad_end - padded
    idx_bits = max(n_assign - 1, 1).bit_length()
    assert N_EXP << idx_bits < 2 ** 31
    key = jnp.sort(e_flat * (1 << idx_bits) + jnp.arange(n_assign, dtype=jnp.int32))
    order = key & ((1 << idx_bits) - 1)
    pos = rank.reshape(n_assign) + jnp.sum(jnp.where(onehot, pad_start[None, :], 0), axis=1)
    blk_first = jnp.arange(n_blocks, dtype=jnp.int32) * MOE_BLOCK
    blk_e = jnp.minimum(jnp.sum(pad_end[None, :] <= blk_first[:, None], axis=1), N_EXP - 1).astype(jnp.int32)
    n_used = (pad_end[-1] // MOE_BLOCK).astype(jnp.int32)
    r = blk_first[:, None] - pad_start[blk_e][:, None] + jnp.arange(MOE_BLOCK, dtype=jnp.int32)[None, :]
    valid = r < counts[blk_e][:, None]
    src = jnp.clip(start[blk_e][:, None] + r, 0, n_assign - 1)
    filler = (blk_first[:, None] + jnp.arange(MOE_BLOCK, dtype=jnp.int32)[None, :]) % T
    tok = jnp.where(valid, order[src] // TOP_K, filler).reshape(n_blocks * MOE_BLOCK).astype(jnp.int32)
    return tok, pos, jnp.concatenate([blk_e, n_used[None]])


def _prep_layer(l, w_in, sgu_w, w_branch, w_out, w_gu, w_dn, gd_A_log, gd_dt_bias, lb):
    offs = np.cumsum([0, BRANCH_W, BRANCH_W, 512, 512, 512, 512, 512, GD_QKV, 8, 8, 512, 3 * NA_W, N_BRANCH * D_MODEL])
    w = w_in[l]
    seg = lambda i, j: w[:, offs[i]:offs[j]]
    w_gd = jnp.concatenate([seg(7, 8), seg(10, 11), seg(8, 10),
                            jnp.zeros((D_MODEL, LANES - 4 * GD_HEADS), F32)], axis=1)
    par = jnp.zeros((2, LANES), F32)
    par = par.at[0, :2 * GD_HEADS].set(gd_A_log[l].reshape(-1)).at[1, :2 * GD_HEADS].set(gd_dt_bias[l].reshape(-1))
    lb_l = lb[:, l]
    return {
        'w_sgu': seg(0, 2).astype(BF16), 'w_hg': seg(2, 7).astype(BF16), 'w_gd': w_gd.astype(BF16),
        'w_na': seg(11, 12).astype(BF16), 'w_mg': seg(12, 13).astype(BF16),
        'sgu_w': sgu_w[l], 'wb': w_branch[l].astype(BF16), 'wo': w_out[l].astype(BF16),
        'w_gu': w_gu, 'w_dn': w_dn, 'layer': l, 'gd_par': par,
        'lbp': jnp.concatenate([jnp.log(lb_l), jnp.log1p(-lb_l), 1.0 - lb_l], axis=0),
    }


def _layer(x, groups, mods, mod_row, lw, p, norm_f, final_norm):
    T = x.shape[0]
    h = _normmod(x, p['norm1'], mods, mod_row, 1024, part_shift=0, part_scale=1)
    z_sgu = _matmul(h, lw['w_sgu'], 1024, 1024)
    z_hg = _matmul(h, lw['w_hg'], 1024, 2560)
    z_gd = _matmul(h, lw['w_gd'], 1024, 2176)
    z_na = _matmul(h, lw['w_na'], 1024, 1536)
    z_mg = _matmul(h, lw['w_mg'], 1024, 2048, out_dtype=BF16)

    x_mix, states = None, []
    for row0, B, L, ctx in groups:
        Tg = B * L
        o_a = _sgu(z_sgu, p['sgu_norm'], lw['sgu_w'], p['sgu_b'], 256, row0, Tg)
        s_hg0 = None if ctx is None else ctx[2]
        s_gd0 = None if ctx is None else ctx[3]
        o_hf, o_hb, s_hg = _hgrn(z_hg, lw['lbp'], s_hg0, B, L, row0)
        qkv = _gdprep(z_gd, p['gd_conv'], L, 256, row0, Tg)
        o_gf, o_gb, s_gd = _gdn(qkv, z_gd, lw['gd_par'], s_gd0, B, L, row0)
        if ctx is None:
            o_d = _ctx_attn(z_na, B, L, row0)
        else:
            o_d = _na_attn(z_na, ctx[0], ctx[1], _na_bias(p['na_rpb'], L // GRID_W), B, L, row0)
        x_mix = _merge(x, o_a, o_hf, o_hb, z_hg, o_gf, o_gb, z_gd, o_d, z_mg, p['hg_onorm'], p['gd_onorm'],
                       lw['wb'], lw['wo'], mods, mod_row, 256, row0, Tg, x_mix)
        states.append((s_hg, s_gd))
    x = x_mix

    h2, top_e, wts, rank, counts = _router(x, p['norm2'], mods, mod_row, p['w_router'], p['b_router'], 512, MOE_SPLIT)
    counts = counts[:, 0, :N_EXP].astype(jnp.int32)
    th = T // MOE_SPLIT
    x_out = None
    for j in range(MOE_SPLIT):
        r0 = j * th
        tok, pos, blk_e = _route(top_e[r0:r0 + th, :TOP_K], rank[r0:r0 + th, :TOP_K], counts[j], th)
        xb = _sc_gather(h2, tok + r0)
        yb = _experts(xb, blk_e, lw['w_gu'], p['b_gu'], lw['w_dn'], p['b_dn'], lw['layer'])
        yg = _sc_gather(yb, pos.reshape(th, TOP_K).T.reshape(-1)).reshape(TOP_K, th, D_MODEL // 2)
        x_out = _combine(x, yg, wts, mods, mod_row, norm_f, final_norm, 512, r0, x_out)
    return x_out, z_na, states


def kernel(x_prompt, x_sample, c, cache_na_k, cache_na_v, state_hgrn, state_gdn, c_ctx, w_ada, b_ada, norm1, norm2, norm_f, w_in, sgu_norm, sgu_w, sgu_b, hg_lb, hg_onorm, gd_conv, gd_A_log, gd_dt_bias, gd_onorm, na_rpb, w_branch, w_out, w_router, b_router, w_gu, b_gu, w_dn, b_dn):
    Bp, Lp, D = x_prompt.shape
    Bs, Ls, _ = x_sample.shape
    ctx_row = Bs
    cvecs = jnp.zeros((MOD_ROWS, D), F32).at[:Bs].set(c).at[ctx_row].set(c_ctx)
    mods = _modulation(cvecs, w_ada, b_ada)

    cs = jnp.cumsum(jax.nn.softmax(hg_lb.astype(F32), axis=1), axis=1)
    lb = cs - cs[:, :1]

    Tp, Ts = Bp * Lp, Bs * Ls
    x = jnp.concatenate([x_prompt.reshape(Tp, D), x_sample.reshape(Ts, D)], axis=0)
    mod_row = lambda r: jnp.where(r < Tp, ctx_row, (r - Tp) // Ls)
    ks_, vs_, hs_, gs_ = [], [], [], []
    for l in range(DEPTH):
        lw = _prep_layer(l, w_in, sgu_w, w_branch, w_out, w_gu, w_dn, gd_A_log, gd_dt_bias, lb)
        p = {'norm1': norm1[l], 'norm2': norm2[l], 'sgu_norm': sgu_norm[l], 'sgu_b': sgu_b[l],
             'gd_conv': gd_conv[l], 'hg_onorm': hg_onorm[l], 'gd_onorm': gd_onorm[l], 'na_rpb': na_rpb[l],
             'w_router': w_router[l], 'b_router': b_router[l], 'b_gu': b_gu, 'b_dn': b_dn}
        ctx = (cache_na_k[:, l].reshape(Bs, -1, NA_W), cache_na_v[:, l].reshape(Bs, -1, NA_W),
               state_hgrn[:, l], state_gdn[:, l])
        groups = [(0, Bp, Lp, None), (Tp, Bs, Ls, ctx)]
        x, z_na, states = _layer(x, groups, mods[l], mod_row, lw, p, norm_f, l == DEPTH - 1)
        ks_.append(z_na[:Tp, NA_W:2 * NA_W].reshape(Bp, Lp, NA_HEADS, NA_DH))
        vs_.append(z_na[:Tp, 2 * NA_W:].reshape(Bp, Lp, NA_HEADS, NA_DH))
        hs_.append(states[0][0])
        gs_.append(states[0][1])

    return (x[:Tp].reshape(Bp, Lp, D), x[Tp:].reshape(Bs, Ls, D),
            jnp.stack(ks_, axis=1), jnp.stack(vs_, axis=1), jnp.stack(hs_, axis=1), jnp.stack(gs_, axis=1))
```

```python
import functools
import math

import numpy as np
import jax
import jax.numpy as jnp
from jax import lax
from jax.experimental import pallas as pl
from jax.experimental.pallas import tpu as pltpu
from jax.experimental.pallas import tpu_sc as plsc

D_MODEL = 1024
DEPTH = 2
GRID_W = 64
BRANCH_W = 512
N_BRANCH = 4
SGU_CHUNK = 128
SGU_GROUPS = 4
HG_HEADS = 4
HG_DK = 128
HG_DV = 128
GD_HEADS = 4
GD_DK = 128
GD_DV = 128
NA_HEADS = 8
NA_DH = 64
NA_KH = 8
NA_KW = 16
N_EXP = 32
TOP_K = 4
D_FF = 1024
SWIGLU_LIMIT = 7.0
SWIGLU_ALPHA = 1.702
EPS = 1e-6

F32 = jnp.float32
BF16 = jnp.bfloat16
HI = lax.Precision.HIGHEST

LANES = 128
MOD_ROWS = 16
SCAN_C = 64
SCAN_ROWS = 4 * SCAN_C
MOE_BLOCK = 512
MOE_SPLIT = 2
NA_QROWS = 4
NA_KROWS = NA_QROWS + NA_KH
VMEM_LIMIT = 48 * 1024 * 1024

NT = (((1,), (1,)), ((), ()))
TN = (((0,), (0,)), ((), ()))


def _cparams(*sem):
    return pltpu.CompilerParams(dimension_semantics=sem, vmem_limit_bytes=VMEM_LIMIT)


def _bdot(a, b):
    return jnp.dot(a.astype(BF16), b.astype(BF16), preferred_element_type=F32)


def _bdot_g(a, b, dims):
    return lax.dot_general(a.astype(BF16), b.astype(BF16), dims, preferred_element_type=F32)


def _hdot(a, b):
    return jnp.dot(a, b, precision=HI, preferred_element_type=F32)


def _dot01(m3, x):
    hi = x.astype(BF16)
    r1 = x - hi.astype(F32)
    mid = r1.astype(BF16)
    lo = (r1 - mid.astype(F32)).astype(BF16)
    return jnp.dot(m3, jnp.concatenate([hi, mid, lo], axis=0), preferred_element_type=F32)


def _dot01_sel(m3, g, sel):
    hi = g.astype(BF16)
    r1 = g - hi.astype(F32)
    mid = r1.astype(BF16)
    lo = (r1 - mid.astype(F32)).astype(BF16)
    terms = [jnp.concatenate([t * sel, t], axis=1) for t in (hi, mid, lo)]
    return jnp.dot(m3, jnp.concatenate(terms, axis=0), preferred_element_type=F32)


def _dot3(a, b):
    ah = a.astype(BF16)
    al = (a - ah.astype(F32)).astype(BF16)
    bh = b.astype(BF16)
    bl = (b - bh.astype(F32)).astype(BF16)
    return jnp.dot(jnp.concatenate([al, ah, ah], axis=1), jnp.concatenate([bh, bl, bh], axis=0),
                   preferred_element_type=F32)


LOG2E = 1.4426950408889634
HI_HALF = 0xFFFF0000


def _pack_bf16_pairs(x):
    w = x.shape[1] // 2
    b = lax.bitcast_convert_type(x.astype(BF16).astype(F32), jnp.uint32)
    return lax.bitcast_convert_type((b[:, :w] >> 16) | (b[:, w:] & jnp.uint32(HI_HALF)), jnp.int32)


def _unpack_bf16_pairs(p):
    p = lax.bitcast_convert_type(p, jnp.uint32)
    lo = lax.bitcast_convert_type(p << 16, F32)
    hi = lax.bitcast_convert_type(p & jnp.uint32(HI_HALF), F32)
    return jnp.concatenate([lo, hi], axis=1)


def _sigmoid(x):
    return 0.5 * jnp.tanh(0.5 * x) + 0.5


def _silu(x):
    return x * _sigmoid(x)


def _logaddexp(a, b):
    return jnp.maximum(a, b) + jnp.log(1.0 + jnp.exp(-jnp.abs(a - b)))


def _softplus(x):
    return jnp.maximum(x, 0.0) + jnp.log(1.0 + jnp.exp(-jnp.abs(x)))


def _ada_kernel(c_ref, w_ref, b_ref, o_ref):
    o_ref[0] = _hdot(_silu(c_ref[...]), w_ref[0]) + b_ref[0]


def _modulation(cvecs, w_ada, b_ada):
    tn = 1536
    out = pl.pallas_call(
        _ada_kernel,
        grid=(DEPTH, 6 * D_MODEL // tn),
        in_specs=[pl.BlockSpec((MOD_ROWS, D_MODEL), lambda l, j: (0, 0)),
                  pl.BlockSpec((1, D_MODEL, tn), lambda l, j: (l, 0, j)),
                  pl.BlockSpec((1, 1, tn), lambda l, j: (l, 0, j))],
        out_specs=pl.BlockSpec((1, MOD_ROWS, tn), lambda l, j: (l, 0, j)),
        out_shape=jax.ShapeDtypeStruct((DEPTH, MOD_ROWS, 6 * D_MODEL), F32),
        compiler_params=_cparams("arbitrary", "arbitrary"),
        name="ada_modulation",
    )(cvecs, w_ada, b_ada.reshape(DEPTH, 1, 6 * D_MODEL))
    return out.reshape(DEPTH, MOD_ROWS * 6, 1, D_MODEL)


def _rms(x):
    return x * lax.rsqrt(jnp.mean(x * x, axis=-1, keepdims=True) + EPS)


def _normmod_kernel(x_ref, g_ref, sc_ref, sh_ref, o_ref):
    h = (_rms(x_ref[...]) * g_ref[...]) * (1.0 + sc_ref[0]) + sh_ref[0]
    o_ref[...] = h.astype(o_ref.dtype)


def _normmod(x, g, mods, mod_row, tm, part_shift, part_scale):
    T = x.shape[0]
    return pl.pallas_call(
        _normmod_kernel,
        grid=(T // tm,),
        in_specs=[pl.BlockSpec((tm, D_MODEL), lambda i: (i, 0)),
                  pl.BlockSpec((1, D_MODEL), lambda i: (0, 0)),
                  pl.BlockSpec((1, 1, D_MODEL), lambda i: (mod_row(i * tm) * 6 + part_scale, 0, 0)),
                  pl.BlockSpec((1, 1, D_MODEL), lambda i: (mod_row(i * tm) * 6 + part_shift, 0, 0))],
        out_specs=pl.BlockSpec((tm, D_MODEL), lambda i: (i, 0)),
        out_shape=jax.ShapeDtypeStruct((T, D_MODEL), BF16),
        compiler_params=_cparams("parallel"),
        name="normmod",
    )(x, g.reshape(1, D_MODEL), mods, mods)


def _mm_kernel(a_ref, w_ref, o_ref):
    o_ref[...] = jnp.dot(a_ref[...], w_ref[...], preferred_element_type=F32).astype(o_ref.dtype)


def _matmul(a, w, tm, tn, out_dtype=F32):
    T, K = a.shape
    N = w.shape[1]
    return pl.pallas_call(
        _mm_kernel,
        grid=(N // tn, T // tm),
        in_specs=[pl.BlockSpec((tm, K), lambda j, i: (i, 0)),
                  pl.BlockSpec((K, tn), lambda j, i: (0, j))],
        out_specs=pl.BlockSpec((tm, tn), lambda j, i: (i, j)),
        out_shape=jax.ShapeDtypeStruct((T, N), out_dtype),
        compiler_params=_cparams("parallel", "parallel"),
        name="in_proj",
    )(a, w)


def _sgu_kernel(u_ref, v_ref, gn_ref, ws_ref, bs_ref, o_ref):
    rows = u_ref.shape[0]
    gw = BRANCH_W // SGU_GROUPS
    u = jax.nn.gelu(u_ref[...])
    v = (_rms(jax.nn.gelu(v_ref[...])) * gn_ref[...]).astype(BF16)
    for n in range(rows // SGU_CHUNK):
        r = slice(n * SGU_CHUNK, (n + 1) * SGU_CHUNK)
        for g in range(SGU_GROUPS):
            cs = slice(g * gw, (g + 1) * gw)
            s = jnp.dot(ws_ref[g], v[r, cs], preferred_element_type=F32) + bs_ref[:, cs]
            o_ref[r, cs] = (u[r, cs] * s).astype(o_ref.dtype)


def _sgu(z_sgu, g_norm, w_s, b_s, rows, row0, T):
    gw = BRANCH_W // SGU_GROUPS
    b_exp = jnp.repeat(b_s.T, gw, axis=1)
    i0 = row0 // rows
    return pl.pallas_call(
        _sgu_kernel,
        grid=(T // rows,),
        in_specs=[pl.BlockSpec((rows, BRANCH_W), lambda i: (i0 + i, 0)),
                  pl.BlockSpec((rows, BRANCH_W), lambda i: (i0 + i, 1)),
                  pl.BlockSpec((1, BRANCH_W), lambda i: (0, 0)),
                  pl.BlockSpec((SGU_GROUPS, SGU_CHUNK, SGU_CHUNK), lambda i: (0, 0, 0)),
                  pl.BlockSpec((SGU_CHUNK, BRANCH_W), lambda i: (0, 0))],
        out_specs=pl.BlockSpec((rows, BRANCH_W), lambda i: (i, 0)),
        out_shape=jax.ShapeDtypeStruct((T, BRANCH_W), BF16),
        compiler_params=_cparams("parallel"),
        name="sgu",
    )(z_sgu, z_sgu, g_norm.reshape(1, BRANCH_W), w_s.astype(BF16), b_exp)


def _order(reverse):
    p = np.arange(SCAN_C)
    return SCAN_C - 1 - p if reverse else p


def _gla_consts():
    C = SCAN_C
    nlev = int(math.log2(C))
    mats, masks = [], []
    for reverse in (False, True):
        p = _order(reverse)
        pt, pr = p[:, None], p[None, :]
        m_d, k_d = [], []
        for lev in range(nlev):
            w = C >> (lev + 1)
            parent = p // (2 * w)
            later = (p % (2 * w)) >= w
            anchor = (parent * 2 * w + w - 1)[:, None]
            m = np.where(later[:, None], (pr > anchor) & (pr <= pt), (pr > pt) & (pr <= anchor))
            m_d.append(m)
            k_d.append((parent[:, None] == parent[None, :]) & later[:, None] & ~later[None, :])
        m_d.append(pr <= pt)
        m_d.append(pr > pt)
        k_d.append(np.eye(C, dtype=bool))
        mats.append(np.concatenate(m_d, axis=0))
        masks.append(np.stack(k_d))
    return (np.stack(mats).astype(np.float32), np.stack(masks).astype(np.float32))


def _delta_consts():
    C = SCAN_C
    tri, sl, incl, strict = [], [], [], []
    for reverse in (False, True):
        p = _order(reverse)
        pt, pr = p[:, None], p[None, :]
        tri.append(np.concatenate([pr <= pt, pr > pt], axis=0))
        sl.append(np.concatenate([pt > pr, np.zeros((C, LANES - C), bool)], axis=1))
        incl.append(pr <= pt)
        strict.append(pr < pt)
    f = lambda a: np.stack(a).astype(np.float32)
    return f(tri), f(sl), f(incl), f(strict)


def _hgrn_kernel(qf_ref, ff_ref, vf_ref, qb_ref, fb_ref, vb_ref, lb_ref, mat_ref, msk_ref, lat_ref, *rest, has_state):
    if has_state:
        s0_ref, of_ref, ob_ref, sfin_ref, st_ref = rest
    else:
        of_ref, ob_ref, sfin_ref, st_ref = rest
    C = SCAN_C
    nlev = msk_ref.shape[1] - 1
    c = pl.program_id(1)
    last_c = pl.num_programs(1) - 1

    @pl.when(c == 0)
    def _():
        for d in range(2):
            for h in range(HG_HEADS):
                if has_state:
                    st_ref[d, h] = s0_ref[0, d, h].T
                else:
                    st_ref[d, h] = jnp.zeros((HG_DV, HG_DK), F32)

    q_refs, f_refs, v_refs, o_refs = (qf_ref, qb_ref), (ff_ref, fb_ref), (vf_ref, vb_ref), (of_ref, ob_ref)
    sub = qf_ref.shape[0] // C
    rows = lambda j: slice(j * C, (j + 1) * C)
    col = lambda h: slice(h * HG_DK, (h + 1) * HG_DK)
    parts = [(d, j) for d in range(2) for j in range(sub)]
    q, k, fac, att = {}, {}, {}, {}
    for d, j in parts:
        zf = f_refs[d][rows(j), :]
        t = jnp.exp(-jnp.abs(zf))
        log_sig = jnp.minimum(zf, 0.0) - jnp.log(1.0 + t)
        logf = _logaddexp(lb_ref[d:d + 1, :], lb_ref[2 + d:3 + d, :] + log_sig)
        k[d, j] = lb_ref[4 + d:5 + d, :] * (jnp.where(zf >= 0.0, t, 1.0) / (1.0 + t))
        q[d, j] = _silu(q_refs[d][rows(j), :]) * (HG_DK ** -0.5)
        fac[d, j] = jnp.exp2(_dot01(mat_ref[d], logf * LOG2E))
    later = {(d, i): lat_ref[d, i] > 0.5 for d in range(2) for i in range(nlev)}
    for d, j in parts:
        for h in range(HG_HEADS):
            qh, kh = q[d, j][:, col(h)], k[d, j][:, col(h)]
            acc = msk_ref[d, nlev] * _bdot_g(qh, kh, NT)
            for i in range(nlev):
                z = (jnp.where(later[d, i], qh, kh) * fac[d, j][i * C:(i + 1) * C, col(h)]).astype(BF16)
                acc = acc + msk_ref[d, i] * lax.dot_general(z, z, NT, preferred_element_type=F32)
            att[d, j, h] = acc
    chains = [(d, h) for d in range(2) for h in range(HG_HEADS)]
    for t in range(sub):
        jd = (t, sub - 1 - t)
        for d, h in chains:
            j = jd[d]
            eb = fac[d, j][nlev * C:(nlev + 1) * C, col(h)]
            o_refs[d][rows(j), col(h)] = (_bdot(att[d, j, h], v_refs[d][rows(j), col(h)])
                                          + _bdot_g(q[d, j][:, col(h)] * eb, st_ref[d, h], NT)).astype(BF16)
        for d, h in chains:
            j = jd[d]
            eb = fac[d, j][nlev * C:(nlev + 1) * C, col(h)]
            er = fac[d, j][(nlev + 1) * C:, col(h)]
            e_last = eb[C - 1:C] if d == 0 else eb[0:1]
            st_ref[d, h] = st_ref[d, h] * e_last + _bdot_g(v_refs[d][rows(j), col(h)], k[d, j][:, col(h)] * er, TN)

    @pl.when(c == last_c)
    def _():
        for d in range(2):
            for h in range(HG_HEADS):
                sfin_ref[0, d, h] = st_ref[d, h].T


def _hgrn(z_hg, lbp, s0, B, L, row0):
    n = L // SCAN_ROWS
    c0 = row0 // SCAN_ROWS
    mats, masks = _gla_consts()
    later = np.broadcast_to(masks[:, :-1].any(axis=3)[..., None], masks[:, :-1].shape[:3] + (HG_DK,)).astype(np.float32)
    mats = np.tile(mats, (1, 1, 3))
    blk = (SCAN_ROWS, HG_HEADS * HG_DK)
    fwd = lambda col: pl.BlockSpec(blk, lambda b, c: (c0 + b * n + c, col))
    bwd = lambda col: pl.BlockSpec(blk, lambda b, c: (c0 + b * n + n - 1 - c, col))
    st_blk = (1, 2, HG_HEADS, HG_DK, HG_DV)
    in_specs = [fwd(0), fwd(1), fwd(3), bwd(0), bwd(2), bwd(3),
                pl.BlockSpec(lbp.shape, lambda b, c: (0, 0)),
                pl.BlockSpec(mats.shape, lambda b, c: (0, 0, 0)),
                pl.BlockSpec(masks.shape, lambda b, c: (0, 0, 0, 0)),
                pl.BlockSpec(later.shape, lambda b, c: (0, 0, 0, 0))]
    args = [z_hg] * 6 + [lbp, jnp.asarray(mats, BF16), jnp.asarray(masks), jnp.asarray(later)]
    if s0 is not None:
        in_specs.append(pl.BlockSpec(st_blk, lambda b, c: (b, 0, 0, 0, 0)))
        args.append(s0)
    return pl.pallas_call(
        functools.partial(_hgrn_kernel, has_state=s0 is not None),
        grid=(B, n),
        in_specs=in_specs,
        out_specs=[pl.BlockSpec(blk, lambda b, c: (b * n + c, 0)),
                   pl.BlockSpec(blk, lambda b, c: (b * n + n - 1 - c, 0)),
                   pl.BlockSpec(st_blk, lambda b, c: (b, 0, 0, 0, 0))],
        out_shape=[jax.ShapeDtypeStruct((B * L, HG_HEADS * HG_DV), BF16),
                   jax.ShapeDtypeStruct((B * L, HG_HEADS * HG_DV), BF16),
                   jax.ShapeDtypeStruct((B,) + st_blk[1:], F32)],
        scratch_shapes=[pltpu.VMEM((2, HG_HEADS, HG_DV, HG_DK), F32)],
        compiler_params=_cparams("parallel", "arbitrary"),
        name="hgrn_scan",
    )(*args)


GD_NQ = GD_HEADS * GD_DK
GD_QKV = 2 * GD_NQ + GD_HEADS * GD_DV
HALO = 8


def _gdprep_kernel(x_ref, prev_ref, next_ref, w_ref, o_ref, *, tiles_per_seq):
    R = x_ref.shape[0]
    t = pl.program_id(0) % tiles_per_seq
    x = x_ref[...]
    prev_row = jnp.where(t == 0, 0.0, prev_ref[HALO - 1:HALO, :])
    next_row = jnp.where(t == tiles_per_seq - 1, 0.0, next_ref[0:1, :])
    row = lax.broadcasted_iota(jnp.int32, x.shape, 0)
    xm1 = jnp.where(row == 0, prev_row, pltpu.roll(x, 1, 0))
    xp1 = jnp.where(row == R - 1, next_row, pltpu.roll(x, R - 1, 0))
    y = _silu(w_ref[0:1, :] * xm1 + w_ref[1:2, :] * x + w_ref[2:3, :] * xp1)
    for j in range(2 * GD_HEADS):
        cs = slice(j * GD_DK, (j + 1) * GD_DK)
        seg = y[:, cs]
        seg = seg * lax.rsqrt(jnp.sum(seg * seg, axis=-1, keepdims=True) + EPS)
        if j < GD_HEADS:
            seg = seg * (GD_DK ** -0.5)
        o_ref[:, cs] = seg
    o_ref[:, 2 * GD_NQ:] = y[:, 2 * GD_NQ:]


def _gdprep(z_gd, conv_w, L, rows, row0, T):
    tps = L // rows
    hb = rows // HALO
    nhalo = z_gd.shape[0] // HALO
    i0 = row0 // rows
    return pl.pallas_call(
        functools.partial(_gdprep_kernel, tiles_per_seq=tps),
        grid=(T // rows,),
        in_specs=[pl.BlockSpec((rows, GD_QKV), lambda i: (i0 + i, 0)),
                  pl.BlockSpec((HALO, GD_QKV), lambda i: (jnp.maximum((i0 + i) * hb - 1, 0), 0)),
                  pl.BlockSpec((HALO, GD_QKV), lambda i: (jnp.minimum((i0 + i + 1) * hb, nhalo - 1), 0)),
                  pl.BlockSpec((3, GD_QKV), lambda i: (0, 0))],
        out_specs=pl.BlockSpec((rows, GD_QKV), lambda i: (i, 0)),
        out_shape=jax.ShapeDtypeStruct((T, GD_QKV), F32),
        compiler_params=_cparams("parallel"),
        name="gdn_prep",
    )(z_gd, z_gd, z_gd, conv_w)


def _gdn_kernel(xf_ref, abf_ref, xb_ref, abb_ref, par_ref, tri_ref, sl_ref, incl_ref, strict_ref, msk_ref, *rest,
                has_state):
    if has_state:
        s0_ref, of_ref, ob_ref, sfin_ref, st_ref = rest
    else:
        of_ref, ob_ref, sfin_ref, st_ref = rest
    C = SCAN_C
    c = pl.program_id(1)
    last_c = pl.num_programs(1) - 1
    nlev = msk_ref.shape[1] - 1

    @pl.when(c == 0)
    def _():
        for d in range(2):
            for h in range(GD_HEADS):
                if has_state:
                    st_ref[d, h] = s0_ref[0, d, h].T
                else:
                    st_ref[d, h] = jnp.zeros((GD_DV, GD_DK), F32)

    x_refs, ab_refs, o_refs = (xf_ref, xb_ref), (abf_ref, abb_ref), (of_ref, ob_ref)
    sub = xf_ref.shape[0] // C
    rows = lambda j: slice(j * C, (j + 1) * C)
    chains = [(d, h) for d in range(2) for h in range(GD_HEADS)]
    parts = [(d, j, h) for d in range(2) for j in range(sub) for h in range(GD_HEADS)]
    q_of = lambda d, j, h: x_refs[d][rows(j), h * GD_DK:(h + 1) * GD_DK]
    k_of = lambda d, j, h: x_refs[d][rows(j), GD_NQ + h * GD_DK:GD_NQ + (h + 1) * GD_DK]
    v_of = lambda d, j, h: x_refs[d][rows(j), 2 * GD_NQ + h * GD_DV:2 * GD_NQ + (h + 1) * GD_DV]
    g_all, beta_all = [], []
    for d in range(2):
        ab = ab_refs[d][...]
        g_all.append(-jnp.exp(par_ref[0:1, :]) * _softplus(ab + par_ref[1:2, :]))
        beta_all.append(_sigmoid(ab))

    def beta_of(d, j, h):
        lane = 2 * GD_HEADS + d * GD_HEADS + h
        return jnp.broadcast_to(beta_all[d][rows(j), lane:lane + 1], (C, LANES))

    decay, e_cum, e_rest, kb, a = {}, {}, {}, {}, {}
    for d, j, h in parts:
        lane = d * GD_HEADS + h
        g_b = jnp.broadcast_to(g_all[d][rows(j), lane:lane + 1], (C, LANES))
        sums = _dot01_sel(tri_ref[d], g_b, sl_ref[d])
        decay[d, j, h] = jnp.exp(sums[:C, :C])
        e_cum[d, j, h] = jnp.exp(sums[:C, LANES:])
        e_rest[d, j, h] = jnp.exp(sums[C:, LANES:])
    for d, j, h in parts:
        k = k_of(d, j, h)
        kb[d, j, h] = k * beta_of(d, j, h)
        a[d, j, h] = strict_ref[d] * decay[d, j, h] * _bdot_g(kb[d, j, h], k, NT)
    inv_m = {pt: -(msk_ref[pt[0], nlev - 1] * a[pt]) for pt in parts}
    for lev in range(nlev - 2, -1, -1):
        a_w = {pt: msk_ref[pt[0], lev] * a[pt] for pt in parts}
        p = {pt: a_w[pt] + _bdot(inv_m[pt], a_w[pt]) for pt in parts}
        inv_m = {pt: inv_m[pt] - p[pt] - _bdot(p[pt], inv_m[pt]) for pt in parts}
    sol, att = {}, {}
    for d, j, h in parts:
        rhs = jnp.concatenate([v_of(d, j, h) * beta_of(d, j, h), kb[d, j, h] * e_cum[d, j, h]], axis=1)
        sol[d, j, h] = rhs + _dot3(inv_m[d, j, h], rhs)
        att[d, j, h] = incl_ref[d] * decay[d, j, h] * _bdot_g(q_of(d, j, h), k_of(d, j, h), NT)
    for t in range(sub):
        jd = (t, sub - 1 - t)
        u = {}
        for d, h in chains:
            pt = (d, jd[d], h)
            u[d, h] = sol[pt][:, :GD_DV] - _bdot_g(sol[pt][:, GD_DV:], st_ref[d, h], NT)
        for d, h in chains:
            pt = (d, jd[d], h)
            o_refs[d][rows(jd[d]), h * GD_DV:(h + 1) * GD_DV] = (
                _bdot_g(q_of(*pt) * e_cum[pt], st_ref[d, h], NT) + _bdot(att[pt], u[d, h])).astype(BF16)
        for d, h in chains:
            pt = (d, jd[d], h)
            e_last = e_cum[pt][C - 1:C] if d == 0 else e_cum[pt][0:1]
            st_ref[d, h] = st_ref[d, h] * e_last + _bdot_g(u[d, h], k_of(*pt) * e_rest[pt], TN)

    @pl.when(c == last_c)
    def _():
        for d in range(2):
            for h in range(GD_HEADS):
                sfin_ref[0, d, h] = st_ref[d, h].T


def _gdn(qkv, z_gd, par, s0, B, L, row0):
    n = L // SCAN_ROWS
    c0 = row0 // SCAN_ROWS
    tri, sl, incl, strict = (jnp.asarray(a) for a in _delta_consts())
    tri = jnp.tile(tri, (1, 1, 3)).astype(BF16)
    sl = sl.astype(BF16)
    masks = jnp.asarray(_gla_consts()[1])
    ab_col = (GD_QKV + GD_HEADS * GD_DV) // LANES
    xblk = (SCAN_ROWS, GD_QKV)
    ablk = (SCAN_ROWS, LANES)
    oblk = (SCAN_ROWS, GD_HEADS * GD_DV)
    st_blk = (1, 2, GD_HEADS, GD_DK, GD_DV)
    fwd = lambda b, c: b * n + c
    bwd = lambda b, c: b * n + n - 1 - c
    const3 = lambda a: pl.BlockSpec(a.shape, lambda b, c: (0, 0, 0))
    in_specs = [pl.BlockSpec(xblk, lambda b, c: (fwd(b, c), 0)),
                pl.BlockSpec(ablk, lambda b, c: (c0 + fwd(b, c), ab_col)),
                pl.BlockSpec(xblk, lambda b, c: (bwd(b, c), 0)),
                pl.BlockSpec(ablk, lambda b, c: (c0 + bwd(b, c), ab_col)),
                pl.BlockSpec(par.shape, lambda b, c: (0, 0)),
                const3(tri), const3(sl), const3(incl), const3(strict),
                pl.BlockSpec(masks.shape, lambda b, c: (0, 0, 0, 0))]
    args = [qkv, z_gd, qkv, z_gd, par, tri, sl, incl, strict, masks]
    if s0 is not None:
        in_specs.append(pl.BlockSpec(st_blk, lambda b, c: (b, 0, 0, 0, 0)))
        args.append(s0)
    return pl.pallas_call(
        functools.partial(_gdn_kernel, has_state=s0 is not None),
        grid=(B, n),
        in_specs=in_specs,
        out_specs=[pl.BlockSpec(oblk, lambda b, c: (fwd(b, c), 0)),
                   pl.BlockSpec(oblk, lambda b, c: (bwd(b, c), 0)),
                   pl.BlockSpec(st_blk, lambda b, c: (b, 0, 0, 0, 0))],
        out_shape=[jax.ShapeDtypeStruct((B * L, GD_HEADS * GD_DV), BF16),
                   jax.ShapeDtypeStruct((B * L, GD_HEADS * GD_DV), BF16),
                   jax.ShapeDtypeStruct((B,) + st_blk[1:], F32)],
        scratch_shapes=[pltpu.VMEM((2, GD_HEADS, GD_DV, GD_DK), F32)],
        compiler_params=_cparams("parallel", "arbitrary"),
        name="gdn_scan",
    )(*args)


NA_W = NA_HEADS * NA_DH


def _softmax_pv(s, v):
    m = jnp.max(s, axis=-1, keepdims=True)
    e = jnp.exp(s - m)
    den = jnp.sum(e, axis=-1, keepdims=True)
    return jnp.dot(e.astype(BF16), v, preferred_element_type=F32) / den


def _ctx_attn_kernel(q_ref, k_ref, v_ref, o_ref):
    for h in range(NA_HEADS):
        cs = slice(h * NA_DH, (h + 1) * NA_DH)
        q = (q_ref[:, cs] * (NA_DH ** -0.5)).astype(BF16)
        s = lax.dot_general(q, k_ref[:, cs].astype(BF16), NT, preferred_element_type=F32)
        o_ref[:, cs] = _softmax_pv(s, v_ref[:, cs].astype(BF16)).astype(o_ref.dtype)


def _ctx_attn(z_na, B, L, row0):
    blk = (L, NA_W)
    b0 = row0 // L
    return pl.pallas_call(
        _ctx_attn_kernel,
        grid=(B,),
        in_specs=[pl.BlockSpec(blk, lambda b: (b0 + b, 0)),
                  pl.BlockSpec(blk, lambda b: (b0 + b, 1)),
                  pl.BlockSpec(blk, lambda b: (b0 + b, 2))],
        out_specs=pl.BlockSpec(blk, lambda b: (b, 0)),
        out_shape=jax.ShapeDtypeStruct((B * L, NA_W), BF16),
        compiler_params=_cparams("parallel"),
        name="ctx_attn",
    )(z_na, z_na, z_na)


def _na_kernel(q_ref, *rest, n_kblk, nkeys_nb):
    k_refs = rest[:n_kblk]
    v_refs = rest[n_kblk:2 * n_kblk]
    kc_ref, vc_ref, bias_ref, o_ref, kbuf, vbuf = rest[2 * n_kblk:]
    qb = q_ref.shape[0]
    for i in range(n_kblk):
        kbuf[i * qb:(i + 1) * qb, :] = k_refs[i][...].astype(BF16)
        vbuf[i * qb:(i + 1) * qb, :] = v_refs[i][...].astype(BF16)
    kbuf[nkeys_nb:, :] = kc_ref[0].astype(BF16)
    vbuf[nkeys_nb:, :] = vc_ref[0].astype(BF16)
    for h in range(NA_HEADS):
        cs = slice(h * NA_DH, (h + 1) * NA_DH)
        q = (q_ref[:, cs] * (NA_DH ** -0.5)).astype(BF16)
        s = lax.dot_general(q, kbuf[:, cs], NT, preferred_element_type=F32)
        s_nb = s[:, :nkeys_nb] + bias_ref[0, h]
        s_cx = s[:, nkeys_nb:]
        m = jnp.maximum(jnp.max(s_nb, axis=-1, keepdims=True), jnp.max(s_cx, axis=-1, keepdims=True))
        e_nb = jnp.exp(s_nb - m)
        e_cx = jnp.exp(s_cx - m)
        den = jnp.sum(e_nb, axis=-1, keepdims=True) + jnp.sum(e_cx, axis=-1, keepdims=True)
        pv = (jnp.dot(e_nb.astype(BF16), vbuf[:nkeys_nb, cs], preferred_element_type=F32)
              + jnp.dot(e_cx.astype(BF16), vbuf[nkeys_nb:, cs], preferred_element_type=F32))
        o_ref[:, cs] = (pv / den).astype(o_ref.dtype)


def _na_bias(rpb, rows):
    qr, kr, col = np.arange(NA_QROWS), np.arange(NA_KROWS), np.arange(GRID_W)
    nblk = rows // NA_QROWS
    ndr, ndc = 2 * NA_KH - 1, 2 * NA_KW - 1
    sel_r, row_ok = [], []
    for m in (0, 1, nblk - 1):
        r = (NA_QROWS * m + qr)[:, None]
        start = np.clip(NA_QROWS * m - NA_KH // 2, 0, rows - NA_KROWS)
        kra = (start + kr)[None, :]
        r0 = np.clip(r - NA_KH // 2, 0, rows - NA_KH)
        row_ok.append((kra >= r0) & (kra < r0 + NA_KH))
        dr = np.clip(kra - r + NA_KH - 1, 0, ndr - 1)
        sel_r.append(dr[..., None] == np.arange(ndr))
    sel_r = np.stack(sel_r).astype(np.float32)
    row_ok = np.stack(row_ok)
    col_start = np.clip(col - NA_KW // 2, 0, GRID_W - NA_KW)[:, None]
    col_ok = (col[None, :] >= col_start) & (col[None, :] < col_start + NA_KW)
    dc = np.clip(col[None, :] - col[:, None], -(NA_KW - 1), NA_KW - 1) + NA_KW - 1
    sel_c = (dc[..., None] == np.arange(ndc)).astype(np.float32)
    bias = jnp.einsum('hab,vqka,xyb->vhqxky', rpb.astype(F32), sel_r, sel_c, precision=HI)
    ok = row_ok[:, None, :, None, :, None] & col_ok[None, None, None, :, None, :]
    bias = jnp.where(ok, bias, -jnp.inf)
    return bias.reshape(3, NA_HEADS, NA_QROWS * GRID_W, NA_KROWS * GRID_W)


def _na_attn(z_na, k_ctx, v_ctx, bias, B, S, row0):
    rows = S // GRID_W
    qb = NA_QROWS * GRID_W
    m0 = row0 // qb
    nblk = rows // NA_QROWS
    n_kblk = NA_KROWS // NA_QROWS
    lc = k_ctx.shape[1]
    nkeys_nb = NA_KROWS * GRID_W
    kstart = lambda m: jnp.clip(m - 1, 0, nblk - n_kblk)
    variant = lambda m: jnp.where(m == 0, 0, jnp.where(m == nblk - 1, 2, 1))
    kv_specs = lambda col: [pl.BlockSpec((qb, NA_W), functools.partial(
        lambda b, m, i, col: (m0 + b * nblk + kstart(m) + i, col), i=i, col=col)) for i in range(n_kblk)]
    return pl.pallas_call(
        functools.partial(_na_kernel, n_kblk=n_kblk, nkeys_nb=nkeys_nb),
        grid=(B, nblk),
        in_specs=([pl.BlockSpec((qb, NA_W), lambda b, m: (m0 + b * nblk + m, 0))] + kv_specs(1) + kv_specs(2)
                  + [pl.BlockSpec((1, lc, NA_W), lambda b, m: (b, 0, 0)),
                     pl.BlockSpec((1, lc, NA_W), lambda b, m: (b, 0, 0)),
                     pl.BlockSpec((1,) + bias.shape[1:], lambda b, m: (variant(m), 0, 0, 0))]),
        out_specs=pl.BlockSpec((qb, NA_W), lambda b, m: (b * nblk + m, 0)),
        out_shape=jax.ShapeDtypeStruct((B * S, NA_W), BF16),
        scratch_shapes=[pltpu.VMEM((nkeys_nb + lc, NA_W), BF16), pltpu.VMEM((nkeys_nb + lc, NA_W), BF16)],
        compiler_params=_cparams("parallel", "arbitrary"),
        name="na_attn",
    )(z_na, *([z_na] * (2 * n_kblk)), k_ctx, v_ctx, bias)


def _head_rms(o, g_row):
    parts = []
    for h in range(o.shape[1] // LANES):
        parts.append(_rms(o[:, h * LANES:(h + 1) * LANES]) * g_row)
    return jnp.concatenate(parts, axis=1)


def _merge_kernel(x_ref, oa_ref, hf_ref, hb_ref, hg_ref, gf_ref, gb_ref, gg_ref, od_ref, mg_ref,
                  hn_ref, gn_ref, wb_ref, wo_ref, g1_ref, *rest):
    o_ref = rest[-1]
    o_b = _head_rms(hf_ref[...].astype(F32) + hb_ref[...].astype(F32), hn_ref[...]) * _silu(hg_ref[...])
    o_c = _head_rms(gf_ref[...].astype(F32) + gb_ref[...].astype(F32), gn_ref[...]) * _silu(gg_ref[...])
    branches = (oa_ref[...], o_b.astype(BF16), o_c.astype(BF16), od_ref[...])
    merged = None
    for n_, o_n in enumerate(branches):
        gate = _sigmoid(mg_ref[:, n_ * D_MODEL:(n_ + 1) * D_MODEL].astype(F32))
        term = gate * jnp.dot(o_n, wb_ref[n_], preferred_element_type=F32)
        merged = term if merged is None else merged + term
    mix = jnp.dot(merged.astype(BF16), wo_ref[...], preferred_element_type=F32)
    o_ref[...] = x_ref[...] + g1_ref[0] * mix


def _merge(x, o_a, o_hf, o_hb, z_hg, o_gf, o_gb, z_gd, o_d, z_mg, hg_onorm, gd_onorm, wb, wo, mods, mod_row, tm,
           row0, T):
    i0 = row0 // tm
    glob = lambda w, col=0: pl.BlockSpec((tm, w), lambda i: (i0 + i, col))
    loc = lambda w: pl.BlockSpec((tm, w), lambda i: (i, 0))
    full = lambda a: pl.BlockSpec(a.shape, lambda i: (0,) * a.ndim)
    hn = hg_onorm.reshape(1, HG_DV)
    gn = gd_onorm.reshape(1, GD_DV)
    in_specs = [glob(D_MODEL), loc(BRANCH_W), loc(BRANCH_W), loc(BRANCH_W), glob(BRANCH_W, 4),
                loc(BRANCH_W), loc(BRANCH_W), glob(BRANCH_W, GD_QKV // BRANCH_W), loc(BRANCH_W),
                glob(N_BRANCH * D_MODEL), full(hn), full(gn), full(wb), full(wo),
                pl.BlockSpec((1, 1, D_MODEL), lambda i: (mod_row(row0 + i * tm) * 6 + 2, 0, 0))]
    args = [x, o_a, o_hf, o_hb, z_hg, o_gf, o_gb, z_gd, o_d, z_mg, hn, gn, wb, wo, mods]
    aliases = {0: 0}
    return pl.pallas_call(
        _merge_kernel,
        grid=(T // tm,),
        in_specs=in_specs,
        out_specs=glob(D_MODEL),
        out_shape=jax.ShapeDtypeStruct(x.shape, F32),
        input_output_aliases=aliases,
        compiler_params=_cparams("parallel"),
        name="merge",
    )(*args)


SC_CORES = 2
SC_SUBCORES = 16
SC_WIN = 64


def _sc_gather(table, idx):
    V, D = table.shape
    N = idx.shape[0]
    nw = SC_CORES * SC_SUBCORES
    per_w = N // nw
    n_win = per_w // SC_WIN
    assert per_w * nw == N and n_win * SC_WIN == per_w
    mesh = plsc.VectorSubcoreMesh(core_axis_name="c", subcore_axis_name="s")

    @functools.partial(
        pl.kernel, mesh=mesh,
        out_type=jax.ShapeDtypeStruct((N, D), table.dtype),
        scratch_types=[pltpu.VMEM((n_win, SC_WIN), jnp.int32),
                       pltpu.VMEM((SC_WIN, D), table.dtype),
                       pltpu.SemaphoreType.DMA],
    )
    def gather_rows(table_hbm, idx_hbm, out_hbm, idx_v, rows_v, sem):
        wid = lax.axis_index("s") * SC_CORES + lax.axis_index("c")
        pltpu.sync_copy(idx_hbm.at[wid], idx_v)

        @pl.loop(0, n_win)
        def _(w):
            pltpu.async_copy(table_hbm.at[idx_v.at[w]], rows_v, sem).wait()
            pltpu.sync_copy(rows_v, out_hbm.at[pl.ds(wid * per_w + w * SC_WIN, SC_WIN)])

    return gather_rows(table, idx.reshape(nw, n_win, SC_WIN))


def _router_kernel(x_ref, g_ref, sc_ref, sh_ref, wr_ref, br_ref, tri_ref, h_ref, e_ref, w_ref, r_ref, cnt_ref, run_ref,
                   *, tiles_per_range):
    @pl.when(pl.program_id(0) % tiles_per_range == 0)
    def _():
        run_ref[...] = jnp.zeros(run_ref.shape, F32)

    h = (_rms(x_ref[...]) * g_ref[...]) * (1.0 + sc_ref[0]) + sh_ref[0]
    h_ref[...] = _pack_bf16_pairs(h)
    h_hi = h.astype(BF16)
    h_lo = (h - h_hi.astype(F32)).astype(BF16)
    w_hi = wr_ref[...].astype(BF16)
    w_lo = (wr_ref[...] - w_hi.astype(F32)).astype(BF16)
    dot = lambda a, b: jnp.dot(a, b, preferred_element_type=F32)
    logits = (dot(h_lo, w_hi) + dot(h_hi, w_lo)) + dot(h_hi, w_hi) + br_ref[...]
    lane = lax.broadcasted_iota(jnp.int32, logits.shape, 1)
    e_out = jnp.zeros(logits.shape, jnp.int32)
    v_out = jnp.zeros(logits.shape, F32)
    onehot = jnp.zeros(logits.shape, F32)
    top0, picks = None, []
    for k in range(TOP_K):
        m = jnp.max(logits, axis=-1, keepdims=True)
        idx = jnp.min(jnp.where(logits == m, lane, LANES), axis=-1, keepdims=True)
        if k == 0:
            top0 = m
        picks.append(lane == idx)
        e_out = jnp.where(lane == k, idx, e_out)
        v_out = jnp.where(lane == k, jnp.exp(m - top0), v_out)
        onehot = jnp.where(picks[k], 1.0, onehot)
        logits = jnp.where(picks[k], -jnp.inf, logits)
    e_ref[...] = e_out
    w_ref[...] = v_out / jnp.sum(v_out, axis=-1, keepdims=True)
    before = run_ref[...] + jnp.dot(tri_ref[...], onehot.astype(BF16), preferred_element_type=F32)
    r_out = jnp.zeros(logits.shape, jnp.int32)
    for k in range(TOP_K):
        rank = jnp.sum(jnp.where(picks[k], before, 0.0), axis=-1, keepdims=True)
        r_out = jnp.where(lane == k, rank.astype(jnp.int32), r_out)
    r_ref[...] = r_out
    run_ref[...] = run_ref[...] + jnp.sum(onehot, axis=0, keepdims=True)
    cnt_ref[0] = run_ref[...]


def _router(x, g, mods, mod_row, w_router, b_router, tm, n_ranges):
    T = x.shape[0]
    wr = jnp.zeros((D_MODEL, LANES), F32).at[:, :N_EXP].set(w_router)
    br = jnp.full((1, LANES), -jnp.inf, F32).at[0, :N_EXP].set(b_router)
    tri = jnp.asarray(np.tril(np.ones((tm, tm), np.float32), -1), BF16)
    tiles_per_range = T // n_ranges // tm
    row = lambda w: pl.BlockSpec((tm, w), lambda i: (i, 0))
    return pl.pallas_call(
        functools.partial(_router_kernel, tiles_per_range=tiles_per_range),
        grid=(T // tm,),
        in_specs=[row(D_MODEL),
                  pl.BlockSpec((1, D_MODEL), lambda i: (0, 0)),
                  pl.BlockSpec((1, 1, D_MODEL), lambda i: (mod_row(i * tm) * 6 + 4, 0, 0)),
                  pl.BlockSpec((1, 1, D_MODEL), lambda i: (mod_row(i * tm) * 6 + 3, 0, 0)),
                  pl.BlockSpec((D_MODEL, LANES), lambda i: (0, 0)),
                  pl.BlockSpec((1, LANES), lambda i: (0, 0)),
                  pl.BlockSpec((tm, tm), lambda i: (0, 0))],
        out_specs=[row(D_MODEL // 2), row(LANES), row(LANES), row(LANES),
                   pl.BlockSpec((1, 1, LANES), lambda i: (i // tiles_per_range, 0, 0))],
        out_shape=[jax.ShapeDtypeStruct((T, D_MODEL // 2), jnp.int32),
                   jax.ShapeDtypeStruct((T, LANES), jnp.int32),
                   jax.ShapeDtypeStruct((T, LANES), F32),
                   jax.ShapeDtypeStruct((T, LANES), jnp.int32),
                   jax.ShapeDtypeStruct((n_ranges, 1, LANES), F32)],
        scratch_shapes=[pltpu.VMEM((1, LANES), F32)],
        compiler_params=_cparams("arbitrary"),
        name="router",
    )(x, g.reshape(1, D_MODEL), mods, mods, wr, br, tri)


def _expert_kernel(blk_e_ref, x_ref, wgu_ref, bgu_ref, wdn_ref, bdn_ref, o_ref, wgu_bf, wdn_bf):
    i = pl.program_id(0)
    n_used = blk_e_ref[pl.num_programs(0)]
    new_expert = jnp.logical_or(i == 0, blk_e_ref[i] != blk_e_ref[jnp.maximum(i - 1, 0)])

    @pl.when(jnp.logical_and(new_expert, i < n_used))
    def _():
        wgu_bf[...] = wgu_ref[0].astype(BF16)
        wdn_bf[...] = wdn_ref[0].astype(BF16)

    @pl.when(i < n_used)
    def _():
        x = _unpack_bf16_pairs(x_ref[...]).astype(BF16)
        gu = jnp.dot(x, wgu_bf[...], preferred_element_type=F32) + bgu_ref[0]
        a = jnp.minimum(gu[:, :D_FF], SWIGLU_LIMIT)
        lin = jnp.clip(gu[:, D_FF:], -SWIGLU_LIMIT, SWIGLU_LIMIT)
        y = a * _sigmoid(SWIGLU_ALPHA * a) * (lin + 1.0)
        o_ref[...] = _pack_bf16_pairs(jnp.dot(y.astype(BF16), wdn_bf[...], preferred_element_type=F32) + bdn_ref[0])

    @pl.when(i >= n_used)
    def _():
        o_ref[...] = jnp.zeros(o_ref.shape, o_ref.dtype)


def _experts(xb, blk_e, w_gu, b_gu, w_dn, b_dn, layer):
    n_pad = xb.shape[0]
    n_blocks = n_pad // MOE_BLOCK
    e0 = layer * N_EXP
    w_gu = w_gu.reshape(DEPTH * N_EXP, D_MODEL, 2 * D_FF)
    w_dn = w_dn.reshape(DEPTH * N_EXP, D_FF, D_MODEL)
    grid_spec = pltpu.PrefetchScalarGridSpec(
        num_scalar_prefetch=1,
        grid=(n_blocks,),
        in_specs=[pl.BlockSpec((MOE_BLOCK, D_MODEL // 2), lambda i, e: (i, 0)),
                  pl.BlockSpec((1, D_MODEL, 2 * D_FF), lambda i, e: (e0 + e[i], 0, 0)),
                  pl.BlockSpec((1, 1, 2 * D_FF), lambda i, e: (e0 + e[i], 0, 0)),
                  pl.BlockSpec((1, D_FF, D_MODEL), lambda i, e: (e0 + e[i], 0, 0)),
                  pl.BlockSpec((1, 1, D_MODEL), lambda i, e: (e0 + e[i], 0, 0))],
        out_specs=pl.BlockSpec((MOE_BLOCK, D_MODEL // 2), lambda i, e: (i, 0)),
        scratch_shapes=[pltpu.VMEM((D_MODEL, 2 * D_FF), BF16), pltpu.VMEM((D_FF, D_MODEL), BF16)],
    )
    return pl.pallas_call(
        _expert_kernel,
        grid_spec=grid_spec,
        out_shape=jax.ShapeDtypeStruct((n_pad, D_MODEL // 2), jnp.int32),
        compiler_params=_cparams("arbitrary"),
        name="experts",
    )(blk_e, xb, w_gu, b_gu.reshape(DEPTH * N_EXP, 1, 2 * D_FF), w_dn, b_dn.reshape(DEPTH * N_EXP, 1, D_MODEL))


def _combine_kernel(x_ref, y_ref, w_ref, g2_ref, nf_ref, *rest, final_norm):
    o_ref = rest[-1]
    acc = None
    for k in range(TOP_K):
        term = _unpack_bf16_pairs(y_ref[k]) * w_ref[:, k:k + 1]
        acc = term if acc is None else acc + term
    x = x_ref[...] + g2_ref[0] * acc
    if final_norm:
        x = _rms(x) * nf_ref[...]
    o_ref[...] = x


def _combine(x, yg, wts, mods, mod_row, norm_f, final_norm, tm, row0):
    T = yg.shape[1]
    i0 = row0 // tm
    glob = lambda w: pl.BlockSpec((tm, w), lambda i: (i0 + i, 0))
    in_specs = [glob(D_MODEL), pl.BlockSpec((TOP_K, tm, D_MODEL // 2), lambda i: (0, i, 0)), glob(LANES),
                pl.BlockSpec((1, 1, D_MODEL), lambda i: (mod_row(row0 + i * tm) * 6 + 5, 0, 0)),
                pl.BlockSpec((1, D_MODEL), lambda i: (0, 0))]
    args = [x, yg, wts, mods, norm_f.reshape(1, D_MODEL)]
    aliases = {0: 0}
    return pl.pallas_call(
        functools.partial(_combine_kernel, final_norm=final_norm),
        grid=(T // tm,),
        in_specs=in_specs,
        out_specs=glob(D_MODEL),
        out_shape=jax.ShapeDtypeStruct(x.shape, F32),
        input_output_aliases=aliases,
        compiler_params=_cparams("parallel"),
        name="combine",
    )(*args)


def _route(top_e, rank, counts, T):
    n_assign = T * TOP_K
    n_blocks = n_assign // MOE_BLOCK + N_EXP
    e_flat = top_e.reshape(n_assign)
    onehot = e_flat[:, None] == jnp.arange(N_EXP, dtype=jnp.int32)[None, :]
    start = jnp.cumsum(counts) - counts
    padded = (counts + MOE_BLOCK - 1) // MOE_BLOCK * MOE_BLOCK
    pad_end = jnp.cumsum(padded)
    pad_start = pad_end - padded
    idx_bits = max(n_assign - 1, 1).bit_length()
    assert N_EXP << idx_bits < 2 ** 31
    key = jnp.sort(e_flat * (1 << idx_bits) + jnp.arange(n_assign, dtype=jnp.int32))
    order = key & ((1 << idx_bits) - 1)
    pos = rank.reshape(n_assign) + jnp.sum(jnp.where(onehot, pad_start[None, :], 0), axis=1)
    blk_first = jnp.arange(n_blocks, dtype=jnp.int32) * MOE_BLOCK
    blk_e = jnp.minimum(jnp.sum(pad_end[None, :] <= blk_first[:, None], axis=1), N_EXP - 1).astype(jnp.int32)
    n_used = (pad_end[-1] // MOE_BLOCK).astype(jnp.int32)
    r = blk_first[:, None] - pad_start[blk_e][:, None] + jnp.arange(MOE_BLOCK, dtype=jnp.int32)[None, :]
    valid = r < counts[blk_e][:, None]
    src = jnp.clip(start[blk_e][:, None] + r, 0, n_assign - 1)
    filler = (blk_first[:, None] + jnp.arange(MOE_BLOCK, dtype=jnp.int32)[None, :]) % T
    tok = jnp.where(valid, order[src] // TOP_K, filler).reshape(n_blocks * MOE_BLOCK).astype(jnp.int32)
    return tok, pos, jnp.concatenate([blk_e, n_used[None]])


def _prep_layer(l, w_in, sgu_w, w_branch, w_out, w_gu, w_dn, gd_A_log, gd_dt_bias, lb):
    offs = np.cumsum([0, BRANCH_W, BRANCH_W, 512, 512, 512, 512, 512, GD_QKV, 8, 8, 512, 3 * NA_W, N_BRANCH * D_MODEL])
    w = w_in[l]
    seg = lambda i, j: w[:, offs[i]:offs[j]]
    w_gd = jnp.concatenate([seg(7, 8), seg(10, 11), seg(8, 10),
                            jnp.zeros((D_MODEL, LANES - 4 * GD_HEADS), F32)], axis=1)
    par = jnp.zeros((2, LANES), F32)
    par = par.at[0, :2 * GD_HEADS].set(gd_A_log[l].reshape(-1)).at[1, :2 * GD_HEADS].set(gd_dt_bias[l].reshape(-1))
    lb_l = lb[:, l]
    return {
        'w_sgu': seg(0, 2).astype(BF16), 'w_hg': seg(2, 7).astype(BF16), 'w_gd': w_gd.astype(BF16),
        'w_na': seg(11, 12).astype(BF16), 'w_mg': seg(12, 13).astype(BF16),
        'sgu_w': sgu_w[l], 'wb': w_branch[l].astype(BF16), 'wo': w_out[l].astype(BF16),
        'w_gu': w_gu, 'w_dn': w_dn, 'layer': l, 'gd_par': par,
        'lbp': jnp.concatenate([jnp.log(lb_l), jnp.log1p(-lb_l), 1.0 - lb_l], axis=0),
    }


def _layer(x, groups, mods, mod_row, lw, p, norm_f, final_norm):
    T = x.shape[0]
    h = _normmod(x, p['norm1'], mods, mod_row, 1024, part_shift=0, part_scale=1)
    z_sgu = _matmul(h, lw['w_sgu'], 1024, 1024)
    z_hg = _matmul(h, lw['w_hg'], 1024, 2560)
    z_gd = _matmul(h, lw['w_gd'], 1024, 2176)
    z_na = _matmul(h, lw['w_na'], 1024, 1536)
    z_mg = _matmul(h, lw['w_mg'], 1024, 2048, out_dtype=BF16)

    states = []
    for row0, B, L, ctx in groups:
        Tg = B * L
        o_a = _sgu(z_sgu, p['sgu_norm'], lw['sgu_w'], p['sgu_b'], 256, row0, Tg)
        s_hg0 = None if ctx is None else ctx[2]
        s_gd0 = None if ctx is None else ctx[3]
        o_hf, o_hb, s_hg = _hgrn(z_hg, lw['lbp'], s_hg0, B, L, row0)
        qkv = _gdprep(z_gd, p['gd_conv'], L, 256, row0, Tg)
        o_gf, o_gb, s_gd = _gdn(qkv, z_gd, lw['gd_par'], s_gd0, B, L, row0)
        if ctx is None:
            o_d = _ctx_attn(z_na, B, L, row0)
        else:
            o_d = _na_attn(z_na, ctx[0], ctx[1], _na_bias(p['na_rpb'], L // GRID_W), B, L, row0)
        x = _merge(x, o_a, o_hf, o_hb, z_hg, o_gf, o_gb, z_gd, o_d, z_mg, p['hg_onorm'], p['gd_onorm'],
                   lw['wb'], lw['wo'], mods, mod_row, 256, row0, Tg)
        states.append((s_hg, s_gd))

    h2, top_e, wts, rank, counts = _router(x, p['norm2'], mods, mod_row, p['w_router'], p['b_router'], 512, MOE_SPLIT)
    counts = counts[:, 0, :N_EXP].astype(jnp.int32)
    th = T // MOE_SPLIT
    for j in range(MOE_SPLIT):
        r0 = j * th
        tok, pos, blk_e = _route(top_e[r0:r0 + th, :TOP_K], rank[r0:r0 + th, :TOP_K], counts[j], th)
        xb = _sc_gather(h2, tok + r0)
        yb = _experts(xb, blk_e, lw['w_gu'], p['b_gu'], lw['w_dn'], p['b_dn'], lw['layer'])
        yg = _sc_gather(yb, pos.reshape(th, TOP_K).T.reshape(-1)).reshape(TOP_K, th, D_MODEL // 2)
        x = _combine(x, yg, wts, mods, mod_row, norm_f, final_norm, 512, r0)
    return x, z_na, states


def kernel(x_prompt, x_sample, c, cache_na_k, cache_na_v, state_hgrn, state_gdn, c_ctx, w_ada, b_ada, norm1, norm2, norm_f, w_in, sgu_norm, sgu_w, sgu_b, hg_lb, hg_onorm, gd_conv, gd_A_log, gd_dt_bias, gd_onorm, na_rpb, w_branch, w_out, w_router, b_router, w_gu, b_gu, w_dn, b_dn):
    Bp, Lp, D = x_prompt.shape
    Bs, Ls, _ = x_sample.shape
    ctx_row = Bs
    cvecs = jnp.zeros((MOD_ROWS, D), F32).at[:Bs].set(c).at[ctx_row].set(c_ctx)
    mods = _modulation(cvecs, w_ada, b_ada)

    cs = jnp.cumsum(jax.nn.softmax(hg_lb.astype(F32), axis=1), axis=1)
    lb = cs - cs[:, :1]

    Tp, Ts = Bp * Lp, Bs * Ls
    x = jnp.concatenate([x_prompt.reshape(Tp, D), x_sample.reshape(Ts, D)], axis=0)
    mod_row = lambda r: jnp.where(r < Tp, ctx_row, (r - Tp) // Ls)
    ks_, vs_, hs_, gs_ = [], [], [], []
    for l in range(DEPTH):
        lw = _prep_layer(l, w_in, sgu_w, w_branch, w_out, w_gu, w_dn, gd_A_log, gd_dt_bias, lb)
        p = {'norm1': norm1[l], 'norm2': norm2[l], 'sgu_norm': sgu_norm[l], 'sgu_b': sgu_b[l],
             'gd_conv': gd_conv[l], 'hg_onorm': hg_onorm[l], 'gd_onorm': gd_onorm[l], 'na_rpb': na_rpb[l],
             'w_router': w_router[l], 'b_router': b_router[l], 'b_gu': b_gu, 'b_dn': b_dn}
        ctx = (cache_na_k[:, l].reshape(Bs, -1, NA_W), cache_na_v[:, l].reshape(Bs, -1, NA_W),
               state_hgrn[:, l], state_gdn[:, l])
        groups = [(0, Bp, Lp, None), (Tp, Bs, Ls, ctx)]
        x, z_na, states = _layer(x, groups, mods[l], mod_row, lw, p, norm_f, l == DEPTH - 1)
        ks_.append(z_na[:Tp, NA_W:2 * NA_W].reshape(Bp, Lp, NA_HEADS, NA_DH))
        vs_.append(z_na[:Tp, 2 * NA_W:].reshape(Bp, Lp, NA_HEADS, NA_DH))
        hs_.append(states[0][0])
        gs_.append(states[0][1])

    return (x[:Tp].reshape(Bp, Lp, D), x[Tp:].reshape(Bs, Ls, D),
            jnp.stack(ks_, axis=1), jnp.stack(vs_, axis=1), jnp.stack(hs_, axis=1), jnp.stack(gs_, axis=1))
```
